```python
import math
import jax, jax.numpy as jnp
from jax import lax
import numpy as np

D_MODEL = 2048
BATCH = 8
SEQ = 2048
DEPTH = 1
DEC_BATCH = 32
DEC_SEQ = 1
PAST_LEN = 16384
PAGE_SIZE = 128

D_MIX = D_MODEL
HEAD_DIM = 128
N_HEADS_A = D_MIX // (2 * HEAD_DIM)
DK_A = HEAD_DIM
DV_A = HEAD_DIM
KEY_DIM_A = N_HEADS_A * DK_A
VAL_DIM_A = N_HEADS_A * DV_A
CONV_DIM = 2 * KEY_DIM_A + VAL_DIM_A
CONV_W = 4
CHUNK = 64
N_HEADS_B = D_MIX // (2 * HEAD_DIM)
N_KV_B = 2
GQA_GROUP = N_HEADS_B // N_KV_B
N_IDX_HEADS = 16
D_IDX = 128
TOPK_MAX = 256
Q_BLOCK = 128
N_BUCKETS = 32
MAX_DISTANCE = 128
D_FF = 4 * D_MODEL
EPS = 1e-6
F32 = jnp.float32

SPLIT_SIZES = (CONV_DIM, N_HEADS_A, N_HEADS_A, VAL_DIM_A,
               N_HEADS_B * HEAD_DIM, N_KV_B * HEAD_DIM, N_KV_B * HEAD_DIM,
               N_IDX_HEADS * D_IDX, N_IDX_HEADS, D_IDX)
SPLIT_OFFSETS = tuple(int(o) for o in np.cumsum(SPLIT_SIZES)[:-1])
N_PROJ = int(sum(SPLIT_SIZES))

kernel_name = "hymba_gdn_dsa_sandwich_adaln_decode_step"


def rmsnorm(x, g):
    xf = x.astype(F32)
    y = xf * lax.rsqrt(jnp.mean(xf * xf, axis=-1, keepdims=True) + EPS)
    return (y * g.astype(F32)).astype(x.dtype)


def layernorm(x, g, b):
    xf = x.astype(F32)
    mu = jnp.mean(xf, axis=-1, keepdims=True)
    var = jnp.mean(jnp.square(xf - mu), axis=-1, keepdims=True)
    return ((xf - mu) * lax.rsqrt(var + EPS) * g.astype(F32) + b.astype(F32)).astype(x.dtype)


def l2norm(x):
    xf = x.astype(F32)
    return xf * lax.rsqrt(jnp.sum(xf * xf, axis=-1, keepdims=True) + EPS)


def adaln(c, w_ada, b_ada):
    m = jax.nn.silu(c) @ w_ada + b_ada
    return jnp.split(m[:, None, :], 6, axis=-1)


def causal_dwconv(x_pad, w):
    y = lax.conv_general_dilated(x_pad, w.astype(x_pad.dtype)[:, None, :], window_strides=(1,),
                                 padding="VALID", dimension_numbers=("NWC", "WIO", "NWC"),
                                 feature_group_count=x_pad.shape[-1])
    return jax.nn.silu(y)


def gdn_chunked(q, k, v, g, beta, s0):
    B, T, H, _ = q.shape
    n = T // CHUNK

    def to_chunks(t):
        t = jnp.moveaxis(t, 2, 1)
        return t.reshape((B, H, n, CHUNK) + t.shape[3:])

    q, k, v, g, beta = (to_chunks(t) for t in (q, k, v, g, beta))
    g = jnp.cumsum(g, axis=-1)
    k_beta = k * beta[..., None]
    v_beta = v * beta[..., None]
    causal = jnp.tril(jnp.ones((CHUNK, CHUNK), dtype=bool))
    strict = jnp.tril(jnp.ones((CHUNK, CHUNK), dtype=bool), k=-1)
    diff = g[..., :, None] - g[..., None, :]
    decay = jnp.where(causal, jnp.exp(jnp.where(causal, diff, 0.0)), 0.0)
    lower = jnp.where(strict, jnp.einsum("bhncd,bhnsd->bhncs", k_beta, k) * decay, 0.0)
    eye = jnp.eye(CHUNK, dtype=F32)
    t_inv = lax.linalg.triangular_solve(eye + lower, jnp.broadcast_to(eye, lower.shape),
                                        left_side=True, lower=True, unit_diagonal=True)
    u = jnp.einsum("bhncs,bhnsv->bhncv", t_inv, v_beta)
    w = jnp.einsum("bhncs,bhnsd->bhncd", t_inv, k_beta * jnp.exp(g)[..., None])
    intra = jnp.where(causal, jnp.einsum("bhncd,bhnsd->bhncs", q, k) * decay, 0.0)

    def step(s, inp):
        q_c, k_c, u_c, w_c, g_c, a_c = inp
        v_new = u_c - jnp.einsum("bhcd,bhdv->bhcv", w_c, s)
        o_c = (jnp.einsum("bhcd,bhdv->bhcv", q_c * jnp.exp(g_c)[..., None], s)
               + jnp.einsum("bhcs,bhsv->bhcv", a_c, v_new))
        g_last = g_c[..., -1]
        k_dec = k_c * jnp.exp(g_last[..., None] - g_c)[..., None]
        s = s * jnp.exp(g_last)[..., None, None] + jnp.einsum("bhcd,bhcv->bhdv", k_dec, v_new)
        return s, o_c

    xs = tuple(jnp.moveaxis(t, 2, 0) for t in (q, k, u, w, g, intra))
    s_new, o = lax.scan(step, s0, xs)
    o = jnp.moveaxis(o, 0, 2).reshape(B, H, T, -1)
    return jnp.moveaxis(o, 1, 2), s_new


def gdn_recurrent(q, k, v, g, beta, s0):
    def step(s, inp):
        q_t, k_t, v_t, g_t, b_t = inp
        s = s * jnp.exp(g_t)[..., None, None]
        kv = jnp.einsum("bhd,bhdv->bhv", k_t, s)
        delta = (v_t - kv) * b_t[..., None]
        s = s + k_t[..., :, None] * delta[..., None, :]
        return s, jnp.einsum("bhd,bhdv->bhv", q_t, s)

    xs = tuple(jnp.moveaxis(t, 1, 0) for t in (q, k, v, g, beta))
    s_new, o = lax.scan(step, s0, xs)
    return jnp.moveaxis(o, 0, 1), s_new


def gated_deltanet(conv_in, a_raw, b_raw, z, conv_prev, ssm_prev, conv_w, a_log, dt_bias, norm_g, chunked):
    B, T, _ = conv_in.shape
    x_pad = jnp.concatenate([conv_prev.astype(conv_in.dtype), conv_in], axis=1)
    conv_new = x_pad[:, x_pad.shape[1] - (CONV_W - 1):]
    y = causal_dwconv(x_pad, conv_w)
    q, k, v = jnp.split(y, (KEY_DIM_A, 2 * KEY_DIM_A), axis=-1)
    q = l2norm(q.reshape(B, T, N_HEADS_A, DK_A)) * (DK_A ** -0.5)
    k = l2norm(k.reshape(B, T, N_HEADS_A, DK_A))
    v = v.reshape(B, T, N_HEADS_A, DV_A).astype(F32)
    g = -jnp.exp(a_log.astype(F32)) * jax.nn.softplus(a_raw.astype(F32) + dt_bias.astype(F32))
    beta = jax.nn.sigmoid(b_raw.astype(F32))
    s0 = ssm_prev.astype(F32)
    if chunked:
        o, s_new = gdn_chunked(q, k, v, g, beta, s0)
    else:
        o, s_new = gdn_recurrent(q, k, v, g, beta, s0)
    zg = z.reshape(B, T, N_HEADS_A, DV_A).astype(F32)
    o = rmsnorm(o, norm_g) * jax.nn.silu(zg)
    return o.reshape(B, T, VAL_DIM_A).astype(conv_in.dtype), s_new, conv_new


def t5_bucket(dist):
    n = jnp.maximum(dist, 0)
    max_exact = N_BUCKETS // 2
    nf = jnp.maximum(n, 1).astype(F32)
    large = max_exact + (jnp.log(nf / max_exact) / math.log(MAX_DISTANCE / max_exact)
                         * (N_BUCKETS - max_exact)).astype(jnp.int32)
    large = jnp.minimum(large, N_BUCKETS - 1)
    return jnp.where(n < max_exact, n, large)


def indexer_scores(q_i, w_i, k_i):
    s = jax.nn.relu(jnp.einsum("bqhd,bld->bqhl", q_i, k_i).astype(F32))
    return jnp.einsum("bqhl,bqh->bql", s, w_i.astype(F32)) * (D_IDX ** -0.5)


def select_topk(score, q_pos, key_pos, n_sel):
    score = jnp.where(key_pos[None, None, :] <= q_pos[None, :, None], score, -jnp.inf)
    top, idx = lax.top_k(score, n_sel)
    return idx, jnp.isfinite(top)


def attend_selected(q, k_sel, v_sel, dist, valid, rel_bias):
    B, Tq = q.shape[:2]
    n_sel = k_sel.shape[2]
    qg = q.reshape(B, Tq, N_KV_B, GQA_GROUP, HEAD_DIM)
    logits = jnp.einsum("bqngd,bqknd->bqngk", qg, k_sel).astype(F32) * (HEAD_DIM ** -0.5)
    bias = rel_bias.astype(F32)[t5_bucket(dist)]
    bias = bias.reshape(B, Tq, n_sel, N_KV_B, GQA_GROUP).transpose(0, 1, 3, 4, 2)
    logits = jnp.where(valid[:, :, None, None, :], logits + bias, -jnp.inf)
    p = jax.nn.softmax(logits, axis=-1).astype(v_sel.dtype)
    o = jnp.einsum("bqngk,bqknd->bqngd", p, v_sel)
    return o.reshape(B, Tq, N_HEADS_B * HEAD_DIM)


def sparse_attention_prompt(q, k, v, q_i, w_i, k_i, rel_bias):
    B, T = q.shape[:2]
    n_sel = min(TOPK_MAX, T // 4)
    n_blk = T // Q_BLOCK
    key_pos = jnp.arange(T, dtype=jnp.int32)
    b_idx = jnp.arange(B)[:, None, None]

    def blocks(t):
        return jnp.moveaxis(t.reshape((B, n_blk, Q_BLOCK) + t.shape[2:]), 1, 0)

    def one_block(args):
        q_blk, qi_blk, wi_blk, q_pos = args
        score = indexer_scores(qi_blk, wi_blk, k_i)
        idx, valid = select_topk(score, q_pos, key_pos, n_sel)
        return attend_selected(q_blk, k[b_idx, idx], v[b_idx, idx],
                               q_pos[None, :, None] - idx, valid, rel_bias)

    q_pos = key_pos.reshape(n_blk, Q_BLOCK)
    out = lax.map(one_block, (blocks(q), blocks(q_i), blocks(w_i), q_pos))
    return jnp.moveaxis(out, 0, 1).reshape(B, T, N_HEADS_B * HEAD_DIM)


def sparse_attention_sample(q, k_new, v_new, q_i, w_i, k_i_new, cache_k, cache_v, cache_kidx,
                            page_table, rel_bias):
    B, T = q.shape[:2]
    n_pages = page_table.shape[1]
    past = n_pages * PAGE_SIZE
    n_keys = past + T
    n_sel = min(TOPK_MAX, n_keys // 4)
    ki_past = cache_kidx[page_table].reshape(B, past, D_IDX)
    ki_all = jnp.concatenate([ki_past.astype(k_i_new.dtype), k_i_new], axis=1)
    q_pos = past + jnp.arange(T, dtype=jnp.int32)
    key_pos = jnp.arange(n_keys, dtype=jnp.int32)
    score = indexer_scores(q_i, w_i, ki_all)
    idx, valid = select_topk(score, q_pos, key_pos, n_sel)
    b_idx = jnp.arange(B)[:, None, None]
    p_idx = jnp.minimum(idx, past - 1)
    phys = page_table[b_idx, p_idx // PAGE_SIZE]
    slot = p_idx % PAGE_SIZE
    n_idx = jnp.clip(idx - past, 0, T - 1)
    in_past = (idx < past)[..., None, None]
    k_sel = jnp.where(in_past, cache_k[phys, slot].astype(k_new.dtype), k_new[b_idx, n_idx])
    v_sel = jnp.where(in_past, cache_v[phys, slot].astype(v_new.dtype), v_new[b_idx, n_idx])
    return attend_selected(q, k_sel, v_sel, q_pos[None, :, None] - idx, valid, rel_bias)


def layer_step(x, c, conv_prev, ssm_prev, past, page_table, w_ada, b_ada, pre1_g, post1_g,
               pre2_g, post2_g, w_in, w_out, conv_w, a_log, dt_bias, gdn_norm_g,
               idx_knorm_g, idx_knorm_b, rel_bias, w_ff1, w_ff2):
    B, T, _ = x.shape
    sh1, sc1, g1, sh2, sc2, g2 = adaln(c, w_ada, b_ada)
    h = rmsnorm(x, pre1_g) * (1 + sc1) + sh1
    (conv_in, a_raw, b_raw, z, q_b, k_b, v_b, q_i, w_i, k_i) = jnp.split(h @ w_in, SPLIT_OFFSETS, axis=-1)
    o_a, ssm_new, conv_new = gated_deltanet(conv_in, a_raw, b_raw, z, conv_prev, ssm_prev, conv_w,
                                            a_log, dt_bias, gdn_norm_g, past is None)
    q_b = q_b.reshape(B, T, N_HEADS_B, HEAD_DIM)
    k_b = k_b.reshape(B, T, N_KV_B, HEAD_DIM)
    v_b = v_b.reshape(B, T, N_KV_B, HEAD_DIM)
    q_i = q_i.reshape(B, T, N_IDX_HEADS, D_IDX)
    w_i = w_i * (N_IDX_HEADS ** -0.5)
    k_i = layernorm(k_i, idx_knorm_g, idx_knorm_b)
    if past is None:
        o_b = sparse_attention_prompt(q_b, k_b, v_b, q_i, w_i, k_i, rel_bias)
    else:
        o_b = sparse_attention_sample(q_b, k_b, v_b, q_i, w_i, k_i, past[0], past[1], past[2],
                                      page_table, rel_bias)
    mix = jnp.concatenate([o_a, o_b.astype(o_a.dtype)], axis=-1) @ w_out
    x = x + g1 * rmsnorm(mix, post1_g)
    h = rmsnorm(x, pre2_g) * (1 + sc2) + sh2
    f = jnp.square(jax.nn.relu(h @ w_ff1)) @ w_ff2
    x = x + g2 * rmsnorm(f, post2_g)
    return x, k_b, v_b, k_i, ssm_new, conv_new


def _normal(k, shape, scale=1.0):
    return scale * jax.random.normal(k, shape, F32)


def setup_inputs(seed: int = 0) -> dict:
    key = jax.random.key(seed)
    ks = jax.random.split(key, 32)
    n_pages = PAST_LEN // PAGE_SIZE
    n_used = DEC_BATCH * n_pages
    n_pool = n_used + max(1, n_used // 4)
    page_table = jax.random.permutation(ks[9], n_pool)[:n_used].reshape(DEC_BATCH, n_pages).astype(jnp.int32)
    dt = jnp.exp(jax.random.uniform(ks[16], (DEPTH, N_HEADS_A), F32, math.log(1e-3), math.log(1e-1)))
    dt_bias = dt + jnp.log(-jnp.expm1(-dt))
    a_log = jnp.log(jax.random.uniform(ks[15], (DEPTH, N_HEADS_A), F32, 1.0, 16.0))
    return {
        "x_prompt": _normal(ks[0], (BATCH, SEQ, D_MODEL)),
        "x_sample": _normal(ks[1], (DEC_BATCH, DEC_SEQ, D_MODEL)),
        "c_prompt": _normal(ks[2], (BATCH, D_MODEL)),
        "c_sample": _normal(ks[3], (DEC_BATCH, D_MODEL)),
        "cache_k": _normal(ks[4], (DEPTH, n_pool, PAGE_SIZE, N_KV_B, HEAD_DIM)),
        "cache_v": _normal(ks[5], (DEPTH, n_pool, PAGE_SIZE, N_KV_B, HEAD_DIM)),
        "cache_kidx": _normal(ks[6], (DEPTH, n_pool, PAGE_SIZE, D_IDX)),
        "state_ssm": _normal(ks[7], (DEPTH, DEC_BATCH, N_HEADS_A, DK_A, DV_A), 0.1),
        "state_conv": _normal(ks[8], (DEPTH, DEC_BATCH, CONV_W - 1, CONV_DIM)),
        "page_table": page_table,
        "w_ada": _normal(ks[10], (DEPTH, D_MODEL, 6 * D_MODEL), 0.5 * D_MODEL ** -0.5),
        "b_ada": _normal(ks[11], (DEPTH, 6 * D_MODEL), 0.01),
        "pre1_g": 1.0 + _normal(ks[12], (DEPTH, D_MODEL), 0.05),
        "post1_g": 1.0 + _normal(ks[13], (DEPTH, D_MODEL), 0.05),
        "pre2_g": 1.0 + _normal(ks[14], (DEPTH, D_MODEL), 0.05),
        "post2_g": 1.0 + _normal(ks[17], (DEPTH, D_MODEL), 0.05),
        "w_in": _normal(ks[18], (DEPTH, D_MODEL, N_PROJ), D_MODEL ** -0.5),
        "w_out": _normal(ks[19], (DEPTH, D_MIX, D_MODEL), D_MIX ** -0.5),
        "conv_w": _normal(ks[20], (DEPTH, CONV_W, CONV_DIM), CONV_W ** -0.5),
        "a_log": a_log,
        "dt_bias": dt_bias,
        "gdn_norm_g": 1.0 + _normal(ks[21], (DEPTH, DV_A), 0.05),
        "idx_knorm_g": 1.0 + _normal(ks[22], (DEPTH, D_IDX), 0.05),
        "idx_knorm_b": _normal(ks[23], (DEPTH, D_IDX), 0.02),
        "rel_bias": _normal(ks[24], (N_BUCKETS, N_HEADS_B), 0.5),
        "w_ff1": _normal(ks[25], (DEPTH, D_MODEL, D_FF), D_MODEL ** -0.5),
        "w_ff2": _normal(ks[26], (DEPTH, D_FF, D_MODEL), D_FF ** -0.5),
    }


def reference(x_prompt, x_sample, c_prompt, c_sample, cache_k, cache_v, cache_kidx, state_ssm,
              state_conv, page_table, w_ada, b_ada, pre1_g, post1_g, pre2_g, post2_g, w_in, w_out,
              conv_w, a_log, dt_bias, gdn_norm_g, idx_knorm_g, idx_knorm_b, rel_bias, w_ff1, w_ff2):
    h_p, h_s = x_prompt, x_sample
    new_p, new_s = [], []
    for l in range(DEPTH):
        conv0 = jnp.zeros((x_prompt.shape[0], CONV_W - 1, CONV_DIM), x_prompt.dtype)
        ssm0 = jnp.zeros((x_prompt.shape[0], N_HEADS_A, DK_A, DV_A), F32)
        h_p, *st_p = layer_step(h_p, c_prompt, conv0, ssm0, None, None, w_ada[l], b_ada[l],
                                pre1_g[l], post1_g[l], pre2_g[l], post2_g[l], w_in[l], w_out[l],
                                conv_w[l], a_log[l], dt_bias[l], gdn_norm_g[l], idx_knorm_g[l],
                                idx_knorm_b[l], rel_bias, w_ff1[l], w_ff2[l])
        new_p.append(st_p)
        h_s, *st_s = layer_step(h_s, c_sample, state_conv[l], state_ssm[l],
                                (cache_k[l], cache_v[l], cache_kidx[l]), page_table, w_ada[l], b_ada[l],
                                pre1_g[l], post1_g[l], pre2_g[l], post2_g[l], w_in[l], w_out[l],
                                conv_w[l], a_log[l], dt_bias[l], gdn_norm_g[l], idx_knorm_g[l],
                                idx_knorm_b[l], rel_bias, w_ff1[l], w_ff2[l])
        new_s.append(st_s)
    p_k, p_v, p_kidx, p_ssm, p_conv = [jnp.stack([st[i] for st in new_p]) for i in range(5)]
    s_k, s_v, s_kidx, s_ssm, s_conv = [jnp.stack([st[i] for st in new_s]) for i in range(5)]
    return (h_p, h_s, p_k, p_v, p_kidx, p_ssm, p_conv, s_k, s_v, s_kidx, s_ssm, s_conv)
```

```python
import functools
import math

import jax
import jax.numpy as jnp
from jax import lax
from jax.experimental import pallas as pl
from jax.experimental.pallas import tpu as pltpu

F32 = jnp.float32
BF16 = jnp.bfloat16
I32 = jnp.int32

D_MODEL = 2048
PAGE_SIZE = 128
HEAD_DIM = 128
N_HEADS_A = 8
KEY_DIM_A = N_HEADS_A * HEAD_DIM
VAL_DIM_A = N_HEADS_A * HEAD_DIM
CONV_DIM = 2 * KEY_DIM_A + VAL_DIM_A
CONV_W = 4
N_HEADS_B = 8
N_KV_B = 2
GQA_GROUP = N_HEADS_B // N_KV_B
N_IDX_HEADS = 16
D_IDX = 128
TOPK_MAX = 256
N_BUCKETS = 32
MAX_DISTANCE = 128
D_FF = 4 * D_MODEL
EPS = 1e-6

SPLIT_SIZES = (CONV_DIM, N_HEADS_A, N_HEADS_A, VAL_DIM_A, N_HEADS_B * HEAD_DIM, N_KV_B * HEAD_DIM,
               N_KV_B * HEAD_DIM, N_IDX_HEADS * D_IDX, N_IDX_HEADS, D_IDX)

OFF_CONV = 0
OFF_Z = OFF_CONV + CONV_DIM
OFF_QI = OFF_Z + VAL_DIM_A
OFF_QB = OFF_QI + N_IDX_HEADS * D_IDX
OFF_KB = OFF_QB + N_HEADS_B * HEAD_DIM
OFF_VB = OFF_KB + N_KV_B * HEAD_DIM
OFF_KI = OFF_VB + N_KV_B * HEAD_DIM
OFF_SM = OFF_KI + D_IDX
NP = 8192
SM_A, SM_B, SM_W = 0, N_HEADS_A, 2 * N_HEADS_A

LANE = 128
VMEM_LIMIT = 56 * 1024 * 1024
INT_MIN = -(2 ** 31)
NEG_BIG = -1e30

GDN_CHUNK = 64
GDN_GROUP = 256
ATT_TQ = 128
ATT_TK = 256


def _cparams(sem):
    return pltpu.CompilerParams(dimension_semantics=sem, vmem_limit_bytes=VMEM_LIMIT)


def _sigmoid(x):
    return 1.0 / (1.0 + jnp.exp(-x))


def _softplus(x):
    return jnp.maximum(x, 0.0) + jnp.log(1.0 + jnp.exp(-jnp.abs(x)))


def _dot(a, b):
    return jnp.dot(a, b, preferred_element_type=F32)


def _dot_nt(a, b):
    return lax.dot_general(a, b, (((1,), (1,)), ((), ())), preferred_element_type=F32)


def _dot_tn(a, b):
    return lax.dot_general(a, b, (((0,), (0,)), ((), ())), preferred_element_type=F32)


def _mod(ref):
    v = ref[...]
    return v[0] if v.ndim == 3 else v


def _adaln_kernel(c_ref, w_ref, b_ref, o_ref):
    c = c_ref[...]
    s = (c * _sigmoid(c)).astype(BF16)
    o_ref[...] = _dot(s, w_ref[...].astype(BF16)) + b_ref[...]


def _adaln_call(c_all, w_ada, b_ada):
    n_rows = c_all.shape[0]
    n_out = w_ada.shape[1]
    tn = 1024
    return pl.pallas_call(
        _adaln_kernel,
        out_shape=jax.ShapeDtypeStruct((n_rows, n_out), F32),
        grid=(n_out // tn,),
        in_specs=[pl.BlockSpec((n_rows, D_MODEL), lambda j: (0, 0)),
                  pl.BlockSpec((D_MODEL, tn), lambda j: (0, j)),
                  pl.BlockSpec((1, tn), lambda j: (0, j))],
        out_specs=pl.BlockSpec((n_rows, tn), lambda j: (0, j)),
        compiler_params=_cparams(("arbitrary",)),
        name="adaln",
    )(c_all, w_ada, b_ada.reshape(1, n_out))


class _ModSpec:
    def __init__(self, m2, per_row, row_off, rows_per_batch, tm):
        self.per_row = per_row
        self.arr = m2 if per_row else m2.reshape(m2.shape[0], 1, m2.shape[1])
        self.row_off = row_off
        self.blocks_per_batch = None if per_row else rows_per_batch // tm
        self.tm = tm

    def spec(self, k, grid_rank):
        if self.per_row:
            if grid_rank == 1:
                return pl.BlockSpec((self.tm, D_MODEL), lambda i: (0, k))
            return pl.BlockSpec((self.tm, D_MODEL), lambda i, j: (0, k))
        bpb, off = self.blocks_per_batch, self.row_off
        if grid_rank == 1:
            return pl.BlockSpec((1, 1, D_MODEL), lambda i: (off + i // bpb, 0, k))
        return pl.BlockSpec((1, 1, D_MODEL), lambda i, j: (off + i // bpb, 0, k))


def _inproj_kernel(x_ref, g_ref, sh_ref, sc_ref, w_ref, o_ref, h_scr):
    @pl.when(pl.program_id(1) == 0)
    def _():
        x = x_ref[...]
        y = x * lax.rsqrt(jnp.mean(x * x, axis=-1, keepdims=True) + EPS) * g_ref[...]
        h_scr[...] = (y * (1.0 + _mod(sc_ref)) + _mod(sh_ref)).astype(BF16)

    o_ref[...] = _dot(h_scr[...], w_ref[...])


def _inproj_call(x2d, g, mod, w_bf, tm, tn):
    m_rows = x2d.shape[0]
    n_cols = w_bf.shape[1]
    return pl.pallas_call(
        _inproj_kernel,
        out_shape=jax.ShapeDtypeStruct((m_rows, n_cols), F32),
        grid=(m_rows // tm, n_cols // tn),
        in_specs=[pl.BlockSpec((tm, D_MODEL), lambda i, j: (i, 0)),
                  pl.BlockSpec((1, D_MODEL), lambda i, j: (0, 0)),
                  mod.spec(0, 2), mod.spec(1, 2),
                  pl.BlockSpec((D_MODEL, tn), lambda i, j: (0, j))],
        out_specs=pl.BlockSpec((tm, tn), lambda i, j: (i, j)),
        scratch_shapes=[pltpu.VMEM((tm, D_MODEL), BF16)],
        compiler_params=_cparams(("arbitrary", "arbitrary")),
        name="inproj",
    )(x2d, g.reshape(1, D_MODEL), mod.arr, mod.arr, w_bf)


def _kvprep_kernel(kb_ref, vb_ref, ki_ref, lg_ref, lb_ref, k_ref, v_ref, kidx_ref, kbf_ref, vbf_ref, kibf_ref):
    kb = kb_ref[...]
    vb = vb_ref[...]
    k_ref[...] = kb
    v_ref[...] = vb
    kbf_ref[...] = kb.astype(BF16)
    vbf_ref[...] = vb.astype(BF16)
    x = ki_ref[...]
    mu = jnp.mean(x, axis=-1, keepdims=True)
    xc = x - mu
    var = jnp.mean(xc * xc, axis=-1, keepdims=True)
    ki = xc * lax.rsqrt(var + EPS) * lg_ref[...] + lb_ref[...]
    kidx_ref[...] = ki
    kibf_ref[...] = ki.astype(BF16)


def _kvprep_call(proj, ln_g, ln_b, tm):
    m_rows = proj.shape[0]
    kvw = N_KV_B * HEAD_DIM
    row = lambda i: (i, 0)
    return pl.pallas_call(
        _kvprep_kernel,
        out_shape=(jax.ShapeDtypeStruct((m_rows, kvw), F32), jax.ShapeDtypeStruct((m_rows, kvw), F32),
                   jax.ShapeDtypeStruct((m_rows, D_IDX), F32),
                   jax.ShapeDtypeStruct((m_rows, kvw), BF16), jax.ShapeDtypeStruct((m_rows, kvw), BF16),
                   jax.ShapeDtypeStruct((m_rows, D_IDX), BF16)),
        grid=(m_rows // tm,),
        in_specs=[pl.BlockSpec((tm, kvw), lambda i: (i, OFF_KB // kvw)),
                  pl.BlockSpec((tm, kvw), lambda i: (i, OFF_VB // kvw)),
                  pl.BlockSpec((tm, D_IDX), lambda i: (i, OFF_KI // D_IDX)),
                  pl.BlockSpec((1, D_IDX), lambda i: (0, 0)),
                  pl.BlockSpec((1, D_IDX), lambda i: (0, 0))],
        out_specs=(pl.BlockSpec((tm, kvw), row), pl.BlockSpec((tm, kvw), row), pl.BlockSpec((tm, D_IDX), row),
                   pl.BlockSpec((tm, kvw), row), pl.BlockSpec((tm, kvw), row), pl.BlockSpec((tm, D_IDX), row)),
        compiler_params=_cparams(("arbitrary",)),
        name="kvprep",
    )(proj, proj, proj, ln_g.reshape(1, D_IDX), ln_b.reshape(1, D_IDX))


def _conv_silu(x, cw, row):
    y = x * cw[CONV_W - 1:CONV_W]
    for s in range(1, CONV_W):
        xs = jnp.where(row >= s, pltpu.roll(x, s, 0), 0.0)
        y = y + xs * cw[CONV_W - 1 - s:CONV_W - s]
    return y * _sigmoid(y)


def _l2norm(x):
    return x * lax.rsqrt(jnp.sum(x * x, axis=-1, keepdims=True) + EPS)


def _gdn_prompt_kernel(q_ref, k_ref, v_ref, z_ref, sm_ref, cwq_ref, cwk_ref, cwv_ref, alog_ref, dtb_ref, ng_ref,
                       o_ref, s_out_ref,
                       qn_s, kn_s, vn_s, gc_s, bt_s, l_s, a_s, x_s, u_s, w_s, vbuf_s, st_s):
    seq = q_ref.shape[1]
    hd = HEAD_DIM
    grp = GDN_GROUP
    chunk = GDN_CHUNK
    n_grp = seq // grp
    n_chunk = seq // chunk
    h = pl.program_id(1)

    row = lax.broadcasted_iota(I32, (seq, hd), 0)
    lane = lax.broadcasted_iota(I32, (seq, hd), 1)

    qn_s[...] = _l2norm(_conv_silu(q_ref[0], cwq_ref[...], row)) * (hd ** -0.5)
    kn_s[...] = _l2norm(_conv_silu(k_ref[0], cwk_ref[...], row))
    vn_s[...] = _conv_silu(v_ref[0], cwv_ref[...], row)

    sm = sm_ref[0]
    g_all = -jnp.exp(alog_ref[...]) * _softplus(sm + dtb_ref[...])
    b_all = _sigmoid(sm)
    g_col = jnp.sum(jnp.where(lane == h + SM_A, g_all, 0.0), axis=1, keepdims=True)
    b_col = jnp.sum(jnp.where(lane == h + SM_B, b_all, 0.0), axis=1, keepdims=True)
    gc = jnp.broadcast_to(g_col, (seq, hd))
    pos = row & (chunk - 1)
    s = 1
    while s < chunk:
        gc = gc + jnp.where(pos >= s, pltpu.roll(gc, s, 0), 0.0)
        s *= 2
    gc_s[...] = gc
    bt_s[...] = jnp.broadcast_to(b_col, (seq, hd))

    ii = lax.broadcasted_iota(I32, (grp, grp), 0)
    jj = lax.broadcasted_iota(I32, (grp, grp), 1)
    xr = ii ^ jj
    same = (xr >> int(math.log2(chunk))) == 0
    causal = same & (ii >= jj)
    strict = same & (ii > jj)
    eye = (ii == jj).astype(F32)

    def build(gi, _):
        r0 = pl.multiple_of(gi * grp, grp)
        kk = kn_s[pl.ds(r0, grp), :]
        qq = qn_s[pl.ds(r0, grp), :]
        bt = bt_s[pl.ds(r0, grp), :]
        gcg = gc_s[pl.ds(r0, grp), :]
        kb = (kk * bt).astype(BF16)
        kkb = kk.astype(BF16)
        kkt = _dot_nt(kb, kkb)
        qkt = _dot_nt(qq.astype(BF16), kkb)
        colb = jnp.concatenate([gcg, gcg], axis=1)
        diff = colb - colb.T
        decay = jnp.where(causal, jnp.exp(jnp.where(causal, diff, 0.0)), 0.0)
        lmat = jnp.where(strict, kkt * decay, 0.0)
        l_s[gi] = lmat
        a_s[pl.ds(r0, grp), :] = jnp.where(causal, qkt * decay, 0.0)
        x_s[gi] = eye - jnp.where(xr == 1, lmat, 0.0)
        return 0

    lax.fori_loop(0, n_grp, build, 0)

    for lvl in range(1, int(math.log2(chunk))):
        def level(gi, _, lvl=lvl):
            xm = x_s[gi]
            cl = jnp.where((xr >> lvl) == 1, l_s[gi], 0.0).astype(BF16)
            xb = xm.astype(BF16)
            x_s[gi] = xm - _dot(_dot(xb, cl).astype(BF16), xb)
            return 0

        lax.fori_loop(0, n_grp, level, 0)

    def apply(gi, _):
        r0 = pl.multiple_of(gi * grp, grp)
        kk = kn_s[pl.ds(r0, grp), :]
        vv = vn_s[pl.ds(r0, grp), :]
        bt = bt_s[pl.ds(r0, grp), :]
        gcg = gc_s[pl.ds(r0, grp), :]
        rhs = jnp.concatenate([vv * bt, kk * bt * jnp.exp(gcg)], axis=1).astype(BF16)
        uw = _dot(x_s[gi].astype(BF16), rhs)
        u_s[pl.ds(r0, grp), :] = uw[:, :hd]
        w_s[pl.ds(r0, grp), :] = uw[:, hd:]
        return 0

    lax.fori_loop(0, n_grp, apply, 0)

    st_s[...] = jnp.zeros_like(st_s)
    vbuf_s[...] = jnp.zeros_like(vbuf_s)
    ng = ng_ref[...]
    cpg = grp // chunk

    def scan(c, _):
        r0 = pl.multiple_of(c * chunk, chunk)
        st = st_s[...]
        stb = st.astype(BF16)
        gcc = gc_s[pl.ds(r0, chunk), :]
        kc = kn_s[pl.ds(r0, chunk), :]
        qc = qn_s[pl.ds(r0, chunk), :]
        v_new = u_s[pl.ds(r0, chunk), :] - _dot(w_s[pl.ds(r0, chunk), :].astype(BF16), stb)
        voff = pl.multiple_of((c % cpg) * chunk, chunk)
        vbuf_s[pl.ds(voff, chunk), :] = v_new
        o = (_dot((qc * jnp.exp(gcc)).astype(BF16), stb)
             + _dot(a_s[pl.ds(r0, chunk), :].astype(BF16), vbuf_s[...].astype(BF16)))
        g_last = gc_s[pl.ds(r0 + chunk - 1, 1), :]
        k_dec = (kc * jnp.exp(g_last - gcc)).astype(BF16)
        st_s[...] = st * jnp.exp(g_last) + _dot_tn(k_dec, v_new.astype(BF16))
        zc = z_ref[0, pl.ds(r0, chunk), :]
        on = o * lax.rsqrt(jnp.mean(o * o, axis=-1, keepdims=True) + EPS) * ng
        o_ref[0, pl.ds(r0, chunk), :] = (on * (zc * _sigmoid(zc))).astype(o_ref.dtype)
        return 0

    lax.fori_loop(0, n_chunk, scan, 0)
    s_out_ref[0, 0] = st_s[...]


def _lane_vec(v, off):
    return jnp.zeros((1, LANE), F32).at[0, off:off + v.shape[0]].set(v.astype(F32))


def _gdn_prompt_call(proj3, conv_w, a_log, dt_bias, norm_g):
    bsz, seq, _ = proj3.shape
    hd = HEAD_DIM
    nh = N_HEADS_A
    n_grp = seq // GDN_GROUP
    col = lambda base: (lambda b, h: (b, 0, base // hd + h))
    cw = lambda base: (lambda b, h: (0, base // hd + h))
    vec = pl.BlockSpec((1, LANE), lambda b, h: (0, 0))
    return pl.pallas_call(
        _gdn_prompt_kernel,
        out_shape=(jax.ShapeDtypeStruct((bsz, seq, VAL_DIM_A), BF16),
                   jax.ShapeDtypeStruct((bsz, nh, hd, hd), F32)),
        grid=(bsz, nh),
        in_specs=[pl.BlockSpec((1, seq, hd), col(OFF_CONV)),
                  pl.BlockSpec((1, seq, hd), col(OFF_CONV + KEY_DIM_A)),
                  pl.BlockSpec((1, seq, hd), col(OFF_CONV + 2 * KEY_DIM_A)),
                  pl.BlockSpec((1, seq, hd), col(OFF_Z)),
                  pl.BlockSpec((1, seq, LANE), lambda b, h: (b, 0, OFF_SM // LANE)),
                  pl.BlockSpec((CONV_W, hd), cw(0)),
                  pl.BlockSpec((CONV_W, hd), cw(KEY_DIM_A)),
                  pl.BlockSpec((CONV_W, hd), cw(2 * KEY_DIM_A)),
                  vec, vec, vec],
        out_specs=(pl.BlockSpec((1, seq, hd), lambda b, h: (b, 0, h)),
                   pl.BlockSpec((1, 1, hd, hd), lambda b, h: (b, h, 0, 0))),
        scratch_shapes=[pltpu.VMEM((seq, hd), F32), pltpu.VMEM((seq, hd), F32), pltpu.VMEM((seq, hd), F32),
                        pltpu.VMEM((seq, hd), F32), pltpu.VMEM((seq, hd), F32),
                        pltpu.VMEM((n_grp, GDN_GROUP, GDN_GROUP), F32),
                        pltpu.VMEM((seq, GDN_GROUP), F32),
                        pltpu.VMEM((n_grp, GDN_GROUP, GDN_GROUP), F32),
                        pltpu.VMEM((seq, hd), F32), pltpu.VMEM((seq, hd), F32),
                        pltpu.VMEM((GDN_GROUP, hd), F32), pltpu.VMEM((hd, hd), F32)],
        compiler_params=_cparams(("arbitrary", "arbitrary")),
        name="gdn_prompt",
    )(proj3, proj3, proj3, proj3, proj3, conv_w, conv_w, conv_w,
      _lane_vec(a_log, SM_A), _lane_vec(dt_bias, SM_A), norm_g.reshape(1, hd).astype(F32))


def _t5_bucket(dist):
    n = jnp.maximum(dist, 0)
    max_exact = N_BUCKETS // 2
    nf = jnp.maximum(n, 1).astype(F32)
    large = max_exact + (jnp.log(nf / max_exact) / math.log(MAX_DISTANCE / max_exact)
                         * (N_BUCKETS - max_exact)).astype(I32)
    large = jnp.minimum(large, N_BUCKETS - 1)
    return jnp.where(n < max_exact, n, large)


def _bias_from_bucket(bucket, rb_ref, head):
    out = jnp.zeros(bucket.shape, F32)
    for b in range(N_BUCKETS):
        out = jnp.where(bucket == b, rb_ref[b, head], out)
    return out


def _order_key(score):
    bits = pltpu.bitcast(score + 0.0, I32)
    return bits ^ ((bits >> 31) & 0x7FFFFFFF)


def _kth_largest_key(count_ge, n_rows, n_sel):
    def body(i, tau):
        bit = 31 - i
        cand = jnp.where(i == 0, tau ^ INT_MIN, tau | (1 << bit))
        return jnp.where(count_ge(cand) >= n_sel, cand, tau)

    return lax.fori_loop(0, 32, body, jnp.full((n_rows, 1), INT_MIN, I32))


def _attn_prompt_kernel(n_sel, qi_ref, sm_ref, qb_ref, ki_ref, k_ref, v_ref, rb_ref, o_ref,
                        qi_s, qb_s, keys_s, bias_s, m_s, l_s, acc_s):
    tq, tk = ATT_TQ, ATT_TK
    hd = HEAD_DIM
    bi = pl.program_id(0)
    qblk = pl.program_id(1)
    n_kb = (qblk * tq) // tk + 1
    rowq = lax.broadcasted_iota(I32, (tq, tk), 0)
    colk = lax.broadcasted_iota(I32, (tq, tk), 1)

    @pl.when((bi == 0) & (qblk == 0))
    def _():
        for off in range(bias_s.shape[1]):
            bucket = _t5_bucket(off * tq + rowq - colk)
            for h in range(N_HEADS_B):
                bias_s[h, off] = _bias_from_bucket(bucket, rb_ref, h)

    qi_s[...] = qi_ref[0].astype(BF16)
    qb_s[...] = qb_ref[0].astype(BF16)
    sm = sm_ref[0]
    t_pos = qblk * tq + rowq

    def idx_body(j, _):
        k0 = pl.multiple_of(j * tk, tk)
        kt = ki_ref[0, pl.ds(k0, tk), :]
        acc = jnp.zeros((tq, tk), F32)
        for h in range(N_IDX_HEADS):
            s = _dot_nt(qi_s[:, h * D_IDX:(h + 1) * D_IDX], kt)
            wh = sm[:, SM_W + h:SM_W + h + 1] * (N_IDX_HEADS ** -0.5)
            acc = acc + jnp.maximum(s, 0.0) * wh
        acc = acc * (D_IDX ** -0.5)
        keys_s[j] = jnp.where(k0 + colk <= t_pos, _order_key(acc), INT_MIN)
        return 0

    lax.fori_loop(0, n_kb, idx_body, 0)

    def count_ge(cand):
        def cb(j, c):
            return c + jnp.where(keys_s[j] >= cand, 1, 0)

        c = lax.fori_loop(0, n_kb, cb, jnp.zeros((tq, tk), I32))
        return jnp.sum(c, axis=1, keepdims=True)

    tau = jnp.maximum(_kth_largest_key(count_ge, tq, n_sel), INT_MIN + 1)

    m_s[...] = jnp.full(m_s.shape, NEG_BIG, F32)
    l_s[...] = jnp.zeros(l_s.shape, F32)
    acc_s[...] = jnp.zeros(acc_s.shape, F32)
    scale = hd ** -0.5

    def att_body(j, _):
        k0 = pl.multiple_of(j * tk, tk)
        sel = keys_s[j] >= tau
        kt = k_ref[0, pl.ds(k0, tk), :]
        vt = v_ref[0, pl.ds(k0, tk), :]
        boff = jnp.minimum((qblk * tq - k0) // tq, bias_s.shape[1] - 1)
        for h in range(N_HEADS_B):
            n = h // GQA_GROUP
            lg = _dot_nt(qb_s[:, h * hd:(h + 1) * hd], kt[:, n * hd:(n + 1) * hd]) * scale + bias_s[h, boff]
            lg = jnp.where(sel, lg, NEG_BIG)
            m_old = m_s[h]
            m_new = jnp.maximum(m_old, jnp.max(lg, axis=1, keepdims=True))
            p = jnp.where(sel, jnp.exp(lg - m_new), 0.0)
            alpha = jnp.exp(m_old - m_new)
            l_s[h] = alpha * l_s[h] + jnp.sum(p, axis=1, keepdims=True)
            acc_s[h] = alpha * acc_s[h] + _dot(p.astype(BF16), vt[:, n * hd:(n + 1) * hd])
            m_s[h] = m_new
        return 0

    lax.fori_loop(0, n_kb, att_body, 0)
    for h in range(N_HEADS_B):
        o_ref[0, :, h * hd:(h + 1) * hd] = (acc_s[h] / l_s[h]).astype(o_ref.dtype)


def _attn_prompt_call(proj3, ki_bf, k_bf, v_bf, rel_bias):
    bsz, seq, _ = proj3.shape
    tq, tk = ATT_TQ, ATT_TK
    n_sel = min(TOPK_MAX, seq // 4)
    qiw = N_IDX_HEADS * D_IDX
    qbw = N_HEADS_B * HEAD_DIM
    kvw = N_KV_B * HEAD_DIM
    n_off = -(-(MAX_DISTANCE + tk - 1) // tq) + 1
    return pl.pallas_call(
        functools.partial(_attn_prompt_kernel, n_sel),
        out_shape=jax.ShapeDtypeStruct((bsz, seq, qbw), BF16),
        grid=(bsz, seq // tq),
        in_specs=[pl.BlockSpec((1, tq, qiw), lambda b, i: (b, i, OFF_QI // qiw)),
                  pl.BlockSpec((1, tq, LANE), lambda b, i: (b, i, OFF_SM // LANE)),
                  pl.BlockSpec((1, tq, qbw), lambda b, i: (b, i, OFF_QB // qbw)),
                  pl.BlockSpec((1, seq, D_IDX), lambda b, i: (b, 0, 0)),
                  pl.BlockSpec((1, seq, kvw), lambda b, i: (b, 0, 0)),
                  pl.BlockSpec((1, seq, kvw), lambda b, i: (b, 0, 0)),
                  pl.BlockSpec(memory_space=pltpu.SMEM)],
        out_specs=pl.BlockSpec((1, tq, qbw), lambda b, i: (b, i, 0)),
        scratch_shapes=[pltpu.VMEM((tq, qiw), BF16), pltpu.VMEM((tq, qbw), BF16),
                        pltpu.VMEM((seq // tk, tq, tk), I32),
                        pltpu.VMEM((N_HEADS_B, n_off, tq, tk), F32),
                        pltpu.VMEM((N_HEADS_B, tq, 1), F32), pltpu.VMEM((N_HEADS_B, tq, 1), F32),
                        pltpu.VMEM((N_HEADS_B, tq, HEAD_DIM), F32)],
        compiler_params=_cparams(("arbitrary", "arbitrary")),
        name="attn_prompt",
    )(proj3, proj3, proj3, ki_bf, k_bf, v_bf, rel_bias.astype(F32))


def _rms(x, g):
    return x * lax.rsqrt(jnp.mean(x * x, axis=-1, keepdims=True) + EPS) * g


def _outproj_kernel(oa_ref, ob_ref, x_ref, wa_ref, wb_ref, pg_ref, g1_ref, p2_ref, sh2_ref, sc2_ref, x1_ref, h2_ref):
    mix = _dot(oa_ref[...], wa_ref[...]) + _dot(ob_ref[...], wb_ref[...])
    x1 = x_ref[...] + _mod(g1_ref) * _rms(mix, pg_ref[...])
    x1_ref[...] = x1
    h2_ref[...] = (_rms(x1, p2_ref[...]) * (1.0 + _mod(sc2_ref)) + _mod(sh2_ref)).astype(BF16)


def _outproj_call(oa, ob, x2d, w_out_bf, post1_g, pre2_g, mod, tm):
    m_rows = x2d.shape[0]
    half = w_out_bf.shape[0] // 2
    row = lambda i: (i, 0)
    vec = pl.BlockSpec((1, D_MODEL), lambda i: (0, 0))
    return pl.pallas_call(
        _outproj_kernel,
        out_shape=(jax.ShapeDtypeStruct((m_rows, D_MODEL), F32), jax.ShapeDtypeStruct((m_rows, D_MODEL), BF16)),
        grid=(m_rows // tm,),
        in_specs=[pl.BlockSpec((tm, half), row), pl.BlockSpec((tm, half), row), pl.BlockSpec((tm, D_MODEL), row),
                  pl.BlockSpec((half, D_MODEL), lambda i: (0, 0)), pl.BlockSpec((half, D_MODEL), lambda i: (1, 0)),
                  vec, mod.spec(2, 1), vec, mod.spec(3, 1), mod.spec(4, 1)],
        out_specs=(pl.BlockSpec((tm, D_MODEL), row), pl.BlockSpec((tm, D_MODEL), row)),
        compiler_params=_cparams(("arbitrary",)),
        name="outproj",
    )(oa, ob, x2d, w_out_bf, w_out_bf, post1_g.reshape(1, D_MODEL), mod.arr, pre2_g.reshape(1, D_MODEL),
      mod.arr, mod.arr)


def _ffn_kernel(h_ref, x1_ref, w1_ref, w2_ref, pg_ref, g2_ref, o_ref, acc_s):
    kk = pl.program_id(1)

    @pl.when(kk == 0)
    def _():
        acc_s[...] = jnp.zeros_like(acc_s)

    a = jnp.maximum(_dot(h_ref[...], w1_ref[...]), 0.0)
    acc_s[...] += _dot((a * a).astype(BF16), w2_ref[...])

    @pl.when(kk == pl.num_programs(1) - 1)
    def _():
        o_ref[...] = x1_ref[...] + _mod(g2_ref) * _rms(acc_s[...], pg_ref[...])


def _ffn_call(h2, x1, w1_bf, w2_bf, post2_g, mod, tm, tf):
    m_rows = x1.shape[0]
    return pl.pallas_call(
        _ffn_kernel,
        out_shape=jax.ShapeDtypeStruct((m_rows, D_MODEL), F32),
        grid=(m_rows // tm, D_FF // tf),
        in_specs=[pl.BlockSpec((tm, D_MODEL), lambda i, k: (i, 0)),
                  pl.BlockSpec((tm, D_MODEL), lambda i, k: (i, 0)),
                  pl.BlockSpec((D_MODEL, tf), lambda i, k: (0, k)),
                  pl.BlockSpec((tf, D_MODEL), lambda i, k: (k, 0)),
                  pl.BlockSpec((1, D_MODEL), lambda i, k: (0, 0)),
                  mod.spec(5, 2)],
        out_specs=pl.BlockSpec((tm, D_MODEL), lambda i, k: (i, 0)),
        scratch_shapes=[pltpu.VMEM((tm, D_MODEL), F32)],
        compiler_params=_cparams(("arbitrary", "arbitrary")),
        name="ffn",
    )(h2, x1, w1_bf, w2_bf, post2_g.reshape(1, D_MODEL), mod.arr)


def _gdn_sample_kernel(cin_ref, z_ref, sm_ref, sconv_ref, ssm_ref, cw_ref, alog_ref, dtb_ref, ng_ref,
                       o_ref, ssm_out_ref, conv_out_ref):
    hd = HEAD_DIM
    nh = N_HEADS_A
    prev = sconv_ref[0]
    xin = cin_ref[0]
    cw = cw_ref[...]
    y = xin * cw[CONV_W - 1:CONV_W]
    for s in range(CONV_W - 1):
        y = y + prev[s:s + 1] * cw[s:s + 1]
    y = y * _sigmoid(y)
    conv_out_ref[0, 0:CONV_W - 2, :] = prev[1:CONV_W - 1]
    conv_out_ref[0, CONV_W - 2:CONV_W - 1, :] = xin

    def heads(base):
        return jnp.concatenate([y[:, base + h * hd:base + (h + 1) * hd] for h in range(nh)], axis=0)

    q8 = _l2norm(heads(0)) * (hd ** -0.5)
    k8 = _l2norm(heads(KEY_DIM_A))
    v8 = heads(2 * KEY_DIM_A)
    sm = sm_ref[0]
    g_all = -jnp.exp(alog_ref[...]) * _softplus(sm + dtb_ref[...])
    b_all = _sigmoid(sm)
    kt = jnp.concatenate([k8, jnp.zeros((LANE - nh, hd), F32)], axis=0).T
    k8b = k8.astype(BF16)
    q8b = q8.astype(BF16)
    z = z_ref[0]
    ng = ng_ref[...]
    for h in range(nh):
        s1 = ssm_ref[0, h] * jnp.exp(g_all[:, SM_A + h:SM_A + h + 1])
        kv = _dot(k8b, s1.astype(BF16))[h:h + 1]
        delta = (v8[h:h + 1] - kv) * b_all[:, SM_B + h:SM_B + h + 1]
        s2 = s1 + kt[:, h:h + 1] * delta
        ssm_out_ref[0, h] = s2
        o = _dot(q8b, s2.astype(BF16))[h:h + 1]
        zh = z[:, h * hd:(h + 1) * hd]
        on = o * lax.rsqrt(jnp.mean(o * o, axis=-1, keepdims=True) + EPS) * ng
        o_ref[0, :, h * hd:(h + 1) * hd] = (on * (zh * _sigmoid(zh))).astype(o_ref.dtype)


def _gdn_sample_call(proj_s3, state_conv, state_ssm, conv_w, a_log, dt_bias, norm_g):
    nb = proj_s3.shape[0]
    hd, nh = HEAD_DIM, N_HEADS_A
    vec = pl.BlockSpec((1, LANE), lambda b: (0, 0))
    return pl.pallas_call(
        _gdn_sample_kernel,
        out_shape=(jax.ShapeDtypeStruct((nb, 1, VAL_DIM_A), BF16),
                   jax.ShapeDtypeStruct((nb, nh, hd, hd), F32),
                   jax.ShapeDtypeStruct((nb, CONV_W - 1, CONV_DIM), F32)),
        grid=(nb,),
        in_specs=[pl.BlockSpec((1, 1, CONV_DIM), lambda b: (b, 0, OFF_CONV // CONV_DIM)),
                  pl.BlockSpec((1, 1, VAL_DIM_A), lambda b: (b, 0, OFF_Z // VAL_DIM_A)),
                  pl.BlockSpec((1, 1, LANE), lambda b: (b, 0, OFF_SM // LANE)),
                  pl.BlockSpec((1, CONV_W - 1, CONV_DIM), lambda b: (b, 0, 0)),
                  pl.BlockSpec((1, nh, hd, hd), lambda b: (b, 0, 0, 0)),
                  pl.BlockSpec((CONV_W, CONV_DIM), lambda b: (0, 0)),
                  vec, vec, vec],
        out_specs=(pl.BlockSpec((1, 1, VAL_DIM_A), lambda b: (b, 0, 0)),
                   pl.BlockSpec((1, nh, hd, hd), lambda b: (b, 0, 0, 0)),
                   pl.BlockSpec((1, CONV_W - 1, CONV_DIM), lambda b: (b, 0, 0))),
        compiler_params=_cparams(("arbitrary",)),
        name="gdn_sample",
    )(proj_s3, proj_s3, proj_s3, state_conv, state_ssm, conv_w,
      _lane_vec(a_log, SM_A), _lane_vec(dt_bias, SM_A), norm_g.reshape(1, hd).astype(F32))


IDX_PAGES_PER_STEP = 16


def _idx_sample_kernel(pt_ref, qi_ref, wi_ref, *rest):
    del pt_ref
    pages = rest[:IDX_PAGES_PER_STEP]
    o_ref = rest[IDX_PAGES_PER_STEP]
    qb = qi_ref[0].astype(BF16)
    w = wi_ref[0] * (N_IDX_HEADS ** -0.5)
    for p, pg in enumerate(pages):
        s = _dot_nt(qb, pg[0].astype(BF16))
        sc = jnp.sum(jnp.maximum(s, 0.0) * w, axis=0, keepdims=True) * (D_IDX ** -0.5)
        o_ref[0, :, p * PAGE_SIZE:(p + 1) * PAGE_SIZE] = sc


def _idx_sample_call(page_table, qi3, wi3, cache_kidx):
    nb, n_pages = page_table.shape
    pp = IDX_PAGES_PER_STEP
    page_specs = [pl.BlockSpec((1, PAGE_SIZE, D_IDX), functools.partial(lambda b, j, pt, p: (pt[b, j * pp + p], 0, 0), p=p))
                  for p in range(pp)]
    grid_spec = pltpu.PrefetchScalarGridSpec(
        num_scalar_prefetch=1,
        grid=(nb, n_pages // pp),
        in_specs=[pl.BlockSpec((1, N_IDX_HEADS, D_IDX), lambda b, j, pt: (b, 0, 0)),
                  pl.BlockSpec((1, N_IDX_HEADS, 1), lambda b, j, pt: (b, 0, 0))] + page_specs,
        out_specs=pl.BlockSpec((1, 1, pp * PAGE_SIZE), lambda b, j, pt: (b, 0, j)),
    )
    return pl.pallas_call(
        _idx_sample_kernel,
        out_shape=jax.ShapeDtypeStruct((nb, 1, n_pages * PAGE_SIZE), F32),
        grid_spec=grid_spec,
        compiler_params=_cparams(("arbitrary", "arbitrary")),
        name="idx_sample",
    )(page_table, qi3, wi3, *([cache_kidx] * pp))


IDX_FLAG = 1 << 20
KEY_POS_INF = 0x7F800000
KEY_NEG_INF = INT_MIN + 0x7FFFFF


def _topk_sample_kernel(n_sel, sc_ref, qi_ref, wi_ref, knew_ref, idx_ref, keys_s, rank_s, pos_s):
    nb, n_past = sc_ref.shape
    n_blk = n_past // LANE
    lane = lax.broadcasted_iota(I32, (nb, LANE), 1)

    kn = knew_ref[...].astype(BF16)
    q = qi_ref[...]
    w = wi_ref[...] * (N_IDX_HEADS ** -0.5)
    s_new = jnp.zeros((nb, 1), F32)
    for h in range(N_IDX_HEADS):
        qh = q[:, h * D_IDX:(h + 1) * D_IDX].astype(BF16).astype(F32)
        dot = jnp.sum(qh * kn.astype(F32), axis=1, keepdims=True)
        s_new = s_new + jnp.maximum(dot, 0.0) * w[:, h:h + 1]
    s_new = s_new * (D_IDX ** -0.5)
    key_new = _order_key(s_new)

    def to_keys(j, _):
        keys_s[j] = _order_key(sc_ref[:, pl.ds(pl.multiple_of(j * LANE, LANE), LANE)])
        return 0

    lax.fori_loop(0, n_blk, to_keys, 0)

    def count_ge(cand):
        def cb(j, c):
            return c + jnp.where(keys_s[j] >= cand, 1, 0)

        c = lax.fori_loop(0, n_blk, cb, jnp.zeros((nb, LANE), I32))
        return jnp.sum(c, axis=1, keepdims=True) + jnp.where(key_new >= cand, 1, 0)

    tau = jnp.maximum(_kth_largest_key(count_ge, nb, n_sel), INT_MIN + 1)

    def count_gt(j, c):
        return c + jnp.where(keys_s[j] > tau, 1, 0)

    c_gt = jnp.sum(lax.fori_loop(0, n_blk, count_gt, jnp.zeros((nb, LANE), I32)), axis=1, keepdims=True)
    quota = n_sel - (c_gt + jnp.where(key_new > tau, 1, 0))

    ri = lax.broadcasted_iota(I32, (LANE, LANE), 0)
    ci = lax.broadcasted_iota(I32, (LANE, LANE), 1)
    upper = (ri <= ci).astype(BF16)

    def excl_prefix(flag):
        return (_dot(flag.astype(BF16), upper) - flag).astype(I32)

    def rank_body(j, carry):
        base, base_eq = carry
        kj = keys_s[j]
        eq = kj == tau
        eqf = jnp.where(eq, 1.0, 0.0)
        take = (kj > tau) | (eq & (base_eq + excl_prefix(eqf) < quota))
        takef = jnp.where(take, 1.0, 0.0)
        rank_s[j] = jnp.where(take, base + excl_prefix(takef), -1)
        finite = (kj < KEY_POS_INF) & (kj > KEY_NEG_INF)
        pos = j * LANE + lane
        pos_s[j] = jnp.where(finite, pos, pos | IDX_FLAG)
        return (base + jnp.sum(takef, axis=1, keepdims=True).astype(I32),
                base_eq + jnp.sum(eqf, axis=1, keepdims=True).astype(I32))

    zero = jnp.zeros((nb, 1), I32)
    n_before_new, eq_before_new = lax.fori_loop(0, n_blk, rank_body, (zero, zero))

    slot = lax.broadcasted_iota(I32, (n_sel, LANE), 0)
    new_sel = (key_new > tau) | ((key_new == tau) & (eq_before_new < quota))
    new_fin = (key_new < KEY_POS_INF) & (key_new > KEY_NEG_INF)
    new_pos = jnp.where(new_fin, n_past, n_past | IDX_FLAG)

    def per_seq(b, _):
        def blk(j, acc):
            rk = rank_s[j, pl.ds(b, 1), :]
            ps = pos_s[j, pl.ds(b, 1), :]
            return acc + jnp.where(rk == slot, ps, 0)

        acc = lax.fori_loop(0, n_blk, blk, jnp.zeros((n_sel, LANE), I32))
        col = jnp.sum(acc, axis=1, keepdims=True)
        rb = lax.broadcasted_iota(I32, (nb, 1), 0) == b
        nbn = jnp.sum(jnp.where(rb, n_before_new, 0), axis=0, keepdims=True)
        nsel = jnp.sum(jnp.where(rb & new_sel, 1, 0), axis=0, keepdims=True)
        npos = jnp.sum(jnp.where(rb, new_pos, 0), axis=0, keepdims=True)
        col = col + jnp.where((slot[:, 0:1] == nbn) & (nsel > 0), npos, 0)
        idx_ref[b] = col
        return 0

    lax.fori_loop(0, nb, per_seq, 0)


def _attn_sample_kernel(n_past, idx_sm, pt_sm, idxc_ref, q_ref, kn_ref, vn_ref, rb_ref, ck_hbm, cv_hbm, o_ref,
                        kbuf, vbuf, sem):
    b = pl.program_id(0)
    n_sel = kbuf.shape[0]
    hd = HEAD_DIM
    page_shift = int(math.log2(PAGE_SIZE))

    def row_copies(r, row):
        return (pltpu.make_async_copy(ck_hbm.at[pl.ds(row, 1), :], kbuf.at[pl.ds(r, 1), :], sem.at[0]),
                pltpu.make_async_copy(cv_hbm.at[pl.ds(row, 1), :], vbuf.at[pl.ds(r, 1), :], sem.at[1]))

    def start(r, _):
        p = jnp.minimum(idx_sm[b, r] & (IDX_FLAG - 1), n_past - 1)
        row = pt_sm[b, p >> page_shift] * PAGE_SIZE + (p & (PAGE_SIZE - 1))
        ck, cv = row_copies(r, row)
        ck.start()
        cv.start()
        return 0

    lax.fori_loop(0, n_sel, start, 0)

    def wait(r, _):
        ck, cv = row_copies(r, 0)
        ck.wait()
        cv.wait()
        return 0

    lax.fori_loop(0, n_sel, wait, 0)

    idxc = idxc_ref[0]
    valid = idxc < IDX_FLAG
    idx = idxc & (IDX_FLAG - 1)
    is_new = idx >= n_past
    ks = jnp.where(is_new, kn_ref[0], kbuf[...])
    vs = jnp.where(is_new, vn_ref[0], vbuf[...])
    bucket = _t5_bucket(n_past - idx)
    q8 = q_ref[0].astype(BF16)
    scale = hd ** -0.5
    for n in range(N_KV_B):
        kn = ks[:, n * hd:(n + 1) * hd].astype(BF16)
        vn = vs[:, n * hd:(n + 1) * hd]
        lg4 = _dot_nt(kn, q8[n * GQA_GROUP:(n + 1) * GQA_GROUP]) * scale
        for g in range(GQA_GROUP):
            h = n * GQA_GROUP + g
            lg = jnp.where(valid, lg4[:, g:g + 1] + _bias_from_bucket(bucket, rb_ref, h), NEG_BIG)
            m = jnp.max(lg, axis=0, keepdims=True)
            p = jnp.where(valid, jnp.exp(lg - m), 0.0)
            p = p / jnp.sum(p, axis=0, keepdims=True)
            o_ref[0, :, h * hd:(h + 1) * hd] = jnp.sum(p * vn, axis=0, keepdims=True).astype(o_ref.dtype)


def _attn_sample_call(idx3, page_table, q3, k_new, v_new, rel_bias, cache_k, cache_v):
    nb, n_sel, _ = idx3.shape
    n_past = page_table.shape[1] * PAGE_SIZE
    kvw = N_KV_B * HEAD_DIM
    qbw = N_HEADS_B * HEAD_DIM
    grid_spec = pltpu.PrefetchScalarGridSpec(
        num_scalar_prefetch=2,
        grid=(nb,),
        in_specs=[pl.BlockSpec((1, n_sel, 1), lambda b, *_: (b, 0, 0)),
                  pl.BlockSpec((1, N_HEADS_B, HEAD_DIM), lambda b, *_: (b, 0, 0)),
                  pl.BlockSpec((1, 1, kvw), lambda b, *_: (b, 0, 0)),
                  pl.BlockSpec((1, 1, kvw), lambda b, *_: (b, 0, 0)),
                  pl.BlockSpec(memory_space=pltpu.SMEM),
                  pl.BlockSpec(memory_space=pl.ANY),
                  pl.BlockSpec(memory_space=pl.ANY)],
        out_specs=pl.BlockSpec((1, 1, qbw), lambda b, *_: (b, 0, 0)),
        scratch_shapes=[pltpu.VMEM((n_sel, kvw), F32), pltpu.VMEM((n_sel, kvw), F32),
                        pltpu.SemaphoreType.DMA((2,))],
    )
    return pl.pallas_call(
        functools.partial(_attn_sample_kernel, n_past),
        out_shape=jax.ShapeDtypeStruct((nb, 1, qbw), BF16),
        grid_spec=grid_spec,
        compiler_params=_cparams(("arbitrary",)),
        name="attn_sample",
    )(idx3.reshape(nb, n_sel), page_table, idx3, q3, k_new.reshape(nb, 1, kvw), v_new.reshape(nb, 1, kvw),
      rel_bias.astype(F32), cache_k.reshape(-1, kvw), cache_v.reshape(-1, kvw))


def _topk_sample_call(scores2, qi2, wi2, ki_new, n_sel):
    nb, n_past = scores2.shape
    n_blk = n_past // LANE
    return pl.pallas_call(
        functools.partial(_topk_sample_kernel, n_sel),
        out_shape=jax.ShapeDtypeStruct((nb, n_sel, 1), I32),
        scratch_shapes=[pltpu.VMEM((n_blk, nb, LANE), I32), pltpu.VMEM((n_blk, nb, LANE), I32),
                        pltpu.VMEM((n_blk, nb, LANE), I32)],
        compiler_params=pltpu.CompilerParams(vmem_limit_bytes=VMEM_LIMIT),
        name="topk_sample",
    )(scores2, qi2, wi2, ki_new)


def _prep_w_in(w):
    offs = [0]
    for s in SPLIT_SIZES:
        offs.append(offs[-1] + s)
    conv, a_raw, b_raw, z, q_b, k_b, v_b, q_i, w_i, k_i = (w[:, offs[i]:offs[i + 1]].astype(BF16) for i in range(10))
    rows = w.shape[0]
    small = jnp.concatenate([a_raw, b_raw, w_i, jnp.zeros((rows, LANE - SM_W - N_IDX_HEADS), BF16)], axis=1)
    pad = jnp.zeros((rows, NP - OFF_SM - LANE), BF16)
    return jnp.concatenate([conv, z, q_i, q_b, k_b, v_b, k_i, small, pad], axis=1)


def kernel(x_prompt, x_sample, c_prompt, c_sample, cache_k, cache_v, cache_kidx, state_ssm, state_conv, page_table,
           w_ada, b_ada, pre1_g, post1_g, pre2_g, post2_g, w_in, w_out, conv_w, a_log, dt_bias, gdn_norm_g,
           idx_knorm_g, idx_knorm_b, rel_bias, w_ff1, w_ff2):
    bsz, seq, _ = x_prompt.shape
    nb, dec_seq, _ = x_sample.shape
    assert dec_seq == 1, "the sample path handles one new token per sequence"
    depth = w_in.shape[0]
    n_past = page_table.shape[1] * PAGE_SIZE
    n_sel_s = min(TOPK_MAX, (n_past + dec_seq) // 4)
    kvw = N_KV_B * HEAD_DIM
    tm_in, tn_in = min(1024, seq), 1024
    tm_out = min(512, seq)
    tf = 1024

    hp = x_prompt.reshape(bsz * seq, D_MODEL)
    hs = x_sample.reshape(nb, D_MODEL)
    c_all = jnp.concatenate([c_sample, c_prompt], axis=0)
    new_p, new_s = [], []
    for l in range(depth):
        m = _adaln_call(c_all, w_ada[l], b_ada[l])
        w_in_bf = _prep_w_in(w_in[l])
        w_out_bf = w_out[l].astype(BF16)
        w1_bf = w_ff1[l].astype(BF16)
        w2_bf = w_ff2[l].astype(BF16)

        proj = _inproj_call(hp, pre1_g[l], _ModSpec(m, False, nb, seq, tm_in), w_in_bf, tm_in, tn_in)
        k_p, v_p, ki_p, k_bf, v_bf, ki_bf = _kvprep_call(proj, idx_knorm_g[l], idx_knorm_b[l], tm_in)
        proj3 = proj.reshape(bsz, seq, NP)
        o_a, ssm_p = _gdn_prompt_call(proj3, conv_w[l], a_log[l], dt_bias[l], gdn_norm_g[l])
        o_b = _attn_prompt_call(proj3, ki_bf.reshape(bsz, seq, D_IDX), k_bf.reshape(bsz, seq, kvw),
                                v_bf.reshape(bsz, seq, kvw), rel_bias)
        mod_p = _ModSpec(m, False, nb, seq, tm_out)
        x1, h2 = _outproj_call(o_a.reshape(bsz * seq, VAL_DIM_A), o_b.reshape(bsz * seq, N_HEADS_B * HEAD_DIM), hp,
                               w_out_bf, post1_g[l], pre2_g[l], mod_p, tm_out)
        hp = _ffn_call(h2, x1, w1_bf, w2_bf, post2_g[l], mod_p, tm_out, tf)
        conv_p = proj3[:, seq - (CONV_W - 1):, OFF_CONV:OFF_CONV + CONV_DIM]
        new_p.append((k_p.reshape(bsz, seq, N_KV_B, HEAD_DIM), v_p.reshape(bsz, seq, N_KV_B, HEAD_DIM),
                      ki_p.reshape(bsz, seq, D_IDX), ssm_p, conv_p))

        mod_s = _ModSpec(m, True, 0, 1, nb)
        proj_s = _inproj_call(hs, pre1_g[l], mod_s, w_in_bf, nb, tn_in)
        k_s, v_s, ki_s, _, _, _ = _kvprep_call(proj_s, idx_knorm_g[l], idx_knorm_b[l], nb)
        o_a_s, ssm_s, conv_s = _gdn_sample_call(proj_s.reshape(nb, 1, NP), state_conv[l], state_ssm[l], conv_w[l],
                                                a_log[l], dt_bias[l], gdn_norm_g[l])
        qi2 = proj_s[:, OFF_QI:OFF_QI + N_IDX_HEADS * D_IDX]
        wi2 = proj_s[:, OFF_SM + SM_W:OFF_SM + SM_W + N_IDX_HEADS]
        scores = _idx_sample_call(page_table, qi2.reshape(nb, N_IDX_HEADS, D_IDX), wi2.reshape(nb, N_IDX_HEADS, 1),
                                  cache_kidx[l])
        idx3 = _topk_sample_call(scores.reshape(nb, n_past), qi2, wi2, ki_s, n_sel_s)
        q3 = proj_s[:, OFF_QB:OFF_QB + N_HEADS_B * HEAD_DIM].reshape(nb, N_HEADS_B, HEAD_DIM)
        o_b_s = _attn_sample_call(idx3, page_table, q3, k_s, v_s, rel_bias, cache_k[l], cache_v[l])
        x1s, h2s = _outproj_call(o_a_s.reshape(nb, VAL_DIM_A), o_b_s.reshape(nb, N_HEADS_B * HEAD_DIM), hs,
                                 w_out_bf, post1_g[l], pre2_g[l], mod_s, nb)
        hs = _ffn_call(h2s, x1s, w1_bf, w2_bf, post2_g[l], mod_s, nb, tf)
        new_s.append((k_s.reshape(nb, 1, N_KV_B, HEAD_DIM), v_s.reshape(nb, 1, N_KV_B, HEAD_DIM),
                      ki_s.reshape(nb, 1, D_IDX), ssm_s, conv_s))

    p_k, p_v, p_kidx, p_ssm, p_conv = [jnp.stack([st[i] for st in new_p]) for i in range(5)]
    s_k, s_v, s_kidx, s_ssm, s_conv = [jnp.stack([st[i] for st in new_s]) for i in range(5)]
    return (hp.reshape(bsz, seq, D_MODEL), hs.reshape(nb, 1, D_MODEL), p_k, p_v, p_kidx, p_ssm, p_conv,
            s_k, s_v, s_kidx, s_ssm, s_conv)
```

```python
import functools
import math

import jax
import jax.numpy as jnp
from jax import lax
from jax.experimental import pallas as pl
from jax.experimental.pallas import tpu as pltpu

F32 = jnp.float32
BF16 = jnp.bfloat16
I32 = jnp.int32

D_MODEL = 2048
PAGE_SIZE = 128
HEAD_DIM = 128
N_HEADS_A = 8
KEY_DIM_A = N_HEADS_A * HEAD_DIM
VAL_DIM_A = N_HEADS_A * HEAD_DIM
CONV_DIM = 2 * KEY_DIM_A + VAL_DIM_A
CONV_W = 4
N_HEADS_B = 8
N_KV_B = 2
GQA_GROUP = N_HEADS_B // N_KV_B
N_IDX_HEADS = 16
D_IDX = 128
TOPK_MAX = 256
N_BUCKETS = 32
MAX_DISTANCE = 128
D_FF = 4 * D_MODEL
EPS = 1e-6

SPLIT_SIZES = (CONV_DIM, N_HEADS_A, N_HEADS_A, VAL_DIM_A, N_HEADS_B * HEAD_DIM, N_KV_B * HEAD_DIM,
               N_KV_B * HEAD_DIM, N_IDX_HEADS * D_IDX, N_IDX_HEADS, D_IDX)

OFF_CONV = 0
OFF_Z = OFF_CONV + CONV_DIM
OFF_QI = OFF_Z + VAL_DIM_A
OFF_QB = OFF_QI + N_IDX_HEADS * D_IDX
OFF_KB = OFF_QB + N_HEADS_B * HEAD_DIM
OFF_VB = OFF_KB + N_KV_B * HEAD_DIM
OFF_KI = OFF_VB + N_KV_B * HEAD_DIM
OFF_SM = OFF_KI + D_IDX
NP = 8192
SM_A, SM_B, SM_W = 0, N_HEADS_A, 2 * N_HEADS_A

LANE = 128
VMEM_LIMIT = 56 * 1024 * 1024
INT_MIN = -(2 ** 31)
NEG_BIG = -1e30

GDN_CHUNK = 64
GDN_GROUP = 256
ATT_TQ = 256
ATT_TK = 256


def _cparams(sem):
    return pltpu.CompilerParams(dimension_semantics=sem, vmem_limit_bytes=VMEM_LIMIT)


def _sigmoid(x):
    return 1.0 / (1.0 + jnp.exp(-x))


def _softplus(x):
    return jnp.maximum(x, 0.0) + jnp.log(1.0 + jnp.exp(-jnp.abs(x)))


def _dot(a, b):
    return jnp.dot(a, b, preferred_element_type=F32)


def _dot_nt(a, b):
    return lax.dot_general(a, b, (((1,), (1,)), ((), ())), preferred_element_type=F32)


def _dot_tn(a, b):
    return lax.dot_general(a, b, (((0,), (0,)), ((), ())), preferred_element_type=F32)


def _mod(ref):
    v = ref[...]
    return v[0] if v.ndim == 3 else v


def _adaln_kernel(c_ref, w_ref, b_ref, o_ref):
    c = c_ref[...]
    s = (c * _sigmoid(c)).astype(BF16)
    o_ref[...] = _dot(s, w_ref[...].astype(BF16)) + b_ref[...]


def _adaln_call(c_all, w_ada, b_ada):
    n_rows = c_all.shape[0]
    n_out = w_ada.shape[1]
    tn = 1024
    return pl.pallas_call(
        _adaln_kernel,
        out_shape=jax.ShapeDtypeStruct((n_rows, n_out), F32),
        grid=(n_out // tn,),
        in_specs=[pl.BlockSpec((n_rows, D_MODEL), lambda j: (0, 0)),
                  pl.BlockSpec((D_MODEL, tn), lambda j: (0, j)),
                  pl.BlockSpec((1, tn), lambda j: (0, j))],
        out_specs=pl.BlockSpec((n_rows, tn), lambda j: (0, j)),
        compiler_params=_cparams(("arbitrary",)),
        name="adaln",
    )(c_all, w_ada, b_ada.reshape(1, n_out))


class _ModSpec:
    def __init__(self, m2, per_row, row_off, rows_per_batch, tm):
        self.per_row = per_row
        self.arr = m2 if per_row else m2.reshape(m2.shape[0], 1, m2.shape[1])
        self.row_off = row_off
        self.blocks_per_batch = None if per_row else rows_per_batch // tm
        self.tm = tm

    def spec(self, k, grid_rank):
        if self.per_row:
            if grid_rank == 1:
                return pl.BlockSpec((self.tm, D_MODEL), lambda i: (0, k))
            return pl.BlockSpec((self.tm, D_MODEL), lambda i, j: (0, k))
        bpb, off = self.blocks_per_batch, self.row_off
        if grid_rank == 1:
            return pl.BlockSpec((1, 1, D_MODEL), lambda i: (off + i // bpb, 0, k))
        return pl.BlockSpec((1, 1, D_MODEL), lambda i, j: (off + i // bpb, 0, k))


def _inproj_kernel(x_ref, g_ref, sh_ref, sc_ref, w_ref, o_ref, h_scr):
    @pl.when(pl.program_id(1) == 0)
    def _():
        x = x_ref[...]
        y = x * lax.rsqrt(jnp.mean(x * x, axis=-1, keepdims=True) + EPS) * g_ref[...]
        h_scr[...] = (y * (1.0 + _mod(sc_ref)) + _mod(sh_ref)).astype(BF16)

    o_ref[...] = _dot(h_scr[...], w_ref[...])


def _inproj_call(x2d, g, mod, w_bf, tm, tn):
    m_rows = x2d.shape[0]
    n_cols = w_bf.shape[1]
    return pl.pallas_call(
        _inproj_kernel,
        out_shape=jax.ShapeDtypeStruct((m_rows, n_cols), F32),
        grid=(m_rows // tm, n_cols // tn),
        in_specs=[pl.BlockSpec((tm, D_MODEL), lambda i, j: (i, 0)),
                  pl.BlockSpec((1, D_MODEL), lambda i, j: (0, 0)),
                  mod.spec(0, 2), mod.spec(1, 2),
                  pl.BlockSpec((D_MODEL, tn), lambda i, j: (0, j))],
        out_specs=pl.BlockSpec((tm, tn), lambda i, j: (i, j)),
        scratch_shapes=[pltpu.VMEM((tm, D_MODEL), BF16)],
        compiler_params=_cparams(("arbitrary", "arbitrary")),
        name="inproj",
    )(x2d, g.reshape(1, D_MODEL), mod.arr, mod.arr, w_bf)


def _kvprep_kernel(kb_ref, vb_ref, ki_ref, lg_ref, lb_ref, k_ref, v_ref, kidx_ref, *bf_refs):
    kb = kb_ref[...]
    vb = vb_ref[...]
    k_ref[...] = kb
    v_ref[...] = vb
    x = ki_ref[...]
    mu = jnp.mean(x, axis=-1, keepdims=True)
    xc = x - mu
    var = jnp.mean(xc * xc, axis=-1, keepdims=True)
    ki = xc * lax.rsqrt(var + EPS) * lg_ref[...] + lb_ref[...]
    kidx_ref[...] = ki
    if bf_refs:
        kbf_ref, va_ref, kibf_ref = bf_refs
        kbf_ref[...] = kb.astype(BF16)
        kibf_ref[...] = ki.astype(BF16)
        hd = HEAD_DIM
        ones_col = jnp.where(lax.broadcasted_iota(I32, (vb.shape[0], hd), 1) == 0, 1.0, 0.0)
        va_ref[...] = jnp.concatenate([piece for n in range(N_KV_B) for piece in (vb[:, n * hd:(n + 1) * hd], ones_col)],
                                      axis=1).astype(BF16)


def _kvprep_call(proj, ln_g, ln_b, tm, with_bf16=False):
    m_rows = proj.shape[0]
    kvw = N_KV_B * HEAD_DIM
    row = lambda i: (i, 0)
    out_shape = [jax.ShapeDtypeStruct((m_rows, kvw), F32), jax.ShapeDtypeStruct((m_rows, kvw), F32),
                 jax.ShapeDtypeStruct((m_rows, D_IDX), F32)]
    out_specs = [pl.BlockSpec((tm, kvw), row), pl.BlockSpec((tm, kvw), row), pl.BlockSpec((tm, D_IDX), row)]
    if with_bf16:
        out_shape += [jax.ShapeDtypeStruct((m_rows, kvw), BF16), jax.ShapeDtypeStruct((m_rows, 2 * kvw), BF16),
                      jax.ShapeDtypeStruct((m_rows, D_IDX), BF16)]
        out_specs += [pl.BlockSpec((tm, kvw), row), pl.BlockSpec((tm, 2 * kvw), row), pl.BlockSpec((tm, D_IDX), row)]
    return pl.pallas_call(
        _kvprep_kernel,
        out_shape=tuple(out_shape),
        grid=(m_rows // tm,),
        in_specs=[pl.BlockSpec((tm, kvw), lambda i: (i, OFF_KB // kvw)),
                  pl.BlockSpec((tm, kvw), lambda i: (i, OFF_VB // kvw)),
                  pl.BlockSpec((tm, D_IDX), lambda i: (i, OFF_KI // D_IDX)),
                  pl.BlockSpec((1, D_IDX), lambda i: (0, 0)),
                  pl.BlockSpec((1, D_IDX), lambda i: (0, 0))],
        out_specs=tuple(out_specs),
        compiler_params=_cparams(("arbitrary",)),
        name="kvprep",
    )(proj, proj, proj, ln_g.reshape(1, D_IDX), ln_b.reshape(1, D_IDX))


def _conv_silu(x, cw, row):
    y = x * cw[CONV_W - 1:CONV_W]
    for s in range(1, CONV_W):
        xs = jnp.where(row >= s, pltpu.roll(x, s, 0), 0.0)
        y = y + xs * cw[CONV_W - 1 - s:CONV_W - s]
    return y * _sigmoid(y)


def _l2norm(x):
    return x * lax.rsqrt(jnp.sum(x * x, axis=-1, keepdims=True) + EPS)


def _gdn_prompt_kernel(q_ref, k_ref, v_ref, z_ref, sm_ref, cwq_ref, cwk_ref, cwv_ref, alog_ref, dtb_ref, ng_ref,
                       o_ref, s_out_ref,
                       qn_s, kn_s, vn_s, gc_s, bt_s, l_s, a_s, x_s, u_s, w_s, mn_s, sts_s):
    seq = q_ref.shape[1]
    hd = HEAD_DIM
    grp = GDN_GROUP
    chunk = GDN_CHUNK
    n_grp = seq // grp
    n_chunk = seq // chunk
    cpg = grp // chunk
    h = pl.program_id(1)

    row = lax.broadcasted_iota(I32, (seq, hd), 0)
    lane = lax.broadcasted_iota(I32, (seq, hd), 1)

    qn_s[...] = _l2norm(_conv_silu(q_ref[0], cwq_ref[...], row)) * (hd ** -0.5)
    kn_s[...] = _l2norm(_conv_silu(k_ref[0], cwk_ref[...], row))
    vn_s[...] = _conv_silu(v_ref[0], cwv_ref[...], row)

    sm = sm_ref[0]
    g_all = -jnp.exp(alog_ref[...]) * _softplus(sm + dtb_ref[...])
    b_all = _sigmoid(sm)
    g_col = jnp.sum(jnp.where(lane == h + SM_A, g_all, 0.0), axis=1, keepdims=True)
    b_col = jnp.sum(jnp.where(lane == h + SM_B, b_all, 0.0), axis=1, keepdims=True)
    gc = jnp.broadcast_to(g_col, (seq, hd))
    pos = row & (chunk - 1)
    s = 1
    while s < chunk:
        gc = gc + jnp.where(pos >= s, pltpu.roll(gc, s, 0), 0.0)
        s *= 2
    gc_s[...] = gc
    bt_s[...] = jnp.broadcast_to(b_col, (seq, hd))

    ii = lax.broadcasted_iota(I32, (grp, grp), 0)
    jj = lax.broadcasted_iota(I32, (grp, grp), 1)
    xr = ii ^ jj
    same = (xr >> int(math.log2(chunk))) == 0
    causal = same & (ii >= jj)
    strict = same & (ii > jj)
    eye = (ii == jj).astype(F32)

    def build(gi, _):
        r0 = pl.multiple_of(gi * grp, grp)
        kk = kn_s[pl.ds(r0, grp), :]
        qq = qn_s[pl.ds(r0, grp), :]
        bt = bt_s[pl.ds(r0, grp), :]
        gcg = gc_s[pl.ds(r0, grp), :]
        kb = (kk * bt).astype(BF16)
        kkb = kk.astype(BF16)
        kkt = _dot_nt(kb, kkb)
        qkt = _dot_nt(qq.astype(BF16), kkb)
        colb = jnp.concatenate([gcg, gcg], axis=1)
        diff = colb - colb.T
        decay = jnp.where(causal, jnp.exp(jnp.where(causal, diff, 0.0)), 0.0)
        lmat = jnp.where(strict, kkt * decay, 0.0)
        l_s[gi] = lmat
        a_s[pl.ds(r0, grp), :] = jnp.where(causal, qkt * decay, 0.0)
        x_s[gi] = eye - jnp.where(xr == 1, lmat, 0.0)
        return 0

    lax.fori_loop(0, n_grp, build, 0)

    for lvl in range(1, int(math.log2(chunk))):
        def level(gi, _, lvl=lvl):
            xm = x_s[gi]
            cl = jnp.where((xr >> lvl) == 1, l_s[gi], 0.0).astype(BF16)
            xb = xm.astype(BF16)
            x_s[gi] = xm - _dot(_dot(xb, cl).astype(BF16), xb)
            return 0

        lax.fori_loop(0, n_grp, level, 0, unroll=2)

    def apply(gi, _):
        r0 = pl.multiple_of(gi * grp, grp)
        kk = kn_s[pl.ds(r0, grp), :]
        vv = vn_s[pl.ds(r0, grp), :]
        bt = bt_s[pl.ds(r0, grp), :]
        gcg = gc_s[pl.ds(r0, grp), :]
        rhs = jnp.concatenate([vv * bt, kk * bt * jnp.exp(gcg)], axis=1).astype(BF16)
        uw = _dot(x_s[gi].astype(BF16), rhs)
        u_s[pl.ds(r0, grp), :] = uw[:, :hd]
        w_s[pl.ds(r0, grp), :] = uw[:, hd:]
        for cc in range(cpg):
            lo = cc * chunk
            g_last = gcg[lo + chunk - 1:lo + chunk, :]
            k_dec = (kk[lo:lo + chunk] * jnp.exp(g_last - gcg[lo:lo + chunk])).astype(BF16)
            wu = jnp.concatenate([uw[lo:lo + chunk, hd:], uw[lo:lo + chunk, :hd]], axis=1).astype(BF16)
            mn_s[gi * cpg + cc] = _dot_tn(k_dec, wu)
        return 0

    lax.fori_loop(0, n_grp, apply, 0)

    def scan(c, st):
        sts_s[c] = st
        g_last = gc_s[pl.ds(c * chunk + chunk - 1, 1), :]
        mn = mn_s[c]
        return st * jnp.exp(g_last) - _dot(mn[:, :hd].astype(BF16), st.astype(BF16)) + mn[:, hd:]

    s_out_ref[0, 0] = lax.fori_loop(0, n_chunk, scan, jnp.zeros((hd, hd), F32))

    ng = ng_ref[...]

    def emit(gi, _):
        r0 = pl.multiple_of(gi * grp, grp)
        gcg = gc_s[pl.ds(r0, grp), :]
        qg = qn_s[pl.ds(r0, grp), :] * jnp.exp(gcg)
        wg = w_s[pl.ds(r0, grp), :]
        ws, qs = [], []
        for cc in range(cpg):
            lo = cc * chunk
            stb = sts_s[gi * cpg + cc].astype(BF16)
            both = _dot(jnp.concatenate([wg[lo:lo + chunk], qg[lo:lo + chunk]], axis=0).astype(BF16), stb)
            ws.append(both[:chunk])
            qs.append(both[chunk:])
        v_new = u_s[pl.ds(r0, grp), :] - jnp.concatenate(ws, axis=0)
        o = jnp.concatenate(qs, axis=0) + _dot(a_s[pl.ds(r0, grp), :].astype(BF16), v_new.astype(BF16))
        zc = z_ref[0, pl.ds(r0, grp), :]
        on = o * lax.rsqrt(jnp.mean(o * o, axis=-1, keepdims=True) + EPS) * ng
        o_ref[0, pl.ds(r0, grp), :] = (on * (zc * _sigmoid(zc))).astype(o_ref.dtype)
        return 0

    lax.fori_loop(0, n_grp, emit, 0)


def _lane_vec(v, off):
    return jnp.zeros((1, LANE), F32).at[0, off:off + v.shape[0]].set(v.astype(F32))


def _gdn_prompt_call(proj3, conv_w, a_log, dt_bias, norm_g):
    bsz, seq, _ = proj3.shape
    hd = HEAD_DIM
    nh = N_HEADS_A
    n_grp = seq // GDN_GROUP
    col = lambda base: (lambda b, h: (b, 0, base // hd + h))
    cw = lambda base: (lambda b, h: (0, base // hd + h))
    vec = pl.BlockSpec((1, LANE), lambda b, h: (0, 0))
    return pl.pallas_call(
        _gdn_prompt_kernel,
        out_shape=(jax.ShapeDtypeStruct((bsz, seq, VAL_DIM_A), BF16),
                   jax.ShapeDtypeStruct((bsz, nh, hd, hd), F32)),
        grid=(bsz, nh),
        in_specs=[pl.BlockSpec((1, seq, hd), col(OFF_CONV)),
                  pl.BlockSpec((1, seq, hd), col(OFF_CONV + KEY_DIM_A)),
                  pl.BlockSpec((1, seq, hd), col(OFF_CONV + 2 * KEY_DIM_A)),
                  pl.BlockSpec((1, seq, hd), col(OFF_Z)),
                  pl.BlockSpec((1, seq, LANE), lambda b, h: (b, 0, OFF_SM // LANE)),
                  pl.BlockSpec((CONV_W, hd), cw(0)),
                  pl.BlockSpec((CONV_W, hd), cw(KEY_DIM_A)),
                  pl.BlockSpec((CONV_W, hd), cw(2 * KEY_DIM_A)),
                  vec, vec, vec],
        out_specs=(pl.BlockSpec((1, seq, hd), lambda b, h: (b, 0, h)),
                   pl.BlockSpec((1, 1, hd, hd), lambda b, h: (b, h, 0, 0))),
        scratch_shapes=[pltpu.VMEM((seq, hd), F32), pltpu.VMEM((seq, hd), F32), pltpu.VMEM((seq, hd), F32),
                        pltpu.VMEM((seq, hd), F32), pltpu.VMEM((seq, hd), F32),
                        pltpu.VMEM((n_grp, GDN_GROUP, GDN_GROUP), F32),
                        pltpu.VMEM((seq, GDN_GROUP), F32),
                        pltpu.VMEM((n_grp, GDN_GROUP, GDN_GROUP), F32),
                        pltpu.VMEM((seq, hd), F32), pltpu.VMEM((seq, hd), F32),
                        pltpu.VMEM((seq // GDN_CHUNK, hd, 2 * hd), F32), pltpu.VMEM((seq // GDN_CHUNK, hd, hd), F32)],
        compiler_params=_cparams(("arbitrary", "arbitrary")),
        name="gdn_prompt",
    )(proj3, proj3, proj3, proj3, proj3, conv_w, conv_w, conv_w,
      _lane_vec(a_log, SM_A), _lane_vec(dt_bias, SM_A), norm_g.reshape(1, hd).astype(F32))


def _t5_bucket(dist):
    n = jnp.maximum(dist, 0)
    max_exact = N_BUCKETS // 2
    nf = jnp.maximum(n, 1).astype(F32)
    large = max_exact + (jnp.log(nf / max_exact) / math.log(MAX_DISTANCE / max_exact)
                         * (N_BUCKETS - max_exact)).astype(I32)
    large = jnp.minimum(large, N_BUCKETS - 1)
    return jnp.where(n < max_exact, n, large)


def _bias_from_bucket(bucket, rb_ref, head):
    out = jnp.zeros(bucket.shape, F32)
    for b in range(N_BUCKETS):
        out = jnp.where(bucket == b, rb_ref[b, head], out)
    return out


def _order_key(score):
    bits = pltpu.bitcast(score + 0.0, I32)
    return bits ^ ((bits >> 31) & 0x7FFFFFFF)


KEY_POS_INF = 0x7F800000
KEY_NEG_INF = INT_MIN + 0x7FFFFF


def _kth_largest_key(count_ge, shape, n_sel):
    def body(i, tau):
        bit = 31 - i
        cand = jnp.where(i == 0, tau ^ INT_MIN, tau | (1 << bit))
        return jnp.where(count_ge(cand) >= n_sel, cand, tau)

    return lax.fori_loop(0, 32, body, jnp.full(shape, INT_MIN, I32))


def _attn_prompt_kernel(n_sel, qi_ref, sm_ref, qb_ref, ki_ref, k_ref, va_ref, rb_ref, o_ref,
                        qi_s, qb_s, wt_s, keys_s, tau_s, bias_s, m_s, acc_s):
    tq, tk = ATT_TQ, ATT_TK
    hd = HEAD_DIM
    bi = pl.program_id(0)
    qblk = pl.program_id(1)
    n_kb = (qblk * tq) // tk + 1
    krow = lax.broadcasted_iota(I32, (tk, tq), 0)
    qcol = lax.broadcasted_iota(I32, (tk, tq), 1)

    @pl.when((bi == 0) & (qblk == 0))
    def _():
        for off in range(bias_s.shape[1]):
            bucket = _t5_bucket(off * tk + krow - qcol)
            for h in range(N_HEADS_B):
                bias_s[h, off] = _bias_from_bucket(bucket, rb_ref, h)

    qi_s[...] = qi_ref[0].astype(BF16)
    qb_s[...] = qb_ref[0].astype(BF16)
    wt_s[...] = sm_ref[0].T * (N_IDX_HEADS ** -0.5)
    t_pos = qblk * tq + qcol

    def idx_body(j, _):
        k0 = pl.multiple_of(j * tk, tk)
        kt = ki_ref[0, pl.ds(k0, tk), :]
        acc = jnp.zeros((tk, tq), F32)
        for h in range(N_IDX_HEADS):
            s = _dot_nt(kt, qi_s[:, h * D_IDX:(h + 1) * D_IDX])
            acc = acc + jnp.maximum(s, 0.0) * wt_s[SM_W + h:SM_W + h + 1, :]
        acc = acc * (D_IDX ** -0.5)
        keys_s[j] = jnp.where(k0 + krow <= t_pos, _order_key(acc), INT_MIN)
        return 0

    lax.fori_loop(0, n_kb, idx_body, 0)

    def count_ge(cand):
        def cb(j, c):
            hit = jnp.where(keys_s[j] >= cand, 1, 0)
            return c + jnp.sum(hit.reshape(tk // 8, 8, tq), axis=0)

        c = lax.fori_loop(0, n_kb, cb, jnp.zeros((8, tq), I32))
        return jnp.sum(c, axis=0, keepdims=True)

    tau = jnp.maximum(_kth_largest_key(count_ge, (1, tq), n_sel), KEY_NEG_INF + 1)

    scale = hd ** -0.5
    lane_reps = tk // LANE
    tau_rep = jnp.broadcast_to(tau, (LANE, tq)).T
    tau_s[...] = jnp.concatenate([tau_rep] * lane_reps, axis=1)

    def block_inputs(j):
        kq = keys_s[j].T
        sel = jnp.where(kq < KEY_POS_INF, kq, INT_MIN) >= tau_s[...]
        return k_ref[0, pl.ds(pl.multiple_of(j * tk, tk), tk), :], sel

    def masked_logits(j, h, kt, sel):
        n = h // GQA_GROUP
        boff = jnp.minimum(qblk - j, bias_s.shape[1] - 1)
        lg = _dot_nt(qb_s[:, h * hd:(h + 1) * hd], kt[:, n * hd:(n + 1) * hd]) * scale + bias_s[h, boff]
        return jnp.where(sel, lg, NEG_BIG)

    m_s[...] = jnp.full(m_s.shape, NEG_BIG, F32)

    def max_body(j, _):
        kt, sel = block_inputs(j)
        for h in range(N_HEADS_B):
            lg = masked_logits(j, h, kt, sel)
            part = lg[:, :LANE]
            for r in range(1, lane_reps):
                part = jnp.maximum(part, lg[:, r * LANE:(r + 1) * LANE])
            m_s[h] = jnp.maximum(m_s[h], part)
        return 0

    lax.fori_loop(0, n_kb, max_body, 0)
    for h in range(N_HEADS_B):
        m_s[h] = jnp.broadcast_to(jnp.max(m_s[h], axis=1, keepdims=True), (tq, LANE))
    acc_s[...] = jnp.zeros(acc_s.shape, F32)

    def sum_body(j, _):
        kt, sel = block_inputs(j)
        va = va_ref[0, pl.ds(pl.multiple_of(j * tk, tk), tk), :]
        for h in range(N_HEADS_B):
            n = h // GQA_GROUP
            mrow = jnp.concatenate([m_s[h]] * lane_reps, axis=1)
            p = jnp.exp(masked_logits(j, h, kt, sel) - mrow)
            acc_s[h] += _dot(p.astype(BF16), va[:, n * 2 * hd:(n + 1) * 2 * hd])
        return 0

    lax.fori_loop(0, n_kb, sum_body, 0)
    for h in range(N_HEADS_B):
        a = acc_s[h]
        o_ref[0, :, h * hd:(h + 1) * hd] = (a[:, :hd] / a[:, hd:hd + 1]).astype(o_ref.dtype)


def _attn_prompt_call(proj3, ki_bf, k_bf, va_bf, rel_bias):
    bsz, seq, _ = proj3.shape
    tq, tk = ATT_TQ, ATT_TK
    assert tq == tk, "the bias-tile offsets assume square blocks"
    n_sel = min(TOPK_MAX, seq // 4)
    qiw = N_IDX_HEADS * D_IDX
    qbw = N_HEADS_B * HEAD_DIM
    kvw = N_KV_B * HEAD_DIM
    n_off = -(-(MAX_DISTANCE + tk - 1) // tk) + 1
    return pl.pallas_call(
        functools.partial(_attn_prompt_kernel, n_sel),
        out_shape=jax.ShapeDtypeStruct((bsz, seq, qbw), BF16),
        grid=(bsz, seq // tq),
        in_specs=[pl.BlockSpec((1, tq, qiw), lambda b, i: (b, i, OFF_QI // qiw)),
                  pl.BlockSpec((1, tq, LANE), lambda b, i: (b, i, OFF_SM // LANE)),
                  pl.BlockSpec((1, tq, qbw), lambda b, i: (b, i, OFF_QB // qbw)),
                  pl.BlockSpec((1, seq, D_IDX), lambda b, i: (b, 0, 0)),
                  pl.BlockSpec((1, seq, kvw), lambda b, i: (b, 0, 0)),
                  pl.BlockSpec((1, seq, 2 * kvw), lambda b, i: (b, 0, 0)),
                  pl.BlockSpec(memory_space=pltpu.SMEM)],
        out_specs=pl.BlockSpec((1, tq, qbw), lambda b, i: (b, i, 0)),
        scratch_shapes=[pltpu.VMEM((tq, qiw), BF16), pltpu.VMEM((tq, qbw), BF16),
                        pltpu.VMEM((LANE, tq), F32),
                        pltpu.VMEM((seq // tk, tk, tq), I32),
                        pltpu.VMEM((tq, tk), I32),
                        pltpu.VMEM((N_HEADS_B, n_off, tq, tk), F32),
                        pltpu.VMEM((N_HEADS_B, tq, LANE), F32),
                        pltpu.VMEM((N_HEADS_B, tq, 2 * HEAD_DIM), F32)],
        compiler_params=_cparams(("arbitrary", "arbitrary")),
        name="attn_prompt",
    )(proj3, proj3, proj3, ki_bf, k_bf, va_bf, rel_bias.astype(F32))


def _rms(x, g):
    return x * lax.rsqrt(jnp.mean(x * x, axis=-1, keepdims=True) + EPS) * g


def _outproj_kernel(oa_ref, ob_ref, x_ref, wa_ref, wb_ref, pg_ref, g1_ref, p2_ref, sh2_ref, sc2_ref, x1_ref, h2_ref):
    mix = _dot(oa_ref[...], wa_ref[...]) + _dot(ob_ref[...], wb_ref[...])
    x1 = x_ref[...] + _mod(g1_ref) * _rms(mix, pg_ref[...])
    x1_ref[...] = x1
    h2_ref[...] = (_rms(x1, p2_ref[...]) * (1.0 + _mod(sc2_ref)) + _mod(sh2_ref)).astype(BF16)


def _outproj_call(oa, ob, x2d, w_out_bf, post1_g, pre2_g, mod, tm):
    m_rows = x2d.shape[0]
    half = w_out_bf.shape[0] // 2
    row = lambda i: (i, 0)
    vec = pl.BlockSpec((1, D_MODEL), lambda i: (0, 0))
    return pl.pallas_call(
        _outproj_kernel,
        out_shape=(jax.ShapeDtypeStruct((m_rows, D_MODEL), F32), jax.ShapeDtypeStruct((m_rows, D_MODEL), BF16)),
        grid=(m_rows // tm,),
        in_specs=[pl.BlockSpec((tm, half), row), pl.BlockSpec((tm, half), row), pl.BlockSpec((tm, D_MODEL), row),
                  pl.BlockSpec((half, D_MODEL), lambda i: (0, 0)), pl.BlockSpec((half, D_MODEL), lambda i: (1, 0)),
                  vec, mod.spec(2, 1), vec, mod.spec(3, 1), mod.spec(4, 1)],
        out_specs=(pl.BlockSpec((tm, D_MODEL), row), pl.BlockSpec((tm, D_MODEL), row)),
        compiler_params=_cparams(("arbitrary",)),
        name="outproj",
    )(oa, ob, x2d, w_out_bf, w_out_bf, post1_g.reshape(1, D_MODEL), mod.arr, pre2_g.reshape(1, D_MODEL),
      mod.arr, mod.arr)


def _ffn_kernel(h_ref, x1_ref, w1_ref, w2_ref, pg_ref, g2_ref, o_ref, acc_s):
    kk = pl.program_id(1)

    @pl.when(kk == 0)
    def _():
        acc_s[...] = jnp.zeros_like(acc_s)

    a = jnp.maximum(_dot(h_ref[...], w1_ref[...]), 0.0)
    acc_s[...] += _dot((a * a).astype(BF16), w2_ref[...])

    @pl.when(kk == pl.num_programs(1) - 1)
    def _():
        o_ref[...] = x1_ref[...] + _mod(g2_ref) * _rms(acc_s[...], pg_ref[...])


def _ffn_call(h2, x1, w1_bf, w2_bf, post2_g, mod, tm, tf):
    m_rows = x1.shape[0]
    return pl.pallas_call(
        _ffn_kernel,
        out_shape=jax.ShapeDtypeStruct((m_rows, D_MODEL), F32),
        grid=(m_rows // tm, D_FF // tf),
        in_specs=[pl.BlockSpec((tm, D_MODEL), lambda i, k: (i, 0)),
                  pl.BlockSpec((tm, D_MODEL), lambda i, k: (i, 0)),
                  pl.BlockSpec((D_MODEL, tf), lambda i, k: (0, k)),
                  pl.BlockSpec((tf, D_MODEL), lambda i, k: (k, 0)),
                  pl.BlockSpec((1, D_MODEL), lambda i, k: (0, 0)),
                  mod.spec(5, 2)],
        out_specs=pl.BlockSpec((tm, D_MODEL), lambda i, k: (i, 0)),
        scratch_shapes=[pltpu.VMEM((tm, D_MODEL), F32)],
        compiler_params=_cparams(("arbitrary", "arbitrary")),
        name="ffn",
    )(h2, x1, w1_bf, w2_bf, post2_g.reshape(1, D_MODEL), mod.arr)


def _gdn_sample_kernel(cin_ref, z_ref, sm_ref, sconv_ref, ssm_ref, cw_ref, alog_ref, dtb_ref, ng_ref,
                       o_ref, ssm_out_ref, conv_out_ref):
    hd = HEAD_DIM
    nh = N_HEADS_A
    prev = sconv_ref[0]
    xin = cin_ref[0]
    cw = cw_ref[...]
    y = xin * cw[CONV_W - 1:CONV_W]
    for s in range(CONV_W - 1):
        y = y + prev[s:s + 1] * cw[s:s + 1]
    y = y * _sigmoid(y)
    conv_out_ref[0, 0:CONV_W - 2, :] = prev[1:CONV_W - 1]
    conv_out_ref[0, CONV_W - 2:CONV_W - 1, :] = xin

    def heads(base):
        return jnp.concatenate([y[:, base + h * hd:base + (h + 1) * hd] for h in range(nh)], axis=0)

    q8 = _l2norm(heads(0)) * (hd ** -0.5)
    k8 = _l2norm(heads(KEY_DIM_A))
    v8 = heads(2 * KEY_DIM_A)
    sm = sm_ref[0]
    g_all = -jnp.exp(alog_ref[...]) * _softplus(sm + dtb_ref[...])
    b_all = _sigmoid(sm)
    kt = jnp.concatenate([k8, jnp.zeros((LANE - nh, hd), F32)], axis=0).T
    k8b = k8.astype(BF16)
    q8b = q8.astype(BF16)
    z = z_ref[0]
    ng = ng_ref[...]
    for h in range(nh):
        s1 = ssm_ref[0, h] * jnp.exp(g_all[:, SM_A + h:SM_A + h + 1])
        kv = _dot(k8b, s1.astype(BF16))[h:h + 1]
        delta = (v8[h:h + 1] - kv) * b_all[:, SM_B + h:SM_B + h + 1]
        s2 = s1 + kt[:, h:h + 1] * delta
        ssm_out_ref[0, h] = s2
        o = _dot(q8b, s2.astype(BF16))[h:h + 1]
        zh = z[:, h * hd:(h + 1) * hd]
        on = o * lax.rsqrt(jnp.mean(o * o, axis=-1, keepdims=True) + EPS) * ng
        o_ref[0, :, h * hd:(h + 1) * hd] = (on * (zh * _sigmoid(zh))).astype(o_ref.dtype)


def _gdn_sample_call(proj_s3, state_conv, state_ssm, conv_w, a_log, dt_bias, norm_g):
    nb = proj_s3.shape[0]
    hd, nh = HEAD_DIM, N_HEADS_A
    vec = pl.BlockSpec((1, LANE), lambda b: (0, 0))
    return pl.pallas_call(
        _gdn_sample_kernel,
        out_shape=(jax.ShapeDtypeStruct((nb, 1, VAL_DIM_A), BF16),
                   jax.ShapeDtypeStruct((nb, nh, hd, hd), F32),
                   jax.ShapeDtypeStruct((nb, CONV_W - 1, CONV_DIM), F32)),
        grid=(nb,),
        in_specs=[pl.BlockSpec((1, 1, CONV_DIM), lambda b: (b, 0, OFF_CONV // CONV_DIM)),
                  pl.BlockSpec((1, 1, VAL_DIM_A), lambda b: (b, 0, OFF_Z // VAL_DIM_A)),
                  pl.BlockSpec((1, 1, LANE), lambda b: (b, 0, OFF_SM // LANE)),
                  pl.BlockSpec((1, CONV_W - 1, CONV_DIM), lambda b: (b, 0, 0)),
                  pl.BlockSpec((1, nh, hd, hd), lambda b: (b, 0, 0, 0)),
                  pl.BlockSpec((CONV_W, CONV_DIM), lambda b: (0, 0)),
                  vec, vec, vec],
        out_specs=(pl.BlockSpec((1, 1, VAL_DIM_A), lambda b: (b, 0, 0)),
                   pl.BlockSpec((1, nh, hd, hd), lambda b: (b, 0, 0, 0)),
                   pl.BlockSpec((1, CONV_W - 1, CONV_DIM), lambda b: (b, 0, 0))),
        compiler_params=_cparams(("arbitrary",)),
        name="gdn_sample",
    )(proj_s3, proj_s3, proj_s3, state_conv, state_ssm, conv_w,
      _lane_vec(a_log, SM_A), _lane_vec(dt_bias, SM_A), norm_g.reshape(1, hd).astype(F32))


IDX_PAGES_PER_STEP = 16


def _idx_sample_kernel(pt_ref, qi_ref, wi_ref, *rest):
    del pt_ref
    pages = rest[:IDX_PAGES_PER_STEP]
    o_ref = rest[IDX_PAGES_PER_STEP]
    qb = qi_ref[0].astype(BF16)
    w = wi_ref[0] * (N_IDX_HEADS ** -0.5)
    for p, pg in enumerate(pages):
        s = _dot_nt(qb, pg[0].astype(BF16))
        sc = jnp.sum(jnp.maximum(s, 0.0) * w, axis=0, keepdims=True) * (D_IDX ** -0.5)
        o_ref[0, :, p * PAGE_SIZE:(p + 1) * PAGE_SIZE] = sc


def _idx_sample_call(page_table, qi3, wi3, cache_kidx):
    nb, n_pages = page_table.shape
    pp = IDX_PAGES_PER_STEP
    page_specs = [pl.BlockSpec((1, PAGE_SIZE, D_IDX), functools.partial(lambda b, j, pt, p: (pt[b, j * pp + p], 0, 0), p=p))
                  for p in range(pp)]
    grid_spec = pltpu.PrefetchScalarGridSpec(
        num_scalar_prefetch=1,
        grid=(nb, n_pages // pp),
        in_specs=[pl.BlockSpec((1, N_IDX_HEADS, D_IDX), lambda b, j, pt: (b, 0, 0)),
                  pl.BlockSpec((1, N_IDX_HEADS, 1), lambda b, j, pt: (b, 0, 0))] + page_specs,
        out_specs=pl.BlockSpec((1, 1, pp * PAGE_SIZE), lambda b, j, pt: (b, 0, j)),
    )
    return pl.pallas_call(
        _idx_sample_kernel,
        out_shape=jax.ShapeDtypeStruct((nb, 1, n_pages * PAGE_SIZE), F32),
        grid_spec=grid_spec,
        compiler_params=_cparams(("arbitrary", "arbitrary")),
        name="idx_sample",
    )(page_table, qi3, wi3, *([cache_kidx] * pp))


IDX_FLAG = 1 << 20


def _topk_sample_kernel(n_sel, sc_ref, qi_ref, wi_ref, knew_ref, idx_ref, keys_s, rank_s, pos_s):
    nb, n_past = sc_ref.shape
    n_blk = n_past // LANE
    lane = lax.broadcasted_iota(I32, (nb, LANE), 1)

    kn = knew_ref[...].astype(BF16)
    q = qi_ref[...]
    w = wi_ref[...] * (N_IDX_HEADS ** -0.5)
    s_new = jnp.zeros((nb, 1), F32)
    for h in range(N_IDX_HEADS):
        qh = q[:, h * D_IDX:(h + 1) * D_IDX].astype(BF16).astype(F32)
        dot = jnp.sum(qh * kn.astype(F32), axis=1, keepdims=True)
        s_new = s_new + jnp.maximum(dot, 0.0) * w[:, h:h + 1]
    s_new = s_new * (D_IDX ** -0.5)
    key_new = _order_key(s_new)

    def to_keys(j, _):
        keys_s[j] = _order_key(sc_ref[:, pl.ds(pl.multiple_of(j * LANE, LANE), LANE)])
        return 0

    lax.fori_loop(0, n_blk, to_keys, 0)

    def count_ge(cand):
        def cb(j, c):
            return c + jnp.where(keys_s[j] >= cand, 1, 0)

        c = lax.fori_loop(0, n_blk, cb, jnp.zeros((nb, LANE), I32))
        return jnp.sum(c, axis=1, keepdims=True) + jnp.where(key_new >= cand, 1, 0)

    tau = jnp.maximum(_kth_largest_key(count_ge, (nb, 1), n_sel), INT_MIN + 1)

    def count_gt(j, c):
        return c + jnp.where(keys_s[j] > tau, 1, 0)

    c_gt = jnp.sum(lax.fori_loop(0, n_blk, count_gt, jnp.zeros((nb, LANE), I32)), axis=1, keepdims=True)
    quota = n_sel - (c_gt + jnp.where(key_new > tau, 1, 0))

    ri = lax.broadcasted_iota(I32, (LANE, LANE), 0)
    ci = lax.broadcasted_iota(I32, (LANE, LANE), 1)
    upper = (ri <= ci).astype(BF16)

    def excl_prefix(flag):
        return (_dot(flag.astype(BF16), upper) - flag).astype(I32)

    def rank_body(j, carry):
        base, base_eq = carry
        kj = keys_s[j]
        eq = kj == tau
        eqf = jnp.where(eq, 1.0, 0.0)
        take = (kj > tau) | (eq & (base_eq + excl_prefix(eqf) < quota))
        takef = jnp.where(take, 1.0, 0.0)
        rank_s[j] = jnp.where(take, base + excl_prefix(takef), -1)
        finite = (kj < KEY_POS_INF) & (kj > KEY_NEG_INF)
        pos = j * LANE + lane
        pos_s[j] = jnp.where(finite, pos, pos | IDX_FLAG)
        return (base + jnp.sum(takef, axis=1, keepdims=True).astype(I32),
                base_eq + jnp.sum(eqf, axis=1, keepdims=True).astype(I32))

    zero = jnp.zeros((nb, 1), I32)
    n_before_new, eq_before_new = lax.fori_loop(0, n_blk, rank_body, (zero, zero))

    slot = lax.broadcasted_iota(I32, (n_sel, LANE), 0)
    new_sel = (key_new > tau) | ((key_new == tau) & (eq_before_new < quota))
    new_fin = (key_new < KEY_POS_INF) & (key_new > KEY_NEG_INF)
    new_pos = jnp.where(new_fin, n_past, n_past | IDX_FLAG)

    def per_seq(b, _):
        def blk(j, acc):
            rk = rank_s[j, pl.ds(b, 1), :]
            ps = pos_s[j, pl.ds(b, 1), :]
            return acc + jnp.where(rk == slot, ps, 0)

        acc = lax.fori_loop(0, n_blk, blk, jnp.zeros((n_sel, LANE), I32))
        col = jnp.sum(acc, axis=1, keepdims=True)
        rb = lax.broadcasted_iota(I32, (nb, 1), 0) == b
        nbn = jnp.sum(jnp.where(rb, n_before_new, 0), axis=0, keepdims=True)
        nsel = jnp.sum(jnp.where(rb & new_sel, 1, 0), axis=0, keepdims=True)
        npos = jnp.sum(jnp.where(rb, new_pos, 0), axis=0, keepdims=True)
        col = col + jnp.where((slot[:, 0:1] == nbn) & (nsel > 0), npos, 0)
        idx_ref[b] = col
        return 0

    lax.fori_loop(0, nb, per_seq, 0)


def _attn_sample_kernel(n_past, idx_sm, pt_sm, idxc_ref, q_ref, kn_ref, vn_ref, rb_ref, ck_hbm, cv_hbm, o_ref,
                        kbuf, vbuf, sem):
    b = pl.program_id(0)
    n_sel = kbuf.shape[0] // N_KV_B
    hd = HEAD_DIM
    page_shift = int(math.log2(PAGE_SIZE))

    def row_copies(r, page, slot):
        dst = pl.ds(N_KV_B * r, N_KV_B)
        return (pltpu.make_async_copy(ck_hbm.at[page, slot], kbuf.at[dst, :], sem.at[0]),
                pltpu.make_async_copy(cv_hbm.at[page, slot], vbuf.at[dst, :], sem.at[1]))

    def start(r, _):
        p = jnp.minimum(idx_sm[b, r] & (IDX_FLAG - 1), n_past - 1)
        ck, cv = row_copies(r, pt_sm[b, p >> page_shift], p & (PAGE_SIZE - 1))
        ck.start()
        cv.start()
        return 0

    lax.fori_loop(0, n_sel, start, 0)

    def wait(r, _):
        ck, cv = row_copies(r, 0, 0)
        ck.wait()
        cv.wait()
        return 0

    lax.fori_loop(0, n_sel, wait, 0)

    idxc = idxc_ref[0]
    valid = idxc < IDX_FLAG
    idx = idxc & (IDX_FLAG - 1)
    is_new = idx >= n_past
    bucket = _t5_bucket(n_past - idx)
    q8 = q_ref[0].astype(BF16)
    scale = hd ** -0.5
    for n in range(N_KV_B):
        rows_n = pl.ds(n, n_sel, stride=N_KV_B)
        kn = jnp.where(is_new, kn_ref[0, :, n * hd:(n + 1) * hd], kbuf[rows_n, :]).astype(BF16)
        vn = jnp.where(is_new, vn_ref[0, :, n * hd:(n + 1) * hd], vbuf[rows_n, :])
        lg4 = _dot_nt(kn, q8[n * GQA_GROUP:(n + 1) * GQA_GROUP]) * scale
        for g in range(GQA_GROUP):
            h = n * GQA_GROUP + g
            lg = jnp.where(valid, lg4[:, g:g + 1] + _bias_from_bucket(bucket, rb_ref, h), NEG_BIG)
            m = jnp.max(lg, axis=0, keepdims=True)
            p = jnp.where(valid, jnp.exp(lg - m), 0.0)
            p = p / jnp.sum(p, axis=0, keepdims=True)
            o_ref[0, :, h * hd:(h + 1) * hd] = jnp.sum(p * vn, axis=0, keepdims=True).astype(o_ref.dtype)


def _attn_sample_call(idx3, page_table, q3, k_new, v_new, rel_bias, cache_k, cache_v):
    nb, n_sel, _ = idx3.shape
    n_past = page_table.shape[1] * PAGE_SIZE
    kvw = N_KV_B * HEAD_DIM
    qbw = N_HEADS_B * HEAD_DIM
    grid_spec = pltpu.PrefetchScalarGridSpec(
        num_scalar_prefetch=2,
        grid=(nb,),
        in_specs=[pl.BlockSpec((1, n_sel, 1), lambda b, *_: (b, 0, 0)),
                  pl.BlockSpec((1, N_HEADS_B, HEAD_DIM), lambda b, *_: (b, 0, 0)),
                  pl.BlockSpec((1, 1, kvw), lambda b, *_: (b, 0, 0)),
                  pl.BlockSpec((1, 1, kvw), lambda b, *_: (b, 0, 0)),
                  pl.BlockSpec(memory_space=pltpu.SMEM),
                  pl.BlockSpec(memory_space=pl.ANY),
                  pl.BlockSpec(memory_space=pl.ANY)],
        out_specs=pl.BlockSpec((1, 1, qbw), lambda b, *_: (b, 0, 0)),
        scratch_shapes=[pltpu.VMEM((n_sel * N_KV_B, HEAD_DIM), F32), pltpu.VMEM((n_sel * N_KV_B, HEAD_DIM), F32),
                        pltpu.SemaphoreType.DMA((2,))],
    )
    return pl.pallas_call(
        functools.partial(_attn_sample_kernel, n_past),
        out_shape=jax.ShapeDtypeStruct((nb, 1, qbw), BF16),
        grid_spec=grid_spec,
        compiler_params=_cparams(("arbitrary",)),
        name="attn_sample",
    )(idx3.reshape(nb, n_sel), page_table, idx3, q3, k_new.reshape(nb, 1, kvw), v_new.reshape(nb, 1, kvw),
      rel_bias.astype(F32), cache_k, cache_v)


def _topk_sample_call(scores2, qi2, wi2, ki_new, n_sel):
    nb, n_past = scores2.shape
    n_blk = n_past // LANE
    return pl.pallas_call(
        functools.partial(_topk_sample_kernel, n_sel),
        out_shape=jax.ShapeDtypeStruct((nb, n_sel, 1), I32),
        scratch_shapes=[pltpu.VMEM((n_blk, nb, LANE), I32), pltpu.VMEM((n_blk, nb, LANE), I32),
                        pltpu.VMEM((n_blk, nb, LANE), I32)],
        compiler_params=pltpu.CompilerParams(vmem_limit_bytes=VMEM_LIMIT),
        name="topk_sample",
    )(scores2, qi2, wi2, ki_new)


def _prep_w_in(w):
    offs = [0]
    for s in SPLIT_SIZES:
        offs.append(offs[-1] + s)
    conv, a_raw, b_raw, z, q_b, k_b, v_b, q_i, w_i, k_i = (w[:, offs[i]:offs[i + 1]].astype(BF16) for i in range(10))
    rows = w.shape[0]
    small = jnp.concatenate([a_raw, b_raw, w_i, jnp.zeros((rows, LANE - SM_W - N_IDX_HEADS), BF16)], axis=1)
    pad = jnp.zeros((rows, NP - OFF_SM - LANE), BF16)
    return jnp.concatenate([conv, z, q_i, q_b, k_b, v_b, k_i, small, pad], axis=1)


def kernel(x_prompt, x_sample, c_prompt, c_sample, cache_k, cache_v, cache_kidx, state_ssm, state_conv, page_table,
           w_ada, b_ada, pre1_g, post1_g, pre2_g, post2_g, w_in, w_out, conv_w, a_log, dt_bias, gdn_norm_g,
           idx_knorm_g, idx_knorm_b, rel_bias, w_ff1, w_ff2):
    bsz, seq, _ = x_prompt.shape
    nb, dec_seq, _ = x_sample.shape
    assert dec_seq == 1, "the sample path handles one new token per sequence"
    depth = w_in.shape[0]
    n_past = page_table.shape[1] * PAGE_SIZE
    n_sel_s = min(TOPK_MAX, (n_past + dec_seq) // 4)
    kvw = N_KV_B * HEAD_DIM
    tm_in, tn_in = min(1024, seq), 1024
    tm_out = min(512, seq)
    tf = 1024

    hp = x_prompt.reshape(bsz * seq, D_MODEL)
    hs = x_sample.reshape(nb, D_MODEL)
    c_all = jnp.concatenate([c_sample, c_prompt], axis=0)
    new_p, new_s = [], []
    for l in range(depth):
        m = _adaln_call(c_all, w_ada[l], b_ada[l])
        w_in_bf = _prep_w_in(w_in[l])
        w_out_bf = w_out[l].astype(BF16)
        w1_bf = w_ff1[l].astype(BF16)
        w2_bf = w_ff2[l].astype(BF16)

        proj = _inproj_call(hp, pre1_g[l], _ModSpec(m, False, nb, seq, tm_in), w_in_bf, tm_in, tn_in)
        k_p, v_p, ki_p, k_bf, va_bf, ki_bf = _kvprep_call(proj, idx_knorm_g[l], idx_knorm_b[l], tm_in, True)
        proj3 = proj.reshape(bsz, seq, NP)
        o_a, ssm_p = _gdn_prompt_call(proj3, conv_w[l], a_log[l], dt_bias[l], gdn_norm_g[l])
        o_b = _attn_prompt_call(proj3, ki_bf.reshape(bsz, seq, D_IDX), k_bf.reshape(bsz, seq, kvw),
                                va_bf.reshape(bsz, seq, 2 * kvw), rel_bias)
        mod_p = _ModSpec(m, False, nb, seq, tm_out)
        x1, h2 = _outproj_call(o_a.reshape(bsz * seq, VAL_DIM_A), o_b.reshape(bsz * seq, N_HEADS_B * HEAD_DIM), hp,
                               w_out_bf, post1_g[l], pre2_g[l], mod_p, tm_out)
        hp = _ffn_call(h2, x1, w1_bf, w2_bf, post2_g[l], mod_p, tm_out, tf)
        conv_p = proj3[:, seq - (CONV_W - 1):, OFF_CONV:OFF_CONV + CONV_DIM]
        new_p.append((k_p.reshape(bsz, seq, N_KV_B, HEAD_DIM), v_p.reshape(bsz, seq, N_KV_B, HEAD_DIM),
                      ki_p.reshape(bsz, seq, D_IDX), ssm_p, conv_p))

        mod_s = _ModSpec(m, True, 0, 1, nb)
        proj_s = _inproj_call(hs, pre1_g[l], mod_s, w_in_bf, nb, tn_in)
        k_s, v_s, ki_s = _kvprep_call(proj_s, idx_knorm_g[l], idx_knorm_b[l], nb)
        o_a_s, ssm_s, conv_s = _gdn_sample_call(proj_s.reshape(nb, 1, NP), state_conv[l], state_ssm[l], conv_w[l],
                                                a_log[l], dt_bias[l], gdn_norm_g[l])
        qi2 = proj_s[:, OFF_QI:OFF_QI + N_IDX_HEADS * D_IDX]
        wi2 = proj_s[:, OFF_SM + SM_W:OFF_SM + SM_W + N_IDX_HEADS]
        scores = _idx_sample_call(page_table, qi2.reshape(nb, N_IDX_HEADS, D_IDX), wi2.reshape(nb, N_IDX_HEADS, 1),
                                  cache_kidx[l])
        idx3 = _topk_sample_call(scores.reshape(nb, n_past), qi2, wi2, ki_s, n_sel_s)
        q3 = proj_s[:, OFF_QB:OFF_QB + N_HEADS_B * HEAD_DIM].reshape(nb, N_HEADS_B, HEAD_DIM)
        o_b_s = _attn_sample_call(idx3, page_table, q3, k_s, v_s, rel_bias, cache_k[l], cache_v[l])
        x1s, h2s = _outproj_call(o_a_s.reshape(nb, VAL_DIM_A), o_b_s.reshape(nb, N_HEADS_B * HEAD_DIM), hs,
                                 w_out_bf, post1_g[l], pre2_g[l], mod_s, nb)
        hs = _ffn_call(h2s, x1s, w1_bf, w2_bf, post2_g[l], mod_s, nb, tf)
        new_s.append((k_s.reshape(nb, 1, N_KV_B, HEAD_DIM), v_s.reshape(nb, 1, N_KV_B, HEAD_DIM),
                      ki_s.reshape(nb, 1, D_IDX), ssm_s, conv_s))

    p_k, p_v, p_kidx, p_ssm, p_conv = [jnp.stack([st[i] for st in new_p]) for i in range(5)]
    s_k, s_v, s_kidx, s_ssm, s_conv = [jnp.stack([st[i] for st in new_s]) for i in range(5)]
    return (hp.reshape(bsz, seq, D_MODEL), hs.reshape(nb, 1, D_MODEL), p_k, p_v, p_kidx, p_ssm, p_conv,
            s_k, s_v, s_kidx, s_ssm, s_conv)
```

```python
import functools
import math

import jax
import jax.numpy as jnp
from jax import lax
from jax.experimental import pallas as pl
from jax.experimental.pallas import tpu as pltpu

F32 = jnp.float32
BF16 = jnp.bfloat16
I32 = jnp.int32

D_MODEL = 2048
PAGE_SIZE = 128
HEAD_DIM = 128
N_HEADS_A = 8
KEY_DIM_A = N_HEADS_A * HEAD_DIM
VAL_DIM_A = N_HEADS_A * HEAD_DIM
CONV_DIM = 2 * KEY_DIM_A + VAL_DIM_A
CONV_W = 4
N_HEADS_B = 8
N_KV_B = 2
GQA_GROUP = N_HEADS_B // N_KV_B
N_IDX_HEADS = 16
D_IDX = 128
TOPK_MAX = 256
N_BUCKETS = 32
MAX_DISTANCE = 128
D_FF = 4 * D_MODEL
EPS = 1e-6

SPLIT_SIZES = (CONV_DIM, N_HEADS_A, N_HEADS_A, VAL_DIM_A, N_HEADS_B * HEAD_DIM, N_KV_B * HEAD_DIM,
               N_KV_B * HEAD_DIM, N_IDX_HEADS * D_IDX, N_IDX_HEADS, D_IDX)

OFF_CONV = 0
OFF_Z = OFF_CONV + CONV_DIM
OFF_QI = OFF_Z + VAL_DIM_A
OFF_QB = OFF_QI + N_IDX_HEADS * D_IDX
OFF_KB = OFF_QB + N_HEADS_B * HEAD_DIM
OFF_VB = OFF_KB + N_KV_B * HEAD_DIM
OFF_KI = OFF_VB + N_KV_B * HEAD_DIM
OFF_SM = OFF_KI + D_IDX
NP = 8192
SM_A, SM_B, SM_W = 0, N_HEADS_A, 2 * N_HEADS_A

LANE = 128
VMEM_LIMIT = 56 * 1024 * 1024
INT_MIN = -(2 ** 31)
NEG_BIG = -1e30

GDN_CHUNK = 64
GDN_GROUP = 256
ATT_TQ = 256
ATT_TK = 256


def _cparams(sem):
    return pltpu.CompilerParams(dimension_semantics=sem, vmem_limit_bytes=VMEM_LIMIT)


def _sigmoid(x):
    return 1.0 / (1.0 + jnp.exp(-x))


def _softplus(x):
    return jnp.maximum(x, 0.0) + jnp.log(1.0 + jnp.exp(-jnp.abs(x)))


def _dot(a, b):
    return jnp.dot(a, b, preferred_element_type=F32)


def _dot_nt(a, b):
    return lax.dot_general(a, b, (((1,), (1,)), ((), ())), preferred_element_type=F32)


def _dot_tn(a, b):
    return lax.dot_general(a, b, (((0,), (0,)), ((), ())), preferred_element_type=F32)


def _mod(ref):
    v = ref[...]
    return v[0] if v.ndim == 3 else v


def _adaln_kernel(c_ref, w_ref, b_ref, o_ref):
    c = c_ref[...]
    s = (c * _sigmoid(c)).astype(BF16)
    o_ref[...] = _dot(s, w_ref[...].astype(BF16)) + b_ref[...]


def _adaln_call(c_all, w_ada, b_ada):
    n_rows = c_all.shape[0]
    n_out = w_ada.shape[1]
    tn = 1024
    return pl.pallas_call(
        _adaln_kernel,
        out_shape=jax.ShapeDtypeStruct((n_rows, n_out), F32),
        grid=(n_out // tn,),
        in_specs=[pl.BlockSpec((n_rows, D_MODEL), lambda j: (0, 0)),
                  pl.BlockSpec((D_MODEL, tn), lambda j: (0, j)),
                  pl.BlockSpec((1, tn), lambda j: (0, j))],
        out_specs=pl.BlockSpec((n_rows, tn), lambda j: (0, j)),
        compiler_params=_cparams(("arbitrary",)),
        name="adaln",
    )(c_all, w_ada, b_ada.reshape(1, n_out))


class _ModSpec:
    def __init__(self, m2, per_row, row_off, rows_per_batch, tm):
        self.per_row = per_row
        self.arr = m2 if per_row else m2.reshape(m2.shape[0], 1, m2.shape[1])
        self.row_off = row_off
        self.blocks_per_batch = None if per_row else rows_per_batch // tm
        self.tm = tm

    def spec(self, k, grid_rank):
        if self.per_row:
            if grid_rank == 1:
                return pl.BlockSpec((self.tm, D_MODEL), lambda i: (0, k))
            return pl.BlockSpec((self.tm, D_MODEL), lambda i, j: (0, k))
        bpb, off = self.blocks_per_batch, self.row_off
        if grid_rank == 1:
            return pl.BlockSpec((1, 1, D_MODEL), lambda i: (off + i // bpb, 0, k))
        return pl.BlockSpec((1, 1, D_MODEL), lambda i, j: (off + i // bpb, 0, k))


def _inproj_kernel(x_ref, g_ref, sh_ref, sc_ref, w_ref, o_ref, h_scr):
    @pl.when(pl.program_id(1) == 0)
    def _():
        x = x_ref[...]
        y = x * lax.rsqrt(jnp.mean(x * x, axis=-1, keepdims=True) + EPS) * g_ref[...]
        h_scr[...] = (y * (1.0 + _mod(sc_ref)) + _mod(sh_ref)).astype(BF16)

    o_ref[...] = _dot(h_scr[...], w_ref[...])


def _inproj_call(x2d, g, mod, w_bf, tm, tn):
    m_rows = x2d.shape[0]
    n_cols = w_bf.shape[1]
    return pl.pallas_call(
        _inproj_kernel,
        out_shape=jax.ShapeDtypeStruct((m_rows, n_cols), F32),
        grid=(m_rows // tm, n_cols // tn),
        in_specs=[pl.BlockSpec((tm, D_MODEL), lambda i, j: (i, 0)),
                  pl.BlockSpec((1, D_MODEL), lambda i, j: (0, 0)),
                  mod.spec(0, 2), mod.spec(1, 2),
                  pl.BlockSpec((D_MODEL, tn), lambda i, j: (0, j))],
        out_specs=pl.BlockSpec((tm, tn), lambda i, j: (i, j)),
        scratch_shapes=[pltpu.VMEM((tm, D_MODEL), BF16)],
        compiler_params=_cparams(("arbitrary", "arbitrary")),
        name="inproj",
    )(x2d, g.reshape(1, D_MODEL), mod.arr, mod.arr, w_bf)


def _kvprep_kernel(kb_ref, vb_ref, ki_ref, lg_ref, lb_ref, k_ref, v_ref, kidx_ref, *bf_refs):
    kb = kb_ref[...]
    vb = vb_ref[...]
    k_ref[...] = kb
    v_ref[...] = vb
    x = ki_ref[...]
    mu = jnp.mean(x, axis=-1, keepdims=True)
    xc = x - mu
    var = jnp.mean(xc * xc, axis=-1, keepdims=True)
    ki = xc * lax.rsqrt(var + EPS) * lg_ref[...] + lb_ref[...]
    kidx_ref[...] = ki
    if bf_refs:
        kbf_ref, va_ref, kibf_ref = bf_refs
        kbf_ref[...] = kb.astype(BF16)
        kibf_ref[...] = ki.astype(BF16)
        hd = HEAD_DIM
        ones_col = jnp.where(lax.broadcasted_iota(I32, (vb.shape[0], hd), 1) == 0, 1.0, 0.0)
        va_ref[...] = jnp.concatenate([piece for n in range(N_KV_B) for piece in (vb[:, n * hd:(n + 1) * hd], ones_col)],
                                      axis=1).astype(BF16)


def _kvprep_call(proj, ln_g, ln_b, tm, with_bf16=False):
    m_rows = proj.shape[0]
    kvw = N_KV_B * HEAD_DIM
    row = lambda i: (i, 0)
    out_shape = [jax.ShapeDtypeStruct((m_rows, kvw), F32), jax.ShapeDtypeStruct((m_rows, kvw), F32),
                 jax.ShapeDtypeStruct((m_rows, D_IDX), F32)]
    out_specs = [pl.BlockSpec((tm, kvw), row), pl.BlockSpec((tm, kvw), row), pl.BlockSpec((tm, D_IDX), row)]
    if with_bf16:
        out_shape += [jax.ShapeDtypeStruct((m_rows, kvw), BF16), jax.ShapeDtypeStruct((m_rows, 2 * kvw), BF16),
                      jax.ShapeDtypeStruct((m_rows, D_IDX), BF16)]
        out_specs += [pl.BlockSpec((tm, kvw), row), pl.BlockSpec((tm, 2 * kvw), row), pl.BlockSpec((tm, D_IDX), row)]
    return pl.pallas_call(
        _kvprep_kernel,
        out_shape=tuple(out_shape),
        grid=(m_rows // tm,),
        in_specs=[pl.BlockSpec((tm, kvw), lambda i: (i, OFF_KB // kvw)),
                  pl.BlockSpec((tm, kvw), lambda i: (i, OFF_VB // kvw)),
                  pl.BlockSpec((tm, D_IDX), lambda i: (i, OFF_KI // D_IDX)),
                  pl.BlockSpec((1, D_IDX), lambda i: (0, 0)),
                  pl.BlockSpec((1, D_IDX), lambda i: (0, 0))],
        out_specs=tuple(out_specs),
        compiler_params=_cparams(("arbitrary",)),
        name="kvprep",
    )(proj, proj, proj, ln_g.reshape(1, D_IDX), ln_b.reshape(1, D_IDX))


def _conv_silu(x, cw, row):
    y = x * cw[CONV_W - 1:CONV_W]
    for s in range(1, CONV_W):
        xs = jnp.where(row >= s, pltpu.roll(x, s, 0), 0.0)
        y = y + xs * cw[CONV_W - 1 - s:CONV_W - s]
    return y * _sigmoid(y)


def _l2norm(x):
    return x * lax.rsqrt(jnp.sum(x * x, axis=-1, keepdims=True) + EPS)


def _gdn_prompt_kernel(q_ref, k_ref, v_ref, z_ref, sm_ref, cwq_ref, cwk_ref, cwv_ref, alog_ref, dtb_ref, ng_ref,
                       o_ref, s_out_ref,
                       qn_s, kn_s, vn_s, gc_s, bt_s, l_s, a_s, x_s, u_s, w_s, mn_s, sts_s):
    seq = q_ref.shape[1]
    hd = HEAD_DIM
    grp = GDN_GROUP
    chunk = GDN_CHUNK
    n_grp = seq // grp
    n_chunk = seq // chunk
    cpg = grp // chunk
    h = pl.program_id(1)

    row = lax.broadcasted_iota(I32, (seq, hd), 0)
    lane = lax.broadcasted_iota(I32, (seq, hd), 1)

    qn_s[...] = _l2norm(_conv_silu(q_ref[0], cwq_ref[...], row)) * (hd ** -0.5)
    kn_s[...] = _l2norm(_conv_silu(k_ref[0], cwk_ref[...], row))
    vn_s[...] = _conv_silu(v_ref[0], cwv_ref[...], row)

    sm = sm_ref[0]
    g_all = -jnp.exp(alog_ref[...]) * _softplus(sm + dtb_ref[...])
    b_all = _sigmoid(sm)
    g_col = jnp.sum(jnp.where(lane == h + SM_A, g_all, 0.0), axis=1, keepdims=True)
    b_col = jnp.sum(jnp.where(lane == h + SM_B, b_all, 0.0), axis=1, keepdims=True)
    gc = jnp.broadcast_to(g_col, (seq, hd))
    pos = row & (chunk - 1)
    s = 1
    while s < chunk:
        gc = gc + jnp.where(pos >= s, pltpu.roll(gc, s, 0), 0.0)
        s *= 2
    gc_s[...] = gc
    bt_s[...] = jnp.broadcast_to(b_col, (seq, hd))

    ii = lax.broadcasted_iota(I32, (grp, grp), 0)
    jj = lax.broadcasted_iota(I32, (grp, grp), 1)
    xr = ii ^ jj
    same = (xr >> int(math.log2(chunk))) == 0
    causal = same & (ii >= jj)
    strict = same & (ii > jj)
    eye = (ii == jj).astype(F32)

    def build(gi, _):
        r0 = pl.multiple_of(gi * grp, grp)
        kk = kn_s[pl.ds(r0, grp), :]
        qq = qn_s[pl.ds(r0, grp), :]
        bt = bt_s[pl.ds(r0, grp), :]
        gcg = gc_s[pl.ds(r0, grp), :]
        kb = (kk * bt).astype(BF16)
        kkb = kk.astype(BF16)
        kkt = _dot_nt(kb, kkb)
        qkt = _dot_nt(qq.astype(BF16), kkb)
        colb = jnp.concatenate([gcg, gcg], axis=1)
        diff = colb - colb.T
        decay = jnp.where(causal, jnp.exp(jnp.where(causal, diff, 0.0)), 0.0)
        lmat = jnp.where(strict, kkt * decay, 0.0)
        l_s[gi] = lmat
        a_s[pl.ds(r0, grp), :] = jnp.where(causal, qkt * decay, 0.0)
        x_s[gi] = eye - jnp.where(xr == 1, lmat, 0.0)
        return 0

    lax.fori_loop(0, n_grp, build, 0, unroll=2)

    for lvl in range(1, int(math.log2(chunk))):
        def level(gi, _, lvl=lvl):
            xm = x_s[gi]
            cl = jnp.where((xr >> lvl) == 1, l_s[gi], 0.0).astype(BF16)
            xb = xm.astype(BF16)
            x_s[gi] = xm - _dot(_dot(xb, cl).astype(BF16), xb)
            return 0

        lax.fori_loop(0, n_grp, level, 0, unroll=4)

    def apply(gi, _):
        r0 = pl.multiple_of(gi * grp, grp)
        kk = kn_s[pl.ds(r0, grp), :]
        vv = vn_s[pl.ds(r0, grp), :]
        bt = bt_s[pl.ds(r0, grp), :]
        gcg = gc_s[pl.ds(r0, grp), :]
        rhs = jnp.concatenate([vv * bt, kk * bt * jnp.exp(gcg)], axis=1).astype(BF16)
        uw = _dot(x_s[gi].astype(BF16), rhs)
        u_s[pl.ds(r0, grp), :] = uw[:, :hd]
        w_s[pl.ds(r0, grp), :] = uw[:, hd:]
        for cc in range(cpg):
            lo = cc * chunk
            g_last = gcg[lo + chunk - 1:lo + chunk, :]
            k_dec = (kk[lo:lo + chunk] * jnp.exp(g_last - gcg[lo:lo + chunk])).astype(BF16)
            wu = jnp.concatenate([uw[lo:lo + chunk, hd:], uw[lo:lo + chunk, :hd]], axis=1).astype(BF16)
            mn_s[gi * cpg + cc] = _dot_tn(k_dec, wu)
        return 0

    lax.fori_loop(0, n_grp, apply, 0, unroll=2)

    def scan(c, st):
        sts_s[c] = st
        g_last = gc_s[pl.ds(c * chunk + chunk - 1, 1), :]
        mn = mn_s[c]
        return st * jnp.exp(g_last) - _dot(mn[:, :hd].astype(BF16), st.astype(BF16)) + mn[:, hd:]

    s_out_ref[0, 0] = lax.fori_loop(0, n_chunk, scan, jnp.zeros((hd, hd), F32))

    ng = ng_ref[...]

    def emit(gi, _):
        r0 = pl.multiple_of(gi * grp, grp)
        gcg = gc_s[pl.ds(r0, grp), :]
        qg = qn_s[pl.ds(r0, grp), :] * jnp.exp(gcg)
        wg = w_s[pl.ds(r0, grp), :]
        ws, qs = [], []
        for cc in range(cpg):
            lo = cc * chunk
            stb = sts_s[gi * cpg + cc].astype(BF16)
            both = _dot(jnp.concatenate([wg[lo:lo + chunk], qg[lo:lo + chunk]], axis=0).astype(BF16), stb)
            ws.append(both[:chunk])
            qs.append(both[chunk:])
        v_new = u_s[pl.ds(r0, grp), :] - jnp.concatenate(ws, axis=0)
        o = jnp.concatenate(qs, axis=0) + _dot(a_s[pl.ds(r0, grp), :].astype(BF16), v_new.astype(BF16))
        zc = z_ref[0, pl.ds(r0, grp), :]
        on = o * lax.rsqrt(jnp.mean(o * o, axis=-1, keepdims=True) + EPS) * ng
        o_ref[0, pl.ds(r0, grp), :] = (on * (zc * _sigmoid(zc))).astype(o_ref.dtype)
        return 0

    lax.fori_loop(0, n_grp, emit, 0, unroll=2)


def _lane_vec(v, off):
    return jnp.zeros((1, LANE), F32).at[0, off:off + v.shape[0]].set(v.astype(F32))


def _gdn_prompt_call(proj3, conv_w, a_log, dt_bias, norm_g):
    bsz, seq, _ = proj3.shape
    hd = HEAD_DIM
    nh = N_HEADS_A
    n_grp = seq // GDN_GROUP
    col = lambda base: (lambda b, h: (b, 0, base // hd + h))
    cw = lambda base: (lambda b, h: (0, base // hd + h))
    vec = pl.BlockSpec((1, LANE), lambda b, h: (0, 0))
    return pl.pallas_call(
        _gdn_prompt_kernel,
        out_shape=(jax.ShapeDtypeStruct((bsz, seq, VAL_DIM_A), BF16),
                   jax.ShapeDtypeStruct((bsz, nh, hd, hd), F32)),
        grid=(bsz, nh),
        in_specs=[pl.BlockSpec((1, seq, hd), col(OFF_CONV)),
                  pl.BlockSpec((1, seq, hd), col(OFF_CONV + KEY_DIM_A)),
                  pl.BlockSpec((1, seq, hd), col(OFF_CONV + 2 * KEY_DIM_A)),
                  pl.BlockSpec((1, seq, hd), col(OFF_Z)),
                  pl.BlockSpec((1, seq, LANE), lambda b, h: (b, 0, OFF_SM // LANE)),
                  pl.BlockSpec((CONV_W, hd), cw(0)),
                  pl.BlockSpec((CONV_W, hd), cw(KEY_DIM_A)),
                  pl.BlockSpec((CONV_W, hd), cw(2 * KEY_DIM_A)),
                  vec, vec, vec],
        out_specs=(pl.BlockSpec((1, seq, hd), lambda b, h: (b, 0, h)),
                   pl.BlockSpec((1, 1, hd, hd), lambda b, h: (b, h, 0, 0))),
        scratch_shapes=[pltpu.VMEM((seq, hd), F32), pltpu.VMEM((seq, hd), F32), pltpu.VMEM((seq, hd), F32),
                        pltpu.VMEM((seq, hd), F32), pltpu.VMEM((seq, hd), F32),
                        pltpu.VMEM((n_grp, GDN_GROUP, GDN_GROUP), F32),
                        pltpu.VMEM((seq, GDN_GROUP), F32),
                        pltpu.VMEM((n_grp, GDN_GROUP, GDN_GROUP), F32),
                        pltpu.VMEM((seq, hd), F32), pltpu.VMEM((seq, hd), F32),
                        pltpu.VMEM((seq // GDN_CHUNK, hd, 2 * hd), F32), pltpu.VMEM((seq // GDN_CHUNK, hd, hd), F32)],
        compiler_params=_cparams(("arbitrary", "arbitrary")),
        name="gdn_prompt",
    )(proj3, proj3, proj3, proj3, proj3, conv_w, conv_w, conv_w,
      _lane_vec(a_log, SM_A), _lane_vec(dt_bias, SM_A), norm_g.reshape(1, hd).astype(F32))


def _t5_bucket(dist):
    n = jnp.maximum(dist, 0)
    max_exact = N_BUCKETS // 2
    nf = jnp.maximum(n, 1).astype(F32)
    large = max_exact + (jnp.log(nf / max_exact) / math.log(MAX_DISTANCE / max_exact)
                         * (N_BUCKETS - max_exact)).astype(I32)
    large = jnp.minimum(large, N_BUCKETS - 1)
    return jnp.where(n < max_exact, n, large)


def _bias_from_bucket(bucket, rb_ref, head):
    out = jnp.zeros(bucket.shape, F32)
    for b in range(N_BUCKETS):
        out = jnp.where(bucket == b, rb_ref[b, head], out)
    return out


def _order_key(score):
    bits = pltpu.bitcast(score + 0.0, I32)
    return bits ^ ((bits >> 31) & 0x7FFFFFFF)


KEY_POS_INF = 0x7F800000
KEY_NEG_INF = INT_MIN + 0x7FFFFF


def _kth_largest_key(count_ge, shape, n_sel):
    def body(i, tau):
        bit = 31 - i
        cand = jnp.where(i == 0, tau ^ INT_MIN, tau | (1 << bit))
        return jnp.where(count_ge(cand) >= n_sel, cand, tau)

    return lax.fori_loop(0, 32, body, jnp.full(shape, INT_MIN, I32))


def _attn_prompt_kernel(n_sel, qi_ref, sm_ref, qb_ref, ki_ref, k_ref, va_ref, rb_ref, o_ref,
                        qi_s, qb_s, wt_s, keys_s, tau_s, bias_s, m_s, acc_s):
    tq, tk = ATT_TQ, ATT_TK
    hd = HEAD_DIM
    bi = pl.program_id(0)
    qblk = pl.program_id(1)
    n_kb = (qblk * tq) // tk + 1
    krow = lax.broadcasted_iota(I32, (tk, tq), 0)
    qcol = lax.broadcasted_iota(I32, (tk, tq), 1)

    @pl.when((bi == 0) & (qblk == 0))
    def _():
        for off in range(bias_s.shape[1]):
            bucket = _t5_bucket(off * tk + krow - qcol)
            for h in range(N_HEADS_B):
                bias_s[h, off] = _bias_from_bucket(bucket, rb_ref, h)

    qi_s[...] = qi_ref[0].astype(BF16)
    qb_s[...] = qb_ref[0].astype(BF16)
    wt_s[...] = sm_ref[0].T * (N_IDX_HEADS ** -0.5)
    t_pos = qblk * tq + qcol

    def idx_body(j, _):
        k0 = pl.multiple_of(j * tk, tk)
        kt = ki_ref[0, pl.ds(k0, tk), :]
        acc = jnp.zeros((tk, tq), F32)
        for h in range(N_IDX_HEADS):
            s = _dot_nt(kt, qi_s[:, h * D_IDX:(h + 1) * D_IDX])
            acc = acc + jnp.maximum(s, 0.0) * wt_s[SM_W + h:SM_W + h + 1, :]
        acc = acc * (D_IDX ** -0.5)
        keys_s[j] = jnp.where(k0 + krow <= t_pos, _order_key(acc), INT_MIN)
        return 0

    lax.fori_loop(0, n_kb, idx_body, 0)

    def count_ge(cand):
        def cb(j, c):
            hit = jnp.where(keys_s[j] >= cand, 1, 0)
            return c + jnp.sum(hit.reshape(tk // 8, 8, tq), axis=0)

        c = lax.fori_loop(0, n_kb, cb, jnp.zeros((8, tq), I32))
        return jnp.sum(c, axis=0, keepdims=True)

    tau = jnp.maximum(_kth_largest_key(count_ge, (1, tq), n_sel), KEY_NEG_INF + 1)

    scale = hd ** -0.5
    lane_reps = tk // LANE
    tau_rep = jnp.broadcast_to(tau, (LANE, tq)).T
    tau_s[...] = jnp.concatenate([tau_rep] * lane_reps, axis=1)

    def block_inputs(j):
        kq = keys_s[j].T
        sel = jnp.where(kq < KEY_POS_INF, kq, INT_MIN) >= tau_s[...]
        return k_ref[0, pl.ds(pl.multiple_of(j * tk, tk), tk), :], sel

    def masked_logits(j, h, kt, sel):
        n = h // GQA_GROUP
        boff = jnp.minimum(qblk - j, bias_s.shape[1] - 1)
        lg = _dot_nt(qb_s[:, h * hd:(h + 1) * hd], kt[:, n * hd:(n + 1) * hd]) * scale + bias_s[h, boff]
        return jnp.where(sel, lg, NEG_BIG)

    m_s[...] = jnp.full(m_s.shape, NEG_BIG, F32)

    def max_body(j, _):
        kt, sel = block_inputs(j)
        for h in range(N_HEADS_B):
            lg = masked_logits(j, h, kt, sel)
            part = lg[:, :LANE]
            for r in range(1, lane_reps):
                part = jnp.maximum(part, lg[:, r * LANE:(r + 1) * LANE])
            m_s[h] = jnp.maximum(m_s[h], part)
        return 0

    lax.fori_loop(0, n_kb, max_body, 0)
    for h in range(N_HEADS_B):
        m_s[h] = jnp.broadcast_to(jnp.max(m_s[h], axis=1, keepdims=True), (tq, LANE))
    acc_s[...] = jnp.zeros(acc_s.shape, F32)

    def sum_body(j, _):
        kt, sel = block_inputs(j)
        va = va_ref[0, pl.ds(pl.multiple_of(j * tk, tk), tk), :]
        for h in range(N_HEADS_B):
            n = h // GQA_GROUP
            mrow = jnp.concatenate([m_s[h]] * lane_reps, axis=1)
            p = jnp.exp(masked_logits(j, h, kt, sel) - mrow)
            acc_s[h] += _dot(p.astype(BF16), va[:, n * 2 * hd:(n + 1) * 2 * hd])
        return 0

    lax.fori_loop(0, n_kb, sum_body, 0)
    for h in range(N_HEADS_B):
        a = acc_s[h]
        o_ref[0, :, h * hd:(h + 1) * hd] = (a[:, :hd] / a[:, hd:hd + 1]).astype(o_ref.dtype)


def _attn_prompt_call(proj3, ki_bf, k_bf, va_bf, rel_bias):
    bsz, seq, _ = proj3.shape
    tq, tk = ATT_TQ, ATT_TK
    assert tq == tk, "the bias-tile offsets assume square blocks"
    n_sel = min(TOPK_MAX, seq // 4)
    qiw = N_IDX_HEADS * D_IDX
    qbw = N_HEADS_B * HEAD_DIM
    kvw = N_KV_B * HEAD_DIM
    n_off = -(-(MAX_DISTANCE + tk - 1) // tk) + 1
    return pl.pallas_call(
        functools.partial(_attn_prompt_kernel, n_sel),
        out_shape=jax.ShapeDtypeStruct((bsz, seq, qbw), BF16),
        grid=(bsz, seq // tq),
        in_specs=[pl.BlockSpec((1, tq, qiw), lambda b, i: (b, i, OFF_QI // qiw)),
                  pl.BlockSpec((1, tq, LANE), lambda b, i: (b, i, OFF_SM // LANE)),
                  pl.BlockSpec((1, tq, qbw), lambda b, i: (b, i, OFF_QB // qbw)),
                  pl.BlockSpec((1, seq, D_IDX), lambda b, i: (b, 0, 0)),
                  pl.BlockSpec((1, seq, kvw), lambda b, i: (b, 0, 0)),
                  pl.BlockSpec((1, seq, 2 * kvw), lambda b, i: (b, 0, 0)),
                  pl.BlockSpec(memory_space=pltpu.SMEM)],
        out_specs=pl.BlockSpec((1, tq, qbw), lambda b, i: (b, i, 0)),
        scratch_shapes=[pltpu.VMEM((tq, qiw), BF16), pltpu.VMEM((tq, qbw), BF16),
                        pltpu.VMEM((LANE, tq), F32),
                        pltpu.VMEM((seq // tk, tk, tq), I32),
                        pltpu.VMEM((tq, tk), I32),
                        pltpu.VMEM((N_HEADS_B, n_off, tq, tk), F32),
                        pltpu.VMEM((N_HEADS_B, tq, LANE), F32),
                        pltpu.VMEM((N_HEADS_B, tq, 2 * HEAD_DIM), F32)],
        compiler_params=_cparams(("arbitrary", "arbitrary")),
        name="attn_prompt",
    )(proj3, proj3, proj3, ki_bf, k_bf, va_bf, rel_bias.astype(F32))


def _rms(x, g):
    return x * lax.rsqrt(jnp.mean(x * x, axis=-1, keepdims=True) + EPS) * g


def _outproj_kernel(oa_ref, ob_ref, x_ref, wa_ref, wb_ref, pg_ref, g1_ref, p2_ref, sh2_ref, sc2_ref, x1_ref, h2_ref):
    mix = _dot(oa_ref[...], wa_ref[...]) + _dot(ob_ref[...], wb_ref[...])
    x1 = x_ref[...] + _mod(g1_ref) * _rms(mix, pg_ref[...])
    x1_ref[...] = x1
    h2_ref[...] = (_rms(x1, p2_ref[...]) * (1.0 + _mod(sc2_ref)) + _mod(sh2_ref)).astype(BF16)


def _outproj_call(oa, ob, x2d, w_out_bf, post1_g, pre2_g, mod, tm):
    m_rows = x2d.shape[0]
    half = w_out_bf.shape[0] // 2
    row = lambda i: (i, 0)
    vec = pl.BlockSpec((1, D_MODEL), lambda i: (0, 0))
    return pl.pallas_call(
        _outproj_kernel,
        out_shape=(jax.ShapeDtypeStruct((m_rows, D_MODEL), F32), jax.ShapeDtypeStruct((m_rows, D_MODEL), BF16)),
        grid=(m_rows // tm,),
        in_specs=[pl.BlockSpec((tm, half), row), pl.BlockSpec((tm, half), row), pl.BlockSpec((tm, D_MODEL), row),
                  pl.BlockSpec((half, D_MODEL), lambda i: (0, 0)), pl.BlockSpec((half, D_MODEL), lambda i: (1, 0)),
                  vec, mod.spec(2, 1), vec, mod.spec(3, 1), mod.spec(4, 1)],
        out_specs=(pl.BlockSpec((tm, D_MODEL), row), pl.BlockSpec((tm, D_MODEL), row)),
        compiler_params=_cparams(("arbitrary",)),
        name="outproj",
    )(oa, ob, x2d, w_out_bf, w_out_bf, post1_g.reshape(1, D_MODEL), mod.arr, pre2_g.reshape(1, D_MODEL),
      mod.arr, mod.arr)


def _ffn_kernel(h_ref, x1_ref, w1_ref, w2_ref, pg_ref, g2_ref, o_ref, acc_s):
    kk = pl.program_id(1)

    @pl.when(kk == 0)
    def _():
        acc_s[...] = jnp.zeros_like(acc_s)

    a = jnp.maximum(_dot(h_ref[...], w1_ref[...]), 0.0)
    acc_s[...] += _dot((a * a).astype(BF16), w2_ref[...])

    @pl.when(kk == pl.num_programs(1) - 1)
    def _():
        o_ref[...] = x1_ref[...] + _mod(g2_ref) * _rms(acc_s[...], pg_ref[...])


def _ffn_call(h2, x1, w1_bf, w2_bf, post2_g, mod, tm, tf):
    m_rows = x1.shape[0]
    return pl.pallas_call(
        _ffn_kernel,
        out_shape=jax.ShapeDtypeStruct((m_rows, D_MODEL), F32),
        grid=(m_rows // tm, D_FF // tf),
        in_specs=[pl.BlockSpec((tm, D_MODEL), lambda i, k: (i, 0)),
                  pl.BlockSpec((tm, D_MODEL), lambda i, k: (i, 0)),
                  pl.BlockSpec((D_MODEL, tf), lambda i, k: (0, k)),
                  pl.BlockSpec((tf, D_MODEL), lambda i, k: (k, 0)),
                  pl.BlockSpec((1, D_MODEL), lambda i, k: (0, 0)),
                  mod.spec(5, 2)],
        out_specs=pl.BlockSpec((tm, D_MODEL), lambda i, k: (i, 0)),
        scratch_shapes=[pltpu.VMEM((tm, D_MODEL), F32)],
        compiler_params=_cparams(("arbitrary", "arbitrary")),
        name="ffn",
    )(h2, x1, w1_bf, w2_bf, post2_g.reshape(1, D_MODEL), mod.arr)


def _gdn_sample_kernel(cin_ref, z_ref, sm_ref, sconv_ref, ssm_ref, cw_ref, alog_ref, dtb_ref, ng_ref,
                       o_ref, ssm_out_ref, conv_out_ref):
    hd = HEAD_DIM
    nh = N_HEADS_A
    prev = sconv_ref[0]
    xin = cin_ref[0]
    cw = cw_ref[...]
    y = xin * cw[CONV_W - 1:CONV_W]
    for s in range(CONV_W - 1):
        y = y + prev[s:s + 1] * cw[s:s + 1]
    y = y * _sigmoid(y)
    conv_out_ref[0, 0:CONV_W - 2, :] = prev[1:CONV_W - 1]
    conv_out_ref[0, CONV_W - 2:CONV_W - 1, :] = xin

    def heads(base):
        return jnp.concatenate([y[:, base + h * hd:base + (h + 1) * hd] for h in range(nh)], axis=0)

    q8 = _l2norm(heads(0)) * (hd ** -0.5)
    k8 = _l2norm(heads(KEY_DIM_A))
    v8 = heads(2 * KEY_DIM_A)
    sm = sm_ref[0]
    g_all = -jnp.exp(alog_ref[...]) * _softplus(sm + dtb_ref[...])
    b_all = _sigmoid(sm)
    kt = jnp.concatenate([k8, jnp.zeros((LANE - nh, hd), F32)], axis=0).T
    k8b = k8.astype(BF16)
    q8b = q8.astype(BF16)
    z = z_ref[0]
    ng = ng_ref[...]
    for h in range(nh):
        s1 = ssm_ref[0, h] * jnp.exp(g_all[:, SM_A + h:SM_A + h + 1])
        kv = _dot(k8b, s1.astype(BF16))[h:h + 1]
        delta = (v8[h:h + 1] - kv) * b_all[:, SM_B + h:SM_B + h + 1]
        s2 = s1 + kt[:, h:h + 1] * delta
        ssm_out_ref[0, h] = s2
        o = _dot(q8b, s2.astype(BF16))[h:h + 1]
        zh = z[:, h * hd:(h + 1) * hd]
        on = o * lax.rsqrt(jnp.mean(o * o, axis=-1, keepdims=True) + EPS) * ng
        o_ref[0, :, h * hd:(h + 1) * hd] = (on * (zh * _sigmoid(zh))).astype(o_ref.dtype)


def _gdn_sample_call(proj_s3, state_conv, state_ssm, conv_w, a_log, dt_bias, norm_g):
    nb = proj_s3.shape[0]
    hd, nh = HEAD_DIM, N_HEADS_A
    vec = pl.BlockSpec((1, LANE), lambda b: (0, 0))
    return pl.pallas_call(
        _gdn_sample_kernel,
        out_shape=(jax.ShapeDtypeStruct((nb, 1, VAL_DIM_A), BF16),
                   jax.ShapeDtypeStruct((nb, nh, hd, hd), F32),
                   jax.ShapeDtypeStruct((nb, CONV_W - 1, CONV_DIM), F32)),
        grid=(nb,),
        in_specs=[pl.BlockSpec((1, 1, CONV_DIM), lambda b: (b, 0, OFF_CONV // CONV_DIM)),
                  pl.BlockSpec((1, 1, VAL_DIM_A), lambda b: (b, 0, OFF_Z // VAL_DIM_A)),
                  pl.BlockSpec((1, 1, LANE), lambda b: (b, 0, OFF_SM // LANE)),
                  pl.BlockSpec((1, CONV_W - 1, CONV_DIM), lambda b: (b, 0, 0)),
                  pl.BlockSpec((1, nh, hd, hd), lambda b: (b, 0, 0, 0)),
                  pl.BlockSpec((CONV_W, CONV_DIM), lambda b: (0, 0)),
                  vec, vec, vec],
        out_specs=(pl.BlockSpec((1, 1, VAL_DIM_A), lambda b: (b, 0, 0)),
                   pl.BlockSpec((1, nh, hd, hd), lambda b: (b, 0, 0, 0)),
                   pl.BlockSpec((1, CONV_W - 1, CONV_DIM), lambda b: (b, 0, 0))),
        compiler_params=_cparams(("arbitrary",)),
        name="gdn_sample",
    )(proj_s3, proj_s3, proj_s3, state_conv, state_ssm, conv_w,
      _lane_vec(a_log, SM_A), _lane_vec(dt_bias, SM_A), norm_g.reshape(1, hd).astype(F32))


IDX_PAGES_PER_STEP = 32


def _idx_sample_kernel(pt_ref, qi_ref, wi_ref, *rest):
    del pt_ref
    pages = rest[:-1]
    o_ref = rest[-1]
    qb = qi_ref[0].astype(BF16)
    w = wi_ref[0] * (N_IDX_HEADS ** -0.5)
    for p, pg in enumerate(pages):
        s = _dot_nt(qb, pg[0].astype(BF16))
        sc = jnp.sum(jnp.maximum(s, 0.0) * w, axis=0, keepdims=True) * (D_IDX ** -0.5)
        o_ref[0, :, p * PAGE_SIZE:(p + 1) * PAGE_SIZE] = sc


def _idx_sample_call(page_table, qi3, wi3, cache_kidx):
    nb, n_pages = page_table.shape
    pp = min(IDX_PAGES_PER_STEP, n_pages)
    page_specs = [pl.BlockSpec((1, PAGE_SIZE, D_IDX), functools.partial(lambda b, j, pt, p: (pt[b, j * pp + p], 0, 0), p=p))
                  for p in range(pp)]
    grid_spec = pltpu.PrefetchScalarGridSpec(
        num_scalar_prefetch=1,
        grid=(nb, n_pages // pp),
        in_specs=[pl.BlockSpec((1, N_IDX_HEADS, D_IDX), lambda b, j, pt: (b, 0, 0)),
                  pl.BlockSpec((1, N_IDX_HEADS, 1), lambda b, j, pt: (b, 0, 0))] + page_specs,
        out_specs=pl.BlockSpec((1, 1, pp * PAGE_SIZE), lambda b, j, pt: (b, 0, j)),
    )
    return pl.pallas_call(
        _idx_sample_kernel,
        out_shape=jax.ShapeDtypeStruct((nb, 1, n_pages * PAGE_SIZE), F32),
        grid_spec=grid_spec,
        compiler_params=_cparams(("arbitrary", "arbitrary")),
        name="idx_sample",
    )(page_table, qi3, wi3, *([cache_kidx] * pp))


IDX_FLAG = 1 << 20


def _topk_sample_kernel(n_sel, sc_ref, qi_ref, wi_ref, knew_ref, idx_ref, keys_s, keyw_s, rank_s, pos_s):
    nb, n_past = sc_ref.shape
    n_blk = n_past // LANE
    lane = lax.broadcasted_iota(I32, (nb, LANE), 1)

    kn = knew_ref[...].astype(BF16)
    q = qi_ref[...]
    w = wi_ref[...] * (N_IDX_HEADS ** -0.5)
    s_new = jnp.zeros((nb, 1), F32)
    for h in range(N_IDX_HEADS):
        qh = q[:, h * D_IDX:(h + 1) * D_IDX].astype(BF16).astype(F32)
        dot = jnp.sum(qh * kn.astype(F32), axis=1, keepdims=True)
        s_new = s_new + jnp.maximum(dot, 0.0) * w[:, h:h + 1]
    s_new = s_new * (D_IDX ** -0.5)
    key_new = _order_key(s_new)

    def to_keys(j, _):
        keys_s[j] = _order_key(sc_ref[:, pl.ds(pl.multiple_of(j * LANE, LANE), LANE)])
        return 0

    lax.fori_loop(0, n_blk, to_keys, 0, unroll=8)
    keyw_s[...] = _order_key(sc_ref[...])
    wide = min(n_past, 16 * LANE)

    def count(pred):
        acc = jnp.zeros((nb, LANE), I32)
        for c in range(n_past // wide):
            hit = jnp.where(pred(keyw_s[:, c * wide:(c + 1) * wide]), 1, 0)
            for k in range(wide // LANE):
                acc = acc + hit[:, k * LANE:(k + 1) * LANE]
        return jnp.sum(acc, axis=1, keepdims=True)

    def count_ge(cand):
        return count(lambda k: k >= cand) + jnp.where(key_new >= cand, 1, 0)

    tau = jnp.maximum(_kth_largest_key(count_ge, (nb, 1), n_sel), INT_MIN + 1)

    quota = n_sel - (count(lambda k: k > tau) + jnp.where(key_new > tau, 1, 0))

    ri = lax.broadcasted_iota(I32, (LANE, LANE), 0)
    ci = lax.broadcasted_iota(I32, (LANE, LANE), 1)
    upper = (ri <= ci).astype(BF16)

    def excl_prefix(flag):
        return (_dot(flag.astype(BF16), upper) - flag).astype(I32)

    def rank_body(j, carry):
        base, base_eq = carry
        kj = keys_s[j]
        eq = kj == tau
        eqf = jnp.where(eq, 1.0, 0.0)
        take = (kj > tau) | (eq & (base_eq + excl_prefix(eqf) < quota))
        takef = jnp.where(take, 1.0, 0.0)
        rank_s[j] = jnp.where(take, base + excl_prefix(takef), -1)
        finite = (kj < KEY_POS_INF) & (kj > KEY_NEG_INF)
        pos = j * LANE + lane
        pos_s[j] = jnp.where(finite, pos, pos | IDX_FLAG)
        return (base + jnp.sum(takef, axis=1, keepdims=True).astype(I32),
                base_eq + jnp.sum(eqf, axis=1, keepdims=True).astype(I32))

    zero = jnp.zeros((nb, 1), I32)
    n_before_new, eq_before_new = lax.fori_loop(0, n_blk, rank_body, (zero, zero), unroll=4)

    slot = lax.broadcasted_iota(I32, (n_sel, LANE), 0)
    new_sel = (key_new > tau) | ((key_new == tau) & (eq_before_new < quota))
    new_fin = (key_new < KEY_POS_INF) & (key_new > KEY_NEG_INF)
    new_pos = jnp.where(new_fin, n_past, n_past | IDX_FLAG)

    def per_seq(b, _):
        def blk(j, acc):
            rk = rank_s[j, pl.ds(b, 1), :]
            ps = pos_s[j, pl.ds(b, 1), :]
            return acc + jnp.where(rk == slot, ps, 0)

        acc = lax.fori_loop(0, n_blk, blk, jnp.zeros((n_sel, LANE), I32), unroll=4)
        col = jnp.sum(acc, axis=1, keepdims=True)
        rb = lax.broadcasted_iota(I32, (nb, 1), 0) == b
        nbn = jnp.sum(jnp.where(rb, n_before_new, 0), axis=0, keepdims=True)
        nsel = jnp.sum(jnp.where(rb & new_sel, 1, 0), axis=0, keepdims=True)
        npos = jnp.sum(jnp.where(rb, new_pos, 0), axis=0, keepdims=True)
        col = col + jnp.where((slot[:, 0:1] == nbn) & (nsel > 0), npos, 0)
        idx_ref[b] = col
        return 0

    lax.fori_loop(0, nb, per_seq, 0)


def _attn_sample_kernel(n_past, idx_sm, pt_sm, idxc_ref, q_ref, kn_ref, vn_ref, rb_ref, ck_hbm, cv_hbm, o_ref,
                        kbuf, vbuf, sem):
    b = pl.program_id(0)
    n_sel = kbuf.shape[0] // N_KV_B
    hd = HEAD_DIM
    page_shift = int(math.log2(PAGE_SIZE))

    def row_copies(r, page, slot):
        dst = pl.ds(N_KV_B * r, N_KV_B)
        return (pltpu.make_async_copy(ck_hbm.at[page, slot], kbuf.at[dst, :], sem.at[0]),
                pltpu.make_async_copy(cv_hbm.at[page, slot], vbuf.at[dst, :], sem.at[1]))

    def start(r, _):
        p = jnp.minimum(idx_sm[b, r] & (IDX_FLAG - 1), n_past - 1)
        ck, cv = row_copies(r, pt_sm[b, p >> page_shift], p & (PAGE_SIZE - 1))
        ck.start()
        cv.start()
        return 0

    lax.fori_loop(0, n_sel, start, 0, unroll=8)

    def wait(r, _):
        ck, cv = row_copies(r, 0, 0)
        ck.wait()
        cv.wait()
        return 0

    lax.fori_loop(0, n_sel, wait, 0, unroll=8)

    idxc = idxc_ref[0]
    valid = idxc < IDX_FLAG
    idx = idxc & (IDX_FLAG - 1)
    is_new = idx >= n_past
    bucket = _t5_bucket(n_past - idx)
    bias = jnp.zeros((n_sel, LANE), F32)
    for bkt in range(N_BUCKETS):
        bias = jnp.where(bucket == bkt, rb_ref[bkt:bkt + 1, :], bias)
    q_pad = jnp.concatenate([q_ref[0], jnp.zeros((LANE - N_HEADS_B, hd), F32)], axis=0).astype(BF16)
    lane = lax.broadcasted_iota(I32, (n_sel, LANE), 1)
    lg = jnp.zeros((n_sel, LANE), F32)
    vals = []
    for n in range(N_KV_B):
        rows_n = pl.ds(n, n_sel, stride=N_KV_B)
        kn = jnp.where(is_new, kn_ref[0, :, n * hd:(n + 1) * hd], kbuf[rows_n, :]).astype(BF16)
        vals.append(jnp.where(is_new, vn_ref[0, :, n * hd:(n + 1) * hd], vbuf[rows_n, :]))
        lg = jnp.where(lane // GQA_GROUP == n, _dot_nt(kn, q_pad), lg)
    lg = jnp.where(valid, lg * (hd ** -0.5) + bias, NEG_BIG)
    m = jnp.max(lg, axis=0, keepdims=True)
    p = jnp.where(valid, jnp.exp(lg - m), 0.0)
    p = p / jnp.sum(p, axis=0, keepdims=True)
    for h in range(N_HEADS_B):
        o_h = jnp.sum(p[:, h:h + 1] * vals[h // GQA_GROUP], axis=0, keepdims=True)
        o_ref[0, :, h * hd:(h + 1) * hd] = o_h.astype(o_ref.dtype)


def _attn_sample_call(idx3, page_table, q3, k_new, v_new, rel_bias, cache_k, cache_v):
    nb, n_sel, _ = idx3.shape
    n_past = page_table.shape[1] * PAGE_SIZE
    kvw = N_KV_B * HEAD_DIM
    qbw = N_HEADS_B * HEAD_DIM
    grid_spec = pltpu.PrefetchScalarGridSpec(
        num_scalar_prefetch=2,
        grid=(nb,),
        in_specs=[pl.BlockSpec((1, n_sel, 1), lambda b, *_: (b, 0, 0)),
                  pl.BlockSpec((1, N_HEADS_B, HEAD_DIM), lambda b, *_: (b, 0, 0)),
                  pl.BlockSpec((1, 1, kvw), lambda b, *_: (b, 0, 0)),
                  pl.BlockSpec((1, 1, kvw), lambda b, *_: (b, 0, 0)),
                  pl.BlockSpec((N_BUCKETS, LANE), lambda b, *_: (0, 0)),
                  pl.BlockSpec(memory_space=pl.ANY),
                  pl.BlockSpec(memory_space=pl.ANY)],
        out_specs=pl.BlockSpec((1, 1, qbw), lambda b, *_: (b, 0, 0)),
        scratch_shapes=[pltpu.VMEM((n_sel * N_KV_B, HEAD_DIM), F32), pltpu.VMEM((n_sel * N_KV_B, HEAD_DIM), F32),
                        pltpu.SemaphoreType.DMA((2,))],
    )
    return pl.pallas_call(
        functools.partial(_attn_sample_kernel, n_past),
        out_shape=jax.ShapeDtypeStruct((nb, 1, qbw), BF16),
        grid_spec=grid_spec,
        compiler_params=_cparams(("arbitrary",)),
        name="attn_sample",
    )(idx3.reshape(nb, n_sel), page_table, idx3, q3, k_new.reshape(nb, 1, kvw), v_new.reshape(nb, 1, kvw),
      jnp.pad(rel_bias.astype(F32), ((0, 0), (0, LANE - N_HEADS_B))), cache_k, cache_v)


def _topk_sample_call(scores2, qi2, wi2, ki_new, n_sel):
    nb, n_past = scores2.shape
    n_blk = n_past // LANE
    return pl.pallas_call(
        functools.partial(_topk_sample_kernel, n_sel),
        out_shape=jax.ShapeDtypeStruct((nb, n_sel, 1), I32),
        scratch_shapes=[pltpu.VMEM((n_blk, nb, LANE), I32), pltpu.VMEM((nb, n_past), I32),
                        pltpu.VMEM((n_blk, nb, LANE), I32), pltpu.VMEM((n_blk, nb, LANE), I32)],
        compiler_params=pltpu.CompilerParams(vmem_limit_bytes=VMEM_LIMIT),
        name="topk_sample",
    )(scores2, qi2, wi2, ki_new)


def _prep_w_in(w):
    offs = [0]
    for s in SPLIT_SIZES:
        offs.append(offs[-1] + s)
    conv, a_raw, b_raw, z, q_b, k_b, v_b, q_i, w_i, k_i = (w[:, offs[i]:offs[i + 1]].astype(BF16) for i in range(10))
    rows = w.shape[0]
    small = jnp.concatenate([a_raw, b_raw, w_i, jnp.zeros((rows, LANE - SM_W - N_IDX_HEADS), BF16)], axis=1)
    pad = jnp.zeros((rows, NP - OFF_SM - LANE), BF16)
    return jnp.concatenate([conv, z, q_i, q_b, k_b, v_b, k_i, small, pad], axis=1)


def kernel(x_prompt, x_sample, c_prompt, c_sample, cache_k, cache_v, cache_kidx, state_ssm, state_conv, page_table,
           w_ada, b_ada, pre1_g, post1_g, pre2_g, post2_g, w_in, w_out, conv_w, a_log, dt_bias, gdn_norm_g,
           idx_knorm_g, idx_knorm_b, rel_bias, w_ff1, w_ff2):
    bsz, seq, _ = x_prompt.shape
    nb, dec_seq, _ = x_sample.shape
    assert dec_seq == 1, "the sample path handles one new token per sequence"
    depth = w_in.shape[0]
    n_past = page_table.shape[1] * PAGE_SIZE
    n_sel_s = min(TOPK_MAX, (n_past + dec_seq) // 4)
    kvw = N_KV_B * HEAD_DIM
    tm_in, tn_in = min(1024, seq), 1024
    tm_out = min(512, seq)
    tf = 1024

    hp = x_prompt.reshape(bsz * seq, D_MODEL)
    hs = x_sample.reshape(nb, D_MODEL)
    c_all = jnp.concatenate([c_sample, c_prompt], axis=0)
    new_p, new_s = [], []
    for l in range(depth):
        m = _adaln_call(c_all, w_ada[l], b_ada[l])
        w_in_bf = _prep_w_in(w_in[l])
        w_out_bf = w_out[l].astype(BF16)
        w1_bf = w_ff1[l].astype(BF16)
        w2_bf = w_ff2[l].astype(BF16)

        proj = _inproj_call(hp, pre1_g[l], _ModSpec(m, False, nb, seq, tm_in), w_in_bf, tm_in, tn_in)
        k_p, v_p, ki_p, k_bf, va_bf, ki_bf = _kvprep_call(proj, idx_knorm_g[l], idx_knorm_b[l], tm_in, True)
        proj3 = proj.reshape(bsz, seq, NP)
        o_a, ssm_p = _gdn_prompt_call(proj3, conv_w[l], a_log[l], dt_bias[l], gdn_norm_g[l])
        o_b = _attn_prompt_call(proj3, ki_bf.reshape(bsz, seq, D_IDX), k_bf.reshape(bsz, seq, kvw),
                                va_bf.reshape(bsz, seq, 2 * kvw), rel_bias)
        mod_p = _ModSpec(m, False, nb, seq, tm_out)
        x1, h2 = _outproj_call(o_a.reshape(bsz * seq, VAL_DIM_A), o_b.reshape(bsz * seq, N_HEADS_B * HEAD_DIM), hp,
                               w_out_bf, post1_g[l], pre2_g[l], mod_p, tm_out)
        hp = _ffn_call(h2, x1, w1_bf, w2_bf, post2_g[l], mod_p, tm_out, tf)
        conv_p = proj3[:, seq - (CONV_W - 1):, OFF_CONV:OFF_CONV + CONV_DIM]
        new_p.append((k_p.reshape(bsz, seq, N_KV_B, HEAD_DIM), v_p.reshape(bsz, seq, N_KV_B, HEAD_DIM),
                      ki_p.reshape(bsz, seq, D_IDX), ssm_p, conv_p))

        mod_s = _ModSpec(m, True, 0, 1, nb)
        proj_s = _inproj_call(hs, pre1_g[l], mod_s, w_in_bf, nb, tn_in)
        k_s, v_s, ki_s = _kvprep_call(proj_s, idx_knorm_g[l], idx_knorm_b[l], nb)
        o_a_s, ssm_s, conv_s = _gdn_sample_call(proj_s.reshape(nb, 1, NP), state_conv[l], state_ssm[l], conv_w[l],
                                                a_log[l], dt_bias[l], gdn_norm_g[l])
        qi2 = proj_s[:, OFF_QI:OFF_QI + N_IDX_HEADS * D_IDX]
        wi2 = proj_s[:, OFF_SM + SM_W:OFF_SM + SM_W + N_IDX_HEADS]
        scores = _idx_sample_call(page_table, qi2.reshape(nb, N_IDX_HEADS, D_IDX), wi2.reshape(nb, N_IDX_HEADS, 1),
                                  cache_kidx[l])
        idx3 = _topk_sample_call(scores.reshape(nb, n_past), qi2, wi2, ki_s, n_sel_s)
        q3 = proj_s[:, OFF_QB:OFF_QB + N_HEADS_B * HEAD_DIM].reshape(nb, N_HEADS_B, HEAD_DIM)
        o_b_s = _attn_sample_call(idx3, page_table, q3, k_s, v_s, rel_bias, cache_k[l], cache_v[l])
        x1s, h2s = _outproj_call(o_a_s.reshape(nb, VAL_DIM_A), o_b_s.reshape(nb, N_HEADS_B * HEAD_DIM), hs,
                                 w_out_bf, post1_g[l], pre2_g[l], mod_s, nb)
        hs = _ffn_call(h2s, x1s, w1_bf, w2_bf, post2_g[l], mod_s, nb, tf)
        new_s.append((k_s.reshape(nb, 1, N_KV_B, HEAD_DIM), v_s.reshape(nb, 1, N_KV_B, HEAD_DIM),
                      ki_s.reshape(nb, 1, D_IDX), ssm_s, conv_s))

    p_k, p_v, p_kidx, p_ssm, p_conv = [jnp.stack([st[i] for st in new_p]) for i in range(5)]
    s_k, s_v, s_kidx, s_ssm, s_conv = [jnp.stack([st[i] for st in new_s]) for i in range(5)]
    return (hp.reshape(bsz, seq, D_MODEL), hs.reshape(nb, 1, D_MODEL), p_k, p_v, p_kidx, p_ssm, p_conv,
            s_k, s_v, s_kidx, s_ssm, s_conv)
```

```python
import functools
import math

import jax
import jax.numpy as jnp
from jax import lax
from jax.experimental import pallas as pl
from jax.experimental.pallas import tpu as pltpu

F32 = jnp.float32
BF16 = jnp.bfloat16
I32 = jnp.int32

D_MODEL = 2048
PAGE_SIZE = 128
HEAD_DIM = 128
N_HEADS_A = 8
KEY_DIM_A = N_HEADS_A * HEAD_DIM
VAL_DIM_A = N_HEADS_A * HEAD_DIM
CONV_DIM = 2 * KEY_DIM_A + VAL_DIM_A
CONV_W = 4
N_HEADS_B = 8
N_KV_B = 2
GQA_GROUP = N_HEADS_B // N_KV_B
N_IDX_HEADS = 16
D_IDX = 128
TOPK_MAX = 256
N_BUCKETS = 32
MAX_DISTANCE = 128
D_FF = 4 * D_MODEL
EPS = 1e-6

SPLIT_SIZES = (CONV_DIM, N_HEADS_A, N_HEADS_A, VAL_DIM_A, N_HEADS_B * HEAD_DIM, N_KV_B * HEAD_DIM,
               N_KV_B * HEAD_DIM, N_IDX_HEADS * D_IDX, N_IDX_HEADS, D_IDX)

OFF_CONV = 0
OFF_Z = OFF_CONV + CONV_DIM
OFF_QI = OFF_Z + VAL_DIM_A
OFF_QB = OFF_QI + N_IDX_HEADS * D_IDX
OFF_KB = OFF_QB + N_HEADS_B * HEAD_DIM
OFF_VB = OFF_KB + N_KV_B * HEAD_DIM
OFF_KI = OFF_VB + N_KV_B * HEAD_DIM
OFF_SM = OFF_KI + D_IDX
NP = 8192
SM_A, SM_B, SM_W = 0, N_HEADS_A, 2 * N_HEADS_A

LANE = 128
VMEM_LIMIT = 56 * 1024 * 1024
INT_MIN = -(2 ** 31)
NEG_BIG = -1e30

GDN_CHUNK = 128
GDN_GROUP = 256
ATT_TQ = 256
ATT_TK = 256


def _cparams(sem):
    return pltpu.CompilerParams(dimension_semantics=sem, vmem_limit_bytes=VMEM_LIMIT)


def _sigmoid(x):
    return 1.0 / (1.0 + jnp.exp(-x))


def _softplus(x):
    return jnp.maximum(x, 0.0) + jnp.log(1.0 + jnp.exp(-jnp.abs(x)))


def _dot(a, b):
    return jnp.dot(a, b, preferred_element_type=F32)


def _dot_nt(a, b):
    return lax.dot_general(a, b, (((1,), (1,)), ((), ())), preferred_element_type=F32)


def _dot_tn(a, b):
    return lax.dot_general(a, b, (((0,), (0,)), ((), ())), preferred_element_type=F32)


def _mod(ref):
    v = ref[...]
    return v[0] if v.ndim == 3 else v


def _adaln_kernel(c_ref, w_ref, b_ref, o_ref):
    c = c_ref[...]
    s = (c * _sigmoid(c)).astype(BF16)
    o_ref[...] = _dot(s, w_ref[...].astype(BF16)) + b_ref[...]


def _adaln_call(c_all, w_ada, b_ada):
    n_rows = c_all.shape[0]
    n_out = w_ada.shape[1]
    tn = 1024
    return pl.pallas_call(
        _adaln_kernel,
        out_shape=jax.ShapeDtypeStruct((n_rows, n_out), F32),
        grid=(n_out // tn,),
        in_specs=[pl.BlockSpec((n_rows, D_MODEL), lambda j: (0, 0)),
                  pl.BlockSpec((D_MODEL, tn), lambda j: (0, j)),
                  pl.BlockSpec((1, tn), lambda j: (0, j))],
        out_specs=pl.BlockSpec((n_rows, tn), lambda j: (0, j)),
        compiler_params=_cparams(("arbitrary",)),
        name="adaln",
    )(c_all, w_ada, b_ada.reshape(1, n_out))


class _ModSpec:
    def __init__(self, m2, per_row, row_off, rows_per_batch, tm):
        self.per_row = per_row
        self.arr = m2 if per_row else m2.reshape(m2.shape[0], 1, m2.shape[1])
        self.row_off = row_off
        self.blocks_per_batch = None if per_row else rows_per_batch // tm
        self.tm = tm

    def spec(self, k, grid_rank):
        if self.per_row:
            if grid_rank == 1:
                return pl.BlockSpec((self.tm, D_MODEL), lambda i: (0, k))
            return pl.BlockSpec((self.tm, D_MODEL), lambda i, j: (0, k))
        bpb, off = self.blocks_per_batch, self.row_off
        if grid_rank == 1:
            return pl.BlockSpec((1, 1, D_MODEL), lambda i: (off + i // bpb, 0, k))
        return pl.BlockSpec((1, 1, D_MODEL), lambda i, j: (off + i // bpb, 0, k))


def _inproj_kernel(x_ref, g_ref, sh_ref, sc_ref, w_ref, o_ref, h_scr):
    @pl.when(pl.program_id(1) == 0)
    def _():
        x = x_ref[...]
        y = x * lax.rsqrt(jnp.mean(x * x, axis=-1, keepdims=True) + EPS) * g_ref[...]
        h_scr[...] = (y * (1.0 + _mod(sc_ref)) + _mod(sh_ref)).astype(BF16)

    o_ref[...] = _dot(h_scr[...], w_ref[...])


def _inproj_call(x2d, g, mod, w_bf, tm, tn):
    m_rows = x2d.shape[0]
    n_cols = w_bf.shape[1]
    return pl.pallas_call(
        _inproj_kernel,
        out_shape=jax.ShapeDtypeStruct((m_rows, n_cols), F32),
        grid=(m_rows // tm, n_cols // tn),
        in_specs=[pl.BlockSpec((tm, D_MODEL), lambda i, j: (i, 0)),
                  pl.BlockSpec((1, D_MODEL), lambda i, j: (0, 0)),
                  mod.spec(0, 2), mod.spec(1, 2),
                  pl.BlockSpec((D_MODEL, tn), lambda i, j: (0, j))],
        out_specs=pl.BlockSpec((tm, tn), lambda i, j: (i, j)),
        scratch_shapes=[pltpu.VMEM((tm, D_MODEL), BF16)],
        compiler_params=_cparams(("arbitrary", "arbitrary")),
        name="inproj",
    )(x2d, g.reshape(1, D_MODEL), mod.arr, mod.arr, w_bf)


def _kvprep_kernel(kb_ref, vb_ref, ki_ref, lg_ref, lb_ref, k_ref, v_ref, kidx_ref, *bf_refs):
    kb = kb_ref[...]
    vb = vb_ref[...]
    for n in range(N_KV_B):
        k_ref[:, n, :] = kb[:, n * HEAD_DIM:(n + 1) * HEAD_DIM]
        v_ref[:, n, :] = vb[:, n * HEAD_DIM:(n + 1) * HEAD_DIM]
    x = ki_ref[...]
    mu = jnp.mean(x, axis=-1, keepdims=True)
    xc = x - mu
    var = jnp.mean(xc * xc, axis=-1, keepdims=True)
    ki = xc * lax.rsqrt(var + EPS) * lg_ref[...] + lb_ref[...]
    kidx_ref[...] = ki
    if bf_refs:
        kbf_ref, va_ref, kibf_ref = bf_refs
        kbf_ref[...] = kb.astype(BF16)
        kibf_ref[...] = ki.astype(BF16)
        hd = HEAD_DIM
        ones_col = jnp.where(lax.broadcasted_iota(I32, (vb.shape[0], hd), 1) == 0, 1.0, 0.0)
        va_ref[...] = jnp.concatenate([piece for n in range(N_KV_B) for piece in (vb[:, n * hd:(n + 1) * hd], ones_col)],
                                      axis=1).astype(BF16)


def _kvprep_call(proj, ln_g, ln_b, tm, with_bf16=False):
    m_rows = proj.shape[0]
    kvw = N_KV_B * HEAD_DIM
    row = lambda i: (i, 0)
    kv_shape = jax.ShapeDtypeStruct((m_rows, N_KV_B, HEAD_DIM), F32)
    kv_spec = pl.BlockSpec((tm, N_KV_B, HEAD_DIM), lambda i: (i, 0, 0))
    out_shape = [kv_shape, kv_shape, jax.ShapeDtypeStruct((m_rows, D_IDX), F32)]
    out_specs = [kv_spec, kv_spec, pl.BlockSpec((tm, D_IDX), row)]
    if with_bf16:
        out_shape += [jax.ShapeDtypeStruct((m_rows, kvw), BF16), jax.ShapeDtypeStruct((m_rows, 2 * kvw), BF16),
                      jax.ShapeDtypeStruct((m_rows, D_IDX), BF16)]
        out_specs += [pl.BlockSpec((tm, kvw), row), pl.BlockSpec((tm, 2 * kvw), row), pl.BlockSpec((tm, D_IDX), row)]
    return pl.pallas_call(
        _kvprep_kernel,
        out_shape=tuple(out_shape),
        grid=(m_rows // tm,),
        in_specs=[pl.BlockSpec((tm, kvw), lambda i: (i, OFF_KB // kvw)),
                  pl.BlockSpec((tm, kvw), lambda i: (i, OFF_VB // kvw)),
                  pl.BlockSpec((tm, D_IDX), lambda i: (i, OFF_KI // D_IDX)),
                  pl.BlockSpec((1, D_IDX), lambda i: (0, 0)),
                  pl.BlockSpec((1, D_IDX), lambda i: (0, 0))],
        out_specs=tuple(out_specs),
        compiler_params=_cparams(("arbitrary",)),
        name="kvprep",
    )(proj, proj, proj, ln_g.reshape(1, D_IDX), ln_b.reshape(1, D_IDX))


def _conv_silu(x, cw, row):
    def taps(xx, masked):
        y = xx * cw[CONV_W - 1:CONV_W]
        for s in range(1, CONV_W):
            xs = pltpu.roll(xx, s, 0)
            if masked:
                xs = jnp.where(row[:8] >= s, xs, 0.0)
            y = y + xs * cw[CONV_W - 1 - s:CONV_W - s]
        return y

    y = jnp.concatenate([taps(x[:8], True), taps(x, False)[8:]], axis=0)
    return y * _sigmoid(y)


def _l2norm(x):
    return x * lax.rsqrt(jnp.sum(x * x, axis=-1, keepdims=True) + EPS)


def _gdn_prompt_kernel(q_ref, k_ref, v_ref, z_ref, sm_ref, cwq_ref, cwk_ref, cwv_ref, alog_ref, dtb_ref, ng_ref,
                       o_ref, s_out_ref,
                       qn_s, kn_s, vn_s, gc_s, bt_s, l_s, a_s, x_s, u_s, w_s, mn_s, sts_s):
    seq = q_ref.shape[1]
    hd = HEAD_DIM
    grp = GDN_GROUP
    chunk = GDN_CHUNK
    n_grp = seq // grp
    n_chunk = seq // chunk
    cpg = grp // chunk
    h = pl.program_id(1)

    row = lax.broadcasted_iota(I32, (seq, hd), 0)
    lane = lax.broadcasted_iota(I32, (seq, hd), 1)

    qn_s[...] = _l2norm(_conv_silu(q_ref[0], cwq_ref[...], row)) * (hd ** -0.5)
    kn_s[...] = _l2norm(_conv_silu(k_ref[0], cwk_ref[...], row))
    vn_s[...] = _conv_silu(v_ref[0], cwv_ref[...], row)

    sm = sm_ref[0]
    g_all = -jnp.exp(alog_ref[...]) * _softplus(sm + dtb_ref[...])
    b_all = _sigmoid(sm)
    g_col = jnp.sum(jnp.where(lane == h + SM_A, g_all, 0.0), axis=1, keepdims=True)
    b_col = jnp.sum(jnp.where(lane == h + SM_B, b_all, 0.0), axis=1, keepdims=True)
    gc = jnp.broadcast_to(g_col, (seq, hd))
    pos = row & (chunk - 1)
    s = 1
    while s < chunk:
        gc = gc + jnp.where(pos >= s, pltpu.roll(gc, s, 0), 0.0)
        s *= 2
    gc_s[...] = gc
    bt_s[...] = jnp.broadcast_to(b_col, (seq, hd))

    ii = lax.broadcasted_iota(I32, (grp, grp), 0)
    jj = lax.broadcasted_iota(I32, (grp, grp), 1)
    xr = ii ^ jj
    same = (xr >> int(math.log2(chunk))) == 0
    causal = same & (ii >= jj)
    strict = same & (ii > jj)
    eye = (ii == jj).astype(F32)

    def build(gi, _):
        r0 = pl.multiple_of(gi * grp, grp)
        kk = kn_s[pl.ds(r0, grp), :]
        qq = qn_s[pl.ds(r0, grp), :]
        bt = bt_s[pl.ds(r0, grp), :]
        gcg = gc_s[pl.ds(r0, grp), :]
        kb = (kk * bt).astype(BF16)
        kkb = kk.astype(BF16)
        kkt = _dot_nt(kb, kkb)
        qkt = _dot_nt(qq.astype(BF16), kkb)
        colb = jnp.concatenate([gcg, gcg], axis=1)
        diff = colb - colb.T
        decay = jnp.where(causal, jnp.exp(jnp.where(causal, diff, 0.0)), 0.0)
        lmat = jnp.where(strict, kkt * decay, 0.0)
        l_s[gi] = lmat
        a_s[pl.ds(r0, grp), :] = jnp.where(causal, qkt * decay, 0.0)
        x_s[gi] = eye - jnp.where(xr == 1, lmat, 0.0)
        return 0

    lax.fori_loop(0, n_grp, build, 0, unroll=2)

    for lvl in range(1, int(math.log2(chunk))):
        def level(gi, _, lvl=lvl):
            xm = x_s[gi]
            cl = jnp.where((xr >> lvl) == 1, l_s[gi], 0.0).astype(BF16)
            xb = xm.astype(BF16)
            x_s[gi] = xm - _dot(_dot(xb, cl).astype(BF16), xb)
            return 0

        lax.fori_loop(0, n_grp, level, 0, unroll=True)

    def apply(gi, _):
        r0 = pl.multiple_of(gi * grp, grp)
        kk = kn_s[pl.ds(r0, grp), :]
        vv = vn_s[pl.ds(r0, grp), :]
        bt = bt_s[pl.ds(r0, grp), :]
        gcg = gc_s[pl.ds(r0, grp), :]
        rhs = jnp.concatenate([vv * bt, kk * bt * jnp.exp(gcg)], axis=1).astype(BF16)
        uw = _dot(x_s[gi].astype(BF16), rhs)
        u_s[pl.ds(r0, grp), :] = uw[:, :hd]
        w_s[pl.ds(r0, grp), :] = uw[:, hd:]
        for cc in range(cpg):
            lo = cc * chunk
            g_last = gcg[lo + chunk - 1:lo + chunk, :]
            k_dec = (kk[lo:lo + chunk] * jnp.exp(g_last - gcg[lo:lo + chunk])).astype(BF16)
            wu = jnp.concatenate([uw[lo:lo + chunk, hd:], uw[lo:lo + chunk, :hd]], axis=1).astype(BF16)
            mn_s[gi * cpg + cc] = _dot_tn(k_dec, wu)
        return 0

    lax.fori_loop(0, n_grp, apply, 0, unroll=2)

    def scan(c, st):
        sts_s[c] = st
        g_last = gc_s[pl.ds(c * chunk + chunk - 1, 1), :]
        mn = mn_s[c]
        return st * jnp.exp(g_last) - _dot(mn[:, :hd].astype(BF16), st.astype(BF16)) + mn[:, hd:]

    s_out_ref[0, 0] = lax.fori_loop(0, n_chunk, scan, jnp.zeros((hd, hd), F32))

    ng = ng_ref[...]

    def emit(gi, _):
        r0 = pl.multiple_of(gi * grp, grp)
        gcg = gc_s[pl.ds(r0, grp), :]
        qg = qn_s[pl.ds(r0, grp), :] * jnp.exp(gcg)
        wg = w_s[pl.ds(r0, grp), :]
        ws, qs = [], []
        for cc in range(cpg):
            lo = cc * chunk
            stb = sts_s[gi * cpg + cc].astype(BF16)
            both = _dot(jnp.concatenate([wg[lo:lo + chunk], qg[lo:lo + chunk]], axis=0).astype(BF16), stb)
            ws.append(both[:chunk])
            qs.append(both[chunk:])
        v_new = u_s[pl.ds(r0, grp), :] - jnp.concatenate(ws, axis=0)
        o = jnp.concatenate(qs, axis=0) + _dot(a_s[pl.ds(r0, grp), :].astype(BF16), v_new.astype(BF16))
        zc = z_ref[0, pl.ds(r0, grp), :]
        on = o * lax.rsqrt(jnp.mean(o * o, axis=-1, keepdims=True) + EPS) * ng
        o_ref[0, pl.ds(r0, grp), :] = (on * (zc * _sigmoid(zc))).astype(o_ref.dtype)
        return 0

    lax.fori_loop(0, n_grp, emit, 0, unroll=2)


def _lane_vec(v, off):
    return jnp.zeros((1, LANE), F32).at[0, off:off + v.shape[0]].set(v.astype(F32))


def _gdn_prompt_call(proj3, conv_w, a_log, dt_bias, norm_g):
    bsz, seq, _ = proj3.shape
    hd = HEAD_DIM
    nh = N_HEADS_A
    n_grp = seq // GDN_GROUP
    col = lambda base: (lambda b, h: (b, 0, base // hd + h))
    cw = lambda base: (lambda b, h: (0, base // hd + h))
    vec = pl.BlockSpec((1, LANE), lambda b, h: (0, 0))
    return pl.pallas_call(
        _gdn_prompt_kernel,
        out_shape=(jax.ShapeDtypeStruct((bsz, seq, VAL_DIM_A), BF16),
                   jax.ShapeDtypeStruct((bsz, nh, hd, hd), F32)),
        grid=(bsz, nh),
        in_specs=[pl.BlockSpec((1, seq, hd), col(OFF_CONV)),
                  pl.BlockSpec((1, seq, hd), col(OFF_CONV + KEY_DIM_A)),
                  pl.BlockSpec((1, seq, hd), col(OFF_CONV + 2 * KEY_DIM_A)),
                  pl.BlockSpec((1, seq, hd), col(OFF_Z)),
                  pl.BlockSpec((1, seq, LANE), lambda b, h: (b, 0, OFF_SM // LANE)),
                  pl.BlockSpec((CONV_W, hd), cw(0)),
                  pl.BlockSpec((CONV_W, hd), cw(KEY_DIM_A)),
                  pl.BlockSpec((CONV_W, hd), cw(2 * KEY_DIM_A)),
                  vec, vec, vec],
        out_specs=(pl.BlockSpec((1, seq, hd), lambda b, h: (b, 0, h)),
                   pl.BlockSpec((1, 1, hd, hd), lambda b, h: (b, h, 0, 0))),
        scratch_shapes=[pltpu.VMEM((seq, hd), F32), pltpu.VMEM((seq, hd), F32), pltpu.VMEM((seq, hd), F32),
                        pltpu.VMEM((seq, hd), F32), pltpu.VMEM((seq, hd), F32),
                        pltpu.VMEM((n_grp, GDN_GROUP, GDN_GROUP), F32),
                        pltpu.VMEM((seq, GDN_GROUP), F32),
                        pltpu.VMEM((n_grp, GDN_GROUP, GDN_GROUP), F32),
                        pltpu.VMEM((seq, hd), F32), pltpu.VMEM((seq, hd), F32),
                        pltpu.VMEM((seq // GDN_CHUNK, hd, 2 * hd), F32), pltpu.VMEM((seq // GDN_CHUNK, hd, hd), F32)],
        compiler_params=_cparams(("arbitrary", "arbitrary")),
        name="gdn_prompt",
    )(proj3, proj3, proj3, proj3, proj3, conv_w, conv_w, conv_w,
      _lane_vec(a_log, SM_A), _lane_vec(dt_bias, SM_A), norm_g.reshape(1, hd).astype(F32))


def _t5_bucket(dist):
    n = jnp.maximum(dist, 0)
    max_exact = N_BUCKETS // 2
    nf = jnp.maximum(n, 1).astype(F32)
    large = max_exact + (jnp.log(nf / max_exact) / math.log(MAX_DISTANCE / max_exact)
                         * (N_BUCKETS - max_exact)).astype(I32)
    large = jnp.minimum(large, N_BUCKETS - 1)
    return jnp.where(n < max_exact, n, large)


def _bias_from_bucket(bucket, rb_ref, head):
    out = jnp.zeros(bucket.shape, F32)
    for b in range(N_BUCKETS):
        out = jnp.where(bucket == b, rb_ref[b, head], out)
    return out


def _order_key(score):
    bits = pltpu.bitcast(score + 0.0, I32)
    return bits ^ ((bits >> 31) & 0x7FFFFFFF)


I16 = jnp.int16
HALF_BIAS = 1 << 15
HALF_ROWS = 16
KEY_POS_INF = 0x7F800000
KEY_NEG_INF = INT_MIN + 0x7FFFFF


def _kth_largest_key(count_ge, shape, n_sel):
    def body(i, tau):
        bit = 31 - i
        cand = jnp.where(i == 0, tau ^ INT_MIN, tau | (1 << bit))
        return jnp.where(count_ge(cand) >= n_sel, cand, tau)

    return lax.fori_loop(0, 32, body, jnp.full(shape, INT_MIN, I32))


def _attn_prompt_kernel(n_sel, qi_ref, sm_ref, qb_ref, ki_ref, k_ref, va_ref, rb_ref, o_ref,
                        qi_s, qb_s, wt_s, keys_s, hi_s, lo_s, tau_s, bias_s, m_s, acc_s):
    tq, tk = ATT_TQ, ATT_TK
    hd = HEAD_DIM
    bi = pl.program_id(0)
    qblk = pl.program_id(1)
    n_kb = (qblk * tq) // tk + 1
    krow = lax.broadcasted_iota(I32, (tk, tq), 0)
    qcol = lax.broadcasted_iota(I32, (tk, tq), 1)

    @pl.when((bi == 0) & (qblk == 0))
    def _():
        for off in range(bias_s.shape[1]):
            bucket = _t5_bucket(off * tk + krow - qcol)
            for h in range(N_HEADS_B):
                bias_s[h, off] = _bias_from_bucket(bucket, rb_ref, h)

    qi_s[...] = qi_ref[0].astype(BF16)
    qb_s[...] = qb_ref[0].astype(BF16)
    wt_s[...] = sm_ref[0].T * (N_IDX_HEADS ** -0.5)
    t_pos = qblk * tq + qcol

    def idx_body(j, _):
        k0 = pl.multiple_of(j * tk, tk)
        kt = ki_ref[0, pl.ds(k0, tk), :]
        acc = jnp.zeros((tk, tq), F32)
        for h in range(N_IDX_HEADS):
            s = _dot_nt(kt, qi_s[:, h * D_IDX:(h + 1) * D_IDX])
            acc = acc + jnp.maximum(s, 0.0) * wt_s[SM_W + h:SM_W + h + 1, :]
        acc = acc * (D_IDX ** -0.5)
        key = jnp.where(k0 + krow <= t_pos, _order_key(acc), INT_MIN)
        keys_s[j] = key
        hi_s[j] = (key >> 16).astype(I16)
        lo_s[j] = ((key & 0xFFFF) - HALF_BIAS).astype(I16)
        return 0

    lax.fori_loop(0, n_kb, idx_body, 0)

    def kth_largest_half(half_s):
        def count_ge(cand):
            def cb(j, c):
                hit = jnp.where(half_s[j] >= cand, jnp.int16(1), jnp.int16(0))
                for r in range(tk // HALF_ROWS):
                    c = c + hit[r * HALF_ROWS:(r + 1) * HALF_ROWS]
                return c

            c = lax.fori_loop(0, n_kb, cb, jnp.zeros((HALF_ROWS, tq), I16))
            return jnp.sum(c.astype(I32), axis=0, keepdims=True)

        def body(i, t):
            cand = jnp.where(i == 0, t ^ (-HALF_BIAS), t | (1 << (15 - i)))
            return jnp.where(count_ge(cand.astype(I16)) >= n_sel, cand, t)

        return lax.fori_loop(0, 16, body, jnp.full((1, tq), -HALF_BIAS, I32))

    tau_hi = kth_largest_half(hi_s)
    tau_hi16 = tau_hi.astype(I16)

    def narrow(j, _):
        hi = hi_s[j]
        lo_s[j] = jnp.where(hi > tau_hi16, jnp.int16(HALF_BIAS - 1),
                            jnp.where(hi == tau_hi16, lo_s[j], jnp.int16(-HALF_BIAS)))
        return 0

    lax.fori_loop(0, n_kb, narrow, 0)
    tau_lo = kth_largest_half(lo_s)
    tau = (tau_hi << 16) | ((tau_lo + HALF_BIAS) & 0xFFFF)
    tau = jnp.maximum(tau, KEY_NEG_INF + 1)

    scale = hd ** -0.5
    lane_reps = tk // LANE
    tau_rep = jnp.broadcast_to(tau, (LANE, tq)).T
    tau_s[...] = jnp.concatenate([tau_rep] * lane_reps, axis=1)

    def block_inputs(j):
        kq = keys_s[j].T
        sel = jnp.where(kq < KEY_POS_INF, kq, INT_MIN) >= tau_s[...]
        return k_ref[0, pl.ds(pl.multiple_of(j * tk, tk), tk), :], sel

    def masked_logits(j, h, kt, sel):
        n = h // GQA_GROUP
        boff = jnp.minimum(qblk - j, bias_s.shape[1] - 1)
        lg = _dot_nt(qb_s[:, h * hd:(h + 1) * hd], kt[:, n * hd:(n + 1) * hd]) * scale + bias_s[h, boff]
        return jnp.where(sel, lg, NEG_BIG)

    m_s[...] = jnp.full(m_s.shape, NEG_BIG, F32)

    def max_body(j, _):
        kt, sel = block_inputs(j)
        for h in range(N_HEADS_B):
            lg = masked_logits(j, h, kt, sel)
            part = lg[:, :LANE]
            for r in range(1, lane_reps):
                part = jnp.maximum(part, lg[:, r * LANE:(r + 1) * LANE])
            m_s[h] = jnp.maximum(m_s[h], part)
        return 0

    lax.fori_loop(0, n_kb, max_body, 0)
    for h in range(N_HEADS_B):
        m_s[h] = jnp.broadcast_to(jnp.max(m_s[h], axis=1, keepdims=True), (tq, LANE))
    acc_s[...] = jnp.zeros(acc_s.shape, F32)

    def sum_body(j, _):
        kt, sel = block_inputs(j)
        va = va_ref[0, pl.ds(pl.multiple_of(j * tk, tk), tk), :]
        for h in range(N_HEADS_B):
            n = h // GQA_GROUP
            mrow = jnp.concatenate([m_s[h]] * lane_reps, axis=1)
            p = jnp.exp(masked_logits(j, h, kt, sel) - mrow)
            acc_s[h] += _dot(p.astype(BF16), va[:, n * 2 * hd:(n + 1) * 2 * hd])
        return 0

    lax.fori_loop(0, n_kb, sum_body, 0)
    for h in range(N_HEADS_B):
        a = acc_s[h]
        o_ref[0, :, h * hd:(h + 1) * hd] = (a[:, :hd] / a[:, hd:hd + 1]).astype(o_ref.dtype)


def _attn_prompt_call(proj3, ki_bf, k_bf, va_bf, rel_bias):
    bsz, seq, _ = proj3.shape
    tq, tk = ATT_TQ, ATT_TK
    assert tq == tk, "the bias-tile offsets assume square blocks"
    n_sel = min(TOPK_MAX, seq // 4)
    qiw = N_IDX_HEADS * D_IDX
    qbw = N_HEADS_B * HEAD_DIM
    kvw = N_KV_B * HEAD_DIM
    n_off = -(-(MAX_DISTANCE + tk - 1) // tk) + 1
    return pl.pallas_call(
        functools.partial(_attn_prompt_kernel, n_sel),
        out_shape=jax.ShapeDtypeStruct((bsz, seq, qbw), BF16),
        grid=(bsz, seq // tq),
        in_specs=[pl.BlockSpec((1, tq, qiw), lambda b, i: (b, i, OFF_QI // qiw)),
                  pl.BlockSpec((1, tq, LANE), lambda b, i: (b, i, OFF_SM // LANE)),
                  pl.BlockSpec((1, tq, qbw), lambda b, i: (b, i, OFF_QB // qbw)),
                  pl.BlockSpec((1, seq, D_IDX), lambda b, i: (b, 0, 0)),
                  pl.BlockSpec((1, seq, kvw), lambda b, i: (b, 0, 0)),
                  pl.BlockSpec((1, seq, 2 * kvw), lambda b, i: (b, 0, 0)),
                  pl.BlockSpec(memory_space=pltpu.SMEM)],
        out_specs=pl.BlockSpec((1, tq, qbw), lambda b, i: (b, i, 0)),
        scratch_shapes=[pltpu.VMEM((tq, qiw), BF16), pltpu.VMEM((tq, qbw), BF16),
                        pltpu.VMEM((LANE, tq), F32),
                        pltpu.VMEM((seq // tk, tk, tq), I32),
                        pltpu.VMEM((seq // tk, tk, tq), I16), pltpu.VMEM((seq // tk, tk, tq), I16),
                        pltpu.VMEM((tq, tk), I32),
                        pltpu.VMEM((N_HEADS_B, n_off, tq, tk), F32),
                        pltpu.VMEM((N_HEADS_B, tq, LANE), F32),
                        pltpu.VMEM((N_HEADS_B, tq, 2 * HEAD_DIM), F32)],
        compiler_params=_cparams(("arbitrary", "arbitrary")),
        name="attn_prompt",
    )(proj3, proj3, proj3, ki_bf, k_bf, va_bf, rel_bias.astype(F32))


def _rms(x, g):
    return x * lax.rsqrt(jnp.mean(x * x, axis=-1, keepdims=True) + EPS) * g


def _outproj_kernel(oa_ref, ob_ref, x_ref, wa_ref, wb_ref, pg_ref, g1_ref, p2_ref, sh2_ref, sc2_ref, x1_ref, h2_ref):
    mix = _dot(oa_ref[...], wa_ref[...]) + _dot(ob_ref[...], wb_ref[...])
    x1 = x_ref[...] + _mod(g1_ref) * _rms(mix, pg_ref[...])
    x1_ref[...] = x1
    h2_ref[...] = (_rms(x1, p2_ref[...]) * (1.0 + _mod(sc2_ref)) + _mod(sh2_ref)).astype(BF16)


def _outproj_call(oa, ob, x2d, w_out_bf, post1_g, pre2_g, mod, tm):
    m_rows = x2d.shape[0]
    half = w_out_bf.shape[0] // 2
    row = lambda i: (i, 0)
    vec = pl.BlockSpec((1, D_MODEL), lambda i: (0, 0))
    return pl.pallas_call(
        _outproj_kernel,
        out_shape=(jax.ShapeDtypeStruct((m_rows, D_MODEL), F32), jax.ShapeDtypeStruct((m_rows, D_MODEL), BF16)),
        grid=(m_rows // tm,),
        in_specs=[pl.BlockSpec((tm, half), row), pl.BlockSpec((tm, half), row), pl.BlockSpec((tm, D_MODEL), row),
                  pl.BlockSpec((half, D_MODEL), lambda i: (0, 0)), pl.BlockSpec((half, D_MODEL), lambda i: (1, 0)),
                  vec, mod.spec(2, 1), vec, mod.spec(3, 1), mod.spec(4, 1)],
        out_specs=(pl.BlockSpec((tm, D_MODEL), row), pl.BlockSpec((tm, D_MODEL), row)),
        compiler_params=_cparams(("arbitrary",)),
        name="outproj",
    )(oa, ob, x2d, w_out_bf, w_out_bf, post1_g.reshape(1, D_MODEL), mod.arr, pre2_g.reshape(1, D_MODEL),
      mod.arr, mod.arr)


def _ffn_kernel(h_ref, x1_ref, w1_ref, w2_ref, pg_ref, g2_ref, o_ref, acc_s):
    kk = pl.program_id(1)

    @pl.when(kk == 0)
    def _():
        acc_s[...] = jnp.zeros_like(acc_s)

    a = jnp.maximum(_dot(h_ref[...], w1_ref[...]), 0.0)
    acc_s[...] += _dot((a * a).astype(BF16), w2_ref[...])

    @pl.when(kk == pl.num_programs(1) - 1)
    def _():
        o_ref[...] = x1_ref[...] + _mod(g2_ref) * _rms(acc_s[...], pg_ref[...])


def _ffn_call(h2, x1, w1_bf, w2_bf, post2_g, mod, tm, tf):
    m_rows = x1.shape[0]
    return pl.pallas_call(
        _ffn_kernel,
        out_shape=jax.ShapeDtypeStruct((m_rows, D_MODEL), F32),
        grid=(m_rows // tm, D_FF // tf),
        in_specs=[pl.BlockSpec((tm, D_MODEL), lambda i, k: (i, 0)),
                  pl.BlockSpec((tm, D_MODEL), lambda i, k: (i, 0)),
                  pl.BlockSpec((D_MODEL, tf), lambda i, k: (0, k)),
                  pl.BlockSpec((tf, D_MODEL), lambda i, k: (k, 0)),
                  pl.BlockSpec((1, D_MODEL), lambda i, k: (0, 0)),
                  mod.spec(5, 2)],
        out_specs=pl.BlockSpec((tm, D_MODEL), lambda i, k: (i, 0)),
        scratch_shapes=[pltpu.VMEM((tm, D_MODEL), F32)],
        compiler_params=_cparams(("arbitrary", "arbitrary")),
        name="ffn",
    )(h2, x1, w1_bf, w2_bf, post2_g.reshape(1, D_MODEL), mod.arr)


def _gdn_sample_kernel(cin_ref, z_ref, sm_ref, sconv_ref, ssm_ref, cw_ref, alog_ref, dtb_ref, ng_ref,
                       o_ref, ssm_out_ref, conv_out_ref):
    hd = HEAD_DIM
    nh = N_HEADS_A
    prev = sconv_ref[0]
    xin = cin_ref[0]
    cw = cw_ref[...]
    y = xin * cw[CONV_W - 1:CONV_W]
    for s in range(CONV_W - 1):
        y = y + prev[s:s + 1] * cw[s:s + 1]
    y = y * _sigmoid(y)
    conv_out_ref[0, 0:CONV_W - 2, :] = prev[1:CONV_W - 1]
    conv_out_ref[0, CONV_W - 2:CONV_W - 1, :] = xin

    def heads(base):
        return jnp.concatenate([y[:, base + h * hd:base + (h + 1) * hd] for h in range(nh)], axis=0)

    q8 = _l2norm(heads(0)) * (hd ** -0.5)
    k8 = _l2norm(heads(KEY_DIM_A))
    v8 = heads(2 * KEY_DIM_A)
    sm = sm_ref[0]
    g_all = -jnp.exp(alog_ref[...]) * _softplus(sm + dtb_ref[...])
    b_all = _sigmoid(sm)
    kt = jnp.concatenate([k8, jnp.zeros((LANE - nh, hd), F32)], axis=0).T
    k8b = k8.astype(BF16)
    q8b = q8.astype(BF16)
    z = z_ref[0]
    ng = ng_ref[...]
    for h in range(nh):
        s1 = ssm_ref[0, h] * jnp.exp(g_all[:, SM_A + h:SM_A + h + 1])
        kv = _dot(k8b, s1.astype(BF16))[h:h + 1]
        delta = (v8[h:h + 1] - kv) * b_all[:, SM_B + h:SM_B + h + 1]
        s2 = s1 + kt[:, h:h + 1] * delta
        ssm_out_ref[0, h] = s2
        o = _dot(q8b, s2.astype(BF16))[h:h + 1]
        zh = z[:, h * hd:(h + 1) * hd]
        on = o * lax.rsqrt(jnp.mean(o * o, axis=-1, keepdims=True) + EPS) * ng
        o_ref[0, :, h * hd:(h + 1) * hd] = (on * (zh * _sigmoid(zh))).astype(o_ref.dtype)


def _gdn_sample_call(proj_s3, state_conv, state_ssm, conv_w, a_log, dt_bias, norm_g):
    nb = proj_s3.shape[0]
    hd, nh = HEAD_DIM, N_HEADS_A
    vec = pl.BlockSpec((1, LANE), lambda b: (0, 0))
    return pl.pallas_call(
        _gdn_sample_kernel,
        out_shape=(jax.ShapeDtypeStruct((nb, 1, VAL_DIM_A), BF16),
                   jax.ShapeDtypeStruct((nb, nh, hd, hd), F32),
                   jax.ShapeDtypeStruct((nb, CONV_W - 1, CONV_DIM), F32)),
        grid=(nb,),
        in_specs=[pl.BlockSpec((1, 1, CONV_DIM), lambda b: (b, 0, OFF_CONV // CONV_DIM)),
                  pl.BlockSpec((1, 1, VAL_DIM_A), lambda b: (b, 0, OFF_Z // VAL_DIM_A)),
                  pl.BlockSpec((1, 1, LANE), lambda b: (b, 0, OFF_SM // LANE)),
                  pl.BlockSpec((1, CONV_W - 1, CONV_DIM), lambda b: (b, 0, 0)),
                  pl.BlockSpec((1, nh, hd, hd), lambda b: (b, 0, 0, 0)),
                  pl.BlockSpec((CONV_W, CONV_DIM), lambda b: (0, 0)),
                  vec, vec, vec],
        out_specs=(pl.BlockSpec((1, 1, VAL_DIM_A), lambda b: (b, 0, 0)),
                   pl.BlockSpec((1, nh, hd, hd), lambda b: (b, 0, 0, 0)),
                   pl.BlockSpec((1, CONV_W - 1, CONV_DIM), lambda b: (b, 0, 0))),
        compiler_params=_cparams(("arbitrary",)),
        name="gdn_sample",
    )(proj_s3, proj_s3, proj_s3, state_conv, state_ssm, conv_w,
      _lane_vec(a_log, SM_A), _lane_vec(dt_bias, SM_A), norm_g.reshape(1, hd).astype(F32))


IDX_PAGES_PER_STEP = 32


def _idx_sample_kernel(pt_ref, qi_ref, wi_ref, *rest):
    del pt_ref
    pages = rest[:-1]
    o_ref = rest[-1]
    qb = qi_ref[0].astype(BF16)
    w = wi_ref[0] * (N_IDX_HEADS ** -0.5)
    for p, pg in enumerate(pages):
        s = _dot_nt(qb, pg[0].astype(BF16))
        sc = jnp.sum(jnp.maximum(s, 0.0) * w, axis=0, keepdims=True) * (D_IDX ** -0.5)
        o_ref[0, :, p * PAGE_SIZE:(p + 1) * PAGE_SIZE] = sc


def _idx_sample_call(page_table, qi3, wi3, cache_kidx):
    nb, n_pages = page_table.shape
    pp = min(IDX_PAGES_PER_STEP, n_pages)
    page_specs = [pl.BlockSpec((1, PAGE_SIZE, D_IDX), functools.partial(lambda b, j, pt, p: (pt[b, j * pp + p], 0, 0), p=p))
                  for p in range(pp)]
    grid_spec = pltpu.PrefetchScalarGridSpec(
        num_scalar_prefetch=1,
        grid=(nb, n_pages // pp),
        in_specs=[pl.BlockSpec((1, N_IDX_HEADS, D_IDX), lambda b, j, pt: (b, 0, 0)),
                  pl.BlockSpec((1, N_IDX_HEADS, 1), lambda b, j, pt: (b, 0, 0))] + page_specs,
        out_specs=pl.BlockSpec((1, 1, pp * PAGE_SIZE), lambda b, j, pt: (b, 0, j)),
    )
    return pl.pallas_call(
        _idx_sample_kernel,
        out_shape=jax.ShapeDtypeStruct((nb, 1, n_pages * PAGE_SIZE), F32),
        grid_spec=grid_spec,
        compiler_params=_cparams(("arbitrary", "arbitrary")),
        name="idx_sample",
    )(page_table, qi3, wi3, *([cache_kidx] * pp))


IDX_FLAG = 1 << 20


def _topk_sample_kernel(n_sel, sc_ref, qi_ref, wi_ref, knew_ref, idx_ref, keys_s, keyw_s, rank_s, pos_s):
    nb, n_past = sc_ref.shape
    n_blk = n_past // LANE
    lane = lax.broadcasted_iota(I32, (nb, LANE), 1)

    kn = knew_ref[...].astype(BF16)
    q = qi_ref[...]
    w = wi_ref[...] * (N_IDX_HEADS ** -0.5)
    s_new = jnp.zeros((nb, 1), F32)
    for h in range(N_IDX_HEADS):
        qh = q[:, h * D_IDX:(h + 1) * D_IDX].astype(BF16).astype(F32)
        dot = jnp.sum(qh * kn.astype(F32), axis=1, keepdims=True)
        s_new = s_new + jnp.maximum(dot, 0.0) * w[:, h:h + 1]
    s_new = s_new * (D_IDX ** -0.5)
    key_new = _order_key(s_new)

    def to_keys(j, _):
        keys_s[j] = _order_key(sc_ref[:, pl.ds(pl.multiple_of(j * LANE, LANE), LANE)])
        return 0

    lax.fori_loop(0, n_blk, to_keys, 0, unroll=8)
    keyw_s[...] = _order_key(sc_ref[...])
    wide = min(n_past, 16 * LANE)

    def count(pred):
        acc = jnp.zeros((nb, LANE), I32)
        for c in range(n_past // wide):
            hit = jnp.where(pred(keyw_s[:, c * wide:(c + 1) * wide]), 1, 0)
            for k in range(wide // LANE):
                acc = acc + hit[:, k * LANE:(k + 1) * LANE]
        return jnp.sum(acc, axis=1, keepdims=True)

    def count_ge(cand):
        return count(lambda k: k >= cand) + jnp.where(key_new >= cand, 1, 0)

    tau = jnp.maximum(_kth_largest_key(count_ge, (nb, 1), n_sel), INT_MIN + 1)

    quota = n_sel - (count(lambda k: k > tau) + jnp.where(key_new > tau, 1, 0))

    ri = lax.broadcasted_iota(I32, (LANE, LANE), 0)
    ci = lax.broadcasted_iota(I32, (LANE, LANE), 1)
    upper = (ri <= ci).astype(BF16)

    def excl_prefix(flag):
        return (_dot(flag.astype(BF16), upper) - flag).astype(I32)

    def rank_body(j, carry):
        base, base_eq = carry
        kj = keys_s[j]
        eq = kj == tau
        eqf = jnp.where(eq, 1.0, 0.0)
        take = (kj > tau) | (eq & (base_eq + excl_prefix(eqf) < quota))
        takef = jnp.where(take, 1.0, 0.0)
        rank_s[j] = jnp.where(take, base + excl_prefix(takef), -1)
        finite = (kj < KEY_POS_INF) & (kj > KEY_NEG_INF)
        pos = j * LANE + lane
        pos_s[j] = jnp.where(finite, pos, pos | IDX_FLAG)
        return (base + jnp.sum(takef, axis=1, keepdims=True).astype(I32),
                base_eq + jnp.sum(eqf, axis=1, keepdims=True).astype(I32))

    zero = jnp.zeros((nb, 1), I32)
    n_before_new, eq_before_new = lax.fori_loop(0, n_blk, rank_body, (zero, zero), unroll=4)

    slot = lax.broadcasted_iota(I32, (n_sel, LANE), 0)
    new_sel = (key_new > tau) | ((key_new == tau) & (eq_before_new < quota))
    new_fin = (key_new < KEY_POS_INF) & (key_new > KEY_NEG_INF)
    new_pos = jnp.where(new_fin, n_past, n_past | IDX_FLAG)

    def per_seq(b, _):
        def blk(j, acc):
            rk = rank_s[j, pl.ds(b, 1), :]
            ps = pos_s[j, pl.ds(b, 1), :]
            return acc + jnp.where(rk == slot, ps, 0)

        acc = lax.fori_loop(0, n_blk, blk, jnp.zeros((n_sel, LANE), I32), unroll=4)
        col = jnp.sum(acc, axis=1, keepdims=True)
        rb = lax.broadcasted_iota(I32, (nb, 1), 0) == b
        nbn = jnp.sum(jnp.where(rb, n_before_new, 0), axis=0, keepdims=True)
        nsel = jnp.sum(jnp.where(rb & new_sel, 1, 0), axis=0, keepdims=True)
        npos = jnp.sum(jnp.where(rb, new_pos, 0), axis=0, keepdims=True)
        col = col + jnp.where((slot[:, 0:1] == nbn) & (nsel > 0), npos, 0)
        idx_ref[b] = col
        return 0

    lax.fori_loop(0, nb, per_seq, 0)


def _attn_sample_kernel(n_past, idx_sm, pt_sm, idxc_ref, q_ref, kn_ref, vn_ref, rb_ref, ck_hbm, cv_hbm, o_ref,
                        kbuf, vbuf, sem):
    b = pl.program_id(0)
    n_sel = kbuf.shape[0] // N_KV_B
    hd = HEAD_DIM
    page_shift = int(math.log2(PAGE_SIZE))

    def row_copies(r, page, slot):
        dst = pl.ds(N_KV_B * r, N_KV_B)
        return (pltpu.make_async_copy(ck_hbm.at[page, slot], kbuf.at[dst, :], sem.at[0]),
                pltpu.make_async_copy(cv_hbm.at[page, slot], vbuf.at[dst, :], sem.at[1]))

    def start(r, _):
        p = jnp.minimum(idx_sm[b, r] & (IDX_FLAG - 1), n_past - 1)
        ck, cv = row_copies(r, pt_sm[b, p >> page_shift], p & (PAGE_SIZE - 1))
        ck.start()
        cv.start()
        return 0

    lax.fori_loop(0, n_sel, start, 0, unroll=8)

    def wait(r, _):
        ck, cv = row_copies(r, 0, 0)
        ck.wait()
        cv.wait()
        return 0

    lax.fori_loop(0, n_sel, wait, 0, unroll=8)

    idxc = idxc_ref[0]
    valid = idxc < IDX_FLAG
    idx = idxc & (IDX_FLAG - 1)
    is_new = idx >= n_past
    bucket = _t5_bucket(n_past - idx)
    bias = jnp.zeros((n_sel, LANE), F32)
    for bkt in range(N_BUCKETS):
        bias = jnp.where(bucket == bkt, rb_ref[bkt:bkt + 1, :], bias)
    q_pad = jnp.concatenate([q_ref[0], jnp.zeros((LANE - N_HEADS_B, hd), F32)], axis=0).astype(BF16)
    lane = lax.broadcasted_iota(I32, (n_sel, LANE), 1)
    lg = jnp.zeros((n_sel, LANE), F32)
    vals = []
    for n in range(N_KV_B):
        rows_n = pl.ds(n, n_sel, stride=N_KV_B)
        kn = jnp.where(is_new, kn_ref[0, :, n * hd:(n + 1) * hd], kbuf[rows_n, :]).astype(BF16)
        vals.append(jnp.where(is_new, vn_ref[0, :, n * hd:(n + 1) * hd], vbuf[rows_n, :]))
        lg = jnp.where(lane // GQA_GROUP == n, _dot_nt(kn, q_pad), lg)
    lg = jnp.where(valid, lg * (hd ** -0.5) + bias, NEG_BIG)
    m = jnp.max(lg, axis=0, keepdims=True)
    p = jnp.where(valid, jnp.exp(lg - m), 0.0)
    p = p / jnp.sum(p, axis=0, keepdims=True)
    for h in range(N_HEADS_B):
        o_h = jnp.sum(p[:, h:h + 1] * vals[h // GQA_GROUP], axis=0, keepdims=True)
        o_ref[0, :, h * hd:(h + 1) * hd] = o_h.astype(o_ref.dtype)


def _attn_sample_call(idx3, page_table, q3, k_new, v_new, rel_bias, cache_k, cache_v):
    nb, n_sel, _ = idx3.shape
    n_past = page_table.shape[1] * PAGE_SIZE
    kvw = N_KV_B * HEAD_DIM
    qbw = N_HEADS_B * HEAD_DIM
    grid_spec = pltpu.PrefetchScalarGridSpec(
        num_scalar_prefetch=2,
        grid=(nb,),
        in_specs=[pl.BlockSpec((1, n_sel, 1), lambda b, *_: (b, 0, 0)),
                  pl.BlockSpec((1, N_HEADS_B, HEAD_DIM), lambda b, *_: (b, 0, 0)),
                  pl.BlockSpec((1, 1, kvw), lambda b, *_: (b, 0, 0)),
                  pl.BlockSpec((1, 1, kvw), lambda b, *_: (b, 0, 0)),
                  pl.BlockSpec((N_BUCKETS, LANE), lambda b, *_: (0, 0)),
                  pl.BlockSpec(memory_space=pl.ANY),
                  pl.BlockSpec(memory_space=pl.ANY)],
        out_specs=pl.BlockSpec((1, 1, qbw), lambda b, *_: (b, 0, 0)),
        scratch_shapes=[pltpu.VMEM((n_sel * N_KV_B, HEAD_DIM), F32), pltpu.VMEM((n_sel * N_KV_B, HEAD_DIM), F32),
                        pltpu.SemaphoreType.DMA((2,))],
    )
    return pl.pallas_call(
        functools.partial(_attn_sample_kernel, n_past),
        out_shape=jax.ShapeDtypeStruct((nb, 1, qbw), BF16),
        grid_spec=grid_spec,
        compiler_params=_cparams(("arbitrary",)),
        name="attn_sample",
    )(idx3.reshape(nb, n_sel), page_table, idx3, q3, k_new.reshape(nb, 1, kvw), v_new.reshape(nb, 1, kvw),
      jnp.pad(rel_bias.astype(F32), ((0, 0), (0, LANE - N_HEADS_B))), cache_k, cache_v)


def _topk_sample_call(scores2, qi2, wi2, ki_new, n_sel):
    nb, n_past = scores2.shape
    n_blk = n_past // LANE
    return pl.pallas_call(
        functools.partial(_topk_sample_kernel, n_sel),
        out_shape=jax.ShapeDtypeStruct((nb, n_sel, 1), I32),
        scratch_shapes=[pltpu.VMEM((n_blk, nb, LANE), I32), pltpu.VMEM((nb, n_past), I32),
                        pltpu.VMEM((n_blk, nb, LANE), I32), pltpu.VMEM((n_blk, nb, LANE), I32)],
        compiler_params=pltpu.CompilerParams(vmem_limit_bytes=VMEM_LIMIT),
        name="topk_sample",
    )(scores2, qi2, wi2, ki_new)


def _wprep_kernel(w_ref, o_ref):
    src = [0]
    for s in SPLIT_SIZES:
        src.append(src[-1] + s)
    dst = (OFF_CONV, OFF_SM + SM_A, OFF_SM + SM_B, OFF_Z, OFF_QB, OFF_KB, OFF_VB, OFF_QI, OFF_SM + SM_W, OFF_KI)
    o_ref[:, OFF_SM:] = jnp.zeros((o_ref.shape[0], NP - OFF_SM), o_ref.dtype)
    for i, d in enumerate(dst):
        o_ref[:, d:d + SPLIT_SIZES[i]] = w_ref[:, src[i]:src[i + 1]].astype(o_ref.dtype)


def _prep_w_in(w):
    rows, n_proj = w.shape
    tr = 256
    return pl.pallas_call(
        _wprep_kernel,
        out_shape=jax.ShapeDtypeStruct((rows, NP), BF16),
        grid=(rows // tr,),
        in_specs=[pl.BlockSpec((tr, n_proj), lambda i: (i, 0))],
        out_specs=pl.BlockSpec((tr, NP), lambda i: (i, 0)),
        compiler_params=_cparams(("arbitrary",)),
        name="wprep",
    )(w)


def kernel(x_prompt, x_sample, c_prompt, c_sample, cache_k, cache_v, cache_kidx, state_ssm, state_conv, page_table,
           w_ada, b_ada, pre1_g, post1_g, pre2_g, post2_g, w_in, w_out, conv_w, a_log, dt_bias, gdn_norm_g,
           idx_knorm_g, idx_knorm_b, rel_bias, w_ff1, w_ff2):
    bsz, seq, _ = x_prompt.shape
    nb, dec_seq, _ = x_sample.shape
    assert dec_seq == 1, "the sample path handles one new token per sequence"
    depth = w_in.shape[0]
    n_past = page_table.shape[1] * PAGE_SIZE
    n_sel_s = min(TOPK_MAX, (n_past + dec_seq) // 4)
    kvw = N_KV_B * HEAD_DIM
    tm_in, tn_in = min(1024, seq), 1024
    tm_out = min(512, seq)
    tf = 1024

    hp = x_prompt.reshape(bsz * seq, D_MODEL)
    hs = x_sample.reshape(nb, D_MODEL)
    c_all = jnp.concatenate([c_sample, c_prompt], axis=0)
    new_p, new_s = [], []
    for l in range(depth):
        m = _adaln_call(c_all, w_ada[l], b_ada[l])
        w_in_bf = _prep_w_in(w_in[l])
        w_out_bf = w_out[l].astype(BF16)
        w1_bf = w_ff1[l].astype(BF16)
        w2_bf = w_ff2[l].astype(BF16)

        proj = _inproj_call(hp, pre1_g[l], _ModSpec(m, False, nb, seq, tm_in), w_in_bf, tm_in, tn_in)
        k_p, v_p, ki_p, k_bf, va_bf, ki_bf = _kvprep_call(proj, idx_knorm_g[l], idx_knorm_b[l], tm_in, True)
        proj3 = proj.reshape(bsz, seq, NP)
        o_a, ssm_p = _gdn_prompt_call(proj3, conv_w[l], a_log[l], dt_bias[l], gdn_norm_g[l])
        o_b = _attn_prompt_call(proj3, ki_bf.reshape(bsz, seq, D_IDX), k_bf.reshape(bsz, seq, kvw),
                                va_bf.reshape(bsz, seq, 2 * kvw), rel_bias)
        mod_p = _ModSpec(m, False, nb, seq, tm_out)
        x1, h2 = _outproj_call(o_a.reshape(bsz * seq, VAL_DIM_A), o_b.reshape(bsz * seq, N_HEADS_B * HEAD_DIM), hp,
                               w_out_bf, post1_g[l], pre2_g[l], mod_p, tm_out)
        hp = _ffn_call(h2, x1, w1_bf, w2_bf, post2_g[l], mod_p, tm_out, tf)
        conv_p = proj3[:, seq - (CONV_W - 1):, OFF_CONV:OFF_CONV + CONV_DIM]
        new_p.append((k_p.reshape(bsz, seq, N_KV_B, HEAD_DIM), v_p.reshape(bsz, seq, N_KV_B, HEAD_DIM),
                      ki_p.reshape(bsz, seq, D_IDX), ssm_p, conv_p))

        mod_s = _ModSpec(m, True, 0, 1, nb)
        proj_s = _inproj_call(hs, pre1_g[l], mod_s, w_in_bf, nb, tn_in)
        k_s, v_s, ki_s = _kvprep_call(proj_s, idx_knorm_g[l], idx_knorm_b[l], nb)
        o_a_s, ssm_s, conv_s = _gdn_sample_call(proj_s.reshape(nb, 1, NP), state_conv[l], state_ssm[l], conv_w[l],
                                                a_log[l], dt_bias[l], gdn_norm_g[l])
        qi2 = proj_s[:, OFF_QI:OFF_QI + N_IDX_HEADS * D_IDX]
        wi2 = proj_s[:, OFF_SM + SM_W:OFF_SM + SM_W + N_IDX_HEADS]
        scores = _idx_sample_call(page_table, qi2.reshape(nb, N_IDX_HEADS, D_IDX), wi2.reshape(nb, N_IDX_HEADS, 1),
                                  cache_kidx[l])
        idx3 = _topk_sample_call(scores.reshape(nb, n_past), qi2, wi2, ki_s, n_sel_s)
        q3 = proj_s[:, OFF_QB:OFF_QB + N_HEADS_B * HEAD_DIM].reshape(nb, N_HEADS_B, HEAD_DIM)
        o_b_s = _attn_sample_call(idx3, page_table, q3, k_s, v_s, rel_bias, cache_k[l], cache_v[l])
        x1s, h2s = _outproj_call(o_a_s.reshape(nb, VAL_DIM_A), o_b_s.reshape(nb, N_HEADS_B * HEAD_DIM), hs,
                                 w_out_bf, post1_g[l], pre2_g[l], mod_s, nb)
        hs = _ffn_call(h2s, x1s, w1_bf, w2_bf, post2_g[l], mod_s, nb, tf)
        new_s.append((k_s.reshape(nb, 1, N_KV_B, HEAD_DIM), v_s.reshape(nb, 1, N_KV_B, HEAD_DIM),
                      ki_s.reshape(nb, 1, D_IDX), ssm_s, conv_s))

    p_k, p_v, p_kidx, p_ssm, p_conv = [jnp.stack([st[i] for st in new_p]) for i in range(5)]
    s_k, s_v, s_kidx, s_ssm, s_conv = [jnp.stack([st[i] for st in new_s]) for i in range(5)]
    return (hp.reshape(bsz, seq, D_MODEL), hs.reshape(nb, 1, D_MODEL), p_k, p_v, p_kidx, p_ssm, p_conv,
            s_k, s_v, s_kidx, s_ssm, s_conv)
```

```python
import functools
import math

import jax
import jax.numpy as jnp
from jax import lax
from jax.experimental import pallas as pl
from jax.experimental.pallas import tpu as pltpu

F32 = jnp.float32
BF16 = jnp.bfloat16
I32 = jnp.int32

D_MODEL = 2048
PAGE_SIZE = 128
HEAD_DIM = 128
N_HEADS_A = 8
KEY_DIM_A = N_HEADS_A * HEAD_DIM
VAL_DIM_A = N_HEADS_A * HEAD_DIM
CONV_DIM = 2 * KEY_DIM_A + VAL_DIM_A
CONV_W = 4
N_HEADS_B = 8
N_KV_B = 2
GQA_GROUP = N_HEADS_B // N_KV_B
N_IDX_HEADS = 16
D_IDX = 128
TOPK_MAX = 256
N_BUCKETS = 32
MAX_DISTANCE = 128
D_FF = 4 * D_MODEL
EPS = 1e-6

SPLIT_SIZES = (CONV_DIM, N_HEADS_A, N_HEADS_A, VAL_DIM_A, N_HEADS_B * HEAD_DIM, N_KV_B * HEAD_DIM,
               N_KV_B * HEAD_DIM, N_IDX_HEADS * D_IDX, N_IDX_HEADS, D_IDX)

OFF_CONV = 0
OFF_Z = OFF_CONV + CONV_DIM
OFF_QI = OFF_Z + VAL_DIM_A
OFF_QB = OFF_QI + N_IDX_HEADS * D_IDX
OFF_KB = OFF_QB + N_HEADS_B * HEAD_DIM
OFF_VB = OFF_KB + N_KV_B * HEAD_DIM
OFF_KI = OFF_VB + N_KV_B * HEAD_DIM
OFF_SM = OFF_KI + D_IDX
NP = 8192
SM_A, SM_B, SM_W = 0, N_HEADS_A, 2 * N_HEADS_A

LANE = 128
VMEM_LIMIT = 56 * 1024 * 1024
INT_MIN = -(2 ** 31)
NEG_BIG = -1e30

GDN_CHUNK = 128
GDN_GROUP = 256
ATT_TQ = 256
ATT_TK = 256


def _cparams(sem):
    return pltpu.CompilerParams(dimension_semantics=sem, vmem_limit_bytes=VMEM_LIMIT)


def _sigmoid(x):
    return 1.0 / (1.0 + jnp.exp(-x))


def _softplus(x):
    return jnp.maximum(x, 0.0) + jnp.log(1.0 + jnp.exp(-jnp.abs(x)))


def _dot(a, b):
    return jnp.dot(a, b, preferred_element_type=F32)


def _dot_nt(a, b):
    return lax.dot_general(a, b, (((1,), (1,)), ((), ())), preferred_element_type=F32)


def _dot_tn(a, b):
    return lax.dot_general(a, b, (((0,), (0,)), ((), ())), preferred_element_type=F32)


def _mod(ref):
    v = ref[...]
    return v[0] if v.ndim == 3 else v


def _adaln_kernel(c_ref, w_ref, b_ref, o_ref):
    c = c_ref[...]
    s = (c * _sigmoid(c)).astype(BF16)
    o_ref[...] = _dot(s, w_ref[...].astype(BF16)) + b_ref[...]


def _adaln_call(c_all, w_ada, b_ada):
    n_rows = c_all.shape[0]
    n_out = w_ada.shape[1]
    tn = 1024
    return pl.pallas_call(
        _adaln_kernel,
        out_shape=jax.ShapeDtypeStruct((n_rows, n_out), F32),
        grid=(n_out // tn,),
        in_specs=[pl.BlockSpec((n_rows, D_MODEL), lambda j: (0, 0)),
                  pl.BlockSpec((D_MODEL, tn), lambda j: (0, j)),
                  pl.BlockSpec((1, tn), lambda j: (0, j))],
        out_specs=pl.BlockSpec((n_rows, tn), lambda j: (0, j)),
        compiler_params=_cparams(("arbitrary",)),
        name="adaln",
    )(c_all, w_ada, b_ada.reshape(1, n_out))


class _ModSpec:
    def __init__(self, m2, per_row, row_off, rows_per_batch, tm):
        self.per_row = per_row
        self.arr = m2 if per_row else m2.reshape(m2.shape[0], 1, m2.shape[1])
        self.row_off = row_off
        self.blocks_per_batch = None if per_row else rows_per_batch // tm
        self.tm = tm

    def spec(self, k, grid_rank):
        if self.per_row:
            if grid_rank == 1:
                return pl.BlockSpec((self.tm, D_MODEL), lambda i: (0, k))
            return pl.BlockSpec((self.tm, D_MODEL), lambda i, j: (0, k))
        bpb, off = self.blocks_per_batch, self.row_off
        if grid_rank == 1:
            return pl.BlockSpec((1, 1, D_MODEL), lambda i: (off + i // bpb, 0, k))
        return pl.BlockSpec((1, 1, D_MODEL), lambda i, j: (off + i // bpb, 0, k))


def _inproj_kernel(x_ref, g_ref, sh_ref, sc_ref, w_ref, o_ref, h_scr):
    @pl.when(pl.program_id(1) == 0)
    def _():
        x = x_ref[...]
        y = x * lax.rsqrt(jnp.mean(x * x, axis=-1, keepdims=True) + EPS) * g_ref[...]
        h_scr[...] = (y * (1.0 + _mod(sc_ref)) + _mod(sh_ref)).astype(BF16)

    o_ref[...] = _dot(h_scr[...], w_ref[...])


def _inproj_call(x2d, g, mod, w_bf, tm, tn):
    m_rows = x2d.shape[0]
    n_cols = w_bf.shape[1]
    return pl.pallas_call(
        _inproj_kernel,
        out_shape=jax.ShapeDtypeStruct((m_rows, n_cols), F32),
        grid=(m_rows // tm, n_cols // tn),
        in_specs=[pl.BlockSpec((tm, D_MODEL), lambda i, j: (i, 0)),
                  pl.BlockSpec((1, D_MODEL), lambda i, j: (0, 0)),
                  mod.spec(0, 2), mod.spec(1, 2),
                  pl.BlockSpec((D_MODEL, tn), lambda i, j: (0, j))],
        out_specs=pl.BlockSpec((tm, tn), lambda i, j: (i, j)),
        scratch_shapes=[pltpu.VMEM((tm, D_MODEL), BF16)],
        compiler_params=_cparams(("arbitrary", "arbitrary")),
        name="inproj",
    )(x2d, g.reshape(1, D_MODEL), mod.arr, mod.arr, w_bf)


def _kvprep_kernel(kb_ref, vb_ref, ki_ref, lg_ref, lb_ref, k_ref, v_ref, kidx_ref, *bf_refs):
    kb = kb_ref[...]
    vb = vb_ref[...]
    for n in range(N_KV_B):
        k_ref[:, n, :] = kb[:, n * HEAD_DIM:(n + 1) * HEAD_DIM]
        v_ref[:, n, :] = vb[:, n * HEAD_DIM:(n + 1) * HEAD_DIM]
    x = ki_ref[...]
    mu = jnp.mean(x, axis=-1, keepdims=True)
    xc = x - mu
    var = jnp.mean(xc * xc, axis=-1, keepdims=True)
    ki = xc * lax.rsqrt(var + EPS) * lg_ref[...] + lb_ref[...]
    kidx_ref[...] = ki
    if bf_refs:
        kbf_ref, va_ref, kibf_ref = bf_refs
        kbf_ref[...] = kb.astype(BF16)
        kibf_ref[...] = ki.astype(BF16)
        hd = HEAD_DIM
        ones_col = jnp.where(lax.broadcasted_iota(I32, (vb.shape[0], hd), 1) == 0, 1.0, 0.0)
        va_ref[...] = jnp.concatenate([piece for n in range(N_KV_B) for piece in (vb[:, n * hd:(n + 1) * hd], ones_col)],
                                      axis=1).astype(BF16)


def _kvprep_call(proj, ln_g, ln_b, tm, with_bf16=False):
    m_rows = proj.shape[0]
    kvw = N_KV_B * HEAD_DIM
    row = lambda i: (i, 0)
    kv_shape = jax.ShapeDtypeStruct((m_rows, N_KV_B, HEAD_DIM), F32)
    kv_spec = pl.BlockSpec((tm, N_KV_B, HEAD_DIM), lambda i: (i, 0, 0))
    out_shape = [kv_shape, kv_shape, jax.ShapeDtypeStruct((m_rows, D_IDX), F32)]
    out_specs = [kv_spec, kv_spec, pl.BlockSpec((tm, D_IDX), row)]
    if with_bf16:
        out_shape += [jax.ShapeDtypeStruct((m_rows, kvw), BF16), jax.ShapeDtypeStruct((m_rows, 2 * kvw), BF16),
                      jax.ShapeDtypeStruct((m_rows, D_IDX), BF16)]
        out_specs += [pl.BlockSpec((tm, kvw), row), pl.BlockSpec((tm, 2 * kvw), row), pl.BlockSpec((tm, D_IDX), row)]
    return pl.pallas_call(
        _kvprep_kernel,
        out_shape=tuple(out_shape),
        grid=(m_rows // tm,),
        in_specs=[pl.BlockSpec((tm, kvw), lambda i: (i, OFF_KB // kvw)),
                  pl.BlockSpec((tm, kvw), lambda i: (i, OFF_VB // kvw)),
                  pl.BlockSpec((tm, D_IDX), lambda i: (i, OFF_KI // D_IDX)),
                  pl.BlockSpec((1, D_IDX), lambda i: (0, 0)),
                  pl.BlockSpec((1, D_IDX), lambda i: (0, 0))],
        out_specs=tuple(out_specs),
        compiler_params=_cparams(("arbitrary",)),
        name="kvprep",
    )(proj, proj, proj, ln_g.reshape(1, D_IDX), ln_b.reshape(1, D_IDX))


def _conv_silu(x, cw, row):
    def taps(xx, masked):
        y = xx * cw[CONV_W - 1:CONV_W]
        for s in range(1, CONV_W):
            xs = pltpu.roll(xx, s, 0)
            if masked:
                xs = jnp.where(row[:8] >= s, xs, 0.0)
            y = y + xs * cw[CONV_W - 1 - s:CONV_W - s]
        return y

    y = jnp.concatenate([taps(x[:8], True), taps(x, False)[8:]], axis=0)
    return y * _sigmoid(y)


def _l2norm(x):
    return x * lax.rsqrt(jnp.sum(x * x, axis=-1, keepdims=True) + EPS)


def _gdn_prompt_kernel(q_ref, k_ref, v_ref, z_ref, sm_ref, cwq_ref, cwk_ref, cwv_ref, alog_ref, dtb_ref, ng_ref,
                       o_ref, s_out_ref,
                       qn_s, kn_s, vn_s, gc_s, bt_s, l_s, a_s, x_s, u_s, w_s, mn_s, sts_s):
    seq = q_ref.shape[1]
    hd = HEAD_DIM
    grp = GDN_GROUP
    chunk = GDN_CHUNK
    n_grp = seq // grp
    n_chunk = seq // chunk
    cpg = grp // chunk
    h = pl.program_id(1)

    row = lax.broadcasted_iota(I32, (seq, hd), 0)
    lane = lax.broadcasted_iota(I32, (seq, hd), 1)

    qn_s[...] = _l2norm(_conv_silu(q_ref[0], cwq_ref[...], row)) * (hd ** -0.5)
    kn_s[...] = _l2norm(_conv_silu(k_ref[0], cwk_ref[...], row))
    vn_s[...] = _conv_silu(v_ref[0], cwv_ref[...], row)

    sm = sm_ref[0]
    g_all = -jnp.exp(alog_ref[...]) * _softplus(sm + dtb_ref[...])
    b_all = _sigmoid(sm)
    g_col = jnp.sum(jnp.where(lane == h + SM_A, g_all, 0.0), axis=1, keepdims=True)
    b_col = jnp.sum(jnp.where(lane == h + SM_B, b_all, 0.0), axis=1, keepdims=True)
    gc = jnp.broadcast_to(g_col, (seq, hd))
    pos = row & (chunk - 1)
    s = 1
    while s < chunk:
        gc = gc + jnp.where(pos >= s, pltpu.roll(gc, s, 0), 0.0)
        s *= 2
    gc_s[...] = gc
    bt_s[...] = jnp.broadcast_to(b_col, (seq, hd))

    ii = lax.broadcasted_iota(I32, (grp, grp), 0)
    jj = lax.broadcasted_iota(I32, (grp, grp), 1)
    xr = ii ^ jj
    same = (xr >> int(math.log2(chunk))) == 0
    causal = same & (ii >= jj)
    strict = same & (ii > jj)
    eye = (ii == jj).astype(F32)

    def build(gi, _):
        r0 = pl.multiple_of(gi * grp, grp)
        kk = kn_s[pl.ds(r0, grp), :]
        qq = qn_s[pl.ds(r0, grp), :]
        bt = bt_s[pl.ds(r0, grp), :]
        gcg = gc_s[pl.ds(r0, grp), :]
        kb = (kk * bt).astype(BF16)
        kkb = kk.astype(BF16)
        kkt = _dot_nt(kb, kkb)
        qkt = _dot_nt(qq.astype(BF16), kkb)
        colb = jnp.concatenate([gcg, gcg], axis=1)
        diff = colb - colb.T
        decay = jnp.where(causal, jnp.exp(jnp.where(causal, diff, 0.0)), 0.0)
        lmat = jnp.where(strict, kkt * decay, 0.0)
        l_s[gi] = lmat
        a_s[pl.ds(r0, grp), :] = jnp.where(causal, qkt * decay, 0.0)
        x_s[gi] = eye - jnp.where(xr == 1, lmat, 0.0)
        return 0

    lax.fori_loop(0, n_grp, build, 0, unroll=2)

    for lvl in range(1, int(math.log2(chunk))):
        def level(gi, _, lvl=lvl):
            xm = x_s[gi]
            cl = jnp.where((xr >> lvl) == 1, l_s[gi], 0.0).astype(BF16)
            xb = xm.astype(BF16)
            x_s[gi] = xm - _dot(_dot(xb, cl).astype(BF16), xb)
            return 0

        lax.fori_loop(0, n_grp, level, 0, unroll=True)

    def apply(gi, _):
        r0 = pl.multiple_of(gi * grp, grp)
        kk = kn_s[pl.ds(r0, grp), :]
        vv = vn_s[pl.ds(r0, grp), :]
        bt = bt_s[pl.ds(r0, grp), :]
        gcg = gc_s[pl.ds(r0, grp), :]
        rhs = jnp.concatenate([vv * bt, kk * bt * jnp.exp(gcg)], axis=1).astype(BF16)
        uw = _dot(x_s[gi].astype(BF16), rhs)
        u_s[pl.ds(r0, grp), :] = uw[:, :hd]
        w_s[pl.ds(r0, grp), :] = uw[:, hd:]
        for cc in range(cpg):
            lo = cc * chunk
            g_last = gcg[lo + chunk - 1:lo + chunk, :]
            k_dec = (kk[lo:lo + chunk] * jnp.exp(g_last - gcg[lo:lo + chunk])).astype(BF16)
            wu = jnp.concatenate([uw[lo:lo + chunk, hd:], uw[lo:lo + chunk, :hd]], axis=1).astype(BF16)
            mn_s[gi * cpg + cc] = _dot_tn(k_dec, wu)
        return 0

    lax.fori_loop(0, n_grp, apply, 0, unroll=2)

    def scan(c, st):
        sts_s[c] = st
        g_last = gc_s[pl.ds(c * chunk + chunk - 1, 1), :]
        mn = mn_s[c]
        return st * jnp.exp(g_last) - _dot(mn[:, :hd].astype(BF16), st.astype(BF16)) + mn[:, hd:]

    s_out_ref[0, 0] = lax.fori_loop(0, n_chunk, scan, jnp.zeros((hd, hd), F32))

    ng = ng_ref[...]

    def emit(gi, _):
        r0 = pl.multiple_of(gi * grp, grp)
        gcg = gc_s[pl.ds(r0, grp), :]
        qg = qn_s[pl.ds(r0, grp), :] * jnp.exp(gcg)
        wg = w_s[pl.ds(r0, grp), :]
        ws, qs = [], []
        for cc in range(cpg):
            lo = cc * chunk
            stb = sts_s[gi * cpg + cc].astype(BF16)
            both = _dot(jnp.concatenate([wg[lo:lo + chunk], qg[lo:lo + chunk]], axis=0).astype(BF16), stb)
            ws.append(both[:chunk])
            qs.append(both[chunk:])
        v_new = u_s[pl.ds(r0, grp), :] - jnp.concatenate(ws, axis=0)
        o = jnp.concatenate(qs, axis=0) + _dot(a_s[pl.ds(r0, grp), :].astype(BF16), v_new.astype(BF16))
        zc = z_ref[0, pl.ds(r0, grp), :]
        on = o * lax.rsqrt(jnp.mean(o * o, axis=-1, keepdims=True) + EPS) * ng
        o_ref[0, pl.ds(r0, grp), :] = (on * (zc * _sigmoid(zc))).astype(o_ref.dtype)
        return 0

    lax.fori_loop(0, n_grp, emit, 0, unroll=2)


def _lane_vec(v, off):
    return jnp.zeros((1, LANE), F32).at[0, off:off + v.shape[0]].set(v.astype(F32))


def _gdn_prompt_call(proj3, conv_w, a_log, dt_bias, norm_g):
    bsz, seq, _ = proj3.shape
    hd = HEAD_DIM
    nh = N_HEADS_A
    n_grp = seq // GDN_GROUP
    col = lambda base: (lambda b, h: (b, 0, base // hd + h))
    cw = lambda base: (lambda b, h: (0, base // hd + h))
    vec = pl.BlockSpec((1, LANE), lambda b, h: (0, 0))
    return pl.pallas_call(
        _gdn_prompt_kernel,
        out_shape=(jax.ShapeDtypeStruct((bsz, seq, VAL_DIM_A), BF16),
                   jax.ShapeDtypeStruct((bsz, nh, hd, hd), F32)),
        grid=(bsz, nh),
        in_specs=[pl.BlockSpec((1, seq, hd), col(OFF_CONV)),
                  pl.BlockSpec((1, seq, hd), col(OFF_CONV + KEY_DIM_A)),
                  pl.BlockSpec((1, seq, hd), col(OFF_CONV + 2 * KEY_DIM_A)),
                  pl.BlockSpec((1, seq, hd), col(OFF_Z)),
                  pl.BlockSpec((1, seq, LANE), lambda b, h: (b, 0, OFF_SM // LANE)),
                  pl.BlockSpec((CONV_W, hd), cw(0)),
                  pl.BlockSpec((CONV_W, hd), cw(KEY_DIM_A)),
                  pl.BlockSpec((CONV_W, hd), cw(2 * KEY_DIM_A)),
                  vec, vec, vec],
        out_specs=(pl.BlockSpec((1, seq, hd), lambda b, h: (b, 0, h)),
                   pl.BlockSpec((1, 1, hd, hd), lambda b, h: (b, h, 0, 0))),
        scratch_shapes=[pltpu.VMEM((seq, hd), F32), pltpu.VMEM((seq, hd), F32), pltpu.VMEM((seq, hd), F32),
                        pltpu.VMEM((seq, hd), F32), pltpu.VMEM((seq, hd), F32),
                        pltpu.VMEM((n_grp, GDN_GROUP, GDN_GROUP), F32),
                        pltpu.VMEM((seq, GDN_GROUP), F32),
                        pltpu.VMEM((n_grp, GDN_GROUP, GDN_GROUP), F32),
                        pltpu.VMEM((seq, hd), F32), pltpu.VMEM((seq, hd), F32),
                        pltpu.VMEM((seq // GDN_CHUNK, hd, 2 * hd), F32), pltpu.VMEM((seq // GDN_CHUNK, hd, hd), F32)],
        compiler_params=_cparams(("arbitrary", "arbitrary")),
        name="gdn_prompt",
    )(proj3, proj3, proj3, proj3, proj3, conv_w, conv_w, conv_w,
      _lane_vec(a_log, SM_A), _lane_vec(dt_bias, SM_A), norm_g.reshape(1, hd).astype(F32))


def _t5_bucket(dist):
    n = jnp.maximum(dist, 0)
    max_exact = N_BUCKETS // 2
    nf = jnp.maximum(n, 1).astype(F32)
    large = max_exact + (jnp.log(nf / max_exact) / math.log(MAX_DISTANCE / max_exact)
                         * (N_BUCKETS - max_exact)).astype(I32)
    large = jnp.minimum(large, N_BUCKETS - 1)
    return jnp.where(n < max_exact, n, large)


def _bias_from_bucket(bucket, rb_ref, head):
    out = jnp.zeros(bucket.shape, F32)
    for b in range(N_BUCKETS):
        out = jnp.where(bucket == b, rb_ref[b, head], out)
    return out


def _order_key(score):
    bits = pltpu.bitcast(score + 0.0, I32)
    return bits ^ ((bits >> 31) & 0x7FFFFFFF)


I16 = jnp.int16
HALF_BIAS = 1 << 15
HALF_ROWS = 16
KEY_POS_INF = 0x7F800000
KEY_NEG_INF = INT_MIN + 0x7FFFFF


def _kth_largest_key(count_ge, shape, n_sel):
    def body(i, tau):
        bit = 31 - i
        cand = jnp.where(i == 0, tau ^ INT_MIN, tau | (1 << bit))
        return jnp.where(count_ge(cand) >= n_sel, cand, tau)

    return lax.fori_loop(0, 32, body, jnp.full(shape, INT_MIN, I32))


def _attn_prompt_kernel(n_sel, qi_ref, sm_ref, qb_ref, ki_ref, k_ref, va_ref, rb_ref, o_ref,
                        qi_s, qb_s, wt_s, keys_s, hi_s, lo_s, tau_s, bias_s, m_s, acc_s):
    tq, tk = ATT_TQ, ATT_TK
    hd = HEAD_DIM
    bi = pl.program_id(0)
    qblk = pl.program_id(1)
    n_kb = (qblk * tq) // tk + 1
    krow = lax.broadcasted_iota(I32, (tk, tq), 0)
    qcol = lax.broadcasted_iota(I32, (tk, tq), 1)

    @pl.when((bi == 0) & (qblk == 0))
    def _():
        for off in range(bias_s.shape[1]):
            bucket = _t5_bucket(off * tk + krow - qcol)
            for h in range(N_HEADS_B):
                bias_s[h, off] = _bias_from_bucket(bucket, rb_ref, h)

    qi_s[...] = qi_ref[0].astype(BF16)
    qb_s[...] = qb_ref[0].astype(BF16)
    wt_s[...] = sm_ref[0].T * (N_IDX_HEADS ** -0.5)
    t_pos = qblk * tq + qcol

    def idx_body(j, _):
        k0 = pl.multiple_of(j * tk, tk)
        kt = ki_ref[0, pl.ds(k0, tk), :]
        acc = jnp.zeros((tk, tq), F32)
        for h in range(N_IDX_HEADS):
            s = _dot_nt(kt, qi_s[:, h * D_IDX:(h + 1) * D_IDX])
            acc = acc + jnp.maximum(s, 0.0) * wt_s[SM_W + h:SM_W + h + 1, :]
        acc = acc * (D_IDX ** -0.5)
        key = jnp.where(k0 + krow <= t_pos, _order_key(acc), INT_MIN)
        keys_s[j] = key
        hi_s[j] = (key >> 16).astype(I16)
        lo_s[j] = ((key & 0xFFFF) - HALF_BIAS).astype(I16)
        return 0

    lax.fori_loop(0, n_kb, idx_body, 0)

    def kth_largest_half(half_s):
        def count_ge(cand):
            def cb(j, c):
                hit = jnp.where(half_s[j] >= cand, jnp.int16(1), jnp.int16(0))
                for r in range(tk // HALF_ROWS):
                    c = c + hit[r * HALF_ROWS:(r + 1) * HALF_ROWS]
                return c

            c = lax.fori_loop(0, n_kb, cb, jnp.zeros((HALF_ROWS, tq), I16))
            return jnp.sum(c.astype(I32), axis=0, keepdims=True)

        def body(i, t):
            cand = jnp.where(i == 0, t ^ (-HALF_BIAS), t | (1 << (15 - i)))
            return jnp.where(count_ge(cand.astype(I16)) >= n_sel, cand, t)

        return lax.fori_loop(0, 16, body, jnp.full((1, tq), -HALF_BIAS, I32))

    tau_hi = kth_largest_half(hi_s)
    tau_hi16 = tau_hi.astype(I16)

    def narrow(j, _):
        hi = hi_s[j]
        lo_s[j] = jnp.where(hi > tau_hi16, jnp.int16(HALF_BIAS - 1),
                            jnp.where(hi == tau_hi16, lo_s[j], jnp.int16(-HALF_BIAS)))
        return 0

    lax.fori_loop(0, n_kb, narrow, 0)
    tau_lo = kth_largest_half(lo_s)
    tau = (tau_hi << 16) | ((tau_lo + HALF_BIAS) & 0xFFFF)
    tau = jnp.maximum(tau, KEY_NEG_INF + 1)

    def count_where(pred):
        def cb(j, c):
            hit = jnp.where(pred(keys_s[j], j * tk + krow), 1, 0)
            return c + jnp.sum(hit.reshape(tk // 8, 8, tq), axis=0)

        c = lax.fori_loop(0, n_kb, cb, jnp.zeros((8, tq), I32))
        return jnp.sum(c, axis=0, keepdims=True)

    @pl.when(jnp.max(count_where(lambda k, pos: k >= tau)) > n_sel)
    def _():
        quota = n_sel - count_where(lambda k, pos: k > tau)
        pos_bits = (keys_s.shape[0] * tk - 1).bit_length()

        def body(i, last):
            cand = last | (1 << (pos_bits - 1 - i))
            before = count_where(lambda k, pos: (k == tau) & (pos < cand))
            return jnp.where(before < quota, cand, last)

        last_kept = lax.fori_loop(0, pos_bits, body, jnp.zeros((1, tq), I32))

        def demote(j, _):
            k = keys_s[j]
            keys_s[j] = jnp.where((k == tau) & (j * tk + krow > last_kept), tau - 1, k)
            return 0

        lax.fori_loop(0, n_kb, demote, 0)

    scale = hd ** -0.5
    lane_reps = tk // LANE
    tau_rep = jnp.broadcast_to(tau, (LANE, tq)).T
    tau_s[...] = jnp.concatenate([tau_rep] * lane_reps, axis=1)

    def block_inputs(j):
        kq = keys_s[j].T
        sel = jnp.where(kq < KEY_POS_INF, kq, INT_MIN) >= tau_s[...]
        return k_ref[0, pl.ds(pl.multiple_of(j * tk, tk), tk), :], sel

    def masked_logits(j, h, kt, sel):
        n = h // GQA_GROUP
        boff = jnp.minimum(qblk - j, bias_s.shape[1] - 1)
        lg = _dot_nt(qb_s[:, h * hd:(h + 1) * hd], kt[:, n * hd:(n + 1) * hd]) * scale + bias_s[h, boff]
        return jnp.where(sel, lg, NEG_BIG)

    m_s[...] = jnp.full(m_s.shape, NEG_BIG, F32)

    def max_body(j, _):
        kt, sel = block_inputs(j)
        for h in range(N_HEADS_B):
            lg = masked_logits(j, h, kt, sel)
            part = lg[:, :LANE]
            for r in range(1, lane_reps):
                part = jnp.maximum(part, lg[:, r * LANE:(r + 1) * LANE])
            m_s[h] = jnp.maximum(m_s[h], part)
        return 0

    lax.fori_loop(0, n_kb, max_body, 0)
    for h in range(N_HEADS_B):
        m_s[h] = jnp.broadcast_to(jnp.max(m_s[h], axis=1, keepdims=True), (tq, LANE))
    acc_s[...] = jnp.zeros(acc_s.shape, F32)

    def sum_body(j, _):
        kt, sel = block_inputs(j)
        va = va_ref[0, pl.ds(pl.multiple_of(j * tk, tk), tk), :]
        for h in range(N_HEADS_B):
            n = h // GQA_GROUP
            mrow = jnp.concatenate([m_s[h]] * lane_reps, axis=1)
            p = jnp.exp(masked_logits(j, h, kt, sel) - mrow)
            acc_s[h] += _dot(p.astype(BF16), va[:, n * 2 * hd:(n + 1) * 2 * hd])
        return 0

    lax.fori_loop(0, n_kb, sum_body, 0)
    for h in range(N_HEADS_B):
        a = acc_s[h]
        o_ref[0, :, h * hd:(h + 1) * hd] = (a[:, :hd] / a[:, hd:hd + 1]).astype(o_ref.dtype)


def _attn_prompt_call(proj3, ki_bf, k_bf, va_bf, rel_bias):
    bsz, seq, _ = proj3.shape
    tq, tk = ATT_TQ, ATT_TK
    assert tq == tk, "the bias-tile offsets assume square blocks"
    n_sel = min(TOPK_MAX, seq // 4)
    qiw = N_IDX_HEADS * D_IDX
    qbw = N_HEADS_B * HEAD_DIM
    kvw = N_KV_B * HEAD_DIM
    n_off = -(-(MAX_DISTANCE + tk - 1) // tk) + 1
    return pl.pallas_call(
        functools.partial(_attn_prompt_kernel, n_sel),
        out_shape=jax.ShapeDtypeStruct((bsz, seq, qbw), BF16),
        grid=(bsz, seq // tq),
        in_specs=[pl.BlockSpec((1, tq, qiw), lambda b, i: (b, i, OFF_QI // qiw)),
                  pl.BlockSpec((1, tq, LANE), lambda b, i: (b, i, OFF_SM // LANE)),
                  pl.BlockSpec((1, tq, qbw), lambda b, i: (b, i, OFF_QB // qbw)),
                  pl.BlockSpec((1, seq, D_IDX), lambda b, i: (b, 0, 0)),
                  pl.BlockSpec((1, seq, kvw), lambda b, i: (b, 0, 0)),
                  pl.BlockSpec((1, seq, 2 * kvw), lambda b, i: (b, 0, 0)),
                  pl.BlockSpec(memory_space=pltpu.SMEM)],
        out_specs=pl.BlockSpec((1, tq, qbw), lambda b, i: (b, i, 0)),
        scratch_shapes=[pltpu.VMEM((tq, qiw), BF16), pltpu.VMEM((tq, qbw), BF16),
                        pltpu.VMEM((LANE, tq), F32),
                        pltpu.VMEM((seq // tk, tk, tq), I32),
                        pltpu.VMEM((seq // tk, tk, tq), I16), pltpu.VMEM((seq // tk, tk, tq), I16),
                        pltpu.VMEM((tq, tk), I32),
                        pltpu.VMEM((N_HEADS_B, n_off, tq, tk), F32),
                        pltpu.VMEM((N_HEADS_B, tq, LANE), F32),
                        pltpu.VMEM((N_HEADS_B, tq, 2 * HEAD_DIM), F32)],
        compiler_params=_cparams(("arbitrary", "arbitrary")),
        name="attn_prompt",
    )(proj3, proj3, proj3, ki_bf, k_bf, va_bf, rel_bias.astype(F32))


def _rms(x, g):
    return x * lax.rsqrt(jnp.mean(x * x, axis=-1, keepdims=True) + EPS) * g


def _outproj_kernel(oa_ref, ob_ref, x_ref, wa_ref, wb_ref, pg_ref, g1_ref, p2_ref, sh2_ref, sc2_ref, x1_ref, h2_ref):
    mix = _dot(oa_ref[...], wa_ref[...]) + _dot(ob_ref[...], wb_ref[...])
    x1 = x_ref[...] + _mod(g1_ref) * _rms(mix, pg_ref[...])
    x1_ref[...] = x1
    h2_ref[...] = (_rms(x1, p2_ref[...]) * (1.0 + _mod(sc2_ref)) + _mod(sh2_ref)).astype(BF16)


def _outproj_call(oa, ob, x2d, w_out_bf, post1_g, pre2_g, mod, tm):
    m_rows = x2d.shape[0]
    half = w_out_bf.shape[0] // 2
    row = lambda i: (i, 0)
    vec = pl.BlockSpec((1, D_MODEL), lambda i: (0, 0))
    return pl.pallas_call(
        _outproj_kernel,
        out_shape=(jax.ShapeDtypeStruct((m_rows, D_MODEL), F32), jax.ShapeDtypeStruct((m_rows, D_MODEL), BF16)),
        grid=(m_rows // tm,),
        in_specs=[pl.BlockSpec((tm, half), row), pl.BlockSpec((tm, half), row), pl.BlockSpec((tm, D_MODEL), row),
                  pl.BlockSpec((half, D_MODEL), lambda i: (0, 0)), pl.BlockSpec((half, D_MODEL), lambda i: (1, 0)),
                  vec, mod.spec(2, 1), vec, mod.spec(3, 1), mod.spec(4, 1)],
        out_specs=(pl.BlockSpec((tm, D_MODEL), row), pl.BlockSpec((tm, D_MODEL), row)),
        compiler_params=_cparams(("arbitrary",)),
        name="outproj",
    )(oa, ob, x2d, w_out_bf, w_out_bf, post1_g.reshape(1, D_MODEL), mod.arr, pre2_g.reshape(1, D_MODEL),
      mod.arr, mod.arr)


def _ffn_kernel(h_ref, x1_ref, w1_ref, w2_ref, pg_ref, g2_ref, o_ref, acc_s):
    kk = pl.program_id(1)

    @pl.when(kk == 0)
    def _():
        acc_s[...] = jnp.zeros_like(acc_s)

    a = jnp.maximum(_dot(h_ref[...], w1_ref[...]), 0.0)
    acc_s[...] += _dot((a * a).astype(BF16), w2_ref[...])

    @pl.when(kk == pl.num_programs(1) - 1)
    def _():
        o_ref[...] = x1_ref[...] + _mod(g2_ref) * _rms(acc_s[...], pg_ref[...])


def _ffn_call(h2, x1, w1_bf, w2_bf, post2_g, mod, tm, tf):
    m_rows = x1.shape[0]
    return pl.pallas_call(
        _ffn_kernel,
        out_shape=jax.ShapeDtypeStruct((m_rows, D_MODEL), F32),
        grid=(m_rows // tm, D_FF // tf),
        in_specs=[pl.BlockSpec((tm, D_MODEL), lambda i, k: (i, 0)),
                  pl.BlockSpec((tm, D_MODEL), lambda i, k: (i, 0)),
                  pl.BlockSpec((D_MODEL, tf), lambda i, k: (0, k)),
                  pl.BlockSpec((tf, D_MODEL), lambda i, k: (k, 0)),
                  pl.BlockSpec((1, D_MODEL), lambda i, k: (0, 0)),
                  mod.spec(5, 2)],
        out_specs=pl.BlockSpec((tm, D_MODEL), lambda i, k: (i, 0)),
        scratch_shapes=[pltpu.VMEM((tm, D_MODEL), F32)],
        compiler_params=_cparams(("arbitrary", "arbitrary")),
        name="ffn",
    )(h2, x1, w1_bf, w2_bf, post2_g.reshape(1, D_MODEL), mod.arr)


def _gdn_sample_kernel(cin_ref, z_ref, sm_ref, sconv_ref, ssm_ref, cw_ref, alog_ref, dtb_ref, ng_ref,
                       o_ref, ssm_out_ref, conv_out_ref):
    hd = HEAD_DIM
    nh = N_HEADS_A
    prev = sconv_ref[0]
    xin = cin_ref[0]
    cw = cw_ref[...]
    y = xin * cw[CONV_W - 1:CONV_W]
    for s in range(CONV_W - 1):
        y = y + prev[s:s + 1] * cw[s:s + 1]
    y = y * _sigmoid(y)
    conv_out_ref[0, 0:CONV_W - 2, :] = prev[1:CONV_W - 1]
    conv_out_ref[0, CONV_W - 2:CONV_W - 1, :] = xin

    def heads(base):
        return jnp.concatenate([y[:, base + h * hd:base + (h + 1) * hd] for h in range(nh)], axis=0)

    q8 = _l2norm(heads(0)) * (hd ** -0.5)
    k8 = _l2norm(heads(KEY_DIM_A))
    v8 = heads(2 * KEY_DIM_A)
    sm = sm_ref[0]
    g_all = -jnp.exp(alog_ref[...]) * _softplus(sm + dtb_ref[...])
    b_all = _sigmoid(sm)
    kt = jnp.concatenate([k8, jnp.zeros((LANE - nh, hd), F32)], axis=0).T
    k8b = k8.astype(BF16)
    q8b = q8.astype(BF16)
    z = z_ref[0]
    ng = ng_ref[...]
    for h in range(nh):
        s1 = ssm_ref[0, h] * jnp.exp(g_all[:, SM_A + h:SM_A + h + 1])
        kv = _dot(k8b, s1.astype(BF16))[h:h + 1]
        delta = (v8[h:h + 1] - kv) * b_all[:, SM_B + h:SM_B + h + 1]
        s2 = s1 + kt[:, h:h + 1] * delta
        ssm_out_ref[0, h] = s2
        o = _dot(q8b, s2.astype(BF16))[h:h + 1]
        zh = z[:, h * hd:(h + 1) * hd]
        on = o * lax.rsqrt(jnp.mean(o * o, axis=-1, keepdims=True) + EPS) * ng
        o_ref[0, :, h * hd:(h + 1) * hd] = (on * (zh * _sigmoid(zh))).astype(o_ref.dtype)


def _gdn_sample_call(proj_s3, state_conv, state_ssm, conv_w, a_log, dt_bias, norm_g):
    nb = proj_s3.shape[0]
    hd, nh = HEAD_DIM, N_HEADS_A
    vec = pl.BlockSpec((1, LANE), lambda b: (0, 0))
    return pl.pallas_call(
        _gdn_sample_kernel,
        out_shape=(jax.ShapeDtypeStruct((nb, 1, VAL_DIM_A), BF16),
                   jax.ShapeDtypeStruct((nb, nh, hd, hd), F32),
                   jax.ShapeDtypeStruct((nb, CONV_W - 1, CONV_DIM), F32)),
        grid=(nb,),
        in_specs=[pl.BlockSpec((1, 1, CONV_DIM), lambda b: (b, 0, OFF_CONV // CONV_DIM)),
                  pl.BlockSpec((1, 1, VAL_DIM_A), lambda b: (b, 0, OFF_Z // VAL_DIM_A)),
                  pl.BlockSpec((1, 1, LANE), lambda b: (b, 0, OFF_SM // LANE)),
                  pl.BlockSpec((1, CONV_W - 1, CONV_DIM), lambda b: (b, 0, 0)),
                  pl.BlockSpec((1, nh, hd, hd), lambda b: (b, 0, 0, 0)),
                  pl.BlockSpec((CONV_W, CONV_DIM), lambda b: (0, 0)),
                  vec, vec, vec],
        out_specs=(pl.BlockSpec((1, 1, VAL_DIM_A), lambda b: (b, 0, 0)),
                   pl.BlockSpec((1, nh, hd, hd), lambda b: (b, 0, 0, 0)),
                   pl.BlockSpec((1, CONV_W - 1, CONV_DIM), lambda b: (b, 0, 0))),
        compiler_params=_cparams(("arbitrary",)),
        name="gdn_sample",
    )(proj_s3, proj_s3, proj_s3, state_conv, state_ssm, conv_w,
      _lane_vec(a_log, SM_A), _lane_vec(dt_bias, SM_A), norm_g.reshape(1, hd).astype(F32))


def _idx_sample_kernel(pt_sm, qi_ref, wi_ref, kidx_hbm, o_ref, buf, sem):
    b = pl.program_id(0)
    n_seq = pl.num_programs(0)
    n_pages = buf.shape[1]

    def page_copy(page, slot, p):
        return pltpu.make_async_copy(kidx_hbm.at[page], buf.at[slot, p], sem.at[slot])

    def gather(seq, slot):
        def body(p, _):
            page_copy(pt_sm[seq, p], slot, p).start()
            return 0

        lax.fori_loop(0, n_pages, body, 0, unroll=8)

    @pl.when(b == 0)
    def _():
        gather(0, 0)

    @pl.when(b + 1 < n_seq)
    def _():
        gather(b + 1, (b + 1) % 2)

    slot = b % 2

    def wait_body(p, _):
        page_copy(0, slot, p).wait()
        return 0

    lax.fori_loop(0, n_pages, wait_body, 0, unroll=8)

    qb = qi_ref[0].astype(BF16)
    w = wi_ref[0] * (N_IDX_HEADS ** -0.5)

    def score(p, _):
        s = _dot_nt(qb, buf[slot, p].astype(BF16))
        o_ref[0, pl.ds(p, 1), :] = jnp.sum(jnp.maximum(s, 0.0) * w, axis=0, keepdims=True) * (D_IDX ** -0.5)
        return 0

    lax.fori_loop(0, n_pages, score, 0, unroll=4)


def _idx_sample_call(page_table, qi3, wi3, cache_kidx):
    nb, n_pages = page_table.shape
    grid_spec = pltpu.PrefetchScalarGridSpec(
        num_scalar_prefetch=1,
        grid=(nb,),
        in_specs=[pl.BlockSpec((1, N_IDX_HEADS, D_IDX), lambda b, pt: (b, 0, 0)),
                  pl.BlockSpec((1, N_IDX_HEADS, 1), lambda b, pt: (b, 0, 0)),
                  pl.BlockSpec(memory_space=pl.ANY)],
        out_specs=pl.BlockSpec((1, n_pages, PAGE_SIZE), lambda b, pt: (b, 0, 0)),
        scratch_shapes=[pltpu.VMEM((2, n_pages, PAGE_SIZE, D_IDX), F32), pltpu.SemaphoreType.DMA((2,))],
    )
    return pl.pallas_call(
        _idx_sample_kernel,
        out_shape=jax.ShapeDtypeStruct((nb, n_pages, PAGE_SIZE), F32),
        grid_spec=grid_spec,
        compiler_params=_cparams(("arbitrary",)),
        name="idx_sample",
    )(page_table, qi3, wi3, cache_kidx)


IDX_FLAG = 1 << 20


def _topk_sample_kernel(n_sel, sc_ref, qi_ref, wi_ref, knew_ref, idx_ref, keys_s, keyw_s, rank_s, pos_s):
    nb, n_past = sc_ref.shape
    n_blk = n_past // LANE
    lane = lax.broadcasted_iota(I32, (nb, LANE), 1)

    kn = knew_ref[...].astype(BF16)
    q = qi_ref[...]
    w = wi_ref[...] * (N_IDX_HEADS ** -0.5)
    s_new = jnp.zeros((nb, 1), F32)
    for h in range(N_IDX_HEADS):
        qh = q[:, h * D_IDX:(h + 1) * D_IDX].astype(BF16).astype(F32)
        dot = jnp.sum(qh * kn.astype(F32), axis=1, keepdims=True)
        s_new = s_new + jnp.maximum(dot, 0.0) * w[:, h:h + 1]
    s_new = s_new * (D_IDX ** -0.5)
    key_new = _order_key(s_new)

    def to_keys(j, _):
        keys_s[j] = _order_key(sc_ref[:, pl.ds(pl.multiple_of(j * LANE, LANE), LANE)])
        return 0

    lax.fori_loop(0, n_blk, to_keys, 0, unroll=8)
    keyw_s[...] = _order_key(sc_ref[...])
    wide = min(n_past, 16 * LANE)

    def count(pred):
        acc = jnp.zeros((nb, LANE), I32)
        for c in range(n_past // wide):
            hit = jnp.where(pred(keyw_s[:, c * wide:(c + 1) * wide]), 1, 0)
            for k in range(wide // LANE):
                acc = acc + hit[:, k * LANE:(k + 1) * LANE]
        return jnp.sum(acc, axis=1, keepdims=True)

    def count_ge(cand):
        return count(lambda k: k >= cand) + jnp.where(key_new >= cand, 1, 0)

    tau = jnp.maximum(_kth_largest_key(count_ge, (nb, 1), n_sel), INT_MIN + 1)

    quota = n_sel - (count(lambda k: k > tau) + jnp.where(key_new > tau, 1, 0))

    ri = lax.broadcasted_iota(I32, (LANE, LANE), 0)
    ci = lax.broadcasted_iota(I32, (LANE, LANE), 1)
    upper = (ri <= ci).astype(BF16)

    def excl_prefix(flag):
        return (_dot(flag.astype(BF16), upper) - flag).astype(I32)

    def rank_body(j, carry):
        base, base_eq = carry
        kj = keys_s[j]
        eq = kj == tau
        eqf = jnp.where(eq, 1.0, 0.0)
        take = (kj > tau) | (eq & (base_eq + excl_prefix(eqf) < quota))
        takef = jnp.where(take, 1.0, 0.0)
        rank_s[j] = jnp.where(take, base + excl_prefix(takef), -1)
        finite = (kj < KEY_POS_INF) & (kj > KEY_NEG_INF)
        pos = j * LANE + lane
        pos_s[j] = jnp.where(finite, pos, pos | IDX_FLAG)
        return (base + jnp.sum(takef, axis=1, keepdims=True).astype(I32),
                base_eq + jnp.sum(eqf, axis=1, keepdims=True).astype(I32))

    zero = jnp.zeros((nb, 1), I32)
    n_before_new, eq_before_new = lax.fori_loop(0, n_blk, rank_body, (zero, zero), unroll=4)

    slot = lax.broadcasted_iota(I32, (n_sel, LANE), 0)
    new_sel = (key_new > tau) | ((key_new == tau) & (eq_before_new < quota))
    new_fin = (key_new < KEY_POS_INF) & (key_new > KEY_NEG_INF)
    new_pos = jnp.where(new_fin, n_past, n_past | IDX_FLAG)

    def per_seq(b, _):
        def blk(j, acc):
            rk = rank_s[j, pl.ds(b, 1), :]
            ps = pos_s[j, pl.ds(b, 1), :]
            return acc + jnp.where(rk == slot, ps, 0)

        acc = lax.fori_loop(0, n_blk, blk, jnp.zeros((n_sel, LANE), I32), unroll=4)
        col = jnp.sum(acc, axis=1, keepdims=True)
        rb = lax.broadcasted_iota(I32, (nb, 1), 0) == b
        nbn = jnp.sum(jnp.where(rb, n_before_new, 0), axis=0, keepdims=True)
        nsel = jnp.sum(jnp.where(rb & new_sel, 1, 0), axis=0, keepdims=True)
        npos = jnp.sum(jnp.where(rb, new_pos, 0), axis=0, keepdims=True)
        col = col + jnp.where((slot[:, 0:1] == nbn) & (nsel > 0), npos, 0)
        idx_ref[b] = col
        return 0

    lax.fori_loop(0, nb, per_seq, 0)


def _attn_sample_kernel(n_past, idx_sm, pt_sm, idxc_ref, q_ref, kn_ref, vn_ref, rb_ref, ck_hbm, cv_hbm, o_ref,
                        kbuf, vbuf, sem):
    b = pl.program_id(0)
    n_seq = pl.num_programs(0)
    n_sel = kbuf.shape[1] // N_KV_B
    hd = HEAD_DIM
    page_shift = int(math.log2(PAGE_SIZE))

    def row_copies(buf, r, page, slot):
        dst = pl.ds(N_KV_B * r, N_KV_B)
        return (pltpu.make_async_copy(ck_hbm.at[page, slot], kbuf.at[buf, dst, :], sem.at[0, buf]),
                pltpu.make_async_copy(cv_hbm.at[page, slot], vbuf.at[buf, dst, :], sem.at[1, buf]))

    def gather(seq, buf):
        def start(r, _):
            p = jnp.minimum(idx_sm[seq, r] & (IDX_FLAG - 1), n_past - 1)
            ck, cv = row_copies(buf, r, pt_sm[seq, p >> page_shift], p & (PAGE_SIZE - 1))
            ck.start()
            cv.start()
            return 0

        lax.fori_loop(0, n_sel, start, 0, unroll=8)

    @pl.when(b == 0)
    def _():
        gather(0, 0)

    @pl.when(b + 1 < n_seq)
    def _():
        gather(b + 1, (b + 1) % 2)

    cur = b % 2

    def wait(r, _):
        ck, cv = row_copies(cur, r, 0, 0)
        ck.wait()
        cv.wait()
        return 0

    lax.fori_loop(0, n_sel, wait, 0, unroll=8)

    idxc = idxc_ref[0]
    valid = idxc < IDX_FLAG
    idx = idxc & (IDX_FLAG - 1)
    is_new = idx >= n_past
    bucket = _t5_bucket(n_past - idx)
    bias = jnp.zeros((n_sel, LANE), F32)
    for bkt in range(N_BUCKETS):
        bias = jnp.where(bucket == bkt, rb_ref[bkt:bkt + 1, :], bias)
    q_pad = jnp.concatenate([q_ref[0], jnp.zeros((LANE - N_HEADS_B, hd), F32)], axis=0).astype(BF16)
    lane = lax.broadcasted_iota(I32, (n_sel, LANE), 1)
    lg = jnp.zeros((n_sel, LANE), F32)
    vals = []
    for n in range(N_KV_B):
        rows_n = pl.ds(n, n_sel, stride=N_KV_B)
        kn = jnp.where(is_new, kn_ref[0, :, n * hd:(n + 1) * hd], kbuf[cur, rows_n, :]).astype(BF16)
        vals.append(jnp.where(is_new, vn_ref[0, :, n * hd:(n + 1) * hd], vbuf[cur, rows_n, :]))
        lg = jnp.where(lane // GQA_GROUP == n, _dot_nt(kn, q_pad), lg)
    lg = jnp.where(valid, lg * (hd ** -0.5) + bias, NEG_BIG)
    m = jnp.max(lg, axis=0, keepdims=True)
    p = jnp.where(valid, jnp.exp(lg - m), 0.0)
    p = p / jnp.sum(p, axis=0, keepdims=True)
    for h in range(N_HEADS_B):
        o_h = jnp.sum(p[:, h:h + 1] * vals[h // GQA_GROUP], axis=0, keepdims=True)
        o_ref[0, :, h * hd:(h + 1) * hd] = o_h.astype(o_ref.dtype)


def _attn_sample_call(idx3, page_table, q3, k_new, v_new, rel_bias, cache_k, cache_v):
    nb, n_sel, _ = idx3.shape
    n_past = page_table.shape[1] * PAGE_SIZE
    kvw = N_KV_B * HEAD_DIM
    qbw = N_HEADS_B * HEAD_DIM
    grid_spec = pltpu.PrefetchScalarGridSpec(
        num_scalar_prefetch=2,
        grid=(nb,),
        in_specs=[pl.BlockSpec((1, n_sel, 1), lambda b, *_: (b, 0, 0)),
                  pl.BlockSpec((1, N_HEADS_B, HEAD_DIM), lambda b, *_: (b, 0, 0)),
                  pl.BlockSpec((1, 1, kvw), lambda b, *_: (b, 0, 0)),
                  pl.BlockSpec((1, 1, kvw), lambda b, *_: (b, 0, 0)),
                  pl.BlockSpec((N_BUCKETS, LANE), lambda b, *_: (0, 0)),
                  pl.BlockSpec(memory_space=pl.ANY),
                  pl.BlockSpec(memory_space=pl.ANY)],
        out_specs=pl.BlockSpec((1, 1, qbw), lambda b, *_: (b, 0, 0)),
        scratch_shapes=[pltpu.VMEM((2, n_sel * N_KV_B, HEAD_DIM), F32), pltpu.VMEM((2, n_sel * N_KV_B, HEAD_DIM), F32),
                        pltpu.SemaphoreType.DMA((2, 2))],
    )
    return pl.pallas_call(
        functools.partial(_attn_sample_kernel, n_past),
        out_shape=jax.ShapeDtypeStruct((nb, 1, qbw), BF16),
        grid_spec=grid_spec,
        compiler_params=_cparams(("arbitrary",)),
        name="attn_sample",
    )(idx3.reshape(nb, n_sel), page_table, idx3, q3, k_new.reshape(nb, 1, kvw), v_new.reshape(nb, 1, kvw),
      jnp.pad(rel_bias.astype(F32), ((0, 0), (0, LANE - N_HEADS_B))), cache_k, cache_v)


def _topk_sample_call(scores2, qi2, wi2, ki_new, n_sel):
    nb, n_past = scores2.shape
    n_blk = n_past // LANE
    return pl.pallas_call(
        functools.partial(_topk_sample_kernel, n_sel),
        out_shape=jax.ShapeDtypeStruct((nb, n_sel, 1), I32),
        scratch_shapes=[pltpu.VMEM((n_blk, nb, LANE), I32), pltpu.VMEM((nb, n_past), I32),
                        pltpu.VMEM((n_blk, nb, LANE), I32), pltpu.VMEM((n_blk, nb, LANE), I32)],
        compiler_params=pltpu.CompilerParams(vmem_limit_bytes=VMEM_LIMIT),
        name="topk_sample",
    )(scores2, qi2, wi2, ki_new)


def _wprep_kernel(w_ref, o_ref):
    src = [0]
    for s in SPLIT_SIZES:
        src.append(src[-1] + s)
    dst = (OFF_CONV, OFF_SM + SM_A, OFF_SM + SM_B, OFF_Z, OFF_QB, OFF_KB, OFF_VB, OFF_QI, OFF_SM + SM_W, OFF_KI)
    o_ref[:, OFF_SM:] = jnp.zeros((o_ref.shape[0], NP - OFF_SM), o_ref.dtype)
    for i, d in enumerate(dst):
        o_ref[:, d:d + SPLIT_SIZES[i]] = w_ref[0, :, src[i]:src[i + 1]].astype(o_ref.dtype)


def _prep_w_in(w_all, layer):
    _, rows, n_proj = w_all.shape
    tr = 256
    return pl.pallas_call(
        _wprep_kernel,
        out_shape=jax.ShapeDtypeStruct((rows, NP), BF16),
        grid=(rows // tr,),
        in_specs=[pl.BlockSpec((1, tr, n_proj), lambda i: (layer, i, 0))],
        out_specs=pl.BlockSpec((tr, NP), lambda i: (i, 0)),
        compiler_params=_cparams(("arbitrary",)),
        name="wprep",
    )(w_all)


def kernel(x_prompt, x_sample, c_prompt, c_sample, cache_k, cache_v, cache_kidx, state_ssm, state_conv, page_table,
           w_ada, b_ada, pre1_g, post1_g, pre2_g, post2_g, w_in, w_out, conv_w, a_log, dt_bias, gdn_norm_g,
           idx_knorm_g, idx_knorm_b, rel_bias, w_ff1, w_ff2):
    bsz, seq, _ = x_prompt.shape
    nb, dec_seq, _ = x_sample.shape
    assert dec_seq == 1, "the sample path handles one new token per sequence"
    depth = w_in.shape[0]
    n_past = page_table.shape[1] * PAGE_SIZE
    n_sel_s = min(TOPK_MAX, (n_past + dec_seq) // 4)
    kvw = N_KV_B * HEAD_DIM
    tm_in, tn_in = min(1024, seq), 1024
    tm_out = min(512, seq)
    tf = 1024

    hp = x_prompt.reshape(bsz * seq, D_MODEL)
    hs = x_sample.reshape(nb, D_MODEL)
    c_all = jnp.concatenate([c_sample, c_prompt], axis=0)
    new_p, new_s = [], []
    for l in range(depth):
        m = _adaln_call(c_all, w_ada[l], b_ada[l])
        w_in_bf = _prep_w_in(w_in, l)
        w_out_bf = w_out[l].astype(BF16)
        w1_bf = w_ff1[l].astype(BF16)
        w2_bf = w_ff2[l].astype(BF16)

        proj = _inproj_call(hp, pre1_g[l], _ModSpec(m, False, nb, seq, tm_in), w_in_bf, tm_in, tn_in)
        k_p, v_p, ki_p, k_bf, va_bf, ki_bf = _kvprep_call(proj, idx_knorm_g[l], idx_knorm_b[l], tm_in, True)
        proj3 = proj.reshape(bsz, seq, NP)
        o_a, ssm_p = _gdn_prompt_call(proj3, conv_w[l], a_log[l], dt_bias[l], gdn_norm_g[l])
        o_b = _attn_prompt_call(proj3, ki_bf.reshape(bsz, seq, D_IDX), k_bf.reshape(bsz, seq, kvw),
                                va_bf.reshape(bsz, seq, 2 * kvw), rel_bias)
        mod_p = _ModSpec(m, False, nb, seq, tm_out)
        x1, h2 = _outproj_call(o_a.reshape(bsz * seq, VAL_DIM_A), o_b.reshape(bsz * seq, N_HEADS_B * HEAD_DIM), hp,
                               w_out_bf, post1_g[l], pre2_g[l], mod_p, tm_out)
        hp = _ffn_call(h2, x1, w1_bf, w2_bf, post2_g[l], mod_p, tm_out, tf)
        conv_p = proj3[:, seq - (CONV_W - 1):, OFF_CONV:OFF_CONV + CONV_DIM]
        new_p.append((k_p.reshape(bsz, seq, N_KV_B, HEAD_DIM), v_p.reshape(bsz, seq, N_KV_B, HEAD_DIM),
                      ki_p.reshape(bsz, seq, D_IDX), ssm_p, conv_p))

        mod_s = _ModSpec(m, True, 0, 1, nb)
        proj_s = _inproj_call(hs, pre1_g[l], mod_s, w_in_bf, nb, tn_in)
        k_s, v_s, ki_s = _kvprep_call(proj_s, idx_knorm_g[l], idx_knorm_b[l], nb)
        o_a_s, ssm_s, conv_s = _gdn_sample_call(proj_s.reshape(nb, 1, NP), state_conv[l], state_ssm[l], conv_w[l],
                                                a_log[l], dt_bias[l], gdn_norm_g[l])
        qi2 = proj_s[:, OFF_QI:OFF_QI + N_IDX_HEADS * D_IDX]
        wi2 = proj_s[:, OFF_SM + SM_W:OFF_SM + SM_W + N_IDX_HEADS]
        scores = _idx_sample_call(page_table, qi2.reshape(nb, N_IDX_HEADS, D_IDX), wi2.reshape(nb, N_IDX_HEADS, 1),
                                  cache_kidx[l])
        idx3 = _topk_sample_call(scores.reshape(nb, n_past), qi2, wi2, ki_s, n_sel_s)
        q3 = proj_s[:, OFF_QB:OFF_QB + N_HEADS_B * HEAD_DIM].reshape(nb, N_HEADS_B, HEAD_DIM)
        o_b_s = _attn_sample_call(idx3, page_table, q3, k_s, v_s, rel_bias, cache_k[l], cache_v[l])
        x1s, h2s = _outproj_call(o_a_s.reshape(nb, VAL_DIM_A), o_b_s.reshape(nb, N_HEADS_B * HEAD_DIM), hs,
                                 w_out_bf, post1_g[l], pre2_g[l], mod_s, nb)
        hs = _ffn_call(h2s, x1s, w1_bf, w2_bf, post2_g[l], mod_s, nb, tf)
        new_s.append((k_s.reshape(nb, 1, N_KV_B, HEAD_DIM), v_s.reshape(nb, 1, N_KV_B, HEAD_DIM),
                      ki_s.reshape(nb, 1, D_IDX), ssm_s, conv_s))

    p_k, p_v, p_kidx, p_ssm, p_conv = [jnp.stack([st[i] for st in new_p]) for i in range(5)]
    s_k, s_v, s_kidx, s_ssm, s_conv = [jnp.stack([st[i] for st in new_s]) for i in range(5)]
    return (hp.reshape(bsz, seq, D_MODEL), hs.reshape(nb, 1, D_MODEL), p_k, p_v, p_kidx, p_ssm, p_conv,
            s_k, s_v, s_kidx, s_ssm, s_conv)
```

```python
import functools
import math

import jax
import jax.numpy as jnp
from jax import lax
from jax.experimental import pallas as pl
from jax.experimental.pallas import tpu as pltpu

F32 = jnp.float32
BF16 = jnp.bfloat16
I32 = jnp.int32

D_MODEL = 2048
PAGE_SIZE = 128
HEAD_DIM = 128
N_HEADS_A = 8
KEY_DIM_A = N_HEADS_A * HEAD_DIM
VAL_DIM_A = N_HEADS_A * HEAD_DIM
CONV_DIM = 2 * KEY_DIM_A + VAL_DIM_A
CONV_W = 4
N_HEADS_B = 8
N_KV_B = 2
GQA_GROUP = N_HEADS_B // N_KV_B
N_IDX_HEADS = 16
D_IDX = 128
TOPK_MAX = 256
N_BUCKETS = 32
MAX_DISTANCE = 128
D_FF = 4 * D_MODEL
EPS = 1e-6

SPLIT_SIZES = (CONV_DIM, N_HEADS_A, N_HEADS_A, VAL_DIM_A, N_HEADS_B * HEAD_DIM, N_KV_B * HEAD_DIM,
               N_KV_B * HEAD_DIM, N_IDX_HEADS * D_IDX, N_IDX_HEADS, D_IDX)

OFF_CONV = 0
OFF_Z = OFF_CONV + CONV_DIM
OFF_QI = OFF_Z + VAL_DIM_A
OFF_QB = OFF_QI + N_IDX_HEADS * D_IDX
OFF_KB = OFF_QB + N_HEADS_B * HEAD_DIM
OFF_VB = OFF_KB + N_KV_B * HEAD_DIM
OFF_KI = OFF_VB + N_KV_B * HEAD_DIM
OFF_SM = OFF_KI + D_IDX
NP = 8192
SM_A, SM_B, SM_W = 0, N_HEADS_A, 2 * N_HEADS_A

LANE = 128
VMEM_LIMIT = 56 * 1024 * 1024
INT_MIN = -(2 ** 31)
NEG_BIG = -1e30

GDN_CHUNK = 128
GDN_GROUP = 256
ATT_TQ = 256
ATT_TK = 256


def _cparams(sem):
    return pltpu.CompilerParams(dimension_semantics=sem, vmem_limit_bytes=VMEM_LIMIT)


def _sigmoid(x):
    return 1.0 / (1.0 + jnp.exp(-x))


def _softplus(x):
    return jnp.maximum(x, 0.0) + jnp.log(1.0 + jnp.exp(-jnp.abs(x)))


def _dot(a, b):
    return jnp.dot(a, b, preferred_element_type=F32)


def _dot_nt(a, b):
    return lax.dot_general(a, b, (((1,), (1,)), ((), ())), preferred_element_type=F32)


def _dot_tn(a, b):
    return lax.dot_general(a, b, (((0,), (0,)), ((), ())), preferred_element_type=F32)


def _mod(ref):
    v = ref[...]
    return v[0] if v.ndim == 3 else v


def _adaln_kernel(c_ref, w_ref, b_ref, o_ref):
    c = c_ref[...]
    s = (c * _sigmoid(c)).astype(BF16)
    o_ref[...] = _dot(s, w_ref[...].astype(BF16)) + b_ref[...]


def _adaln_call(c_all, w_ada, b_ada):
    n_rows = c_all.shape[0]
    n_out = w_ada.shape[1]
    tn = 1024
    return pl.pallas_call(
        _adaln_kernel,
        out_shape=jax.ShapeDtypeStruct((n_rows, n_out), F32),
        grid=(n_out // tn,),
        in_specs=[pl.BlockSpec((n_rows, D_MODEL), lambda j: (0, 0)),
                  pl.BlockSpec((D_MODEL, tn), lambda j: (0, j)),
                  pl.BlockSpec((1, tn), lambda j: (0, j))],
        out_specs=pl.BlockSpec((n_rows, tn), lambda j: (0, j)),
        compiler_params=_cparams(("arbitrary",)),
        name="adaln",
    )(c_all, w_ada, b_ada.reshape(1, n_out))


class _ModSpec:
    def __init__(self, m2, per_row, row_off, rows_per_batch, tm):
        self.per_row = per_row
        self.arr = m2 if per_row else m2.reshape(m2.shape[0], 1, m2.shape[1])
        self.row_off = row_off
        self.blocks_per_batch = None if per_row else rows_per_batch // tm
        self.tm = tm

    def spec(self, k, grid_rank):
        if self.per_row:
            if grid_rank == 1:
                return pl.BlockSpec((self.tm, D_MODEL), lambda i: (0, k))
            return pl.BlockSpec((self.tm, D_MODEL), lambda i, j: (0, k))
        bpb, off = self.blocks_per_batch, self.row_off
        if grid_rank == 1:
            return pl.BlockSpec((1, 1, D_MODEL), lambda i: (off + i // bpb, 0, k))
        return pl.BlockSpec((1, 1, D_MODEL), lambda i, j: (off + i // bpb, 0, k))


def _inproj_kernel(x_ref, g_ref, sh_ref, sc_ref, w_ref, o_ref, h_scr):
    @pl.when(pl.program_id(1) == 0)
    def _():
        x = x_ref[...]
        y = x * lax.rsqrt(jnp.mean(x * x, axis=-1, keepdims=True) + EPS) * g_ref[...]
        h_scr[...] = (y * (1.0 + _mod(sc_ref)) + _mod(sh_ref)).astype(BF16)

    o_ref[...] = _dot(h_scr[...], w_ref[...])


def _inproj_call(x2d, g, mod, w_bf, tm, tn):
    m_rows = x2d.shape[0]
    n_cols = w_bf.shape[1]
    return pl.pallas_call(
        _inproj_kernel,
        out_shape=jax.ShapeDtypeStruct((m_rows, n_cols), F32),
        grid=(m_rows // tm, n_cols // tn),
        in_specs=[pl.BlockSpec((tm, D_MODEL), lambda i, j: (i, 0)),
                  pl.BlockSpec((1, D_MODEL), lambda i, j: (0, 0)),
                  mod.spec(0, 2), mod.spec(1, 2),
                  pl.BlockSpec((D_MODEL, tn), lambda i, j: (0, j))],
        out_specs=pl.BlockSpec((tm, tn), lambda i, j: (i, j)),
        scratch_shapes=[pltpu.VMEM((tm, D_MODEL), BF16)],
        compiler_params=_cparams(("arbitrary", "arbitrary")),
        name="inproj",
    )(x2d, g.reshape(1, D_MODEL), mod.arr, mod.arr, w_bf)


def _kvprep_kernel(kb_ref, vb_ref, ki_ref, lg_ref, lb_ref, k_ref, v_ref, kidx_ref, *bf_refs):
    kb = kb_ref[...]
    vb = vb_ref[...]
    for n in range(N_KV_B):
        k_ref[:, n, :] = kb[:, n * HEAD_DIM:(n + 1) * HEAD_DIM]
        v_ref[:, n, :] = vb[:, n * HEAD_DIM:(n + 1) * HEAD_DIM]
    x = ki_ref[...]
    mu = jnp.mean(x, axis=-1, keepdims=True)
    xc = x - mu
    var = jnp.mean(xc * xc, axis=-1, keepdims=True)
    ki = xc * lax.rsqrt(var + EPS) * lg_ref[...] + lb_ref[...]
    kidx_ref[...] = ki
    if bf_refs:
        kbf_ref, va_ref, kibf_ref = bf_refs
        kbf_ref[...] = kb.astype(BF16)
        kibf_ref[...] = ki.astype(BF16)
        hd = HEAD_DIM
        ones_col = jnp.where(lax.broadcasted_iota(I32, (vb.shape[0], hd), 1) == 0, 1.0, 0.0)
        va_ref[...] = jnp.concatenate([piece for n in range(N_KV_B) for piece in (vb[:, n * hd:(n + 1) * hd], ones_col)],
                                      axis=1).astype(BF16)


def _kvprep_call(proj, ln_g, ln_b, tm, with_bf16=False):
    m_rows = proj.shape[0]
    kvw = N_KV_B * HEAD_DIM
    row = lambda i: (i, 0)
    kv_shape = jax.ShapeDtypeStruct((m_rows, N_KV_B, HEAD_DIM), F32)
    kv_spec = pl.BlockSpec((tm, N_KV_B, HEAD_DIM), lambda i: (i, 0, 0))
    out_shape = [kv_shape, kv_shape, jax.ShapeDtypeStruct((m_rows, D_IDX), F32)]
    out_specs = [kv_spec, kv_spec, pl.BlockSpec((tm, D_IDX), row)]
    if with_bf16:
        out_shape += [jax.ShapeDtypeStruct((m_rows, kvw), BF16), jax.ShapeDtypeStruct((m_rows, 2 * kvw), BF16),
                      jax.ShapeDtypeStruct((m_rows, D_IDX), BF16)]
        out_specs += [pl.BlockSpec((tm, kvw), row), pl.BlockSpec((tm, 2 * kvw), row), pl.BlockSpec((tm, D_IDX), row)]
    return pl.pallas_call(
        _kvprep_kernel,
        out_shape=tuple(out_shape),
        grid=(m_rows // tm,),
        in_specs=[pl.BlockSpec((tm, kvw), lambda i: (i, OFF_KB // kvw)),
                  pl.BlockSpec((tm, kvw), lambda i: (i, OFF_VB // kvw)),
                  pl.BlockSpec((tm, D_IDX), lambda i: (i, OFF_KI // D_IDX)),
                  pl.BlockSpec((1, D_IDX), lambda i: (0, 0)),
                  pl.BlockSpec((1, D_IDX), lambda i: (0, 0))],
        out_specs=tuple(out_specs),
        compiler_params=_cparams(("arbitrary",)),
        name="kvprep",
    )(proj, proj, proj, ln_g.reshape(1, D_IDX), ln_b.reshape(1, D_IDX))


def _conv_silu(x, cw, row):
    def taps(xx, masked):
        y = xx * cw[CONV_W - 1:CONV_W]
        for s in range(1, CONV_W):
            xs = pltpu.roll(xx, s, 0)
            if masked:
                xs = jnp.where(row[:8] >= s, xs, 0.0)
            y = y + xs * cw[CONV_W - 1 - s:CONV_W - s]
        return y

    y = jnp.concatenate([taps(x[:8], True), taps(x, False)[8:]], axis=0)
    return y * _sigmoid(y)


def _l2norm(x):
    return x * lax.rsqrt(jnp.sum(x * x, axis=-1, keepdims=True) + EPS)


def _gdn_prompt_kernel(q_ref, k_ref, v_ref, z_ref, sm_ref, cwq_ref, cwk_ref, cwv_ref, alog_ref, dtb_ref, ng_ref,
                       o_ref, s_out_ref,
                       qn_s, kn_s, vn_s, gc_s, bt_s, l_s, a_s, x_s, u_s, w_s, mn_s, sts_s):
    seq = q_ref.shape[1]
    hd = HEAD_DIM
    grp = GDN_GROUP
    chunk = GDN_CHUNK
    n_grp = seq // grp
    n_chunk = seq // chunk
    cpg = grp // chunk
    h = pl.program_id(1)

    row = lax.broadcasted_iota(I32, (seq, hd), 0)
    lane = lax.broadcasted_iota(I32, (seq, hd), 1)

    qn_s[...] = _l2norm(_conv_silu(q_ref[0], cwq_ref[...], row)) * (hd ** -0.5)
    kn_s[...] = _l2norm(_conv_silu(k_ref[0], cwk_ref[...], row))
    vn_s[...] = _conv_silu(v_ref[0], cwv_ref[...], row)

    sm = sm_ref[0]
    g_all = -jnp.exp(alog_ref[...]) * _softplus(sm + dtb_ref[...])
    b_all = _sigmoid(sm)
    g_col = jnp.sum(jnp.where(lane == h + SM_A, g_all, 0.0), axis=1, keepdims=True)
    b_col = jnp.sum(jnp.where(lane == h + SM_B, b_all, 0.0), axis=1, keepdims=True)
    gc = jnp.broadcast_to(g_col, (seq, hd))
    pos = row & (chunk - 1)
    s = 1
    while s < chunk:
        gc = gc + jnp.where(pos >= s, pltpu.roll(gc, s, 0), 0.0)
        s *= 2
    gc_s[...] = gc
    bt_s[...] = jnp.broadcast_to(b_col, (seq, hd))

    ii = lax.broadcasted_iota(I32, (grp, grp), 0)
    jj = lax.broadcasted_iota(I32, (grp, grp), 1)
    xr = ii ^ jj
    same = (xr >> int(math.log2(chunk))) == 0
    causal = same & (ii >= jj)
    strict = same & (ii > jj)
    eye = (ii == jj).astype(F32)

    def build(gi, _):
        r0 = pl.multiple_of(gi * grp, grp)
        kk = kn_s[pl.ds(r0, grp), :]
        qq = qn_s[pl.ds(r0, grp), :]
        bt = bt_s[pl.ds(r0, grp), :]
        gcg = gc_s[pl.ds(r0, grp), :]
        kb = (kk * bt).astype(BF16)
        kkb = kk.astype(BF16)
        kkt = _dot_nt(kb, kkb)
        qkt = _dot_nt(qq.astype(BF16), kkb)
        colb = jnp.concatenate([gcg, gcg], axis=1)
        diff = colb - colb.T
        decay = jnp.where(causal, jnp.exp(jnp.where(causal, diff, 0.0)), 0.0)
        lmat = jnp.where(strict, kkt * decay, 0.0)
        l_s[gi] = lmat
        a_s[pl.ds(r0, grp), :] = jnp.where(causal, qkt * decay, 0.0)
        x_s[gi] = eye - jnp.where(xr == 1, lmat, 0.0)
        return 0

    lax.fori_loop(0, n_grp, build, 0, unroll=2)

    for lvl in range(1, int(math.log2(chunk))):
        def level(gi, _, lvl=lvl):
            xm = x_s[gi]
            cl = jnp.where((xr >> lvl) == 1, l_s[gi], 0.0).astype(BF16)
            xb = xm.astype(BF16)
            x_s[gi] = xm - _dot(_dot(xb, cl).astype(BF16), xb)
            return 0

        lax.fori_loop(0, n_grp, level, 0, unroll=True)

    def apply(gi, _):
        r0 = pl.multiple_of(gi * grp, grp)
        kk = kn_s[pl.ds(r0, grp), :]
        vv = vn_s[pl.ds(r0, grp), :]
        bt = bt_s[pl.ds(r0, grp), :]
        gcg = gc_s[pl.ds(r0, grp), :]
        rhs = jnp.concatenate([vv * bt, kk * bt * jnp.exp(gcg)], axis=1).astype(BF16)
        uw = _dot(x_s[gi].astype(BF16), rhs)
        u_s[pl.ds(r0, grp), :] = uw[:, :hd]
        w_s[pl.ds(r0, grp), :] = uw[:, hd:]
        for cc in range(cpg):
            lo = cc * chunk
            g_last = gcg[lo + chunk - 1:lo + chunk, :]
            k_dec = (kk[lo:lo + chunk] * jnp.exp(g_last - gcg[lo:lo + chunk])).astype(BF16)
            wu = jnp.concatenate([uw[lo:lo + chunk, hd:], uw[lo:lo + chunk, :hd]], axis=1).astype(BF16)
            mn_s[gi * cpg + cc] = _dot_tn(k_dec, wu)
        return 0

    lax.fori_loop(0, n_grp, apply, 0, unroll=2)

    def scan(c, st):
        sts_s[c] = st
        g_last = gc_s[pl.ds(c * chunk + chunk - 1, 1), :]
        mn = mn_s[c]
        return st * jnp.exp(g_last) - _dot(mn[:, :hd].astype(BF16), st.astype(BF16)) + mn[:, hd:]

    s_out_ref[0, 0] = lax.fori_loop(0, n_chunk, scan, jnp.zeros((hd, hd), F32))

    ng = ng_ref[...]

    def emit(gi, _):
        r0 = pl.multiple_of(gi * grp, grp)
        gcg = gc_s[pl.ds(r0, grp), :]
        qg = qn_s[pl.ds(r0, grp), :] * jnp.exp(gcg)
        wg = w_s[pl.ds(r0, grp), :]
        ws, qs = [], []
        for cc in range(cpg):
            lo = cc * chunk
            stb = sts_s[gi * cpg + cc].astype(BF16)
            both = _dot(jnp.concatenate([wg[lo:lo + chunk], qg[lo:lo + chunk]], axis=0).astype(BF16), stb)
            ws.append(both[:chunk])
            qs.append(both[chunk:])
        v_new = u_s[pl.ds(r0, grp), :] - jnp.concatenate(ws, axis=0)
        o = jnp.concatenate(qs, axis=0) + _dot(a_s[pl.ds(r0, grp), :].astype(BF16), v_new.astype(BF16))
        zc = z_ref[0, pl.ds(r0, grp), :]
        on = o * lax.rsqrt(jnp.mean(o * o, axis=-1, keepdims=True) + EPS) * ng
        o_ref[0, pl.ds(r0, grp), :] = (on * (zc * _sigmoid(zc))).astype(o_ref.dtype)
        return 0

    lax.fori_loop(0, n_grp, emit, 0, unroll=2)


def _lane_vec(v, off):
    return jnp.zeros((1, LANE), F32).at[0, off:off + v.shape[0]].set(v.astype(F32))


def _gdn_prompt_call(proj3, conv_w, a_log, dt_bias, norm_g):
    bsz, seq, _ = proj3.shape
    hd = HEAD_DIM
    nh = N_HEADS_A
    n_grp = seq // GDN_GROUP
    col = lambda base: (lambda b, h: (b, 0, base // hd + h))
    cw = lambda base: (lambda b, h: (0, base // hd + h))
    vec = pl.BlockSpec((1, LANE), lambda b, h: (0, 0))
    return pl.pallas_call(
        _gdn_prompt_kernel,
        out_shape=(jax.ShapeDtypeStruct((bsz, seq, VAL_DIM_A), BF16),
                   jax.ShapeDtypeStruct((bsz, nh, hd, hd), F32)),
        grid=(bsz, nh),
        in_specs=[pl.BlockSpec((1, seq, hd), col(OFF_CONV)),
                  pl.BlockSpec((1, seq, hd), col(OFF_CONV + KEY_DIM_A)),
                  pl.BlockSpec((1, seq, hd), col(OFF_CONV + 2 * KEY_DIM_A)),
                  pl.BlockSpec((1, seq, hd), col(OFF_Z)),
                  pl.BlockSpec((1, seq, LANE), lambda b, h: (b, 0, OFF_SM // LANE)),
                  pl.BlockSpec((CONV_W, hd), cw(0)),
                  pl.BlockSpec((CONV_W, hd), cw(KEY_DIM_A)),
                  pl.BlockSpec((CONV_W, hd), cw(2 * KEY_DIM_A)),
                  vec, vec, vec],
        out_specs=(pl.BlockSpec((1, seq, hd), lambda b, h: (b, 0, h)),
                   pl.BlockSpec((1, 1, hd, hd), lambda b, h: (b, h, 0, 0))),
        scratch_shapes=[pltpu.VMEM((seq, hd), F32), pltpu.VMEM((seq, hd), F32), pltpu.VMEM((seq, hd), F32),
                        pltpu.VMEM((seq, hd), F32), pltpu.VMEM((seq, hd), F32),
                        pltpu.VMEM((n_grp, GDN_GROUP, GDN_GROUP), F32),
                        pltpu.VMEM((seq, GDN_GROUP), F32),
                        pltpu.VMEM((n_grp, GDN_GROUP, GDN_GROUP), F32),
                        pltpu.VMEM((seq, hd), F32), pltpu.VMEM((seq, hd), F32),
                        pltpu.VMEM((seq // GDN_CHUNK, hd, 2 * hd), F32), pltpu.VMEM((seq // GDN_CHUNK, hd, hd), F32)],
        compiler_params=_cparams(("arbitrary", "arbitrary")),
        name="gdn_prompt",
    )(proj3, proj3, proj3, proj3, proj3, conv_w, conv_w, conv_w,
      _lane_vec(a_log, SM_A), _lane_vec(dt_bias, SM_A), norm_g.reshape(1, hd).astype(F32))


def _t5_bucket(dist):
    n = jnp.maximum(dist, 0)
    max_exact = N_BUCKETS // 2
    nf = jnp.maximum(n, 1).astype(F32)
    large = max_exact + (jnp.log(nf / max_exact) / math.log(MAX_DISTANCE / max_exact)
                         * (N_BUCKETS - max_exact)).astype(I32)
    large = jnp.minimum(large, N_BUCKETS - 1)
    return jnp.where(n < max_exact, n, large)


def _bias_from_bucket(bucket, rb_ref, head):
    out = jnp.zeros(bucket.shape, F32)
    for b in range(N_BUCKETS):
        out = jnp.where(bucket == b, rb_ref[b, head], out)
    return out


def _order_key(score):
    bits = pltpu.bitcast(score + 0.0, I32)
    return bits ^ ((bits >> 31) & 0x7FFFFFFF)


I16 = jnp.int16
HALF_BIAS = 1 << 15
HALF_ROWS = 16
KEY_POS_INF = 0x7F800000
KEY_NEG_INF = INT_MIN + 0x7FFFFF


def _kth_largest_key(count_ge, shape, n_sel):
    def body(i, tau):
        bit = 31 - i
        cand = jnp.where(i == 0, tau ^ INT_MIN, tau | (1 << bit))
        return jnp.where(count_ge(cand) >= n_sel, cand, tau)

    return lax.fori_loop(0, 32, body, jnp.full(shape, INT_MIN, I32))


def _attn_prompt_kernel(n_sel, qi_ref, sm_ref, qb_ref, ki_ref, k_ref, va_ref, rb_ref, o_ref,
                        qi_s, qb_s, wt_s, keys_s, hi_s, lo_s, tau_s, bias_s, m_s, acc_s):
    tq, tk = ATT_TQ, ATT_TK
    hd = HEAD_DIM
    bi = pl.program_id(0)
    qblk = pl.program_id(1)
    n_kb = (qblk * tq) // tk + 1
    krow = lax.broadcasted_iota(I32, (tk, tq), 0)
    qcol = lax.broadcasted_iota(I32, (tk, tq), 1)

    @pl.when((bi == 0) & (qblk == 0))
    def _():
        for off in range(bias_s.shape[1]):
            bucket = _t5_bucket(off * tk + krow - qcol)
            for h in range(N_HEADS_B):
                bias_s[h, off] = _bias_from_bucket(bucket, rb_ref, h)

    qi_s[...] = qi_ref[0].astype(BF16)
    qb_s[...] = qb_ref[0].astype(BF16)
    wt_s[...] = sm_ref[0].T * (N_IDX_HEADS ** -0.5)
    t_pos = qblk * tq + qcol

    def idx_body(j, _):
        k0 = pl.multiple_of(j * tk, tk)
        kt = ki_ref[0, pl.ds(k0, tk), :]
        acc = jnp.zeros((tk, tq), F32)
        for h in range(N_IDX_HEADS):
            s = _dot_nt(kt, qi_s[:, h * D_IDX:(h + 1) * D_IDX])
            acc = acc + jnp.maximum(s, 0.0) * wt_s[SM_W + h:SM_W + h + 1, :]
        acc = acc * (D_IDX ** -0.5)
        key = jnp.where(k0 + krow <= t_pos, _order_key(acc), INT_MIN)
        keys_s[j] = key
        hi_s[j] = (key >> 16).astype(I16)
        lo_s[j] = ((key & 0xFFFF) - HALF_BIAS).astype(I16)
        return 0

    lax.fori_loop(0, n_kb, idx_body, 0)

    def kth_largest_half(half_s, n_at_start):
        def count_ge(cand):
            def cb(j, c):
                hit = jnp.where(half_s[j] >= cand, jnp.int16(1), jnp.int16(0))
                for r in range(tk // HALF_ROWS):
                    c = c + hit[r * HALF_ROWS:(r + 1) * HALF_ROWS]
                return c

            c = lax.fori_loop(0, n_kb, cb, jnp.zeros((HALF_ROWS, tq), I16))
            return jnp.sum(c.astype(I32), axis=0, keepdims=True)

        def body(i, carry):
            t, n_t = carry
            cand = jnp.where(i == 0, t ^ (-HALF_BIAS), t | (1 << (15 - i)))
            n_cand = count_ge(cand.astype(I16))
            ok = n_cand >= n_sel
            return jnp.where(ok, cand, t), jnp.where(ok, n_cand, n_t)

        return lax.fori_loop(0, 16, body, (jnp.full((1, tq), -HALF_BIAS, I32), n_at_start))

    tau_hi, n_ge_hi = kth_largest_half(hi_s, jnp.zeros((1, tq), I32))
    tau_hi16 = tau_hi.astype(I16)

    def narrow(j, _):
        hi = hi_s[j]
        lo_s[j] = jnp.where(hi > tau_hi16, jnp.int16(HALF_BIAS - 1),
                            jnp.where(hi == tau_hi16, lo_s[j], jnp.int16(-HALF_BIAS)))
        return 0

    lax.fori_loop(0, n_kb, narrow, 0)
    tau_lo, n_ge_tau = kth_largest_half(lo_s, n_ge_hi)
    tau = (tau_hi << 16) | ((tau_lo + HALF_BIAS) & 0xFFFF)
    tau = jnp.maximum(tau, KEY_NEG_INF + 1)

    def count_where(pred):
        def cb(j, c):
            hit = jnp.where(pred(keys_s[j], j * tk + krow), 1, 0)
            return c + jnp.sum(hit.reshape(tk // 8, 8, tq), axis=0)

        c = lax.fori_loop(0, n_kb, cb, jnp.zeros((8, tq), I32))
        return jnp.sum(c, axis=0, keepdims=True)

    @pl.when(jnp.max(n_ge_tau) > n_sel)
    def _():
        quota = n_sel - count_where(lambda k, pos: k > tau)
        pos_bits = (keys_s.shape[0] * tk - 1).bit_length()

        def body(i, last):
            cand = last | (1 << (pos_bits - 1 - i))
            before = count_where(lambda k, pos: (k == tau) & (pos < cand))
            return jnp.where(before < quota, cand, last)

        last_kept = lax.fori_loop(0, pos_bits, body, jnp.zeros((1, tq), I32))

        def demote(j, _):
            k = keys_s[j]
            keys_s[j] = jnp.where((k == tau) & (j * tk + krow > last_kept), tau - 1, k)
            return 0

        lax.fori_loop(0, n_kb, demote, 0)

    scale = hd ** -0.5
    lane_reps = tk // LANE
    tau_rep = jnp.broadcast_to(tau, (LANE, tq)).T
    tau_s[...] = jnp.concatenate([tau_rep] * lane_reps, axis=1)

    def block_inputs(j):
        kq = keys_s[j].T
        sel = jnp.where(kq < KEY_POS_INF, kq, INT_MIN) >= tau_s[...]
        return k_ref[0, pl.ds(pl.multiple_of(j * tk, tk), tk), :], sel

    def masked_logits(j, h, kt, sel):
        n = h // GQA_GROUP
        boff = jnp.minimum(qblk - j, bias_s.shape[1] - 1)
        lg = _dot_nt(qb_s[:, h * hd:(h + 1) * hd], kt[:, n * hd:(n + 1) * hd]) * scale + bias_s[h, boff]
        return jnp.where(sel, lg, NEG_BIG)

    m_s[...] = jnp.full(m_s.shape, NEG_BIG, F32)

    def max_body(j, _):
        kt, sel = block_inputs(j)
        for h in range(N_HEADS_B):
            lg = masked_logits(j, h, kt, sel)
            part = lg[:, :LANE]
            for r in range(1, lane_reps):
                part = jnp.maximum(part, lg[:, r * LANE:(r + 1) * LANE])
            m_s[h] = jnp.maximum(m_s[h], part)
        return 0

    lax.fori_loop(0, n_kb, max_body, 0)
    for h in range(N_HEADS_B):
        m_s[h] = jnp.broadcast_to(jnp.max(m_s[h], axis=1, keepdims=True), (tq, LANE))
    acc_s[...] = jnp.zeros(acc_s.shape, F32)

    def sum_body(j, _):
        kt, sel = block_inputs(j)
        va = va_ref[0, pl.ds(pl.multiple_of(j * tk, tk), tk), :]
        for h in range(N_HEADS_B):
            n = h // GQA_GROUP
            mrow = jnp.concatenate([m_s[h]] * lane_reps, axis=1)
            p = jnp.exp(masked_logits(j, h, kt, sel) - mrow)
            acc_s[h] += _dot(p.astype(BF16), va[:, n * 2 * hd:(n + 1) * 2 * hd])
        return 0

    lax.fori_loop(0, n_kb, sum_body, 0)
    for h in range(N_HEADS_B):
        a = acc_s[h]
        o_ref[0, :, h * hd:(h + 1) * hd] = (a[:, :hd] / a[:, hd:hd + 1]).astype(o_ref.dtype)


def _attn_prompt_call(proj3, ki_bf, k_bf, va_bf, rel_bias):
    bsz, seq, _ = proj3.shape
    tq, tk = ATT_TQ, ATT_TK
    assert tq == tk, "the bias-tile offsets assume square blocks"
    n_sel = min(TOPK_MAX, seq // 4)
    qiw = N_IDX_HEADS * D_IDX
    qbw = N_HEADS_B * HEAD_DIM
    kvw = N_KV_B * HEAD_DIM
    n_off = -(-(MAX_DISTANCE + tk - 1) // tk) + 1
    return pl.pallas_call(
        functools.partial(_attn_prompt_kernel, n_sel),
        out_shape=jax.ShapeDtypeStruct((bsz, seq, qbw), BF16),
        grid=(bsz, seq // tq),
        in_specs=[pl.BlockSpec((1, tq, qiw), lambda b, i: (b, i, OFF_QI // qiw)),
                  pl.BlockSpec((1, tq, LANE), lambda b, i: (b, i, OFF_SM // LANE)),
                  pl.BlockSpec((1, tq, qbw), lambda b, i: (b, i, OFF_QB // qbw)),
                  pl.BlockSpec((1, seq, D_IDX), lambda b, i: (b, 0, 0)),
                  pl.BlockSpec((1, seq, kvw), lambda b, i: (b, 0, 0)),
                  pl.BlockSpec((1, seq, 2 * kvw), lambda b, i: (b, 0, 0)),
                  pl.BlockSpec(memory_space=pltpu.SMEM)],
        out_specs=pl.BlockSpec((1, tq, qbw), lambda b, i: (b, i, 0)),
        scratch_shapes=[pltpu.VMEM((tq, qiw), BF16), pltpu.VMEM((tq, qbw), BF16),
                        pltpu.VMEM((LANE, tq), F32),
                        pltpu.VMEM((seq // tk, tk, tq), I32),
                        pltpu.VMEM((seq // tk, tk, tq), I16), pltpu.VMEM((seq // tk, tk, tq), I16),
                        pltpu.VMEM((tq, tk), I32),
                        pltpu.VMEM((N_HEADS_B, n_off, tq, tk), F32),
                        pltpu.VMEM((N_HEADS_B, tq, LANE), F32),
                        pltpu.VMEM((N_HEADS_B, tq, 2 * HEAD_DIM), F32)],
        compiler_params=_cparams(("arbitrary", "arbitrary")),
        name="attn_prompt",
    )(proj3, proj3, proj3, ki_bf, k_bf, va_bf, rel_bias.astype(F32))


def _rms(x, g):
    return x * lax.rsqrt(jnp.mean(x * x, axis=-1, keepdims=True) + EPS) * g


def _outproj_kernel(oa_ref, ob_ref, x_ref, wa_ref, wb_ref, pg_ref, g1_ref, p2_ref, sh2_ref, sc2_ref, x1_ref, h2_ref):
    rows = x_ref.shape[0]
    per_row = _mod(g1_ref).shape[0] == rows
    n_piece = 2 if rows % 16 == 0 and rows >= 256 else 1
    for r in range(n_piece):
        sl = slice(r * rows // n_piece, (r + 1) * rows // n_piece)
        pick = (lambda ref: _mod(ref)[sl]) if per_row else _mod
        mix = _dot(oa_ref[sl, :], wa_ref[...]) + _dot(ob_ref[sl, :], wb_ref[...])
        x1 = x_ref[sl, :] + pick(g1_ref) * _rms(mix, pg_ref[...])
        x1_ref[sl, :] = x1
        h2_ref[sl, :] = (_rms(x1, p2_ref[...]) * (1.0 + pick(sc2_ref)) + pick(sh2_ref)).astype(BF16)


def _outproj_call(oa, ob, x2d, w_out_bf, post1_g, pre2_g, mod, tm):
    m_rows = x2d.shape[0]
    half = w_out_bf.shape[0] // 2
    row = lambda i: (i, 0)
    vec = pl.BlockSpec((1, D_MODEL), lambda i: (0, 0))
    return pl.pallas_call(
        _outproj_kernel,
        out_shape=(jax.ShapeDtypeStruct((m_rows, D_MODEL), F32), jax.ShapeDtypeStruct((m_rows, D_MODEL), BF16)),
        grid=(m_rows // tm,),
        in_specs=[pl.BlockSpec((tm, half), row), pl.BlockSpec((tm, half), row), pl.BlockSpec((tm, D_MODEL), row),
                  pl.BlockSpec((half, D_MODEL), lambda i: (0, 0)), pl.BlockSpec((half, D_MODEL), lambda i: (1, 0)),
                  vec, mod.spec(2, 1), vec, mod.spec(3, 1), mod.spec(4, 1)],
        out_specs=(pl.BlockSpec((tm, D_MODEL), row), pl.BlockSpec((tm, D_MODEL), row)),
        compiler_params=_cparams(("arbitrary",)),
        name="outproj",
    )(oa, ob, x2d, w_out_bf, w_out_bf, post1_g.reshape(1, D_MODEL), mod.arr, pre2_g.reshape(1, D_MODEL),
      mod.arr, mod.arr)


def _ffn_kernel(h_ref, x1_ref, w1_ref, w2_ref, pg_ref, g2_ref, o_ref, acc_s):
    kk = pl.program_id(1)

    @pl.when(kk == 0)
    def _():
        acc_s[...] = jnp.zeros_like(acc_s)

    a = jnp.maximum(_dot(h_ref[...], w1_ref[...]), 0.0)
    acc_s[...] += _dot((a * a).astype(BF16), w2_ref[...])

    @pl.when(kk == pl.num_programs(1) - 1)
    def _():
        o_ref[...] = x1_ref[...] + _mod(g2_ref) * _rms(acc_s[...], pg_ref[...])


def _ffn_call(h2, x1, w1_bf, w2_bf, post2_g, mod, tm, tf):
    m_rows = x1.shape[0]
    return pl.pallas_call(
        _ffn_kernel,
        out_shape=jax.ShapeDtypeStruct((m_rows, D_MODEL), F32),
        grid=(m_rows // tm, D_FF // tf),
        in_specs=[pl.BlockSpec((tm, D_MODEL), lambda i, k: (i, 0)),
                  pl.BlockSpec((tm, D_MODEL), lambda i, k: (i, 0)),
                  pl.BlockSpec((D_MODEL, tf), lambda i, k: (0, k)),
                  pl.BlockSpec((tf, D_MODEL), lambda i, k: (k, 0)),
                  pl.BlockSpec((1, D_MODEL), lambda i, k: (0, 0)),
                  mod.spec(5, 2)],
        out_specs=pl.BlockSpec((tm, D_MODEL), lambda i, k: (i, 0)),
        scratch_shapes=[pltpu.VMEM((tm, D_MODEL), F32)],
        compiler_params=_cparams(("arbitrary", "arbitrary")),
        name="ffn",
    )(h2, x1, w1_bf, w2_bf, post2_g.reshape(1, D_MODEL), mod.arr)


def _gdn_sample_kernel(cin_ref, z_ref, sm_ref, sconv_ref, ssm_ref, cw_ref, alog_ref, dtb_ref, ng_ref,
                       o_ref, ssm_out_ref, conv_out_ref):
    hd = HEAD_DIM
    nh = N_HEADS_A
    prev = sconv_ref[0]
    xin = cin_ref[0]
    cw = cw_ref[...]
    y = xin * cw[CONV_W - 1:CONV_W]
    for s in range(CONV_W - 1):
        y = y + prev[s:s + 1] * cw[s:s + 1]
    y = y * _sigmoid(y)
    conv_out_ref[0, 0:CONV_W - 2, :] = prev[1:CONV_W - 1]
    conv_out_ref[0, CONV_W - 2:CONV_W - 1, :] = xin

    def heads(base):
        return jnp.concatenate([y[:, base + h * hd:base + (h + 1) * hd] for h in range(nh)], axis=0)

    q8 = _l2norm(heads(0)) * (hd ** -0.5)
    k8 = _l2norm(heads(KEY_DIM_A))
    v8 = heads(2 * KEY_DIM_A)
    sm = sm_ref[0]
    g_all = -jnp.exp(alog_ref[...]) * _softplus(sm + dtb_ref[...])
    b_all = _sigmoid(sm)
    kt = jnp.concatenate([k8, jnp.zeros((LANE - nh, hd), F32)], axis=0).T
    k8b = k8.astype(BF16)
    q8b = q8.astype(BF16)
    z = z_ref[0]
    ng = ng_ref[...]
    for h in range(nh):
        s1 = ssm_ref[0, h] * jnp.exp(g_all[:, SM_A + h:SM_A + h + 1])
        kv = _dot(k8b, s1.astype(BF16))[h:h + 1]
        delta = (v8[h:h + 1] - kv) * b_all[:, SM_B + h:SM_B + h + 1]
        s2 = s1 + kt[:, h:h + 1] * delta
        ssm_out_ref[0, h] = s2
        o = _dot(q8b, s2.astype(BF16))[h:h + 1]
        zh = z[:, h * hd:(h + 1) * hd]
        on = o * lax.rsqrt(jnp.mean(o * o, axis=-1, keepdims=True) + EPS) * ng
        o_ref[0, :, h * hd:(h + 1) * hd] = (on * (zh * _sigmoid(zh))).astype(o_ref.dtype)


def _gdn_sample_call(proj_s3, state_conv, state_ssm, conv_w, a_log, dt_bias, norm_g):
    nb = proj_s3.shape[0]
    hd, nh = HEAD_DIM, N_HEADS_A
    vec = pl.BlockSpec((1, LANE), lambda b: (0, 0))
    return pl.pallas_call(
        _gdn_sample_kernel,
        out_shape=(jax.ShapeDtypeStruct((nb, 1, VAL_DIM_A), BF16),
                   jax.ShapeDtypeStruct((nb, nh, hd, hd), F32),
                   jax.ShapeDtypeStruct((nb, CONV_W - 1, CONV_DIM), F32)),
        grid=(nb,),
        in_specs=[pl.BlockSpec((1, 1, CONV_DIM), lambda b: (b, 0, OFF_CONV // CONV_DIM)),
                  pl.BlockSpec((1, 1, VAL_DIM_A), lambda b: (b, 0, OFF_Z // VAL_DIM_A)),
                  pl.BlockSpec((1, 1, LANE), lambda b: (b, 0, OFF_SM // LANE)),
                  pl.BlockSpec((1, CONV_W - 1, CONV_DIM), lambda b: (b, 0, 0)),
                  pl.BlockSpec((1, nh, hd, hd), lambda b: (b, 0, 0, 0)),
                  pl.BlockSpec((CONV_W, CONV_DIM), lambda b: (0, 0)),
                  vec, vec, vec],
        out_specs=(pl.BlockSpec((1, 1, VAL_DIM_A), lambda b: (b, 0, 0)),
                   pl.BlockSpec((1, nh, hd, hd), lambda b: (b, 0, 0, 0)),
                   pl.BlockSpec((1, CONV_W - 1, CONV_DIM), lambda b: (b, 0, 0))),
        compiler_params=_cparams(("arbitrary",)),
        name="gdn_sample",
    )(proj_s3, proj_s3, proj_s3, state_conv, state_ssm, conv_w,
      _lane_vec(a_log, SM_A), _lane_vec(dt_bias, SM_A), norm_g.reshape(1, hd).astype(F32))


IDX_PAGES_PER_MATMUL = 4


def _idx_sample_kernel(pt_sm, qi_ref, wi_ref, kidx_hbm, o_ref, buf, sem):
    b = pl.program_id(0)
    n_seq = pl.num_programs(0)
    n_pages = buf.shape[1]

    def page_copy(page, slot, p):
        return pltpu.make_async_copy(kidx_hbm.at[page], buf.at[slot, p], sem.at[slot])

    def gather(seq, slot):
        def body(p, _):
            page_copy(pt_sm[seq, p], slot, p).start()
            return 0

        lax.fori_loop(0, n_pages, body, 0, unroll=8)

    @pl.when(b == 0)
    def _():
        gather(0, 0)

    @pl.when(b + 1 < n_seq)
    def _():
        gather(b + 1, (b + 1) % 2)

    slot = b % 2

    def wait_body(p, _):
        page_copy(0, slot, p).wait()
        return 0

    lax.fori_loop(0, n_pages, wait_body, 0, unroll=8)

    qb = qi_ref[0].astype(BF16)
    w = wi_ref[0] * (N_IDX_HEADS ** -0.5)

    ppm = math.gcd(n_pages, IDX_PAGES_PER_MATMUL)

    def score(i, _):
        p0 = pl.multiple_of(i * ppm, ppm)
        keys = buf[slot, pl.ds(p0, ppm)].reshape(ppm * PAGE_SIZE, D_IDX).astype(BF16)
        s = _dot_nt(qb, keys)
        sc = jnp.sum(jnp.maximum(s, 0.0) * w, axis=0, keepdims=True) * (D_IDX ** -0.5)
        for k in range(ppm):
            o_ref[0, pl.ds(p0 + k, 1), :] = sc[:, k * PAGE_SIZE:(k + 1) * PAGE_SIZE]
        return 0

    lax.fori_loop(0, n_pages // ppm, score, 0, unroll=4)


def _idx_sample_call(page_table, qi3, wi3, cache_kidx):
    nb, n_pages = page_table.shape
    grid_spec = pltpu.PrefetchScalarGridSpec(
        num_scalar_prefetch=1,
        grid=(nb,),
        in_specs=[pl.BlockSpec((1, N_IDX_HEADS, D_IDX), lambda b, pt: (b, 0, 0)),
                  pl.BlockSpec((1, N_IDX_HEADS, 1), lambda b, pt: (b, 0, 0)),
                  pl.BlockSpec(memory_space=pl.ANY)],
        out_specs=pl.BlockSpec((1, n_pages, PAGE_SIZE), lambda b, pt: (b, 0, 0)),
        scratch_shapes=[pltpu.VMEM((2, n_pages, PAGE_SIZE, D_IDX), F32), pltpu.SemaphoreType.DMA((2,))],
    )
    return pl.pallas_call(
        _idx_sample_kernel,
        out_shape=jax.ShapeDtypeStruct((nb, n_pages, PAGE_SIZE), F32),
        grid_spec=grid_spec,
        compiler_params=_cparams(("arbitrary",)),
        name="idx_sample",
    )(page_table, qi3, wi3, cache_kidx)


IDX_FLAG = 1 << 20


def _topk_sample_kernel(n_sel, sc_ref, qi_ref, wi_ref, knew_ref, idx_ref, keys_s, keyw_s, rank_s, flag_s, cnt_s):
    nb, n_past = sc_ref.shape
    n_blk = n_past // LANE
    lane = lax.broadcasted_iota(I32, (nb, LANE), 1)

    kn = knew_ref[...].astype(BF16)
    q = qi_ref[...]
    w = wi_ref[...] * (N_IDX_HEADS ** -0.5)
    s_new = jnp.zeros((nb, 1), F32)
    for h in range(N_IDX_HEADS):
        qh = q[:, h * D_IDX:(h + 1) * D_IDX].astype(BF16).astype(F32)
        dot = jnp.sum(qh * kn.astype(F32), axis=1, keepdims=True)
        s_new = s_new + jnp.maximum(dot, 0.0) * w[:, h:h + 1]
    s_new = s_new * (D_IDX ** -0.5)
    key_new = _order_key(s_new)

    def to_keys(j, _):
        keys_s[j] = _order_key(sc_ref[:, pl.ds(pl.multiple_of(j * LANE, LANE), LANE)])
        return 0

    lax.fori_loop(0, n_blk, to_keys, 0, unroll=8)
    keyw_s[...] = _order_key(sc_ref[...])
    wide = min(n_past, 16 * LANE)

    def count(pred):
        acc = jnp.zeros((nb, LANE), I32)
        for c in range(n_past // wide):
            hit = jnp.where(pred(keyw_s[:, c * wide:(c + 1) * wide]), 1, 0)
            for k in range(wide // LANE):
                acc = acc + hit[:, k * LANE:(k + 1) * LANE]
        return jnp.sum(acc, axis=1, keepdims=True)

    def count_ge(cand):
        return count(lambda k: k >= cand) + jnp.where(key_new >= cand, 1, 0)

    tau = jnp.maximum(_kth_largest_key(count_ge, (nb, 1), n_sel), INT_MIN + 1)

    quota = n_sel - (count(lambda k: k > tau) + jnp.where(key_new > tau, 1, 0))

    ri = lax.broadcasted_iota(I32, (LANE, LANE), 0)
    ci = lax.broadcasted_iota(I32, (LANE, LANE), 1)
    upper = (ri <= ci).astype(BF16)

    def excl_prefix(flag):
        return (_dot(flag.astype(BF16), upper) - flag).astype(I32)

    def rank_body(j, carry):
        base, base_eq = carry
        kj = keys_s[j]
        eq = kj == tau
        eqf = jnp.where(eq, 1.0, 0.0)
        take = (kj > tau) | (eq & (base_eq + excl_prefix(eqf) < quota))
        takef = jnp.where(take, 1.0, 0.0)
        n_take = jnp.sum(takef, axis=1, keepdims=True)
        rows = pl.ds(pl.multiple_of(j * nb, nb), nb)
        rank_s[rows, :] = jnp.where(take, (base + excl_prefix(takef) + 1).astype(F32), 0.0)
        flag_s[rows, :] = jnp.where((kj < KEY_POS_INF) & (kj > KEY_NEG_INF), 0.0, 1.0)
        cnt_s[rows, :] = jnp.broadcast_to(n_take, (nb, LANE))
        return (base + n_take.astype(I32), base_eq + jnp.sum(eqf, axis=1, keepdims=True).astype(I32))

    zero = jnp.zeros((nb, 1), I32)
    n_before_new, eq_before_new = lax.fori_loop(0, n_blk, rank_body, (zero, zero), unroll=4)

    new_sel = (key_new > tau) | ((key_new == tau) & (eq_before_new < quota))
    new_fin = (key_new < KEY_POS_INF) & (key_new > KEY_NEG_INF)
    new_pos = jnp.where(new_fin, n_past, n_past | IDX_FLAG)
    bi = lax.broadcasted_iota(I32, (n_blk, n_blk), 0)
    bj = lax.broadcasted_iota(I32, (n_blk, n_blk), 1)
    before = (bj < bi).astype(BF16)
    blk_id = lax.broadcasted_iota(I32, (n_blk, LANE), 0).astype(F32)
    slot_lane = lax.broadcasted_iota(I32, (n_blk, n_sel), 1).astype(F32)
    slot_row = lax.broadcasted_iota(I32, (n_sel, LANE), 0)
    lane_sel = lax.broadcasted_iota(I32, (n_sel, LANE), 1)
    reps = n_sel // LANE

    def per_seq(b, _):
        rows = pl.ds(b, n_blk, stride=nb)
        cnt = cnt_s[rows, :]
        first = _dot(before, cnt.astype(BF16))
        first_w = jnp.concatenate([first] * reps, axis=1)
        cnt_w = jnp.concatenate([cnt] * reps, axis=1)
        onehot = jnp.where((slot_lane >= first_w) & (slot_lane < first_w + cnt_w), 1.0, 0.0).astype(BF16)
        picked = _dot_tn(onehot, jnp.concatenate([rank_s[rows, :], flag_s[rows, :], blk_id], axis=1).astype(BF16))
        hit = picked[:, :LANE] == (slot_row + 1).astype(F32)
        pos = (picked[:, 2 * LANE:] * LANE + picked[:, LANE:2 * LANE] * IDX_FLAG).astype(I32) + lane_sel
        col = jnp.sum(jnp.where(hit, pos, 0), axis=1, keepdims=True)
        rb = lax.broadcasted_iota(I32, (nb, 1), 0) == b
        nbn = jnp.sum(jnp.where(rb, n_before_new, 0), axis=0, keepdims=True)
        nsel = jnp.sum(jnp.where(rb & new_sel, 1, 0), axis=0, keepdims=True)
        npos = jnp.sum(jnp.where(rb, new_pos, 0), axis=0, keepdims=True)
        col = col + jnp.where((slot_row[:, 0:1] == nbn) & (nsel > 0), npos, 0)
        idx_ref[b] = col
        return 0

    lax.fori_loop(0, nb, per_seq, 0)


def _attn_sample_kernel(n_past, idx_sm, pt_sm, idxc_ref, q_ref, kn_ref, vn_ref, rb_ref, ck_hbm, cv_hbm, o_ref,
                        kbuf, vbuf, sem):
    b = pl.program_id(0)
    n_seq = pl.num_programs(0)
    n_sel = kbuf.shape[1] // N_KV_B
    hd = HEAD_DIM
    page_shift = int(math.log2(PAGE_SIZE))

    def row_copies(buf, r, page, slot):
        dst = pl.ds(N_KV_B * r, N_KV_B)
        return (pltpu.make_async_copy(ck_hbm.at[page, slot], kbuf.at[buf, dst, :], sem.at[0, buf]),
                pltpu.make_async_copy(cv_hbm.at[page, slot], vbuf.at[buf, dst, :], sem.at[1, buf]))

    def gather(seq, buf):
        def start(r, _):
            p = jnp.minimum(idx_sm[seq, r] & (IDX_FLAG - 1), n_past - 1)
            ck, cv = row_copies(buf, r, pt_sm[seq, p >> page_shift], p & (PAGE_SIZE - 1))
            ck.start()
            cv.start()
            return 0

        lax.fori_loop(0, n_sel, start, 0, unroll=8)

    @pl.when(b == 0)
    def _():
        gather(0, 0)

    @pl.when(b + 1 < n_seq)
    def _():
        gather(b + 1, (b + 1) % 2)

    cur = b % 2

    def wait(r, _):
        ck, cv = row_copies(cur, r, 0, 0)
        ck.wait()
        cv.wait()
        return 0

    lax.fori_loop(0, n_sel, wait, 0, unroll=8)

    idxc = idxc_ref[0]
    valid = idxc < IDX_FLAG
    idx = idxc & (IDX_FLAG - 1)
    is_new = idx >= n_past
    bucket = _t5_bucket(n_past - idx)
    bias = jnp.zeros((n_sel, LANE), F32)
    for bkt in range(N_BUCKETS):
        bias = jnp.where(bucket == bkt, rb_ref[bkt:bkt + 1, :], bias)
    q_pad = jnp.concatenate([q_ref[0], jnp.zeros((LANE - N_HEADS_B, hd), F32)], axis=0).astype(BF16)
    lane = lax.broadcasted_iota(I32, (n_sel, LANE), 1)
    lg = jnp.zeros((n_sel, LANE), F32)
    vals = []
    for n in range(N_KV_B):
        rows_n = pl.ds(n, n_sel, stride=N_KV_B)
        kn = jnp.where(is_new, kn_ref[0, :, n * hd:(n + 1) * hd], kbuf[cur, rows_n, :]).astype(BF16)
        vals.append(jnp.where(is_new, vn_ref[0, :, n * hd:(n + 1) * hd], vbuf[cur, rows_n, :]))
        lg = jnp.where(lane // GQA_GROUP == n, _dot_nt(kn, q_pad), lg)
    lg = jnp.where(valid, lg * (hd ** -0.5) + bias, NEG_BIG)
    m = jnp.max(lg, axis=0, keepdims=True)
    p = jnp.where(valid, jnp.exp(lg - m), 0.0)
    p = p / jnp.sum(p, axis=0, keepdims=True)
    for h in range(N_HEADS_B):
        o_h = jnp.sum(p[:, h:h + 1] * vals[h // GQA_GROUP], axis=0, keepdims=True)
        o_ref[0, :, h * hd:(h + 1) * hd] = o_h.astype(o_ref.dtype)


def _attn_sample_call(idx3, page_table, q3, k_new, v_new, rel_bias, cache_k, cache_v):
    nb, n_sel, _ = idx3.shape
    n_past = page_table.shape[1] * PAGE_SIZE
    kvw = N_KV_B * HEAD_DIM
    qbw = N_HEADS_B * HEAD_DIM
    grid_spec = pltpu.PrefetchScalarGridSpec(
        num_scalar_prefetch=2,
        grid=(nb,),
        in_specs=[pl.BlockSpec((1, n_sel, 1), lambda b, *_: (b, 0, 0)),
                  pl.BlockSpec((1, N_HEADS_B, HEAD_DIM), lambda b, *_: (b, 0, 0)),
                  pl.BlockSpec((1, 1, kvw), lambda b, *_: (b, 0, 0)),
                  pl.BlockSpec((1, 1, kvw), lambda b, *_: (b, 0, 0)),
                  pl.BlockSpec((N_BUCKETS, LANE), lambda b, *_: (0, 0)),
                  pl.BlockSpec(memory_space=pl.ANY),
                  pl.BlockSpec(memory_space=pl.ANY)],
        out_specs=pl.BlockSpec((1, 1, qbw), lambda b, *_: (b, 0, 0)),
        scratch_shapes=[pltpu.VMEM((2, n_sel * N_KV_B, HEAD_DIM), F32), pltpu.VMEM((2, n_sel * N_KV_B, HEAD_DIM), F32),
                        pltpu.SemaphoreType.DMA((2, 2))],
    )
    return pl.pallas_call(
        functools.partial(_attn_sample_kernel, n_past),
        out_shape=jax.ShapeDtypeStruct((nb, 1, qbw), BF16),
        grid_spec=grid_spec,
        compiler_params=_cparams(("arbitrary",)),
        name="attn_sample",
    )(idx3.reshape(nb, n_sel), page_table, idx3, q3, k_new.reshape(nb, 1, kvw), v_new.reshape(nb, 1, kvw),
      jnp.pad(rel_bias.astype(F32), ((0, 0), (0, LANE - N_HEADS_B))), cache_k, cache_v)


def _topk_sample_call(scores2, qi2, wi2, ki_new, n_sel):
    nb, n_past = scores2.shape
    n_blk = n_past // LANE
    return pl.pallas_call(
        functools.partial(_topk_sample_kernel, n_sel),
        out_shape=jax.ShapeDtypeStruct((nb, n_sel, 1), I32),
        scratch_shapes=[pltpu.VMEM((n_blk, nb, LANE), I32), pltpu.VMEM((nb, n_past), I32),
                        pltpu.VMEM((n_blk * nb, LANE), F32), pltpu.VMEM((n_blk * nb, LANE), F32),
                        pltpu.VMEM((n_blk * nb, LANE), F32)],
        compiler_params=pltpu.CompilerParams(vmem_limit_bytes=VMEM_LIMIT),
        name="topk_sample",
    )(scores2, qi2, wi2, ki_new)


def _wprep_kernel(w_ref, o_ref):
    src = [0]
    for s in SPLIT_SIZES:
        src.append(src[-1] + s)
    dst = (OFF_CONV, OFF_SM + SM_A, OFF_SM + SM_B, OFF_Z, OFF_QB, OFF_KB, OFF_VB, OFF_QI, OFF_SM + SM_W, OFF_KI)
    o_ref[:, OFF_SM:] = jnp.zeros((o_ref.shape[0], NP - OFF_SM), o_ref.dtype)
    for i, d in enumerate(dst):
        o_ref[:, d:d + SPLIT_SIZES[i]] = w_ref[0, :, src[i]:src[i + 1]].astype(o_ref.dtype)


def _prep_w_in(w_all, layer):
    _, rows, n_proj = w_all.shape
    tr = 256
    return pl.pallas_call(
        _wprep_kernel,
        out_shape=jax.ShapeDtypeStruct((rows, NP), BF16),
        grid=(rows // tr,),
        in_specs=[pl.BlockSpec((1, tr, n_proj), lambda i: (layer, i, 0))],
        out_specs=pl.BlockSpec((tr, NP), lambda i: (i, 0)),
        compiler_params=_cparams(("arbitrary",)),
        name="wprep",
    )(w_all)


def kernel(x_prompt, x_sample, c_prompt, c_sample, cache_k, cache_v, cache_kidx, state_ssm, state_conv, page_table,
           w_ada, b_ada, pre1_g, post1_g, pre2_g, post2_g, w_in, w_out, conv_w, a_log, dt_bias, gdn_norm_g,
           idx_knorm_g, idx_knorm_b, rel_bias, w_ff1, w_ff2):
    bsz, seq, _ = x_prompt.shape
    nb, dec_seq, _ = x_sample.shape
    assert dec_seq == 1, "the sample path handles one new token per sequence"
    depth = w_in.shape[0]
    n_past = page_table.shape[1] * PAGE_SIZE
    n_sel_s = min(TOPK_MAX, (n_past + dec_seq) // 4)
    kvw = N_KV_B * HEAD_DIM
    tm_in, tn_in = min(1024, seq), 1024
    tm_out = min(512, seq)
    tf = 1024

    hp = x_prompt.reshape(bsz * seq, D_MODEL)
    hs = x_sample.reshape(nb, D_MODEL)
    c_all = jnp.concatenate([c_sample, c_prompt], axis=0)
    new_p, new_s = [], []
    for l in range(depth):
        m = _adaln_call(c_all, w_ada[l], b_ada[l])
        w_in_bf = _prep_w_in(w_in, l)
        w_out_bf = w_out[l].astype(BF16)
        w1_bf = w_ff1[l].astype(BF16)
        w2_bf = w_ff2[l].astype(BF16)

        proj = _inproj_call(hp, pre1_g[l], _ModSpec(m, False, nb, seq, tm_in), w_in_bf, tm_in, tn_in)
        k_p, v_p, ki_p, k_bf, va_bf, ki_bf = _kvprep_call(proj, idx_knorm_g[l], idx_knorm_b[l], tm_in, True)
        proj3 = proj.reshape(bsz, seq, NP)
        o_a, ssm_p = _gdn_prompt_call(proj3, conv_w[l], a_log[l], dt_bias[l], gdn_norm_g[l])
        o_b = _attn_prompt_call(proj3, ki_bf.reshape(bsz, seq, D_IDX), k_bf.reshape(bsz, seq, kvw),
                                va_bf.reshape(bsz, seq, 2 * kvw), rel_bias)
        mod_p = _ModSpec(m, False, nb, seq, tm_out)
        x1, h2 = _outproj_call(o_a.reshape(bsz * seq, VAL_DIM_A), o_b.reshape(bsz * seq, N_HEADS_B * HEAD_DIM), hp,
                               w_out_bf, post1_g[l], pre2_g[l], mod_p, tm_out)
        hp = _ffn_call(h2, x1, w1_bf, w2_bf, post2_g[l], mod_p, tm_out, tf)
        conv_p = proj3[:, seq - (CONV_W - 1):, OFF_CONV:OFF_CONV + CONV_DIM]
        new_p.append((k_p.reshape(bsz, seq, N_KV_B, HEAD_DIM), v_p.reshape(bsz, seq, N_KV_B, HEAD_DIM),
                      ki_p.reshape(bsz, seq, D_IDX), ssm_p, conv_p))

        mod_s = _ModSpec(m, True, 0, 1, nb)
        proj_s = _inproj_call(hs, pre1_g[l], mod_s, w_in_bf, nb, tn_in)
        k_s, v_s, ki_s = _kvprep_call(proj_s, idx_knorm_g[l], idx_knorm_b[l], nb)
        o_a_s, ssm_s, conv_s = _gdn_sample_call(proj_s.reshape(nb, 1, NP), state_conv[l], state_ssm[l], conv_w[l],
                                                a_log[l], dt_bias[l], gdn_norm_g[l])
        qi2 = proj_s[:, OFF_QI:OFF_QI + N_IDX_HEADS * D_IDX]
        wi2 = proj_s[:, OFF_SM + SM_W:OFF_SM + SM_W + N_IDX_HEADS]
        scores = _idx_sample_call(page_table, qi2.reshape(nb, N_IDX_HEADS, D_IDX), wi2.reshape(nb, N_IDX_HEADS, 1),
                                  cache_kidx[l])
        idx3 = _topk_sample_call(scores.reshape(nb, n_past), qi2, wi2, ki_s, n_sel_s)
        q3 = proj_s[:, OFF_QB:OFF_QB + N_HEADS_B * HEAD_DIM].reshape(nb, N_HEADS_B, HEAD_DIM)
        o_b_s = _attn_sample_call(idx3, page_table, q3, k_s, v_s, rel_bias, cache_k[l], cache_v[l])
        x1s, h2s = _outproj_call(o_a_s.reshape(nb, VAL_DIM_A), o_b_s.reshape(nb, N_HEADS_B * HEAD_DIM), hs,
                                 w_out_bf, post1_g[l], pre2_g[l], mod_s, nb)
        hs = _ffn_call(h2s, x1s, w1_bf, w2_bf, post2_g[l], mod_s, nb, tf)
        new_s.append((k_s.reshape(nb, 1, N_KV_B, HEAD_DIM), v_s.reshape(nb, 1, N_KV_B, HEAD_DIM),
                      ki_s.reshape(nb, 1, D_IDX), ssm_s, conv_s))

    p_k, p_v, p_kidx, p_ssm, p_conv = [jnp.stack([st[i] for st in new_p]) for i in range(5)]
    s_k, s_v, s_kidx, s_ssm, s_conv = [jnp.stack([st[i] for st in new_s]) for i in range(5)]
    return (hp.reshape(bsz, seq, D_MODEL), hs.reshape(nb, 1, D_MODEL), p_k, p_v, p_kidx, p_ssm, p_conv,
            s_k, s_v, s_kidx, s_ssm, s_conv)
```

```python
import functools
import math

import jax
import jax.numpy as jnp
from jax import lax
from jax.experimental import pallas as pl
from jax.experimental.pallas import tpu as pltpu

F32 = jnp.float32
BF16 = jnp.bfloat16
I32 = jnp.int32

D_MODEL = 2048
PAGE_SIZE = 128
HEAD_DIM = 128
N_HEADS_A = 8
KEY_DIM_A = N_HEADS_A * HEAD_DIM
VAL_DIM_A = N_HEADS_A * HEAD_DIM
CONV_DIM = 2 * KEY_DIM_A + VAL_DIM_A
CONV_W = 4
N_HEADS_B = 8
N_KV_B = 2
GQA_GROUP = N_HEADS_B // N_KV_B
N_IDX_HEADS = 16
D_IDX = 128
TOPK_MAX = 256
N_BUCKETS = 32
MAX_DISTANCE = 128
D_FF = 4 * D_MODEL
EPS = 1e-6

SPLIT_SIZES = (CONV_DIM, N_HEADS_A, N_HEADS_A, VAL_DIM_A, N_HEADS_B * HEAD_DIM, N_KV_B * HEAD_DIM,
               N_KV_B * HEAD_DIM, N_IDX_HEADS * D_IDX, N_IDX_HEADS, D_IDX)

OFF_CONV = 0
OFF_Z = OFF_CONV + CONV_DIM
OFF_QI = OFF_Z + VAL_DIM_A
OFF_QB = OFF_QI + N_IDX_HEADS * D_IDX
OFF_KB = OFF_QB + N_HEADS_B * HEAD_DIM
OFF_VB = OFF_KB + N_KV_B * HEAD_DIM
OFF_KI = OFF_VB + N_KV_B * HEAD_DIM
OFF_SM = OFF_KI + D_IDX
NP = 8192
SM_A, SM_B, SM_W = 0, N_HEADS_A, 2 * N_HEADS_A

LANE = 128
VMEM_LIMIT = 56 * 1024 * 1024
INT_MIN = -(2 ** 31)
NEG_BIG = -1e30

GDN_CHUNK = 128
GDN_GROUP = 256
ATT_TQ = 256
ATT_TK = 256


def _cparams(sem):
    return pltpu.CompilerParams(dimension_semantics=sem, vmem_limit_bytes=VMEM_LIMIT)


def _sigmoid(x):
    return 1.0 / (1.0 + jnp.exp(-x))


def _softplus(x):
    return jnp.maximum(x, 0.0) + jnp.log(1.0 + jnp.exp(-jnp.abs(x)))


def _dot(a, b):
    return jnp.dot(a, b, preferred_element_type=F32)


def _dot_nt(a, b):
    return lax.dot_general(a, b, (((1,), (1,)), ((), ())), preferred_element_type=F32)


def _dot_tn(a, b):
    return lax.dot_general(a, b, (((0,), (0,)), ((), ())), preferred_element_type=F32)


def _mod(ref):
    v = ref[...]
    return v[0] if v.ndim == 3 else v


def _adaln_kernel(c_ref, w_ref, b_ref, o_ref):
    c = c_ref[...]
    s = (c * _sigmoid(c)).astype(BF16)
    o_ref[...] = _dot(s, w_ref[...].astype(BF16)) + b_ref[...]


def _adaln_call(c_all, w_ada, b_ada):
    n_rows = c_all.shape[0]
    n_out = w_ada.shape[1]
    tn = 1024
    return pl.pallas_call(
        _adaln_kernel,
        out_shape=jax.ShapeDtypeStruct((n_rows, n_out), F32),
        grid=(n_out // tn,),
        in_specs=[pl.BlockSpec((n_rows, D_MODEL), lambda j: (0, 0)),
                  pl.BlockSpec((D_MODEL, tn), lambda j: (0, j)),
                  pl.BlockSpec((1, tn), lambda j: (0, j))],
        out_specs=pl.BlockSpec((n_rows, tn), lambda j: (0, j)),
        compiler_params=_cparams(("arbitrary",)),
        name="adaln",
    )(c_all, w_ada, b_ada.reshape(1, n_out))


class _ModSpec:
    def __init__(self, m2, per_row, row_off, rows_per_batch, tm):
        self.per_row = per_row
        self.arr = m2 if per_row else m2.reshape(m2.shape[0], 1, m2.shape[1])
        self.row_off = row_off
        self.blocks_per_batch = None if per_row else rows_per_batch // tm
        self.tm = tm

    def spec(self, k, grid_rank):
        if self.per_row:
            if grid_rank == 1:
                return pl.BlockSpec((self.tm, D_MODEL), lambda i: (0, k))
            return pl.BlockSpec((self.tm, D_MODEL), lambda i, j: (0, k))
        bpb, off = self.blocks_per_batch, self.row_off
        if grid_rank == 1:
            return pl.BlockSpec((1, 1, D_MODEL), lambda i: (off + i // bpb, 0, k))
        return pl.BlockSpec((1, 1, D_MODEL), lambda i, j: (off + i // bpb, 0, k))


CONV_TAIL = 8


def _l2norm(x):
    return x * lax.rsqrt(jnp.sum(x * x, axis=-1, keepdims=True) + EPS)


def _inproj_kernel(blocks_per_seq, x_ref, g_ref, sh_ref, sc_ref, w_ref, *rest):
    j = pl.program_id(1)
    if blocks_per_seq is None:
        o_ref, h_scr = rest
    else:
        cw_ref, o_ref, tail_ref, h_scr, carry_s = rest

    @pl.when(j == 0)
    def _():
        x = x_ref[...]
        y = x * lax.rsqrt(jnp.mean(x * x, axis=-1, keepdims=True) + EPS) * g_ref[...]
        h_scr[...] = (y * (1.0 + _mod(sc_ref)) + _mod(sh_ref)).astype(BF16)

    if blocks_per_seq is None:
        o_ref[...] = _dot(h_scr[...], w_ref[...])
        return

    n_conv = CONV_DIM // o_ref.shape[1]

    @pl.when(j >= n_conv)
    def _():
        o_ref[...] = _dot(h_scr[...], w_ref[...])

    for blk in range(n_conv):
        @pl.when(j == blk)
        def _(blk=blk):
            raw = _dot(h_scr[...], w_ref[...])
            tm = raw.shape[0]
            cw = cw_ref[...]
            first = pl.program_id(0) % blocks_per_seq == 0
            prev = jnp.where(first, 0.0, carry_s[blk])
            carry_s[blk] = raw[tm - CONV_TAIL:]
            tail_ref[0] = raw[tm - CONV_TAIL:]

            def taps(xx):
                y = xx * cw[CONV_W - 1:CONV_W]
                for s in range(1, CONV_W):
                    y = y + pltpu.roll(xx, s, 0) * cw[CONV_W - 1 - s:CONV_W - s]
                return y

            head = taps(jnp.concatenate([prev, raw[:CONV_TAIL]], axis=0))[CONV_TAIL:]
            y = jnp.concatenate([head, taps(raw)[CONV_TAIL:]], axis=0)
            y = y * _sigmoid(y)
            if blk == 2:
                o_ref[...] = y
            else:
                for h in range(o_ref.shape[1] // HEAD_DIM):
                    yh = _l2norm(y[:, h * HEAD_DIM:(h + 1) * HEAD_DIM])
                    o_ref[:, h * HEAD_DIM:(h + 1) * HEAD_DIM] = yh * (HEAD_DIM ** -0.5) if blk == 0 else yh


def _inproj_call(x2d, g, mod, w_bf, tm, tn, conv_w=None, seq=None):
    m_rows = x2d.shape[0]
    n_cols = w_bf.shape[1]
    in_specs = [pl.BlockSpec((tm, D_MODEL), lambda i, j: (i, 0)),
                pl.BlockSpec((1, D_MODEL), lambda i, j: (0, 0)),
                mod.spec(0, 2), mod.spec(1, 2),
                pl.BlockSpec((D_MODEL, tn), lambda i, j: (0, j))]
    args = [x2d, g.reshape(1, D_MODEL), mod.arr, mod.arr, w_bf]
    out_shape = jax.ShapeDtypeStruct((m_rows, n_cols), F32)
    out_specs = pl.BlockSpec((tm, tn), lambda i, j: (i, j))
    scratch = [pltpu.VMEM((tm, D_MODEL), BF16)]
    bps = None
    if conv_w is not None:
        assert tn == KEY_DIM_A == VAL_DIM_A and OFF_CONV == 0, "q, k, v must each be one leading column block"
        bps = seq // tm
        n_conv = CONV_DIM // tn
        last = n_conv - 1
        in_specs.append(pl.BlockSpec((CONV_W, tn), lambda i, j: (0, jnp.minimum(j, last))))
        args.append(conv_w)
        out_shape = (out_shape, jax.ShapeDtypeStruct((m_rows // tm, CONV_TAIL, CONV_DIM), F32))
        out_specs = (out_specs, pl.BlockSpec((1, CONV_TAIL, tn), lambda i, j: (i, 0, jnp.minimum(j, last))))
        scratch.append(pltpu.VMEM((n_conv, CONV_TAIL, tn), F32))
    return pl.pallas_call(
        functools.partial(_inproj_kernel, bps),
        out_shape=out_shape,
        grid=(m_rows // tm, n_cols // tn),
        in_specs=in_specs,
        out_specs=out_specs,
        scratch_shapes=scratch,
        compiler_params=_cparams(("arbitrary", "arbitrary")),
        name="inproj",
    )(*args)


def _kvprep_kernel(kb_ref, vb_ref, ki_ref, lg_ref, lb_ref, k_ref, v_ref, kidx_ref, *bf_refs):
    kb = kb_ref[...]
    vb = vb_ref[...]
    for n in range(N_KV_B):
        k_ref[:, n, :] = kb[:, n * HEAD_DIM:(n + 1) * HEAD_DIM]
        v_ref[:, n, :] = vb[:, n * HEAD_DIM:(n + 1) * HEAD_DIM]
    x = ki_ref[...]
    mu = jnp.mean(x, axis=-1, keepdims=True)
    xc = x - mu
    var = jnp.mean(xc * xc, axis=-1, keepdims=True)
    ki = xc * lax.rsqrt(var + EPS) * lg_ref[...] + lb_ref[...]
    kidx_ref[...] = ki
    if bf_refs:
        kbf_ref, va_ref, kibf_ref = bf_refs
        kbf_ref[...] = kb.astype(BF16)
        kibf_ref[...] = ki.astype(BF16)
        hd = HEAD_DIM
        ones_col = jnp.where(lax.broadcasted_iota(I32, (vb.shape[0], hd), 1) == 0, 1.0, 0.0)
        va_ref[...] = jnp.concatenate([piece for n in range(N_KV_B) for piece in (vb[:, n * hd:(n + 1) * hd], ones_col)],
                                      axis=1).astype(BF16)


def _kvprep_call(proj, ln_g, ln_b, tm, with_bf16=False):
    m_rows = proj.shape[0]
    kvw = N_KV_B * HEAD_DIM
    row = lambda i: (i, 0)
    kv_shape = jax.ShapeDtypeStruct((m_rows, N_KV_B, HEAD_DIM), F32)
    kv_spec = pl.BlockSpec((tm, N_KV_B, HEAD_DIM), lambda i: (i, 0, 0))
    out_shape = [kv_shape, kv_shape, jax.ShapeDtypeStruct((m_rows, D_IDX), F32)]
    out_specs = [kv_spec, kv_spec, pl.BlockSpec((tm, D_IDX), row)]
    if with_bf16:
        out_shape += [jax.ShapeDtypeStruct((m_rows, kvw), BF16), jax.ShapeDtypeStruct((m_rows, 2 * kvw), BF16),
                      jax.ShapeDtypeStruct((m_rows, D_IDX), BF16)]
        out_specs += [pl.BlockSpec((tm, kvw), row), pl.BlockSpec((tm, 2 * kvw), row), pl.BlockSpec((tm, D_IDX), row)]
    return pl.pallas_call(
        _kvprep_kernel,
        out_shape=tuple(out_shape),
        grid=(m_rows // tm,),
        in_specs=[pl.BlockSpec((tm, kvw), lambda i: (i, OFF_KB // kvw)),
                  pl.BlockSpec((tm, kvw), lambda i: (i, OFF_VB // kvw)),
                  pl.BlockSpec((tm, D_IDX), lambda i: (i, OFF_KI // D_IDX)),
                  pl.BlockSpec((1, D_IDX), lambda i: (0, 0)),
                  pl.BlockSpec((1, D_IDX), lambda i: (0, 0))],
        out_specs=tuple(out_specs),
        compiler_params=_cparams(("arbitrary",)),
        name="kvprep",
    )(proj, proj, proj, ln_g.reshape(1, D_IDX), ln_b.reshape(1, D_IDX))


def _gdn_prompt_kernel(q_ref, k_ref, v_ref, z_ref, sm_ref, alog_ref, dtb_ref, ng_ref,
                       o_ref, s_out_ref,
                       gc_s, bt_s, l_s, a_s, x_s, u_s, w_s, mn_s, sts_s):
    seq = q_ref.shape[1]
    hd = HEAD_DIM
    grp = GDN_GROUP
    chunk = GDN_CHUNK
    n_grp = seq // grp
    n_chunk = seq // chunk
    cpg = grp // chunk
    h = pl.program_id(1)

    row = lax.broadcasted_iota(I32, (seq, hd), 0)
    lane = lax.broadcasted_iota(I32, (seq, hd), 1)

    qn_s, kn_s, vn_s = q_ref.at[0], k_ref.at[0], v_ref.at[0]

    sm = sm_ref[0]
    g_all = -jnp.exp(alog_ref[...]) * _softplus(sm + dtb_ref[...])
    b_all = _sigmoid(sm)
    g_col = jnp.sum(jnp.where(lane == h + SM_A, g_all, 0.0), axis=1, keepdims=True)
    b_col = jnp.sum(jnp.where(lane == h + SM_B, b_all, 0.0), axis=1, keepdims=True)
    gc = jnp.broadcast_to(g_col, (seq, hd))
    pos = row & (chunk - 1)
    s = 1
    while s < chunk:
        gc = gc + jnp.where(pos >= s, pltpu.roll(gc, s, 0), 0.0)
        s *= 2
    gc_s[...] = gc
    bt_s[...] = jnp.broadcast_to(b_col, (seq, hd))

    ii = lax.broadcasted_iota(I32, (grp, grp), 0)
    jj = lax.broadcasted_iota(I32, (grp, grp), 1)
    xr = ii ^ jj
    same = (xr >> int(math.log2(chunk))) == 0
    causal = same & (ii >= jj)
    strict = same & (ii > jj)
    eye = (ii == jj).astype(F32)

    def build(gi, _):
        r0 = pl.multiple_of(gi * grp, grp)
        kk = kn_s[pl.ds(r0, grp), :]
        qq = qn_s[pl.ds(r0, grp), :]
        bt = bt_s[pl.ds(r0, grp), :]
        gcg = gc_s[pl.ds(r0, grp), :]
        kb = (kk * bt).astype(BF16)
        kkb = kk.astype(BF16)
        kkt = _dot_nt(kb, kkb)
        qkt = _dot_nt(qq.astype(BF16), kkb)
        colb = jnp.concatenate([gcg, gcg], axis=1)
        diff = colb - colb.T
        decay = jnp.where(causal, jnp.exp(jnp.where(causal, diff, 0.0)), 0.0)
        lmat = jnp.where(strict, kkt * decay, 0.0)
        l_s[gi] = lmat
        a_s[pl.ds(r0, grp), :] = jnp.where(causal, qkt * decay, 0.0)
        x_s[gi] = eye - jnp.where(xr == 1, lmat, 0.0)
        return 0

    lax.fori_loop(0, n_grp, build, 0, unroll=2)

    for lvl in range(1, int(math.log2(chunk))):
        def level(gi, _, lvl=lvl):
            xm = x_s[gi]
            cl = jnp.where((xr >> lvl) == 1, l_s[gi], 0.0).astype(BF16)
            xb = xm.astype(BF16)
            x_s[gi] = xm - _dot(_dot(xb, cl).astype(BF16), xb)
            return 0

        lax.fori_loop(0, n_grp, level, 0, unroll=True)

    def apply(gi, _):
        r0 = pl.multiple_of(gi * grp, grp)
        kk = kn_s[pl.ds(r0, grp), :]
        vv = vn_s[pl.ds(r0, grp), :]
        bt = bt_s[pl.ds(r0, grp), :]
        gcg = gc_s[pl.ds(r0, grp), :]
        rhs = jnp.concatenate([vv * bt, kk * bt * jnp.exp(gcg)], axis=1).astype(BF16)
        uw = _dot(x_s[gi].astype(BF16), rhs)
        u_s[pl.ds(r0, grp), :] = uw[:, :hd]
        w_s[pl.ds(r0, grp), :] = uw[:, hd:]
        for cc in range(cpg):
            lo = cc * chunk
            g_last = gcg[lo + chunk - 1:lo + chunk, :]
            k_dec = (kk[lo:lo + chunk] * jnp.exp(g_last - gcg[lo:lo + chunk])).astype(BF16)
            wu = jnp.concatenate([uw[lo:lo + chunk, hd:], uw[lo:lo + chunk, :hd]], axis=1).astype(BF16)
            mn_s[gi * cpg + cc] = _dot_tn(k_dec, wu)
        return 0

    lax.fori_loop(0, n_grp, apply, 0, unroll=2)

    def scan(c, st):
        sts_s[c] = st
        g_last = gc_s[pl.ds(c * chunk + chunk - 1, 1), :]
        mn = mn_s[c]
        return st * jnp.exp(g_last) - _dot(mn[:, :hd].astype(BF16), st.astype(BF16)) + mn[:, hd:]

    s_out_ref[0, 0] = lax.fori_loop(0, n_chunk, scan, jnp.zeros((hd, hd), F32))

    ng = ng_ref[...]

    def emit(gi, _):
        r0 = pl.multiple_of(gi * grp, grp)
        gcg = gc_s[pl.ds(r0, grp), :]
        qg = qn_s[pl.ds(r0, grp), :] * jnp.exp(gcg)
        wg = w_s[pl.ds(r0, grp), :]
        ws, qs = [], []
        for cc in range(cpg):
            lo = cc * chunk
            stb = sts_s[gi * cpg + cc].astype(BF16)
            both = _dot(jnp.concatenate([wg[lo:lo + chunk], qg[lo:lo + chunk]], axis=0).astype(BF16), stb)
            ws.append(both[:chunk])
            qs.append(both[chunk:])
        v_new = u_s[pl.ds(r0, grp), :] - jnp.concatenate(ws, axis=0)
        o = jnp.concatenate(qs, axis=0) + _dot(a_s[pl.ds(r0, grp), :].astype(BF16), v_new.astype(BF16))
        zc = z_ref[0, pl.ds(r0, grp), :]
        on = o * lax.rsqrt(jnp.mean(o * o, axis=-1, keepdims=True) + EPS) * ng
        o_ref[0, pl.ds(r0, grp), :] = (on * (zc * _sigmoid(zc))).astype(o_ref.dtype)
        return 0

    lax.fori_loop(0, n_grp, emit, 0, unroll=2)


def _lane_vec(v, off):
    return jnp.zeros((1, LANE), F32).at[0, off:off + v.shape[0]].set(v.astype(F32))


def _gdn_prompt_call(proj3, a_log, dt_bias, norm_g):
    bsz, seq, _ = proj3.shape
    hd = HEAD_DIM
    nh = N_HEADS_A
    n_grp = seq // GDN_GROUP
    col = lambda base: (lambda b, h: (b, 0, base // hd + h))
    vec = pl.BlockSpec((1, LANE), lambda b, h: (0, 0))
    return pl.pallas_call(
        _gdn_prompt_kernel,
        out_shape=(jax.ShapeDtypeStruct((bsz, seq, VAL_DIM_A), BF16),
                   jax.ShapeDtypeStruct((bsz, nh, hd, hd), F32)),
        grid=(bsz, nh),
        in_specs=[pl.BlockSpec((1, seq, hd), col(OFF_CONV)),
                  pl.BlockSpec((1, seq, hd), col(OFF_CONV + KEY_DIM_A)),
                  pl.BlockSpec((1, seq, hd), col(OFF_CONV + 2 * KEY_DIM_A)),
                  pl.BlockSpec((1, seq, hd), col(OFF_Z)),
                  pl.BlockSpec((1, seq, LANE), lambda b, h: (b, 0, OFF_SM // LANE)),
                  vec, vec, vec],
        out_specs=(pl.BlockSpec((1, seq, hd), lambda b, h: (b, 0, h)),
                   pl.BlockSpec((1, 1, hd, hd), lambda b, h: (b, h, 0, 0))),
        scratch_shapes=[pltpu.VMEM((seq, hd), F32), pltpu.VMEM((seq, hd), F32),
                        pltpu.VMEM((n_grp, GDN_GROUP, GDN_GROUP), F32),
                        pltpu.VMEM((seq, GDN_GROUP), F32),
                        pltpu.VMEM((n_grp, GDN_GROUP, GDN_GROUP), F32),
                        pltpu.VMEM((seq, hd), F32), pltpu.VMEM((seq, hd), F32),
                        pltpu.VMEM((seq // GDN_CHUNK, hd, 2 * hd), F32), pltpu.VMEM((seq // GDN_CHUNK, hd, hd), F32)],
        compiler_params=_cparams(("arbitrary", "arbitrary")),
        name="gdn_prompt",
    )(proj3, proj3, proj3, proj3, proj3,
      _lane_vec(a_log, SM_A), _lane_vec(dt_bias, SM_A), norm_g.reshape(1, hd).astype(F32))


def _t5_bucket(dist):
    n = jnp.maximum(dist, 0)
    max_exact = N_BUCKETS // 2
    nf = jnp.maximum(n, 1).astype(F32)
    large = max_exact + (jnp.log(nf / max_exact) / math.log(MAX_DISTANCE / max_exact)
                         * (N_BUCKETS - max_exact)).astype(I32)
    large = jnp.minimum(large, N_BUCKETS - 1)
    return jnp.where(n < max_exact, n, large)


def _bias_from_bucket(bucket, rb_ref, head):
    out = jnp.zeros(bucket.shape, F32)
    for b in range(N_BUCKETS):
        out = jnp.where(bucket == b, rb_ref[b, head], out)
    return out


def _order_key(score):
    bits = pltpu.bitcast(score + 0.0, I32)
    return bits ^ ((bits >> 31) & 0x7FFFFFFF)


I16 = jnp.int16
HALF_BIAS = 1 << 15
HALF_ROWS = 16
KEY_POS_INF = 0x7F800000
KEY_NEG_INF = INT_MIN + 0x7FFFFF


def _kth_largest_key(count_ge, shape, n_sel):
    def body(i, tau):
        bit = 31 - i
        cand = jnp.where(i == 0, tau ^ INT_MIN, tau | (1 << bit))
        return jnp.where(count_ge(cand) >= n_sel, cand, tau)

    return lax.fori_loop(0, 32, body, jnp.full(shape, INT_MIN, I32))


def _attn_prompt_kernel(n_sel, qi_ref, sm_ref, qb_ref, ki_ref, k_ref, va_ref, rb_ref, o_ref,
                        qi_s, qb_s, wt_s, keys_s, hi_s, lo_s, tau_s, bias_s, m_s, acc_s):
    tq, tk = ATT_TQ, ATT_TK
    hd = HEAD_DIM
    bi = pl.program_id(0)
    qblk = pl.program_id(1)
    n_kb = (qblk * tq) // tk + 1
    krow = lax.broadcasted_iota(I32, (tk, tq), 0)
    qcol = lax.broadcasted_iota(I32, (tk, tq), 1)

    @pl.when((bi == 0) & (qblk == 0))
    def _():
        for off in range(bias_s.shape[1]):
            bucket = _t5_bucket(off * tk + krow - qcol)
            for h in range(N_HEADS_B):
                bias_s[h, off] = _bias_from_bucket(bucket, rb_ref, h)

    qi_s[...] = qi_ref[0].astype(BF16)
    qb_s[...] = qb_ref[0].astype(BF16)
    wt_s[...] = sm_ref[0].T * (N_IDX_HEADS ** -0.5)
    t_pos = qblk * tq + qcol

    def idx_body(j, _):
        k0 = pl.multiple_of(j * tk, tk)
        kt = ki_ref[0, pl.ds(k0, tk), :]
        acc = jnp.zeros((tk, tq), F32)
        for h in range(N_IDX_HEADS):
            s = _dot_nt(kt, qi_s[:, h * D_IDX:(h + 1) * D_IDX])
            acc = acc + jnp.maximum(s, 0.0) * wt_s[SM_W + h:SM_W + h + 1, :]
        acc = acc * (D_IDX ** -0.5)
        key = jnp.where(k0 + krow <= t_pos, _order_key(acc), INT_MIN)
        keys_s[j] = key
        hi_s[j] = (key >> 16).astype(I16)
        lo_s[j] = ((key & 0xFFFF) - HALF_BIAS).astype(I16)
        return 0

    lax.fori_loop(0, n_kb, idx_body, 0)

    def kth_largest_half(half_s, n_at_start):
        def count_ge(cand):
            def cb(j, c):
                hit = jnp.where(half_s[j] >= cand, jnp.int16(1), jnp.int16(0))
                for r in range(tk // HALF_ROWS):
                    c = c + hit[r * HALF_ROWS:(r + 1) * HALF_ROWS]
                return c

            c = lax.fori_loop(0, n_kb, cb, jnp.zeros((HALF_ROWS, tq), I16))
            return jnp.sum(c.astype(I32), axis=0, keepdims=True)

        def body(i, carry):
            t, n_t = carry
            cand = jnp.where(i == 0, t ^ (-HALF_BIAS), t | (1 << (15 - i)))
            n_cand = count_ge(cand.astype(I16))
            ok = n_cand >= n_sel
            return jnp.where(ok, cand, t), jnp.where(ok, n_cand, n_t)

        return lax.fori_loop(0, 16, body, (jnp.full((1, tq), -HALF_BIAS, I32), n_at_start))

    tau_hi, n_ge_hi = kth_largest_half(hi_s, jnp.zeros((1, tq), I32))
    tau_hi16 = tau_hi.astype(I16)

    def narrow(j, _):
        hi = hi_s[j]
        lo_s[j] = jnp.where(hi > tau_hi16, jnp.int16(HALF_BIAS - 1),
                            jnp.where(hi == tau_hi16, lo_s[j], jnp.int16(-HALF_BIAS)))
        return 0

    lax.fori_loop(0, n_kb, narrow, 0)
    tau_lo, n_ge_tau = kth_largest_half(lo_s, n_ge_hi)
    tau = (tau_hi << 16) | ((tau_lo + HALF_BIAS) & 0xFFFF)
    tau = jnp.maximum(tau, KEY_NEG_INF + 1)

    def count_where(pred):
        def cb(j, c):
            hit = jnp.where(pred(keys_s[j], j * tk + krow), 1, 0)
            return c + jnp.sum(hit.reshape(tk // 8, 8, tq), axis=0)

        c = lax.fori_loop(0, n_kb, cb, jnp.zeros((8, tq), I32))
        return jnp.sum(c, axis=0, keepdims=True)

    @pl.when(jnp.max(n_ge_tau) > n_sel)
    def _():
        quota = n_sel - count_where(lambda k, pos: k > tau)
        pos_bits = (keys_s.shape[0] * tk - 1).bit_length()

        def body(i, last):
            cand = last | (1 << (pos_bits - 1 - i))
            before = count_where(lambda k, pos: (k == tau) & (pos < cand))
            return jnp.where(before < quota, cand, last)

        last_kept = lax.fori_loop(0, pos_bits, body, jnp.zeros((1, tq), I32))

        def demote(j, _):
            k = keys_s[j]
            keys_s[j] = jnp.where((k == tau) & (j * tk + krow > last_kept), tau - 1, k)
            return 0

        lax.fori_loop(0, n_kb, demote, 0)

    scale = hd ** -0.5
    lane_reps = tk // LANE
    tau_rep = jnp.broadcast_to(tau, (LANE, tq)).T
    tau_s[...] = jnp.concatenate([tau_rep] * lane_reps, axis=1)

    def block_inputs(j):
        kq = keys_s[j].T
        sel = jnp.where(kq < KEY_POS_INF, kq, INT_MIN) >= tau_s[...]
        return k_ref[0, pl.ds(pl.multiple_of(j * tk, tk), tk), :], sel

    def masked_logits(j, h, kt, sel):
        n = h // GQA_GROUP
        boff = jnp.minimum(qblk - j, bias_s.shape[1] - 1)
        lg = _dot_nt(qb_s[:, h * hd:(h + 1) * hd], kt[:, n * hd:(n + 1) * hd]) * scale + bias_s[h, boff]
        return jnp.where(sel, lg, NEG_BIG)

    m_s[...] = jnp.full(m_s.shape, NEG_BIG, F32)

    def max_body(j, _):
        kt, sel = block_inputs(j)
        for h in range(N_HEADS_B):
            lg = masked_logits(j, h, kt, sel)
            part = lg[:, :LANE]
            for r in range(1, lane_reps):
                part = jnp.maximum(part, lg[:, r * LANE:(r + 1) * LANE])
            m_s[h] = jnp.maximum(m_s[h], part)
        return 0

    lax.fori_loop(0, n_kb, max_body, 0)
    for h in range(N_HEADS_B):
        m_s[h] = jnp.broadcast_to(jnp.max(m_s[h], axis=1, keepdims=True), (tq, LANE))
    acc_s[...] = jnp.zeros(acc_s.shape, F32)

    def sum_body(j, _):
        kt, sel = block_inputs(j)
        va = va_ref[0, pl.ds(pl.multiple_of(j * tk, tk), tk), :]
        for h in range(N_HEADS_B):
            n = h // GQA_GROUP
            mrow = jnp.concatenate([m_s[h]] * lane_reps, axis=1)
            p = jnp.exp(masked_logits(j, h, kt, sel) - mrow)
            acc_s[h] += _dot(p.astype(BF16), va[:, n * 2 * hd:(n + 1) * 2 * hd])
        return 0

    lax.fori_loop(0, n_kb, sum_body, 0)
    for h in range(N_HEADS_B):
        a = acc_s[h]
        o_ref[0, :, h * hd:(h + 1) * hd] = (a[:, :hd] / a[:, hd:hd + 1]).astype(o_ref.dtype)


def _attn_prompt_call(proj3, ki_bf, k_bf, va_bf, rel_bias):
    bsz, seq, _ = proj3.shape
    tq, tk = ATT_TQ, ATT_TK
    assert tq == tk, "the bias-tile offsets assume square blocks"
    n_sel = min(TOPK_MAX, seq // 4)
    qiw = N_IDX_HEADS * D_IDX
    qbw = N_HEADS_B * HEAD_DIM
    kvw = N_KV_B * HEAD_DIM
    n_off = -(-(MAX_DISTANCE + tk - 1) // tk) + 1
    return pl.pallas_call(
        functools.partial(_attn_prompt_kernel, n_sel),
        out_shape=jax.ShapeDtypeStruct((bsz, seq, qbw), BF16),
        grid=(bsz, seq // tq),
        in_specs=[pl.BlockSpec((1, tq, qiw), lambda b, i: (b, i, OFF_QI // qiw)),
                  pl.BlockSpec((1, tq, LANE), lambda b, i: (b, i, OFF_SM // LANE)),
                  pl.BlockSpec((1, tq, qbw), lambda b, i: (b, i, OFF_QB // qbw)),
                  pl.BlockSpec((1, seq, D_IDX), lambda b, i: (b, 0, 0)),
                  pl.BlockSpec((1, seq, kvw), lambda b, i: (b, 0, 0)),
                  pl.BlockSpec((1, seq, 2 * kvw), lambda b, i: (b, 0, 0)),
                  pl.BlockSpec(memory_space=pltpu.SMEM)],
        out_specs=pl.BlockSpec((1, tq, qbw), lambda b, i: (b, i, 0)),
        scratch_shapes=[pltpu.VMEM((tq, qiw), BF16), pltpu.VMEM((tq, qbw), BF16),
                        pltpu.VMEM((LANE, tq), F32),
                        pltpu.VMEM((seq // tk, tk, tq), I32),
                        pltpu.VMEM((seq // tk, tk, tq), I16), pltpu.VMEM((seq // tk, tk, tq), I16),
                        pltpu.VMEM((tq, tk), I32),
                        pltpu.VMEM((N_HEADS_B, n_off, tq, tk), F32),
                        pltpu.VMEM((N_HEADS_B, tq, LANE), F32),
                        pltpu.VMEM((N_HEADS_B, tq, 2 * HEAD_DIM), F32)],
        compiler_params=_cparams(("arbitrary", "arbitrary")),
        name="attn_prompt",
    )(proj3, proj3, proj3, ki_bf, k_bf, va_bf, rel_bias.astype(F32))


def _rms(x, g):
    return x * lax.rsqrt(jnp.mean(x * x, axis=-1, keepdims=True) + EPS) * g


def _outproj_kernel(oa_ref, ob_ref, x_ref, wa_ref, wb_ref, pg_ref, g1_ref, p2_ref, sh2_ref, sc2_ref, x1_ref, h2_ref):
    rows = x_ref.shape[0]
    per_row = _mod(g1_ref).shape[0] == rows
    n_piece = 2 if rows % 16 == 0 and rows >= 256 else 1
    for r in range(n_piece):
        sl = slice(r * rows // n_piece, (r + 1) * rows // n_piece)
        pick = (lambda ref: _mod(ref)[sl]) if per_row else _mod
        mix = _dot(oa_ref[sl, :], wa_ref[...]) + _dot(ob_ref[sl, :], wb_ref[...])
        x1 = x_ref[sl, :] + pick(g1_ref) * _rms(mix, pg_ref[...])
        x1_ref[sl, :] = x1
        h2_ref[sl, :] = (_rms(x1, p2_ref[...]) * (1.0 + pick(sc2_ref)) + pick(sh2_ref)).astype(BF16)


def _outproj_call(oa, ob, x2d, w_out_bf, post1_g, pre2_g, mod, tm):
    m_rows = x2d.shape[0]
    half = w_out_bf.shape[0] // 2
    row = lambda i: (i, 0)
    vec = pl.BlockSpec((1, D_MODEL), lambda i: (0, 0))
    return pl.pallas_call(
        _outproj_kernel,
        out_shape=(jax.ShapeDtypeStruct((m_rows, D_MODEL), F32), jax.ShapeDtypeStruct((m_rows, D_MODEL), BF16)),
        grid=(m_rows // tm,),
        in_specs=[pl.BlockSpec((tm, half), row), pl.BlockSpec((tm, half), row), pl.BlockSpec((tm, D_MODEL), row),
                  pl.BlockSpec((half, D_MODEL), lambda i: (0, 0)), pl.BlockSpec((half, D_MODEL), lambda i: (1, 0)),
                  vec, mod.spec(2, 1), vec, mod.spec(3, 1), mod.spec(4, 1)],
        out_specs=(pl.BlockSpec((tm, D_MODEL), row), pl.BlockSpec((tm, D_MODEL), row)),
        compiler_params=_cparams(("arbitrary",)),
        name="outproj",
    )(oa, ob, x2d, w_out_bf, w_out_bf, post1_g.reshape(1, D_MODEL), mod.arr, pre2_g.reshape(1, D_MODEL),
      mod.arr, mod.arr)


def _ffn_kernel(h_ref, x1_ref, w1_ref, w2_ref, pg_ref, g2_ref, o_ref, acc_s):
    kk = pl.program_id(1)

    @pl.when(kk == 0)
    def _():
        acc_s[...] = jnp.zeros_like(acc_s)

    a = jnp.maximum(_dot(h_ref[...], w1_ref[...]), 0.0)
    acc_s[...] += _dot((a * a).astype(BF16), w2_ref[...])

    @pl.when(kk == pl.num_programs(1) - 1)
    def _():
        o_ref[...] = x1_ref[...] + _mod(g2_ref) * _rms(acc_s[...], pg_ref[...])


def _ffn_call(h2, x1, w1_bf, w2_bf, post2_g, mod, tm, tf):
    m_rows = x1.shape[0]
    return pl.pallas_call(
        _ffn_kernel,
        out_shape=jax.ShapeDtypeStruct((m_rows, D_MODEL), F32),
        grid=(m_rows // tm, D_FF // tf),
        in_specs=[pl.BlockSpec((tm, D_MODEL), lambda i, k: (i, 0)),
                  pl.BlockSpec((tm, D_MODEL), lambda i, k: (i, 0)),
                  pl.BlockSpec((D_MODEL, tf), lambda i, k: (0, k)),
                  pl.BlockSpec((tf, D_MODEL), lambda i, k: (k, 0)),
                  pl.BlockSpec((1, D_MODEL), lambda i, k: (0, 0)),
                  mod.spec(5, 2)],
        out_specs=pl.BlockSpec((tm, D_MODEL), lambda i, k: (i, 0)),
        scratch_shapes=[pltpu.VMEM((tm, D_MODEL), F32)],
        compiler_params=_cparams(("arbitrary", "arbitrary")),
        name="ffn",
    )(h2, x1, w1_bf, w2_bf, post2_g.reshape(1, D_MODEL), mod.arr)


def _gdn_sample_kernel(cin_ref, z_ref, sm_ref, sconv_ref, ssm_ref, cw_ref, alog_ref, dtb_ref, ng_ref,
                       o_ref, ssm_out_ref, conv_out_ref):
    hd = HEAD_DIM
    nh = N_HEADS_A
    prev = sconv_ref[0]
    xin = cin_ref[0]
    cw = cw_ref[...]
    y = xin * cw[CONV_W - 1:CONV_W]
    for s in range(CONV_W - 1):
        y = y + prev[s:s + 1] * cw[s:s + 1]
    y = y * _sigmoid(y)
    conv_out_ref[0, 0:CONV_W - 2, :] = prev[1:CONV_W - 1]
    conv_out_ref[0, CONV_W - 2:CONV_W - 1, :] = xin

    def heads(base):
        return jnp.concatenate([y[:, base + h * hd:base + (h + 1) * hd] for h in range(nh)], axis=0)

    q8 = _l2norm(heads(0)) * (hd ** -0.5)
    k8 = _l2norm(heads(KEY_DIM_A))
    v8 = heads(2 * KEY_DIM_A)
    sm = sm_ref[0]
    g_all = -jnp.exp(alog_ref[...]) * _softplus(sm + dtb_ref[...])
    b_all = _sigmoid(sm)
    kt = jnp.concatenate([k8, jnp.zeros((LANE - nh, hd), F32)], axis=0).T
    k8b = k8.astype(BF16)
    q8b = q8.astype(BF16)
    z = z_ref[0]
    ng = ng_ref[...]
    for h in range(nh):
        s1 = ssm_ref[0, h] * jnp.exp(g_all[:, SM_A + h:SM_A + h + 1])
        kv = _dot(k8b, s1.astype(BF16))[h:h + 1]
        delta = (v8[h:h + 1] - kv) * b_all[:, SM_B + h:SM_B + h + 1]
        s2 = s1 + kt[:, h:h + 1] * delta
        ssm_out_ref[0, h] = s2
        o = _dot(q8b, s2.astype(BF16))[h:h + 1]
        zh = z[:, h * hd:(h + 1) * hd]
        on = o * lax.rsqrt(jnp.mean(o * o, axis=-1, keepdims=True) + EPS) * ng
        o_ref[0, :, h * hd:(h + 1) * hd] = (on * (zh * _sigmoid(zh))).astype(o_ref.dtype)


def _gdn_sample_call(proj_s3, state_conv, state_ssm, conv_w, a_log, dt_bias, norm_g):
    nb = proj_s3.shape[0]
    hd, nh = HEAD_DIM, N_HEADS_A
    vec = pl.BlockSpec((1, LANE), lambda b: (0, 0))
    return pl.pallas_call(
        _gdn_sample_kernel,
        out_shape=(jax.ShapeDtypeStruct((nb, 1, VAL_DIM_A), BF16),
                   jax.ShapeDtypeStruct((nb, nh, hd, hd), F32),
                   jax.ShapeDtypeStruct((nb, CONV_W - 1, CONV_DIM), F32)),
        grid=(nb,),
        in_specs=[pl.BlockSpec((1, 1, CONV_DIM), lambda b: (b, 0, OFF_CONV // CONV_DIM)),
                  pl.BlockSpec((1, 1, VAL_DIM_A), lambda b: (b, 0, OFF_Z // VAL_DIM_A)),
                  pl.BlockSpec((1, 1, LANE), lambda b: (b, 0, OFF_SM // LANE)),
                  pl.BlockSpec((1, CONV_W - 1, CONV_DIM), lambda b: (b, 0, 0)),
                  pl.BlockSpec((1, nh, hd, hd), lambda b: (b, 0, 0, 0)),
                  pl.BlockSpec((CONV_W, CONV_DIM), lambda b: (0, 0)),
                  vec, vec, vec],
        out_specs=(pl.BlockSpec((1, 1, VAL_DIM_A), lambda b: (b, 0, 0)),
                   pl.BlockSpec((1, nh, hd, hd), lambda b: (b, 0, 0, 0)),
                   pl.BlockSpec((1, CONV_W - 1, CONV_DIM), lambda b: (b, 0, 0))),
        compiler_params=_cparams(("arbitrary",)),
        name="gdn_sample",
    )(proj_s3, proj_s3, proj_s3, state_conv, state_ssm, conv_w,
      _lane_vec(a_log, SM_A), _lane_vec(dt_bias, SM_A), norm_g.reshape(1, hd).astype(F32))


IDX_PAGES_PER_MATMUL = 4


def _idx_sample_kernel(pt_sm, qi_ref, wi_ref, kidx_hbm, o_ref, buf, sem):
    b = pl.program_id(0)
    n_seq = pl.num_programs(0)
    n_pages = buf.shape[1]

    def page_copy(page, slot, p):
        return pltpu.make_async_copy(kidx_hbm.at[page], buf.at[slot, p], sem.at[slot])

    def gather(seq, slot):
        def body(p, _):
            page_copy(pt_sm[seq, p], slot, p).start()
            return 0

        lax.fori_loop(0, n_pages, body, 0, unroll=8)

    @pl.when(b == 0)
    def _():
        gather(0, 0)

    @pl.when(b + 1 < n_seq)
    def _():
        gather(b + 1, (b + 1) % 2)

    slot = b % 2

    def wait_body(p, _):
        page_copy(0, slot, p).wait()
        return 0

    lax.fori_loop(0, n_pages, wait_body, 0, unroll=8)

    qb = qi_ref[0].astype(BF16)
    w = wi_ref[0] * (N_IDX_HEADS ** -0.5)

    ppm = math.gcd(n_pages, IDX_PAGES_PER_MATMUL)

    def score(i, _):
        p0 = pl.multiple_of(i * ppm, ppm)
        keys = buf[slot, pl.ds(p0, ppm)].reshape(ppm * PAGE_SIZE, D_IDX).astype(BF16)
        s = _dot_nt(qb, keys)
        sc = jnp.sum(jnp.maximum(s, 0.0) * w, axis=0, keepdims=True) * (D_IDX ** -0.5)
        for k in range(ppm):
            o_ref[0, pl.ds(p0 + k, 1), :] = sc[:, k * PAGE_SIZE:(k + 1) * PAGE_SIZE]
        return 0

    lax.fori_loop(0, n_pages // ppm, score, 0, unroll=4)


def _idx_sample_call(page_table, qi3, wi3, cache_kidx):
    nb, n_pages = page_table.shape
    grid_spec = pltpu.PrefetchScalarGridSpec(
        num_scalar_prefetch=1,
        grid=(nb,),
        in_specs=[pl.BlockSpec((1, N_IDX_HEADS, D_IDX), lambda b, pt: (b, 0, 0)),
                  pl.BlockSpec((1, N_IDX_HEADS, 1), lambda b, pt: (b, 0, 0)),
                  pl.BlockSpec(memory_space=pl.ANY)],
        out_specs=pl.BlockSpec((1, n_pages, PAGE_SIZE), lambda b, pt: (b, 0, 0)),
        scratch_shapes=[pltpu.VMEM((2, n_pages, PAGE_SIZE, D_IDX), F32), pltpu.SemaphoreType.DMA((2,))],
    )
    return pl.pallas_call(
        _idx_sample_kernel,
        out_shape=jax.ShapeDtypeStruct((nb, n_pages, PAGE_SIZE), F32),
        grid_spec=grid_spec,
        compiler_params=_cparams(("arbitrary",)),
        name="idx_sample",
    )(page_table, qi3, wi3, cache_kidx)


IDX_FLAG = 1 << 20


def _topk_sample_kernel(n_sel, sc_ref, qi_ref, wi_ref, knew_ref, idx_ref, keys_s, keyw_s, rank_s, flag_s, cnt_s):
    nb, n_past = sc_ref.shape
    n_blk = n_past // LANE
    lane = lax.broadcasted_iota(I32, (nb, LANE), 1)

    kn = knew_ref[...].astype(BF16)
    q = qi_ref[...]
    w = wi_ref[...] * (N_IDX_HEADS ** -0.5)
    s_new = jnp.zeros((nb, 1), F32)
    for h in range(N_IDX_HEADS):
        qh = q[:, h * D_IDX:(h + 1) * D_IDX].astype(BF16).astype(F32)
        dot = jnp.sum(qh * kn.astype(F32), axis=1, keepdims=True)
        s_new = s_new + jnp.maximum(dot, 0.0) * w[:, h:h + 1]
    s_new = s_new * (D_IDX ** -0.5)
    key_new = _order_key(s_new)

    def to_keys(j, _):
        keys_s[j] = _order_key(sc_ref[:, pl.ds(pl.multiple_of(j * LANE, LANE), LANE)])
        return 0

    lax.fori_loop(0, n_blk, to_keys, 0, unroll=8)
    keyw_s[...] = _order_key(sc_ref[...])
    wide = min(n_past, 16 * LANE)

    def count(pred):
        acc = jnp.zeros((nb, LANE), I32)
        for c in range(n_past // wide):
            hit = jnp.where(pred(keyw_s[:, c * wide:(c + 1) * wide]), 1, 0)
            for k in range(wide // LANE):
                acc = acc + hit[:, k * LANE:(k + 1) * LANE]
        return jnp.sum(acc, axis=1, keepdims=True)

    def count_ge(cand):
        return count(lambda k: k >= cand) + jnp.where(key_new >= cand, 1, 0)

    tau = jnp.maximum(_kth_largest_key(count_ge, (nb, 1), n_sel), INT_MIN + 1)

    quota = n_sel - (count(lambda k: k > tau) + jnp.where(key_new > tau, 1, 0))

    ri = lax.broadcasted_iota(I32, (LANE, LANE), 0)
    ci = lax.broadcasted_iota(I32, (LANE, LANE), 1)
    upper = (ri <= ci).astype(BF16)

    def excl_prefix(flag):
        return (_dot(flag.astype(BF16), upper) - flag).astype(I32)

    def rank_body(j, carry):
        base, base_eq = carry
        kj = keys_s[j]
        eq = kj == tau
        eqf = jnp.where(eq, 1.0, 0.0)
        take = (kj > tau) | (eq & (base_eq + excl_prefix(eqf) < quota))
        takef = jnp.where(take, 1.0, 0.0)
        n_take = jnp.sum(takef, axis=1, keepdims=True)
        rows = pl.ds(pl.multiple_of(j * nb, nb), nb)
        rank_s[rows, :] = jnp.where(take, (base + excl_prefix(takef) + 1).astype(F32), 0.0)
        flag_s[rows, :] = jnp.where((kj < KEY_POS_INF) & (kj > KEY_NEG_INF), 0.0, 1.0)
        cnt_s[rows, :] = jnp.broadcast_to(n_take, (nb, LANE))
        return (base + n_take.astype(I32), base_eq + jnp.sum(eqf, axis=1, keepdims=True).astype(I32))

    zero = jnp.zeros((nb, 1), I32)
    n_before_new, eq_before_new = lax.fori_loop(0, n_blk, rank_body, (zero, zero), unroll=4)

    new_sel = (key_new > tau) | ((key_new == tau) & (eq_before_new < quota))
    new_fin = (key_new < KEY_POS_INF) & (key_new > KEY_NEG_INF)
    new_pos = jnp.where(new_fin, n_past, n_past | IDX_FLAG)
    bi = lax.broadcasted_iota(I32, (n_blk, n_blk), 0)
    bj = lax.broadcasted_iota(I32, (n_blk, n_blk), 1)
    before = (bj < bi).astype(BF16)
    blk_id = lax.broadcasted_iota(I32, (n_blk, LANE), 0).astype(F32)
    slot_lane = lax.broadcasted_iota(I32, (n_blk, n_sel), 1).astype(F32)
    slot_row = lax.broadcasted_iota(I32, (n_sel, LANE), 0)
    lane_sel = lax.broadcasted_iota(I32, (n_sel, LANE), 1)
    reps = n_sel // LANE

    def per_seq(b, _):
        rows = pl.ds(b, n_blk, stride=nb)
        cnt = cnt_s[rows, :]
        first = _dot(before, cnt.astype(BF16))
        first_w = jnp.concatenate([first] * reps, axis=1)
        cnt_w = jnp.concatenate([cnt] * reps, axis=1)
        onehot = jnp.where((slot_lane >= first_w) & (slot_lane < first_w + cnt_w), 1.0, 0.0).astype(BF16)
        picked = _dot_tn(onehot, jnp.concatenate([rank_s[rows, :], flag_s[rows, :], blk_id], axis=1).astype(BF16))
        hit = picked[:, :LANE] == (slot_row + 1).astype(F32)
        pos = (picked[:, 2 * LANE:] * LANE + picked[:, LANE:2 * LANE] * IDX_FLAG).astype(I32) + lane_sel
        col = jnp.sum(jnp.where(hit, pos, 0), axis=1, keepdims=True)
        rb = lax.broadcasted_iota(I32, (nb, 1), 0) == b
        nbn = jnp.sum(jnp.where(rb, n_before_new, 0), axis=0, keepdims=True)
        nsel = jnp.sum(jnp.where(rb & new_sel, 1, 0), axis=0, keepdims=True)
        npos = jnp.sum(jnp.where(rb, new_pos, 0), axis=0, keepdims=True)
        col = col + jnp.where((slot_row[:, 0:1] == nbn) & (nsel > 0), npos, 0)
        idx_ref[b] = col
        return 0

    lax.fori_loop(0, nb, per_seq, 0)


def _attn_sample_kernel(n_past, idx_sm, pt_sm, idxc_ref, q_ref, kn_ref, vn_ref, rb_ref, ck_hbm, cv_hbm, o_ref,
                        kbuf, vbuf, sem):
    b = pl.program_id(0)
    n_seq = pl.num_programs(0)
    n_sel = kbuf.shape[1] // N_KV_B
    hd = HEAD_DIM
    page_shift = int(math.log2(PAGE_SIZE))

    def row_copies(buf, r, page, slot):
        dst = pl.ds(N_KV_B * r, N_KV_B)
        return (pltpu.make_async_copy(ck_hbm.at[page, slot], kbuf.at[buf, dst, :], sem.at[0, buf]),
                pltpu.make_async_copy(cv_hbm.at[page, slot], vbuf.at[buf, dst, :], sem.at[1, buf]))

    def gather(seq, buf):
        def start(r, _):
            p = jnp.minimum(idx_sm[seq, r] & (IDX_FLAG - 1), n_past - 1)
            ck, cv = row_copies(buf, r, pt_sm[seq, p >> page_shift], p & (PAGE_SIZE - 1))
            ck.start()
            cv.start()
            return 0

        lax.fori_loop(0, n_sel, start, 0, unroll=8)

    @pl.when(b == 0)
    def _():
        gather(0, 0)

    @pl.when(b + 1 < n_seq)
    def _():
        gather(b + 1, (b + 1) % 2)

    cur = b % 2

    def wait(r, _):
        ck, cv = row_copies(cur, r, 0, 0)
        ck.wait()
        cv.wait()
        return 0

    lax.fori_loop(0, n_sel, wait, 0, unroll=8)

    idxc = idxc_ref[0]
    valid = idxc < IDX_FLAG
    idx = idxc & (IDX_FLAG - 1)
    is_new = idx >= n_past
    bucket = _t5_bucket(n_past - idx)
    bias = jnp.zeros((n_sel, LANE), F32)
    for bkt in range(N_BUCKETS):
        bias = jnp.where(bucket == bkt, rb_ref[bkt:bkt + 1, :], bias)
    q_pad = jnp.concatenate([q_ref[0], jnp.zeros((LANE - N_HEADS_B, hd), F32)], axis=0).astype(BF16)
    lane = lax.broadcasted_iota(I32, (n_sel, LANE), 1)
    lg = jnp.zeros((n_sel, LANE), F32)
    vals = []
    for n in range(N_KV_B):
        rows_n = pl.ds(n, n_sel, stride=N_KV_B)
        kn = jnp.where(is_new, kn_ref[0, :, n * hd:(n + 1) * hd], kbuf[cur, rows_n, :]).astype(BF16)
        vals.append(jnp.where(is_new, vn_ref[0, :, n * hd:(n + 1) * hd], vbuf[cur, rows_n, :]))
        lg = jnp.where(lane // GQA_GROUP == n, _dot_nt(kn, q_pad), lg)
    lg = jnp.where(valid, lg * (hd ** -0.5) + bias, NEG_BIG)
    m = jnp.max(lg, axis=0, keepdims=True)
    p = jnp.where(valid, jnp.exp(lg - m), 0.0)
    p = p / jnp.sum(p, axis=0, keepdims=True)
    for h in range(N_HEADS_B):
        o_h = jnp.sum(p[:, h:h + 1] * vals[h // GQA_GROUP], axis=0, keepdims=True)
        o_ref[0, :, h * hd:(h + 1) * hd] = o_h.astype(o_ref.dtype)


def _attn_sample_call(idx3, page_table, q3, k_new, v_new, rel_bias, cache_k, cache_v):
    nb, n_sel, _ = idx3.shape
    n_past = page_table.shape[1] * PAGE_SIZE
    kvw = N_KV_B * HEAD_DIM
    qbw = N_HEADS_B * HEAD_DIM
    grid_spec = pltpu.PrefetchScalarGridSpec(
        num_scalar_prefetch=2,
        grid=(nb,),
        in_specs=[pl.BlockSpec((1, n_sel, 1), lambda b, *_: (b, 0, 0)),
                  pl.BlockSpec((1, N_HEADS_B, HEAD_DIM), lambda b, *_: (b, 0, 0)),
                  pl.BlockSpec((1, 1, kvw), lambda b, *_: (b, 0, 0)),
                  pl.BlockSpec((1, 1, kvw), lambda b, *_: (b, 0, 0)),
                  pl.BlockSpec((N_BUCKETS, LANE), lambda b, *_: (0, 0)),
                  pl.BlockSpec(memory_space=pl.ANY),
                  pl.BlockSpec(memory_space=pl.ANY)],
        out_specs=pl.BlockSpec((1, 1, qbw), lambda b, *_: (b, 0, 0)),
        scratch_shapes=[pltpu.VMEM((2, n_sel * N_KV_B, HEAD_DIM), F32), pltpu.VMEM((2, n_sel * N_KV_B, HEAD_DIM), F32),
                        pltpu.SemaphoreType.DMA((2, 2))],
    )
    return pl.pallas_call(
        functools.partial(_attn_sample_kernel, n_past),
        out_shape=jax.ShapeDtypeStruct((nb, 1, qbw), BF16),
        grid_spec=grid_spec,
        compiler_params=_cparams(("arbitrary",)),
        name="attn_sample",
    )(idx3.reshape(nb, n_sel), page_table, idx3, q3, k_new.reshape(nb, 1, kvw), v_new.reshape(nb, 1, kvw),
      jnp.pad(rel_bias.astype(F32), ((0, 0), (0, LANE - N_HEADS_B))), cache_k, cache_v)


def _topk_sample_call(scores2, qi2, wi2, ki_new, n_sel):
    nb, n_past = scores2.shape
    n_blk = n_past // LANE
    return pl.pallas_call(
        functools.partial(_topk_sample_kernel, n_sel),
        out_shape=jax.ShapeDtypeStruct((nb, n_sel, 1), I32),
        scratch_shapes=[pltpu.VMEM((n_blk, nb, LANE), I32), pltpu.VMEM((nb, n_past), I32),
                        pltpu.VMEM((n_blk * nb, LANE), F32), pltpu.VMEM((n_blk * nb, LANE), F32),
                        pltpu.VMEM((n_blk * nb, LANE), F32)],
        compiler_params=pltpu.CompilerParams(vmem_limit_bytes=VMEM_LIMIT),
        name="topk_sample",
    )(scores2, qi2, wi2, ki_new)


def _wprep_kernel(w_ref, o_ref):
    src = [0]
    for s in SPLIT_SIZES:
        src.append(src[-1] + s)
    dst = (OFF_CONV, OFF_SM + SM_A, OFF_SM + SM_B, OFF_Z, OFF_QB, OFF_KB, OFF_VB, OFF_QI, OFF_SM + SM_W, OFF_KI)
    o_ref[:, OFF_SM:] = jnp.zeros((o_ref.shape[0], NP - OFF_SM), o_ref.dtype)
    for i, d in enumerate(dst):
        o_ref[:, d:d + SPLIT_SIZES[i]] = w_ref[0, :, src[i]:src[i + 1]].astype(o_ref.dtype)


def _prep_w_in(w_all, layer):
    _, rows, n_proj = w_all.shape
    tr = 256
    return pl.pallas_call(
        _wprep_kernel,
        out_shape=jax.ShapeDtypeStruct((rows, NP), BF16),
        grid=(rows // tr,),
        in_specs=[pl.BlockSpec((1, tr, n_proj), lambda i: (layer, i, 0))],
        out_specs=pl.BlockSpec((tr, NP), lambda i: (i, 0)),
        compiler_params=_cparams(("arbitrary",)),
        name="wprep",
    )(w_all)


def kernel(x_prompt, x_sample, c_prompt, c_sample, cache_k, cache_v, cache_kidx, state_ssm, state_conv, page_table,
           w_ada, b_ada, pre1_g, post1_g, pre2_g, post2_g, w_in, w_out, conv_w, a_log, dt_bias, gdn_norm_g,
           idx_knorm_g, idx_knorm_b, rel_bias, w_ff1, w_ff2):
    bsz, seq, _ = x_prompt.shape
    nb, dec_seq, _ = x_sample.shape
    assert dec_seq == 1, "the sample path handles one new token per sequence"
    depth = w_in.shape[0]
    n_past = page_table.shape[1] * PAGE_SIZE
    n_sel_s = min(TOPK_MAX, (n_past + dec_seq) // 4)
    kvw = N_KV_B * HEAD_DIM
    tm_in, tn_in = min(1024, seq), 1024
    tm_out = min(512, seq)
    tf = 1024

    hp = x_prompt.reshape(bsz * seq, D_MODEL)
    hs = x_sample.reshape(nb, D_MODEL)
    c_all = jnp.concatenate([c_sample, c_prompt], axis=0)
    new_p, new_s = [], []
    for l in range(depth):
        m = _adaln_call(c_all, w_ada[l], b_ada[l])
        w_in_bf = _prep_w_in(w_in, l)
        w_out_bf = w_out[l].astype(BF16)
        w1_bf = w_ff1[l].astype(BF16)
        w2_bf = w_ff2[l].astype(BF16)

        proj, conv_tail = _inproj_call(hp, pre1_g[l], _ModSpec(m, False, nb, seq, tm_in), w_in_bf, tm_in, tn_in,
                                       conv_w[l], seq)
        k_p, v_p, ki_p, k_bf, va_bf, ki_bf = _kvprep_call(proj, idx_knorm_g[l], idx_knorm_b[l], tm_in, True)
        proj3 = proj.reshape(bsz, seq, NP)
        o_a, ssm_p = _gdn_prompt_call(proj3, a_log[l], dt_bias[l], gdn_norm_g[l])
        o_b = _attn_prompt_call(proj3, ki_bf.reshape(bsz, seq, D_IDX), k_bf.reshape(bsz, seq, kvw),
                                va_bf.reshape(bsz, seq, 2 * kvw), rel_bias)
        mod_p = _ModSpec(m, False, nb, seq, tm_out)
        x1, h2 = _outproj_call(o_a.reshape(bsz * seq, VAL_DIM_A), o_b.reshape(bsz * seq, N_HEADS_B * HEAD_DIM), hp,
                               w_out_bf, post1_g[l], pre2_g[l], mod_p, tm_out)
        hp = _ffn_call(h2, x1, w1_bf, w2_bf, post2_g[l], mod_p, tm_out, tf)
        conv_p = conv_tail.reshape(bsz, seq // tm_in, CONV_TAIL, CONV_DIM)[:, -1, CONV_TAIL - (CONV_W - 1):, :]
        new_p.append((k_p.reshape(bsz, seq, N_KV_B, HEAD_DIM), v_p.reshape(bsz, seq, N_KV_B, HEAD_DIM),
                      ki_p.reshape(bsz, seq, D_IDX), ssm_p, conv_p))

        mod_s = _ModSpec(m, True, 0, 1, nb)
        proj_s = _inproj_call(hs, pre1_g[l], mod_s, w_in_bf, nb, tn_in)
        k_s, v_s, ki_s = _kvprep_call(proj_s, idx_knorm_g[l], idx_knorm_b[l], nb)
        o_a_s, ssm_s, conv_s = _gdn_sample_call(proj_s.reshape(nb, 1, NP), state_conv[l], state_ssm[l], conv_w[l],
                                                a_log[l], dt_bias[l], gdn_norm_g[l])
        qi2 = proj_s[:, OFF_QI:OFF_QI + N_IDX_HEADS * D_IDX]
        wi2 = proj_s[:, OFF_SM + SM_W:OFF_SM + SM_W + N_IDX_HEADS]
        scores = _idx_sample_call(page_table, qi2.reshape(nb, N_IDX_HEADS, D_IDX), wi2.reshape(nb, N_IDX_HEADS, 1),
                                  cache_kidx[l])
        idx3 = _topk_sample_call(scores.reshape(nb, n_past), qi2, wi2, ki_s, n_sel_s)
        q3 = proj_s[:, OFF_QB:OFF_QB + N_HEADS_B * HEAD_DIM].reshape(nb, N_HEADS_B, HEAD_DIM)
        o_b_s = _attn_sample_call(idx3, page_table, q3, k_s, v_s, rel_bias, cache_k[l], cache_v[l])
        x1s, h2s = _outproj_call(o_a_s.reshape(nb, VAL_DIM_A), o_b_s.reshape(nb, N_HEADS_B * HEAD_DIM), hs,
                                 w_out_bf, post1_g[l], pre2_g[l], mod_s, nb)
        hs = _ffn_call(h2s, x1s, w1_bf, w2_bf, post2_g[l], mod_s, nb, tf)
        new_s.append((k_s.reshape(nb, 1, N_KV_B, HEAD_DIM), v_s.reshape(nb, 1, N_KV_B, HEAD_DIM),
                      ki_s.reshape(nb, 1, D_IDX), ssm_s, conv_s))

    p_k, p_v, p_kidx, p_ssm, p_conv = [jnp.stack([st[i] for st in new_p]) for i in range(5)]
    s_k, s_v, s_kidx, s_ssm, s_conv = [jnp.stack([st[i] for st in new_s]) for i in range(5)]
    return (hp.reshape(bsz, seq, D_MODEL), hs.reshape(nb, 1, D_MODEL), p_k, p_v, p_kidx, p_ssm, p_conv,
            s_k, s_v, s_kidx, s_ssm, s_conv)
```

```python
import functools
import math

import jax
import jax.numpy as jnp
from jax import lax
from jax.experimental import pallas as pl
from jax.experimental.pallas import tpu as pltpu

F32 = jnp.float32
BF16 = jnp.bfloat16
I32 = jnp.int32

D_MODEL = 2048
PAGE_SIZE = 128
HEAD_DIM = 128
N_HEADS_A = 8
KEY_DIM_A = N_HEADS_A * HEAD_DIM
VAL_DIM_A = N_HEADS_A * HEAD_DIM
CONV_DIM = 2 * KEY_DIM_A + VAL_DIM_A
CONV_W = 4
N_HEADS_B = 8
N_KV_B = 2
GQA_GROUP = N_HEADS_B // N_KV_B
N_IDX_HEADS = 16
D_IDX = 128
TOPK_MAX = 256
N_BUCKETS = 32
MAX_DISTANCE = 128
D_FF = 4 * D_MODEL
EPS = 1e-6

SPLIT_SIZES = (CONV_DIM, N_HEADS_A, N_HEADS_A, VAL_DIM_A, N_HEADS_B * HEAD_DIM, N_KV_B * HEAD_DIM,
               N_KV_B * HEAD_DIM, N_IDX_HEADS * D_IDX, N_IDX_HEADS, D_IDX)

OFF_CONV = 0
OFF_Z = OFF_CONV + CONV_DIM
OFF_QI = OFF_Z + VAL_DIM_A
OFF_QB = OFF_QI + N_IDX_HEADS * D_IDX
OFF_KB = OFF_QB + N_HEADS_B * HEAD_DIM
OFF_VB = OFF_KB + N_KV_B * HEAD_DIM
OFF_KI = OFF_VB + N_KV_B * HEAD_DIM
OFF_SM = OFF_KI + D_IDX
NP = 8192
SM_A, SM_B, SM_W = 0, N_HEADS_A, 2 * N_HEADS_A

LANE = 128
VMEM_LIMIT = 56 * 1024 * 1024
INT_MIN = -(2 ** 31)
NEG_BIG = -1e30

GDN_CHUNK = 128
GDN_GROUP = 256
ATT_TQ = 256
ATT_TK = 256


def _cparams(sem):
    return pltpu.CompilerParams(dimension_semantics=sem, vmem_limit_bytes=VMEM_LIMIT)


def _sigmoid(x):
    return 1.0 / (1.0 + jnp.exp(-x))


def _softplus(x):
    return jnp.maximum(x, 0.0) + jnp.log(1.0 + jnp.exp(-jnp.abs(x)))


def _dot(a, b):
    return jnp.dot(a, b, preferred_element_type=F32)


def _dot_nt(a, b):
    return lax.dot_general(a, b, (((1,), (1,)), ((), ())), preferred_element_type=F32)


def _dot_tn(a, b):
    return lax.dot_general(a, b, (((0,), (0,)), ((), ())), preferred_element_type=F32)


def _mod(ref):
    v = ref[...]
    return v[0] if v.ndim == 3 else v


def _adaln_kernel(c_ref, w_ref, b_ref, o_ref):
    c = c_ref[...]
    s = (c * _sigmoid(c)).astype(BF16)
    o_ref[...] = _dot(s, w_ref[...].astype(BF16)) + b_ref[...]


def _adaln_call(c_all, w_ada, b_ada):
    n_rows = c_all.shape[0]
    n_out = w_ada.shape[1]
    tn = 1024
    return pl.pallas_call(
        _adaln_kernel,
        out_shape=jax.ShapeDtypeStruct((n_rows, n_out), F32),
        grid=(n_out // tn,),
        in_specs=[pl.BlockSpec((n_rows, D_MODEL), lambda j: (0, 0)),
                  pl.BlockSpec((D_MODEL, tn), lambda j: (0, j)),
                  pl.BlockSpec((1, tn), lambda j: (0, j))],
        out_specs=pl.BlockSpec((n_rows, tn), lambda j: (0, j)),
        compiler_params=_cparams(("arbitrary",)),
        name="adaln",
    )(c_all, w_ada, b_ada.reshape(1, n_out))


class _ModSpec:
    def __init__(self, m2, per_row, row_off, rows_per_batch, tm):
        self.per_row = per_row
        self.arr = m2 if per_row else m2.reshape(m2.shape[0], 1, m2.shape[1])
        self.row_off = row_off
        self.blocks_per_batch = None if per_row else rows_per_batch // tm
        self.tm = tm

    def spec(self, k, grid_rank):
        if self.per_row:
            if grid_rank == 1:
                return pl.BlockSpec((self.tm, D_MODEL), lambda i: (0, k))
            return pl.BlockSpec((self.tm, D_MODEL), lambda i, j: (0, k))
        bpb, off = self.blocks_per_batch, self.row_off
        if grid_rank == 1:
            return pl.BlockSpec((1, 1, D_MODEL), lambda i: (off + i // bpb, 0, k))
        return pl.BlockSpec((1, 1, D_MODEL), lambda i, j: (off + i // bpb, 0, k))


CONV_TAIL = 8


def _l2norm(x):
    return x * lax.rsqrt(jnp.sum(x * x, axis=-1, keepdims=True) + EPS)


def _inproj_kernel(blocks_per_seq, x_ref, g_ref, sh_ref, sc_ref, w_ref, *rest):
    j = pl.program_id(1)
    if blocks_per_seq is None:
        o_ref, h_scr = rest
    else:
        cw_ref, o_ref, tail_ref, h_scr, carry_s = rest

    @pl.when(j == 0)
    def _():
        x = x_ref[...]
        y = x * lax.rsqrt(jnp.mean(x * x, axis=-1, keepdims=True) + EPS) * g_ref[...]
        h_scr[...] = (y * (1.0 + _mod(sc_ref)) + _mod(sh_ref)).astype(BF16)

    if blocks_per_seq is None:
        o_ref[...] = _dot(h_scr[...], w_ref[...])
        return

    n_conv = CONV_DIM // o_ref.shape[1]

    @pl.when(j >= n_conv)
    def _():
        o_ref[...] = _dot(h_scr[...], w_ref[...])

    for blk in range(n_conv):
        @pl.when(j == blk)
        def _(blk=blk):
            raw = _dot(h_scr[...], w_ref[...])
            tm = raw.shape[0]
            cw = cw_ref[...]
            first = pl.program_id(0) % blocks_per_seq == 0
            prev = jnp.where(first, 0.0, carry_s[blk])
            carry_s[blk] = raw[tm - CONV_TAIL:]
            tail_ref[0] = raw[tm - CONV_TAIL:]

            def taps(xx):
                y = xx * cw[CONV_W - 1:CONV_W]
                for s in range(1, CONV_W):
                    y = y + pltpu.roll(xx, s, 0) * cw[CONV_W - 1 - s:CONV_W - s]
                return y

            head = taps(jnp.concatenate([prev, raw[:CONV_TAIL]], axis=0))[CONV_TAIL:]
            y = jnp.concatenate([head, taps(raw)[CONV_TAIL:]], axis=0)
            y = y * _sigmoid(y)
            if blk == 2:
                o_ref[...] = y
            else:
                for h in range(o_ref.shape[1] // HEAD_DIM):
                    yh = _l2norm(y[:, h * HEAD_DIM:(h + 1) * HEAD_DIM])
                    o_ref[:, h * HEAD_DIM:(h + 1) * HEAD_DIM] = yh * (HEAD_DIM ** -0.5) if blk == 0 else yh


def _inproj_call(x2d, g, mod, w_bf, tm, tn, conv_w=None, seq=None):
    m_rows = x2d.shape[0]
    n_cols = w_bf.shape[1]
    in_specs = [pl.BlockSpec((tm, D_MODEL), lambda i, j: (i, 0)),
                pl.BlockSpec((1, D_MODEL), lambda i, j: (0, 0)),
                mod.spec(0, 2), mod.spec(1, 2),
                pl.BlockSpec((D_MODEL, tn), lambda i, j: (0, j))]
    args = [x2d, g.reshape(1, D_MODEL), mod.arr, mod.arr, w_bf]
    out_shape = jax.ShapeDtypeStruct((m_rows, n_cols), F32)
    out_specs = pl.BlockSpec((tm, tn), lambda i, j: (i, j))
    scratch = [pltpu.VMEM((tm, D_MODEL), BF16)]
    bps = None
    if conv_w is not None:
        assert tn == KEY_DIM_A == VAL_DIM_A and OFF_CONV == 0, "q, k, v must each be one leading column block"
        bps = seq // tm
        n_conv = CONV_DIM // tn
        last = n_conv - 1
        in_specs.append(pl.BlockSpec((CONV_W, tn), lambda i, j: (0, jnp.minimum(j, last))))
        args.append(conv_w)
        out_shape = (out_shape, jax.ShapeDtypeStruct((m_rows // tm, CONV_TAIL, CONV_DIM), F32))
        out_specs = (out_specs, pl.BlockSpec((1, CONV_TAIL, tn), lambda i, j: (i, 0, jnp.minimum(j, last))))
        scratch.append(pltpu.VMEM((n_conv, CONV_TAIL, tn), F32))
    return pl.pallas_call(
        functools.partial(_inproj_kernel, bps),
        out_shape=out_shape,
        grid=(m_rows // tm, n_cols // tn),
        in_specs=in_specs,
        out_specs=out_specs,
        scratch_shapes=scratch,
        compiler_params=_cparams(("arbitrary", "arbitrary")),
        name="inproj",
    )(*args)


def _kvprep_kernel(kb_ref, vb_ref, ki_ref, lg_ref, lb_ref, k_ref, v_ref, kidx_ref, *bf_refs):
    kb = kb_ref[...]
    vb = vb_ref[...]
    for n in range(N_KV_B):
        k_ref[:, n, :] = kb[:, n * HEAD_DIM:(n + 1) * HEAD_DIM]
        v_ref[:, n, :] = vb[:, n * HEAD_DIM:(n + 1) * HEAD_DIM]
    x = ki_ref[...]
    mu = jnp.mean(x, axis=-1, keepdims=True)
    xc = x - mu
    var = jnp.mean(xc * xc, axis=-1, keepdims=True)
    ki = xc * lax.rsqrt(var + EPS) * lg_ref[...] + lb_ref[...]
    kidx_ref[...] = ki
    if bf_refs:
        kbf_ref, va_ref, kibf_ref = bf_refs
        kbf_ref[...] = kb.astype(BF16)
        kibf_ref[...] = ki.astype(BF16)
        hd = HEAD_DIM
        ones_col = jnp.where(lax.broadcasted_iota(I32, (vb.shape[0], hd), 1) == 0, 1.0, 0.0)
        va_ref[...] = jnp.concatenate([piece for n in range(N_KV_B) for piece in (vb[:, n * hd:(n + 1) * hd], ones_col)],
                                      axis=1).astype(BF16)


def _kvprep_call(proj, ln_g, ln_b, tm, with_bf16=False):
    m_rows = proj.shape[0]
    kvw = N_KV_B * HEAD_DIM
    row = lambda i: (i, 0)
    kv_shape = jax.ShapeDtypeStruct((m_rows, N_KV_B, HEAD_DIM), F32)
    kv_spec = pl.BlockSpec((tm, N_KV_B, HEAD_DIM), lambda i: (i, 0, 0))
    out_shape = [kv_shape, kv_shape, jax.ShapeDtypeStruct((m_rows, D_IDX), F32)]
    out_specs = [kv_spec, kv_spec, pl.BlockSpec((tm, D_IDX), row)]
    if with_bf16:
        out_shape += [jax.ShapeDtypeStruct((m_rows, kvw), BF16), jax.ShapeDtypeStruct((m_rows, 2 * kvw), BF16),
                      jax.ShapeDtypeStruct((m_rows, D_IDX), BF16)]
        out_specs += [pl.BlockSpec((tm, kvw), row), pl.BlockSpec((tm, 2 * kvw), row), pl.BlockSpec((tm, D_IDX), row)]
    return pl.pallas_call(
        _kvprep_kernel,
        out_shape=tuple(out_shape),
        grid=(m_rows // tm,),
        in_specs=[pl.BlockSpec((tm, kvw), lambda i: (i, OFF_KB // kvw)),
                  pl.BlockSpec((tm, kvw), lambda i: (i, OFF_VB // kvw)),
                  pl.BlockSpec((tm, D_IDX), lambda i: (i, OFF_KI // D_IDX)),
                  pl.BlockSpec((1, D_IDX), lambda i: (0, 0)),
                  pl.BlockSpec((1, D_IDX), lambda i: (0, 0))],
        out_specs=tuple(out_specs),
        compiler_params=_cparams(("arbitrary",)),
        name="kvprep",
    )(proj, proj, proj, ln_g.reshape(1, D_IDX), ln_b.reshape(1, D_IDX))


def _gdn_prompt_kernel(q_ref, k_ref, v_ref, z_ref, sm_ref, alog_ref, dtb_ref, ng_ref,
                       o_ref, s_out_ref,
                       gc_s, bt_s, l_s, a_s, x_s, u_s, w_s, mn_s, sts_s):
    seq = q_ref.shape[1]
    hd = HEAD_DIM
    grp = GDN_GROUP
    chunk = GDN_CHUNK
    n_grp = seq // grp
    n_chunk = seq // chunk
    cpg = grp // chunk
    h = pl.program_id(1)

    row = lax.broadcasted_iota(I32, (seq, hd), 0)
    lane = lax.broadcasted_iota(I32, (seq, hd), 1)

    qn_s, kn_s, vn_s = q_ref.at[0], k_ref.at[0], v_ref.at[0]

    sm = sm_ref[0]
    g_all = -jnp.exp(alog_ref[...]) * _softplus(sm + dtb_ref[...])
    b_all = _sigmoid(sm)
    g_col = jnp.sum(jnp.where(lane == h + SM_A, g_all, 0.0), axis=1, keepdims=True)
    b_col = jnp.sum(jnp.where(lane == h + SM_B, b_all, 0.0), axis=1, keepdims=True)
    gc = jnp.broadcast_to(g_col, (seq, hd))
    pos = row & (chunk - 1)
    s = 1
    while s < chunk:
        gc = gc + jnp.where(pos >= s, pltpu.roll(gc, s, 0), 0.0)
        s *= 2
    gc_s[...] = gc
    bt_s[...] = jnp.broadcast_to(b_col, (seq, hd))

    ii = lax.broadcasted_iota(I32, (grp, grp), 0)
    jj = lax.broadcasted_iota(I32, (grp, grp), 1)
    xr = ii ^ jj
    same = (xr >> int(math.log2(chunk))) == 0
    causal = same & (ii >= jj)
    strict = same & (ii > jj)
    eye = (ii == jj).astype(F32)

    def build(gi, _):
        r0 = pl.multiple_of(gi * grp, grp)
        kk = kn_s[pl.ds(r0, grp), :]
        qq = qn_s[pl.ds(r0, grp), :]
        bt = bt_s[pl.ds(r0, grp), :]
        gcg = gc_s[pl.ds(r0, grp), :]
        kb = (kk * bt).astype(BF16)
        kkb = kk.astype(BF16)
        kkt = _dot_nt(kb, kkb)
        qkt = _dot_nt(qq.astype(BF16), kkb)
        colb = jnp.concatenate([gcg, gcg], axis=1)
        diff = colb - colb.T
        decay = jnp.where(causal, jnp.exp(jnp.where(causal, diff, 0.0)), 0.0)
        lmat = jnp.where(strict, kkt * decay, 0.0)
        l_s[gi] = lmat
        a_s[pl.ds(r0, grp), :] = jnp.where(causal, qkt * decay, 0.0)
        x_s[gi] = eye - jnp.where(xr == 1, lmat, 0.0)
        return 0

    lax.fori_loop(0, n_grp, build, 0, unroll=True)

    for lvl in range(1, int(math.log2(chunk))):
        def level(gi, _, lvl=lvl):
            xm = x_s[gi]
            cl = jnp.where((xr >> lvl) == 1, l_s[gi], 0.0).astype(BF16)
            xb = xm.astype(BF16)
            x_s[gi] = xm - _dot(_dot(xb, cl).astype(BF16), xb)
            return 0

        lax.fori_loop(0, n_grp, level, 0, unroll=True)

    def apply(gi, _):
        r0 = pl.multiple_of(gi * grp, grp)
        kk = kn_s[pl.ds(r0, grp), :]
        vv = vn_s[pl.ds(r0, grp), :]
        bt = bt_s[pl.ds(r0, grp), :]
        gcg = gc_s[pl.ds(r0, grp), :]
        rhs = jnp.concatenate([vv * bt, kk * bt * jnp.exp(gcg)], axis=1).astype(BF16)
        uw = _dot(x_s[gi].astype(BF16), rhs)
        u_s[pl.ds(r0, grp), :] = uw[:, :hd]
        w_s[pl.ds(r0, grp), :] = uw[:, hd:]
        for cc in range(cpg):
            lo = cc * chunk
            g_last = gcg[lo + chunk - 1:lo + chunk, :]
            k_dec = (kk[lo:lo + chunk] * jnp.exp(g_last - gcg[lo:lo + chunk])).astype(BF16)
            wu = jnp.concatenate([uw[lo:lo + chunk, hd:], uw[lo:lo + chunk, :hd]], axis=1).astype(BF16)
            mn_s[gi * cpg + cc] = _dot_tn(k_dec, wu)
        return 0

    lax.fori_loop(0, n_grp, apply, 0, unroll=True)

    def scan(c, st):
        sts_s[c] = st
        g_last = gc_s[pl.ds(c * chunk + chunk - 1, 1), :]
        mn = mn_s[c]
        return st * jnp.exp(g_last) - _dot(mn[:, :hd].astype(BF16), st.astype(BF16)) + mn[:, hd:]

    s_out_ref[0, 0] = lax.fori_loop(0, n_chunk, scan, jnp.zeros((hd, hd), F32))

    ng = ng_ref[...]

    def emit(gi, _):
        r0 = pl.multiple_of(gi * grp, grp)
        gcg = gc_s[pl.ds(r0, grp), :]
        qg = qn_s[pl.ds(r0, grp), :] * jnp.exp(gcg)
        wg = w_s[pl.ds(r0, grp), :]
        ws, qs = [], []
        for cc in range(cpg):
            lo = cc * chunk
            stb = sts_s[gi * cpg + cc].astype(BF16)
            both = _dot(jnp.concatenate([wg[lo:lo + chunk], qg[lo:lo + chunk]], axis=0).astype(BF16), stb)
            ws.append(both[:chunk])
            qs.append(both[chunk:])
        v_new = u_s[pl.ds(r0, grp), :] - jnp.concatenate(ws, axis=0)
        o = jnp.concatenate(qs, axis=0) + _dot(a_s[pl.ds(r0, grp), :].astype(BF16), v_new.astype(BF16))
        zc = z_ref[0, pl.ds(r0, grp), :]
        on = o * lax.rsqrt(jnp.mean(o * o, axis=-1, keepdims=True) + EPS) * ng
        o_ref[0, pl.ds(r0, grp), :] = (on * (zc * _sigmoid(zc))).astype(o_ref.dtype)
        return 0

    lax.fori_loop(0, n_grp, emit, 0, unroll=True)


def _lane_vec(v, off):
    return jnp.zeros((1, LANE), F32).at[0, off:off + v.shape[0]].set(v.astype(F32))


def _gdn_prompt_call(proj3, a_log, dt_bias, norm_g):
    bsz, seq, _ = proj3.shape
    hd = HEAD_DIM
    nh = N_HEADS_A
    n_grp = seq // GDN_GROUP
    col = lambda base: (lambda b, h: (b, 0, base // hd + h))
    vec = pl.BlockSpec((1, LANE), lambda b, h: (0, 0))
    return pl.pallas_call(
        _gdn_prompt_kernel,
        out_shape=(jax.ShapeDtypeStruct((bsz, seq, VAL_DIM_A), BF16),
                   jax.ShapeDtypeStruct((bsz, nh, hd, hd), F32)),
        grid=(bsz, nh),
        in_specs=[pl.BlockSpec((1, seq, hd), col(OFF_CONV)),
                  pl.BlockSpec((1, seq, hd), col(OFF_CONV + KEY_DIM_A)),
                  pl.BlockSpec((1, seq, hd), col(OFF_CONV + 2 * KEY_DIM_A)),
                  pl.BlockSpec((1, seq, hd), col(OFF_Z)),
                  pl.BlockSpec((1, seq, LANE), lambda b, h: (b, 0, OFF_SM // LANE)),
                  vec, vec, vec],
        out_specs=(pl.BlockSpec((1, seq, hd), lambda b, h: (b, 0, h)),
                   pl.BlockSpec((1, 1, hd, hd), lambda b, h: (b, h, 0, 0))),
        scratch_shapes=[pltpu.VMEM((seq, hd), F32), pltpu.VMEM((seq, hd), F32),
                        pltpu.VMEM((n_grp, GDN_GROUP, GDN_GROUP), F32),
                        pltpu.VMEM((seq, GDN_GROUP), F32),
                        pltpu.VMEM((n_grp, GDN_GROUP, GDN_GROUP), F32),
                        pltpu.VMEM((seq, hd), F32), pltpu.VMEM((seq, hd), F32),
                        pltpu.VMEM((seq // GDN_CHUNK, hd, 2 * hd), F32), pltpu.VMEM((seq // GDN_CHUNK, hd, hd), F32)],
        compiler_params=_cparams(("arbitrary", "arbitrary")),
        name="gdn_prompt",
    )(proj3, proj3, proj3, proj3, proj3,
      _lane_vec(a_log, SM_A), _lane_vec(dt_bias, SM_A), norm_g.reshape(1, hd).astype(F32))


def _t5_bucket(dist):
    n = jnp.maximum(dist, 0)
    max_exact = N_BUCKETS // 2
    nf = jnp.maximum(n, 1).astype(F32)
    large = max_exact + (jnp.log(nf / max_exact) / math.log(MAX_DISTANCE / max_exact)
                         * (N_BUCKETS - max_exact)).astype(I32)
    large = jnp.minimum(large, N_BUCKETS - 1)
    return jnp.where(n < max_exact, n, large)


def _bias_from_bucket(bucket, rb_ref, head):
    out = jnp.zeros(bucket.shape, F32)
    for b in range(N_BUCKETS):
        out = jnp.where(bucket == b, rb_ref[b, head], out)
    return out


def _order_key(score):
    bits = pltpu.bitcast(score + 0.0, I32)
    return bits ^ ((bits >> 31) & 0x7FFFFFFF)


I16 = jnp.int16
HALF_BIAS = 1 << 15
HALF_ROWS = 16
KEY_POS_INF = 0x7F800000
KEY_NEG_INF = INT_MIN + 0x7FFFFF


def _kth_largest_key(count_ge, shape, n_sel):
    def body(i, tau):
        bit = 31 - i
        cand = jnp.where(i == 0, tau ^ INT_MIN, tau | (1 << bit))
        return jnp.where(count_ge(cand) >= n_sel, cand, tau)

    return lax.fori_loop(0, 32, body, jnp.full(shape, INT_MIN, I32))


def _attn_prompt_kernel(n_sel, qi_ref, sm_ref, qb_ref, ki_ref, k_ref, va_ref, rb_ref, o_ref,
                        qi_s, qb_s, wt_s, keys_s, hi_s, lo_s, tau_s, bias_s, m_s, acc_s):
    tq, tk = ATT_TQ, ATT_TK
    hd = HEAD_DIM
    bi = pl.program_id(0)
    qblk = pl.program_id(1)
    n_kb = (qblk * tq) // tk + 1
    krow = lax.broadcasted_iota(I32, (tk, tq), 0)
    qcol = lax.broadcasted_iota(I32, (tk, tq), 1)

    @pl.when((bi == 0) & (qblk == 0))
    def _():
        for off in range(bias_s.shape[1]):
            bucket = _t5_bucket(off * tk + krow - qcol)
            for h in range(N_HEADS_B):
                bias_s[h, off] = _bias_from_bucket(bucket, rb_ref, h)

    qi_s[...] = qi_ref[0].astype(BF16)
    qb_s[...] = qb_ref[0].astype(BF16)
    wt_s[...] = sm_ref[0].T * (N_IDX_HEADS ** -0.5)
    t_pos = qblk * tq + qcol

    def idx_body(j, _):
        k0 = pl.multiple_of(j * tk, tk)
        kt = ki_ref[0, pl.ds(k0, tk), :]
        acc = jnp.zeros((tk, tq), F32)
        for h in range(N_IDX_HEADS):
            s = _dot_nt(kt, qi_s[:, h * D_IDX:(h + 1) * D_IDX])
            acc = acc + jnp.maximum(s, 0.0) * wt_s[SM_W + h:SM_W + h + 1, :]
        acc = acc * (D_IDX ** -0.5)
        key = jnp.where(k0 + krow <= t_pos, _order_key(acc), INT_MIN)
        keys_s[j] = key
        hi_s[j] = (key >> 16).astype(I16)
        lo_s[j] = ((key & 0xFFFF) - HALF_BIAS).astype(I16)
        return 0

    lax.fori_loop(0, n_kb, idx_body, 0)

    def kth_largest_half(half_s, n_at_start):
        def count_ge(cand):
            def cb(j, c):
                hit = jnp.where(half_s[j] >= cand, jnp.int16(1), jnp.int16(0))
                for r in range(tk // HALF_ROWS):
                    c = c + hit[r * HALF_ROWS:(r + 1) * HALF_ROWS]
                return c

            c = lax.fori_loop(0, n_kb, cb, jnp.zeros((HALF_ROWS, tq), I16))
            return jnp.sum(c.astype(I32), axis=0, keepdims=True)

        def body(i, carry):
            t, n_t = carry
            cand = jnp.where(i == 0, t ^ (-HALF_BIAS), t | (1 << (15 - i)))
            n_cand = count_ge(cand.astype(I16))
            ok = n_cand >= n_sel
            return jnp.where(ok, cand, t), jnp.where(ok, n_cand, n_t)

        return lax.fori_loop(0, 16, body, (jnp.full((1, tq), -HALF_BIAS, I32), n_at_start))

    tau_hi, n_ge_hi = kth_largest_half(hi_s, jnp.zeros((1, tq), I32))
    tau_hi16 = tau_hi.astype(I16)

    def narrow(j, _):
        hi = hi_s[j]
        lo_s[j] = jnp.where(hi > tau_hi16, jnp.int16(HALF_BIAS - 1),
                            jnp.where(hi == tau_hi16, lo_s[j], jnp.int16(-HALF_BIAS)))
        return 0

    lax.fori_loop(0, n_kb, narrow, 0)
    tau_lo, n_ge_tau = kth_largest_half(lo_s, n_ge_hi)
    tau = (tau_hi << 16) | ((tau_lo + HALF_BIAS) & 0xFFFF)
    tau = jnp.maximum(tau, KEY_NEG_INF + 1)

    def count_where(pred):
        def cb(j, c):
            hit = jnp.where(pred(keys_s[j], j * tk + krow), 1, 0)
            return c + jnp.sum(hit.reshape(tk // 8, 8, tq), axis=0)

        c = lax.fori_loop(0, n_kb, cb, jnp.zeros((8, tq), I32))
        return jnp.sum(c, axis=0, keepdims=True)

    @pl.when(jnp.max(n_ge_tau) > n_sel)
    def _():
        quota = n_sel - count_where(lambda k, pos: k > tau)
        pos_bits = (keys_s.shape[0] * tk - 1).bit_length()

        def body(i, last):
            cand = last | (1 << (pos_bits - 1 - i))
            before = count_where(lambda k, pos: (k == tau) & (pos < cand))
            return jnp.where(before < quota, cand, last)

        last_kept = lax.fori_loop(0, pos_bits, body, jnp.zeros((1, tq), I32))

        def demote(j, _):
            k = keys_s[j]
            keys_s[j] = jnp.where((k == tau) & (j * tk + krow > last_kept), tau - 1, k)
            return 0

        lax.fori_loop(0, n_kb, demote, 0)

    scale = hd ** -0.5
    lane_reps = tk // LANE
    tau_rep = jnp.broadcast_to(tau, (LANE, tq)).T
    tau_s[...] = jnp.concatenate([tau_rep] * lane_reps, axis=1)

    def block_inputs(j):
        kq = keys_s[j].T
        sel = jnp.where(kq < KEY_POS_INF, kq, INT_MIN) >= tau_s[...]
        return k_ref[0, pl.ds(pl.multiple_of(j * tk, tk), tk), :], sel

    def masked_logits(j, h, kt, sel):
        n = h // GQA_GROUP
        boff = jnp.minimum(qblk - j, bias_s.shape[1] - 1)
        lg = _dot_nt(qb_s[:, h * hd:(h + 1) * hd], kt[:, n * hd:(n + 1) * hd]) * scale + bias_s[h, boff]
        return jnp.where(sel, lg, NEG_BIG)

    m_s[...] = jnp.full(m_s.shape, NEG_BIG, F32)

    def max_body(j, _):
        kt, sel = block_inputs(j)
        for h in range(N_HEADS_B):
            lg = masked_logits(j, h, kt, sel)
            part = lg[:, :LANE]
            for r in range(1, lane_reps):
                part = jnp.maximum(part, lg[:, r * LANE:(r + 1) * LANE])
            m_s[h] = jnp.maximum(m_s[h], part)
        return 0

    lax.fori_loop(0, n_kb, max_body, 0)
    for h in range(N_HEADS_B):
        m_s[h] = jnp.broadcast_to(jnp.max(m_s[h], axis=1, keepdims=True), (tq, LANE))
    acc_s[...] = jnp.zeros(acc_s.shape, F32)

    def sum_body(j, _):
        kt, sel = block_inputs(j)
        va = va_ref[0, pl.ds(pl.multiple_of(j * tk, tk), tk), :]
        for h in range(N_HEADS_B):
            n = h // GQA_GROUP
            mrow = jnp.concatenate([m_s[h]] * lane_reps, axis=1)
            p = jnp.exp(masked_logits(j, h, kt, sel) - mrow)
            acc_s[h] += _dot(p.astype(BF16), va[:, n * 2 * hd:(n + 1) * 2 * hd])
        return 0

    lax.fori_loop(0, n_kb, sum_body, 0)
    for h in range(N_HEADS_B):
        a = acc_s[h]
        o_ref[0, :, h * hd:(h + 1) * hd] = (a[:, :hd] / a[:, hd:hd + 1]).astype(o_ref.dtype)


def _attn_prompt_call(proj3, ki_bf, k_bf, va_bf, rel_bias):
    bsz, seq, _ = proj3.shape
    tq, tk = ATT_TQ, ATT_TK
    assert tq == tk, "the bias-tile offsets assume square blocks"
    n_sel = min(TOPK_MAX, seq // 4)
    qiw = N_IDX_HEADS * D_IDX
    qbw = N_HEADS_B * HEAD_DIM
    kvw = N_KV_B * HEAD_DIM
    n_off = -(-(MAX_DISTANCE + tk - 1) // tk) + 1
    return pl.pallas_call(
        functools.partial(_attn_prompt_kernel, n_sel),
        out_shape=jax.ShapeDtypeStruct((bsz, seq, qbw), BF16),
        grid=(bsz, seq // tq),
        in_specs=[pl.BlockSpec((1, tq, qiw), lambda b, i: (b, i, OFF_QI // qiw)),
                  pl.BlockSpec((1, tq, LANE), lambda b, i: (b, i, OFF_SM // LANE)),
                  pl.BlockSpec((1, tq, qbw), lambda b, i: (b, i, OFF_QB // qbw)),
                  pl.BlockSpec((1, seq, D_IDX), lambda b, i: (b, 0, 0)),
                  pl.BlockSpec((1, seq, kvw), lambda b, i: (b, 0, 0)),
                  pl.BlockSpec((1, seq, 2 * kvw), lambda b, i: (b, 0, 0)),
                  pl.BlockSpec(memory_space=pltpu.SMEM)],
        out_specs=pl.BlockSpec((1, tq, qbw), lambda b, i: (b, i, 0)),
        scratch_shapes=[pltpu.VMEM((tq, qiw), BF16), pltpu.VMEM((tq, qbw), BF16),
                        pltpu.VMEM((LANE, tq), F32),
                        pltpu.VMEM((seq // tk, tk, tq), I32),
                        pltpu.VMEM((seq // tk, tk, tq), I16), pltpu.VMEM((seq // tk, tk, tq), I16),
                        pltpu.VMEM((tq, tk), I32),
                        pltpu.VMEM((N_HEADS_B, n_off, tq, tk), F32),
                        pltpu.VMEM((N_HEADS_B, tq, LANE), F32),
                        pltpu.VMEM((N_HEADS_B, tq, 2 * HEAD_DIM), F32)],
        compiler_params=_cparams(("arbitrary", "arbitrary")),
        name="attn_prompt",
    )(proj3, proj3, proj3, ki_bf, k_bf, va_bf, rel_bias.astype(F32))


def _rms(x, g):
    return x * lax.rsqrt(jnp.mean(x * x, axis=-1, keepdims=True) + EPS) * g


def _outproj_kernel(oa_ref, ob_ref, x_ref, wa_ref, wb_ref, pg_ref, g1_ref, p2_ref, sh2_ref, sc2_ref, x1_ref, h2_ref):
    rows = x_ref.shape[0]
    per_row = _mod(g1_ref).shape[0] == rows
    n_piece = 2 if rows % 16 == 0 and rows >= 256 else 1
    for r in range(n_piece):
        sl = slice(r * rows // n_piece, (r + 1) * rows // n_piece)
        pick = (lambda ref: _mod(ref)[sl]) if per_row else _mod
        mix = _dot(oa_ref[sl, :], wa_ref[...]) + _dot(ob_ref[sl, :], wb_ref[...])
        x1 = x_ref[sl, :] + pick(g1_ref) * _rms(mix, pg_ref[...])
        x1_ref[sl, :] = x1
        h2_ref[sl, :] = (_rms(x1, p2_ref[...]) * (1.0 + pick(sc2_ref)) + pick(sh2_ref)).astype(BF16)


def _outproj_call(oa, ob, x2d, w_out_bf, post1_g, pre2_g, mod, tm):
    m_rows = x2d.shape[0]
    half = w_out_bf.shape[0] // 2
    row = lambda i: (i, 0)
    vec = pl.BlockSpec((1, D_MODEL), lambda i: (0, 0))
    return pl.pallas_call(
        _outproj_kernel,
        out_shape=(jax.ShapeDtypeStruct((m_rows, D_MODEL), F32), jax.ShapeDtypeStruct((m_rows, D_MODEL), BF16)),
        grid=(m_rows // tm,),
        in_specs=[pl.BlockSpec((tm, half), row), pl.BlockSpec((tm, half), row), pl.BlockSpec((tm, D_MODEL), row),
                  pl.BlockSpec((half, D_MODEL), lambda i: (0, 0)), pl.BlockSpec((half, D_MODEL), lambda i: (1, 0)),
                  vec, mod.spec(2, 1), vec, mod.spec(3, 1), mod.spec(4, 1)],
        out_specs=(pl.BlockSpec((tm, D_MODEL), row), pl.BlockSpec((tm, D_MODEL), row)),
        compiler_params=_cparams(("arbitrary",)),
        name="outproj",
    )(oa, ob, x2d, w_out_bf, w_out_bf, post1_g.reshape(1, D_MODEL), mod.arr, pre2_g.reshape(1, D_MODEL),
      mod.arr, mod.arr)


def _ffn_kernel(h_ref, x1_ref, w1_ref, w2_ref, pg_ref, g2_ref, o_ref, acc_s):
    kk = pl.program_id(1)

    @pl.when(kk == 0)
    def _():
        acc_s[...] = jnp.zeros_like(acc_s)

    a = jnp.maximum(_dot(h_ref[...], w1_ref[...]), 0.0)
    acc_s[...] += _dot((a * a).astype(BF16), w2_ref[...])

    @pl.when(kk == pl.num_programs(1) - 1)
    def _():
        o_ref[...] = x1_ref[...] + _mod(g2_ref) * _rms(acc_s[...], pg_ref[...])


def _ffn_call(h2, x1, w1_bf, w2_bf, post2_g, mod, tm, tf):
    m_rows = x1.shape[0]
    return pl.pallas_call(
        _ffn_kernel,
        out_shape=jax.ShapeDtypeStruct((m_rows, D_MODEL), F32),
        grid=(m_rows // tm, D_FF // tf),
        in_specs=[pl.BlockSpec((tm, D_MODEL), lambda i, k: (i, 0)),
                  pl.BlockSpec((tm, D_MODEL), lambda i, k: (i, 0)),
                  pl.BlockSpec((D_MODEL, tf), lambda i, k: (0, k)),
                  pl.BlockSpec((tf, D_MODEL), lambda i, k: (k, 0)),
                  pl.BlockSpec((1, D_MODEL), lambda i, k: (0, 0)),
                  mod.spec(5, 2)],
        out_specs=pl.BlockSpec((tm, D_MODEL), lambda i, k: (i, 0)),
        scratch_shapes=[pltpu.VMEM((tm, D_MODEL), F32)],
        compiler_params=_cparams(("arbitrary", "arbitrary")),
        name="ffn",
    )(h2, x1, w1_bf, w2_bf, post2_g.reshape(1, D_MODEL), mod.arr)


def _gdn_sample_kernel(cin_ref, z_ref, sm_ref, sconv_ref, ssm_ref, cw_ref, alog_ref, dtb_ref, ng_ref,
                       o_ref, ssm_out_ref, conv_out_ref):
    hd = HEAD_DIM
    nh = N_HEADS_A
    prev = sconv_ref[0]
    xin = cin_ref[0]
    cw = cw_ref[...]
    y = xin * cw[CONV_W - 1:CONV_W]
    for s in range(CONV_W - 1):
        y = y + prev[s:s + 1] * cw[s:s + 1]
    y = y * _sigmoid(y)
    conv_out_ref[0, 0:CONV_W - 2, :] = prev[1:CONV_W - 1]
    conv_out_ref[0, CONV_W - 2:CONV_W - 1, :] = xin

    def heads(base):
        return jnp.concatenate([y[:, base + h * hd:base + (h + 1) * hd] for h in range(nh)], axis=0)

    q8 = _l2norm(heads(0)) * (hd ** -0.5)
    k8 = _l2norm(heads(KEY_DIM_A))
    v8 = heads(2 * KEY_DIM_A)
    sm = sm_ref[0]
    g_all = -jnp.exp(alog_ref[...]) * _softplus(sm + dtb_ref[...])
    b_all = _sigmoid(sm)
    kt = jnp.concatenate([k8, jnp.zeros((LANE - nh, hd), F32)], axis=0).T
    k8b = k8.astype(BF16)
    q8b = q8.astype(BF16)
    z = z_ref[0]
    ng = ng_ref[...]
    for h in range(nh):
        s1 = ssm_ref[0, h] * jnp.exp(g_all[:, SM_A + h:SM_A + h + 1])
        kv = _dot(k8b, s1.astype(BF16))[h:h + 1]
        delta = (v8[h:h + 1] - kv) * b_all[:, SM_B + h:SM_B + h + 1]
        s2 = s1 + kt[:, h:h + 1] * delta
        ssm_out_ref[0, h] = s2
        o = _dot(q8b, s2.astype(BF16))[h:h + 1]
        zh = z[:, h * hd:(h + 1) * hd]
        on = o * lax.rsqrt(jnp.mean(o * o, axis=-1, keepdims=True) + EPS) * ng
        o_ref[0, :, h * hd:(h + 1) * hd] = (on * (zh * _sigmoid(zh))).astype(o_ref.dtype)


def _gdn_sample_call(proj_s3, state_conv, state_ssm, conv_w, a_log, dt_bias, norm_g):
    nb = proj_s3.shape[0]
    hd, nh = HEAD_DIM, N_HEADS_A
    vec = pl.BlockSpec((1, LANE), lambda b: (0, 0))
    return pl.pallas_call(
        _gdn_sample_kernel,
        out_shape=(jax.ShapeDtypeStruct((nb, 1, VAL_DIM_A), BF16),
                   jax.ShapeDtypeStruct((nb, nh, hd, hd), F32),
                   jax.ShapeDtypeStruct((nb, CONV_W - 1, CONV_DIM), F32)),
        grid=(nb,),
        in_specs=[pl.BlockSpec((1, 1, CONV_DIM), lambda b: (b, 0, OFF_CONV // CONV_DIM)),
                  pl.BlockSpec((1, 1, VAL_DIM_A), lambda b: (b, 0, OFF_Z // VAL_DIM_A)),
                  pl.BlockSpec((1, 1, LANE), lambda b: (b, 0, OFF_SM // LANE)),
                  pl.BlockSpec((1, CONV_W - 1, CONV_DIM), lambda b: (b, 0, 0)),
                  pl.BlockSpec((1, nh, hd, hd), lambda b: (b, 0, 0, 0)),
                  pl.BlockSpec((CONV_W, CONV_DIM), lambda b: (0, 0)),
                  vec, vec, vec],
        out_specs=(pl.BlockSpec((1, 1, VAL_DIM_A), lambda b: (b, 0, 0)),
                   pl.BlockSpec((1, nh, hd, hd), lambda b: (b, 0, 0, 0)),
                   pl.BlockSpec((1, CONV_W - 1, CONV_DIM), lambda b: (b, 0, 0))),
        compiler_params=_cparams(("arbitrary",)),
        name="gdn_sample",
    )(proj_s3, proj_s3, proj_s3, state_conv, state_ssm, conv_w,
      _lane_vec(a_log, SM_A), _lane_vec(dt_bias, SM_A), norm_g.reshape(1, hd).astype(F32))


IDX_PAGES_PER_MATMUL = 4


def _idx_sample_kernel(pt_sm, qi_ref, wi_ref, kidx_hbm, o_ref, buf, sem):
    b = pl.program_id(0)
    n_seq = pl.num_programs(0)
    n_pages = buf.shape[1]

    def page_copy(page, slot, p):
        return pltpu.make_async_copy(kidx_hbm.at[page], buf.at[slot, p], sem.at[slot])

    def gather(seq, slot):
        def body(p, _):
            page_copy(pt_sm[seq, p], slot, p).start()
            return 0

        lax.fori_loop(0, n_pages, body, 0, unroll=8)

    @pl.when(b == 0)
    def _():
        gather(0, 0)

    @pl.when(b + 1 < n_seq)
    def _():
        gather(b + 1, (b + 1) % 2)

    slot = b % 2

    def wait_body(p, _):
        page_copy(0, slot, p).wait()
        return 0

    lax.fori_loop(0, n_pages, wait_body, 0, unroll=8)

    qb = qi_ref[0].astype(BF16)
    w = wi_ref[0] * (N_IDX_HEADS ** -0.5)

    ppm = math.gcd(n_pages, IDX_PAGES_PER_MATMUL)

    def score(i, _):
        p0 = pl.multiple_of(i * ppm, ppm)
        keys = buf[slot, pl.ds(p0, ppm)].reshape(ppm * PAGE_SIZE, D_IDX).astype(BF16)
        s = _dot_nt(qb, keys)
        sc = jnp.sum(jnp.maximum(s, 0.0) * w, axis=0, keepdims=True) * (D_IDX ** -0.5)
        for k in range(ppm):
            o_ref[0, pl.ds(p0 + k, 1), :] = sc[:, k * PAGE_SIZE:(k + 1) * PAGE_SIZE]
        return 0

    lax.fori_loop(0, n_pages // ppm, score, 0, unroll=4)


def _idx_sample_call(page_table, qi3, wi3, cache_kidx):
    nb, n_pages = page_table.shape
    grid_spec = pltpu.PrefetchScalarGridSpec(
        num_scalar_prefetch=1,
        grid=(nb,),
        in_specs=[pl.BlockSpec((1, N_IDX_HEADS, D_IDX), lambda b, pt: (b, 0, 0)),
                  pl.BlockSpec((1, N_IDX_HEADS, 1), lambda b, pt: (b, 0, 0)),
                  pl.BlockSpec(memory_space=pl.ANY)],
        out_specs=pl.BlockSpec((1, n_pages, PAGE_SIZE), lambda b, pt: (b, 0, 0)),
        scratch_shapes=[pltpu.VMEM((2, n_pages, PAGE_SIZE, D_IDX), F32), pltpu.SemaphoreType.DMA((2,))],
    )
    return pl.pallas_call(
        _idx_sample_kernel,
        out_shape=jax.ShapeDtypeStruct((nb, n_pages, PAGE_SIZE), F32),
        grid_spec=grid_spec,
        compiler_params=_cparams(("arbitrary",)),
        name="idx_sample",
    )(page_table, qi3, wi3, cache_kidx)


IDX_FLAG = 1 << 20


def _topk_sample_kernel(n_sel, sc_ref, qi_ref, wi_ref, knew_ref, idx_ref, keys_s, keyw_s, rank_s, flag_s, cnt_s):
    nb, n_past = sc_ref.shape
    n_blk = n_past // LANE
    lane = lax.broadcasted_iota(I32, (nb, LANE), 1)

    kn = knew_ref[...].astype(BF16)
    q = qi_ref[...]
    w = wi_ref[...] * (N_IDX_HEADS ** -0.5)
    s_new = jnp.zeros((nb, 1), F32)
    for h in range(N_IDX_HEADS):
        qh = q[:, h * D_IDX:(h + 1) * D_IDX].astype(BF16).astype(F32)
        dot = jnp.sum(qh * kn.astype(F32), axis=1, keepdims=True)
        s_new = s_new + jnp.maximum(dot, 0.0) * w[:, h:h + 1]
    s_new = s_new * (D_IDX ** -0.5)
    key_new = _order_key(s_new)

    def to_keys(j, _):
        keys_s[j] = _order_key(sc_ref[:, pl.ds(pl.multiple_of(j * LANE, LANE), LANE)])
        return 0

    lax.fori_loop(0, n_blk, to_keys, 0, unroll=8)
    keyw_s[...] = _order_key(sc_ref[...])
    wide = min(n_past, 16 * LANE)

    def count(pred):
        acc = jnp.zeros((nb, LANE), I32)
        for c in range(n_past // wide):
            hit = jnp.where(pred(keyw_s[:, c * wide:(c + 1) * wide]), 1, 0)
            for k in range(wide // LANE):
                acc = acc + hit[:, k * LANE:(k + 1) * LANE]
        return jnp.sum(acc, axis=1, keepdims=True)

    def count_ge(cand):
        return count(lambda k: k >= cand) + jnp.where(key_new >= cand, 1, 0)

    tau = jnp.maximum(_kth_largest_key(count_ge, (nb, 1), n_sel), INT_MIN + 1)

    quota = n_sel - (count(lambda k: k > tau) + jnp.where(key_new > tau, 1, 0))

    ri = lax.broadcasted_iota(I32, (LANE, LANE), 0)
    ci = lax.broadcasted_iota(I32, (LANE, LANE), 1)
    upper = (ri <= ci).astype(BF16)

    def excl_prefix(flag):
        return (_dot(flag.astype(BF16), upper) - flag).astype(I32)

    def rank_body(j, carry):
        base, base_eq = carry
        kj = keys_s[j]
        eq = kj == tau
        eqf = jnp.where(eq, 1.0, 0.0)
        take = (kj > tau) | (eq & (base_eq + excl_prefix(eqf) < quota))
        takef = jnp.where(take, 1.0, 0.0)
        n_take = jnp.sum(takef, axis=1, keepdims=True)
        rows = pl.ds(pl.multiple_of(j * nb, nb), nb)
        rank_s[rows, :] = jnp.where(take, (base + excl_prefix(takef) + 1).astype(F32), 0.0)
        flag_s[rows, :] = jnp.where((kj < KEY_POS_INF) & (kj > KEY_NEG_INF), 0.0, 1.0)
        cnt_s[rows, :] = jnp.broadcast_to(n_take, (nb, LANE))
        return (base + n_take.astype(I32), base_eq + jnp.sum(eqf, axis=1, keepdims=True).astype(I32))

    zero = jnp.zeros((nb, 1), I32)
    n_before_new, eq_before_new = lax.fori_loop(0, n_blk, rank_body, (zero, zero), unroll=4)

    new_sel = (key_new > tau) | ((key_new == tau) & (eq_before_new < quota))
    new_fin = (key_new < KEY_POS_INF) & (key_new > KEY_NEG_INF)
    new_pos = jnp.where(new_fin, n_past, n_past | IDX_FLAG)
    bi = lax.broadcasted_iota(I32, (n_blk, n_blk), 0)
    bj = lax.broadcasted_iota(I32, (n_blk, n_blk), 1)
    before = (bj < bi).astype(BF16)
    blk_id = lax.broadcasted_iota(I32, (n_blk, LANE), 0).astype(F32)
    slot_lane = lax.broadcasted_iota(I32, (n_blk, n_sel), 1).astype(F32)
    slot_row = lax.broadcasted_iota(I32, (n_sel, LANE), 0)
    lane_sel = lax.broadcasted_iota(I32, (n_sel, LANE), 1)
    reps = n_sel // LANE

    def per_seq(b, _):
        rows = pl.ds(b, n_blk, stride=nb)
        cnt = cnt_s[rows, :]
        first = _dot(before, cnt.astype(BF16))
        first_w = jnp.concatenate([first] * reps, axis=1)
        cnt_w = jnp.concatenate([cnt] * reps, axis=1)
        onehot = jnp.where((slot_lane >= first_w) & (slot_lane < first_w + cnt_w), 1.0, 0.0).astype(BF16)
        picked = _dot_tn(onehot, jnp.concatenate([rank_s[rows, :], flag_s[rows, :], blk_id], axis=1).astype(BF16))
        hit = picked[:, :LANE] == (slot_row + 1).astype(F32)
        pos = (picked[:, 2 * LANE:] * LANE + picked[:, LANE:2 * LANE] * IDX_FLAG).astype(I32) + lane_sel
        col = jnp.sum(jnp.where(hit, pos, 0), axis=1, keepdims=True)
        rb = lax.broadcasted_iota(I32, (nb, 1), 0) == b
        nbn = jnp.sum(jnp.where(rb, n_before_new, 0), axis=0, keepdims=True)
        nsel = jnp.sum(jnp.where(rb & new_sel, 1, 0), axis=0, keepdims=True)
        npos = jnp.sum(jnp.where(rb, new_pos, 0), axis=0, keepdims=True)
        col = col + jnp.where((slot_row[:, 0:1] == nbn) & (nsel > 0), npos, 0)
        idx_ref[b] = col
        return 0

    lax.fori_loop(0, nb, per_seq, 0)


def _attn_sample_kernel(n_past, idx_sm, pt_sm, idxc_ref, q_ref, kn_ref, vn_ref, rb_ref, ck_hbm, cv_hbm, o_ref,
                        kbuf, vbuf, sem):
    b = pl.program_id(0)
    n_seq = pl.num_programs(0)
    n_sel = kbuf.shape[1] // N_KV_B
    hd = HEAD_DIM
    page_shift = int(math.log2(PAGE_SIZE))

    def row_copies(buf, r, page, slot):
        dst = pl.ds(N_KV_B * r, N_KV_B)
        return (pltpu.make_async_copy(ck_hbm.at[page, slot], kbuf.at[buf, dst, :], sem.at[0, buf]),
                pltpu.make_async_copy(cv_hbm.at[page, slot], vbuf.at[buf, dst, :], sem.at[1, buf]))

    def gather(seq, buf):
        def start(r, _):
            p = jnp.minimum(idx_sm[seq, r] & (IDX_FLAG - 1), n_past - 1)
            ck, cv = row_copies(buf, r, pt_sm[seq, p >> page_shift], p & (PAGE_SIZE - 1))
            ck.start()
            cv.start()
            return 0

        lax.fori_loop(0, n_sel, start, 0, unroll=8)

    @pl.when(b == 0)
    def _():
        gather(0, 0)

    @pl.when(b + 1 < n_seq)
    def _():
        gather(b + 1, (b + 1) % 2)

    cur = b % 2

    def wait(r, _):
        ck, cv = row_copies(cur, r, 0, 0)
        ck.wait()
        cv.wait()
        return 0

    lax.fori_loop(0, n_sel, wait, 0, unroll=8)

    idxc = idxc_ref[0]
    valid = idxc < IDX_FLAG
    idx = idxc & (IDX_FLAG - 1)
    is_new = idx >= n_past
    bucket = _t5_bucket(n_past - idx)
    bias = jnp.zeros((n_sel, LANE), F32)
    for bkt in range(N_BUCKETS):
        bias = jnp.where(bucket == bkt, rb_ref[bkt:bkt + 1, :], bias)
    q_pad = jnp.concatenate([q_ref[0], jnp.zeros((LANE - N_HEADS_B, hd), F32)], axis=0).astype(BF16)
    lane = lax.broadcasted_iota(I32, (n_sel, LANE), 1)
    lg = jnp.zeros((n_sel, LANE), F32)
    vals = []
    for n in range(N_KV_B):
        rows_n = pl.ds(n, n_sel, stride=N_KV_B)
        kn = jnp.where(is_new, kn_ref[0, :, n * hd:(n + 1) * hd], kbuf[cur, rows_n, :]).astype(BF16)
        vals.append(jnp.where(is_new, vn_ref[0, :, n * hd:(n + 1) * hd], vbuf[cur, rows_n, :]))
        lg = jnp.where(lane // GQA_GROUP == n, _dot_nt(kn, q_pad), lg)
    lg = jnp.where(valid, lg * (hd ** -0.5) + bias, NEG_BIG)
    m = jnp.max(lg, axis=0, keepdims=True)
    p = jnp.where(valid, jnp.exp(lg - m), 0.0)
    p = p / jnp.sum(p, axis=0, keepdims=True)
    for h in range(N_HEADS_B):
        o_h = jnp.sum(p[:, h:h + 1] * vals[h // GQA_GROUP], axis=0, keepdims=True)
        o_ref[0, :, h * hd:(h + 1) * hd] = o_h.astype(o_ref.dtype)


def _attn_sample_call(idx3, page_table, q3, k_new, v_new, rel_bias, cache_k, cache_v):
    nb, n_sel, _ = idx3.shape
    n_past = page_table.shape[1] * PAGE_SIZE
    kvw = N_KV_B * HEAD_DIM
    qbw = N_HEADS_B * HEAD_DIM
    grid_spec = pltpu.PrefetchScalarGridSpec(
        num_scalar_prefetch=2,
        grid=(nb,),
        in_specs=[pl.BlockSpec((1, n_sel, 1), lambda b, *_: (b, 0, 0)),
                  pl.BlockSpec((1, N_HEADS_B, HEAD_DIM), lambda b, *_: (b, 0, 0)),
                  pl.BlockSpec((1, 1, kvw), lambda b, *_: (b, 0, 0)),
                  pl.BlockSpec((1, 1, kvw), lambda b, *_: (b, 0, 0)),
                  pl.BlockSpec((N_BUCKETS, LANE), lambda b, *_: (0, 0)),
                  pl.BlockSpec(memory_space=pl.ANY),
                  pl.BlockSpec(memory_space=pl.ANY)],
        out_specs=pl.BlockSpec((1, 1, qbw), lambda b, *_: (b, 0, 0)),
        scratch_shapes=[pltpu.VMEM((2, n_sel * N_KV_B, HEAD_DIM), F32), pltpu.VMEM((2, n_sel * N_KV_B, HEAD_DIM), F32),
                        pltpu.SemaphoreType.DMA((2, 2))],
    )
    return pl.pallas_call(
        functools.partial(_attn_sample_kernel, n_past),
        out_shape=jax.ShapeDtypeStruct((nb, 1, qbw), BF16),
        grid_spec=grid_spec,
        compiler_params=_cparams(("arbitrary",)),
        name="attn_sample",
    )(idx3.reshape(nb, n_sel), page_table, idx3, q3, k_new.reshape(nb, 1, kvw), v_new.reshape(nb, 1, kvw),
      jnp.pad(rel_bias.astype(F32), ((0, 0), (0, LANE - N_HEADS_B))), cache_k, cache_v)


def _topk_sample_call(scores2, qi2, wi2, ki_new, n_sel):
    nb, n_past = scores2.shape
    n_blk = n_past // LANE
    return pl.pallas_call(
        functools.partial(_topk_sample_kernel, n_sel),
        out_shape=jax.ShapeDtypeStruct((nb, n_sel, 1), I32),
        scratch_shapes=[pltpu.VMEM((n_blk, nb, LANE), I32), pltpu.VMEM((nb, n_past), I32),
                        pltpu.VMEM((n_blk * nb, LANE), F32), pltpu.VMEM((n_blk * nb, LANE), F32),
                        pltpu.VMEM((n_blk * nb, LANE), F32)],
        compiler_params=pltpu.CompilerParams(vmem_limit_bytes=VMEM_LIMIT),
        name="topk_sample",
    )(scores2, qi2, wi2, ki_new)


WPREP_COLS = 256


def _wprep_plan():
    src = [0]
    for s in SPLIT_SIZES:
        src.append(src[-1] + s)
    conv, a_raw, b_raw, z, q_b, k_b, v_b, q_i, w_i, k_i = src[:10]
    starts = {OFF_CONV: conv, OFF_Z: z, OFF_QI: q_i, OFF_QB: q_b, OFF_KB: k_b, OFF_VB: v_b}
    sizes = {OFF_CONV: CONV_DIM, OFF_Z: VAL_DIM_A, OFF_QI: N_IDX_HEADS * D_IDX, OFF_QB: N_HEADS_B * HEAD_DIM,
             OFF_KB: N_KV_B * HEAD_DIM, OFF_VB: N_KV_B * HEAD_DIM}
    offs = []
    for off, start in starts.items():
        assert off % WPREP_COLS == 0 and sizes[off] % WPREP_COLS == 0
        offs += [(off + c, start + c) for c in range(0, sizes[off], WPREP_COLS)]
    offs = [s for _, s in sorted(offs)]
    assert len(offs) * WPREP_COLS == OFF_KI and OFF_KI + D_IDX == OFF_SM and OFF_SM + LANE == OFF_KI + WPREP_COLS
    assert a_raw + N_HEADS_A == b_raw and w_i + N_IDX_HEADS == k_i and k_i + D_IDX == src[10]
    tail_start = src[10] - WPREP_COLS
    offs += [tail_start, 0]
    return offs, a_raw, w_i - tail_start, k_i - tail_start


def _wprep_kernel(offs_ref, wt_ref, ab_ref, o_ref):
    del offs_ref
    i = pl.program_id(0)
    n_blk = pl.num_programs(0)
    _, _, wi_at, ki_at = _wprep_plan()

    @pl.when(i < n_blk - 2)
    def _():
        o_ref[...] = wt_ref[0].T.astype(o_ref.dtype)

    @pl.when(i == n_blk - 2)
    def _():
        t = wt_ref[0].T
        ab = jnp.concatenate([ab_ref[0], jnp.zeros((LANE - ab_ref.shape[1], ab_ref.shape[2]), F32)], axis=0).T
        small = jnp.concatenate([ab[:, :SM_W], t[:, wi_at:wi_at + N_IDX_HEADS],
                                 jnp.zeros((t.shape[0], LANE - SM_W - N_IDX_HEADS), F32)], axis=1)
        o_ref[...] = jnp.concatenate([t[:, ki_at:ki_at + D_IDX], small], axis=1).astype(o_ref.dtype)

    @pl.when(i == n_blk - 1)
    def _():
        o_ref[...] = jnp.zeros(o_ref.shape, o_ref.dtype)


def _prep_w_in(w_all, layer):
    offs, ab_at, _, _ = _wprep_plan()
    wt = jnp.swapaxes(w_all, 1, 2)
    d_in = wt.shape[2]
    n_ab = 2 * N_HEADS_A
    assert ab_at % n_ab == 0
    grid_spec = pltpu.PrefetchScalarGridSpec(
        num_scalar_prefetch=1,
        grid=(NP // WPREP_COLS,),
        in_specs=[pl.BlockSpec((pl.Element(1), pl.Element(WPREP_COLS), pl.Element(d_in)),
                               lambda i, offs_ref: (layer, pl.multiple_of(offs_ref[i], n_ab), 0)),
                  pl.BlockSpec((1, n_ab, d_in), lambda i, offs_ref: (layer, ab_at // n_ab, 0))],
        out_specs=pl.BlockSpec((d_in, WPREP_COLS), lambda i, offs_ref: (0, i)),
    )
    return pl.pallas_call(
        _wprep_kernel,
        out_shape=jax.ShapeDtypeStruct((d_in, NP), BF16),
        grid_spec=grid_spec,
        compiler_params=_cparams(("arbitrary",)),
        name="wprep",
    )(jnp.asarray(offs, I32), wt, wt)


def kernel(x_prompt, x_sample, c_prompt, c_sample, cache_k, cache_v, cache_kidx, state_ssm, state_conv, page_table,
           w_ada, b_ada, pre1_g, post1_g, pre2_g, post2_g, w_in, w_out, conv_w, a_log, dt_bias, gdn_norm_g,
           idx_knorm_g, idx_knorm_b, rel_bias, w_ff1, w_ff2):
    bsz, seq, _ = x_prompt.shape
    nb, dec_seq, _ = x_sample.shape
    assert dec_seq == 1, "the sample path handles one new token per sequence"
    depth = w_in.shape[0]
    n_past = page_table.shape[1] * PAGE_SIZE
    n_sel_s = min(TOPK_MAX, (n_past + dec_seq) // 4)
    kvw = N_KV_B * HEAD_DIM
    tm_in, tn_in = min(1024, seq), 1024
    tm_out = min(512, seq)
    tf = 1024

    hp = x_prompt.reshape(bsz * seq, D_MODEL)
    hs = x_sample.reshape(nb, D_MODEL)
    c_all = jnp.concatenate([c_sample, c_prompt], axis=0)
    new_p, new_s = [], []
    for l in range(depth):
        m = _adaln_call(c_all, w_ada[l], b_ada[l])
        w_in_bf = _prep_w_in(w_in, l)
        w_out_bf = w_out[l].astype(BF16)
        w1_bf = w_ff1[l].astype(BF16)
        w2_bf = w_ff2[l].astype(BF16)

        proj, conv_tail = _inproj_call(hp, pre1_g[l], _ModSpec(m, False, nb, seq, tm_in), w_in_bf, tm_in, tn_in,
                                       conv_w[l], seq)
        k_p, v_p, ki_p, k_bf, va_bf, ki_bf = _kvprep_call(proj, idx_knorm_g[l], idx_knorm_b[l], tm_in, True)
        proj3 = proj.reshape(bsz, seq, NP)
        o_a, ssm_p = _gdn_prompt_call(proj3, a_log[l], dt_bias[l], gdn_norm_g[l])
        o_b = _attn_prompt_call(proj3, ki_bf.reshape(bsz, seq, D_IDX), k_bf.reshape(bsz, seq, kvw),
                                va_bf.reshape(bsz, seq, 2 * kvw), rel_bias)
        mod_p = _ModSpec(m, False, nb, seq, tm_out)
        x1, h2 = _outproj_call(o_a.reshape(bsz * seq, VAL_DIM_A), o_b.reshape(bsz * seq, N_HEADS_B * HEAD_DIM), hp,
                               w_out_bf, post1_g[l], pre2_g[l], mod_p, tm_out)
        hp = _ffn_call(h2, x1, w1_bf, w2_bf, post2_g[l], mod_p, tm_out, tf)
        conv_p = conv_tail.reshape(bsz, seq // tm_in, CONV_TAIL, CONV_DIM)[:, -1, CONV_TAIL - (CONV_W - 1):, :]
        new_p.append((k_p.reshape(bsz, seq, N_KV_B, HEAD_DIM), v_p.reshape(bsz, seq, N_KV_B, HEAD_DIM),
                      ki_p.reshape(bsz, seq, D_IDX), ssm_p, conv_p))

        mod_s = _ModSpec(m, True, 0, 1, nb)
        proj_s = _inproj_call(hs, pre1_g[l], mod_s, w_in_bf, nb, tn_in)
        k_s, v_s, ki_s = _kvprep_call(proj_s, idx_knorm_g[l], idx_knorm_b[l], nb)
        o_a_s, ssm_s, conv_s = _gdn_sample_call(proj_s.reshape(nb, 1, NP), state_conv[l], state_ssm[l], conv_w[l],
                                                a_log[l], dt_bias[l], gdn_norm_g[l])
        qi2 = proj_s[:, OFF_QI:OFF_QI + N_IDX_HEADS * D_IDX]
        wi2 = proj_s[:, OFF_SM + SM_W:OFF_SM + SM_W + N_IDX_HEADS]
        scores = _idx_sample_call(page_table, qi2.reshape(nb, N_IDX_HEADS, D_IDX), wi2.reshape(nb, N_IDX_HEADS, 1),
                                  cache_kidx[l])
        idx3 = _topk_sample_call(scores.reshape(nb, n_past), qi2, wi2, ki_s, n_sel_s)
        q3 = proj_s[:, OFF_QB:OFF_QB + N_HEADS_B * HEAD_DIM].reshape(nb, N_HEADS_B, HEAD_DIM)
        o_b_s = _attn_sample_call(idx3, page_table, q3, k_s, v_s, rel_bias, cache_k[l], cache_v[l])
        x1s, h2s = _outproj_call(o_a_s.reshape(nb, VAL_DIM_A), o_b_s.reshape(nb, N_HEADS_B * HEAD_DIM), hs,
                                 w_out_bf, post1_g[l], pre2_g[l], mod_s, nb)
        hs = _ffn_call(h2s, x1s, w1_bf, w2_bf, post2_g[l], mod_s, nb, tf)
        new_s.append((k_s.reshape(nb, 1, N_KV_B, HEAD_DIM), v_s.reshape(nb, 1, N_KV_B, HEAD_DIM),
                      ki_s.reshape(nb, 1, D_IDX), ssm_s, conv_s))

    p_k, p_v, p_kidx, p_ssm, p_conv = [jnp.stack([st[i] for st in new_p]) for i in range(5)]
    s_k, s_v, s_kidx, s_ssm, s_conv = [jnp.stack([st[i] for st in new_s]) for i in range(5)]
    return (hp.reshape(bsz, seq, D_MODEL), hs.reshape(nb, 1, D_MODEL), p_k, p_v, p_kidx, p_ssm, p_conv,
            s_k, s_v, s_kidx, s_ssm, s_conv)
```

```python
import functools
import math

import jax
import jax.numpy as jnp
from jax import lax
from jax.experimental import pallas as pl
from jax.experimental.pallas import tpu as pltpu

F32 = jnp.float32
BF16 = jnp.bfloat16
I32 = jnp.int32

D_MODEL = 2048
PAGE_SIZE = 128
HEAD_DIM = 128
N_HEADS_A = 8
KEY_DIM_A = N_HEADS_A * HEAD_DIM
VAL_DIM_A = N_HEADS_A * HEAD_DIM
CONV_DIM = 2 * KEY_DIM_A + VAL_DIM_A
CONV_W = 4
N_HEADS_B = 8
N_KV_B = 2
GQA_GROUP = N_HEADS_B // N_KV_B
N_IDX_HEADS = 16
D_IDX = 128
TOPK_MAX = 256
N_BUCKETS = 32
MAX_DISTANCE = 128
D_FF = 4 * D_MODEL
EPS = 1e-6

SPLIT_SIZES = (CONV_DIM, N_HEADS_A, N_HEADS_A, VAL_DIM_A, N_HEADS_B * HEAD_DIM, N_KV_B * HEAD_DIM,
               N_KV_B * HEAD_DIM, N_IDX_HEADS * D_IDX, N_IDX_HEADS, D_IDX)

OFF_CONV = 0
OFF_Z = OFF_CONV + CONV_DIM
OFF_QI = OFF_Z + VAL_DIM_A
OFF_QB = OFF_QI + N_IDX_HEADS * D_IDX
OFF_KB = OFF_QB + N_HEADS_B * HEAD_DIM
OFF_VB = OFF_KB + N_KV_B * HEAD_DIM
OFF_KI = OFF_VB + N_KV_B * HEAD_DIM
OFF_SM = OFF_KI + D_IDX
NP = 8192
SM_A, SM_B, SM_W = 0, N_HEADS_A, 2 * N_HEADS_A

LANE = 128
VMEM_LIMIT = 56 * 1024 * 1024
INT_MIN = -(2 ** 31)
NEG_BIG = -1e30

GDN_CHUNK = 128
GDN_GROUP = 256
ATT_TQ = 256
ATT_TK = 256


def _cparams(sem):
    return pltpu.CompilerParams(dimension_semantics=sem, vmem_limit_bytes=VMEM_LIMIT)


def _sigmoid(x):
    return 1.0 / (1.0 + jnp.exp(-x))


def _softplus(x):
    return jnp.maximum(x, 0.0) + jnp.log(1.0 + jnp.exp(-jnp.abs(x)))


def _dot(a, b):
    return jnp.dot(a, b, preferred_element_type=F32)


def _dot_nt(a, b):
    return lax.dot_general(a, b, (((1,), (1,)), ((), ())), preferred_element_type=F32)


def _dot_tn(a, b):
    return lax.dot_general(a, b, (((0,), (0,)), ((), ())), preferred_element_type=F32)


def _mod(ref):
    v = ref[...]
    return v[0] if v.ndim == 3 else v


def _adaln_kernel(c_ref, w_ref, b_ref, o_ref):
    c = c_ref[...]
    s = (c * _sigmoid(c)).astype(BF16)
    o_ref[...] = _dot(s, w_ref[...].astype(BF16)) + b_ref[...]


def _adaln_call(c_all, w_ada, b_ada):
    n_rows = c_all.shape[0]
    n_out = w_ada.shape[1]
    tn = 1024
    return pl.pallas_call(
        _adaln_kernel,
        out_shape=jax.ShapeDtypeStruct((n_rows, n_out), F32),
        grid=(n_out // tn,),
        in_specs=[pl.BlockSpec((n_rows, D_MODEL), lambda j: (0, 0)),
                  pl.BlockSpec((D_MODEL, tn), lambda j: (0, j)),
                  pl.BlockSpec((1, tn), lambda j: (0, j))],
        out_specs=pl.BlockSpec((n_rows, tn), lambda j: (0, j)),
        compiler_params=_cparams(("arbitrary",)),
        name="adaln",
    )(c_all, w_ada, b_ada.reshape(1, n_out))


class _ModSpec:
    def __init__(self, m2, per_row, row_off, rows_per_batch, tm):
        self.per_row = per_row
        self.arr = m2 if per_row else m2.reshape(m2.shape[0], 1, m2.shape[1])
        self.row_off = row_off
        self.blocks_per_batch = None if per_row else rows_per_batch // tm
        self.tm = tm

    def spec(self, k, grid_rank):
        if self.per_row:
            if grid_rank == 1:
                return pl.BlockSpec((self.tm, D_MODEL), lambda i: (0, k))
            return pl.BlockSpec((self.tm, D_MODEL), lambda i, j: (0, k))
        bpb, off = self.blocks_per_batch, self.row_off
        if grid_rank == 1:
            return pl.BlockSpec((1, 1, D_MODEL), lambda i: (off + i // bpb, 0, k))
        return pl.BlockSpec((1, 1, D_MODEL), lambda i, j: (off + i // bpb, 0, k))


CONV_TAIL = 8


def _l2norm(x):
    return x * lax.rsqrt(jnp.sum(x * x, axis=-1, keepdims=True) + EPS)


def _inproj_kernel(blocks_per_seq, x_ref, g_ref, sh_ref, sc_ref, w_ref, *rest):
    j = pl.program_id(1)
    if blocks_per_seq is None:
        o_ref, h_scr = rest
    else:
        cw_ref, o_ref, tail_ref, h_scr, carry_s = rest

    @pl.when(j == 0)
    def _():
        x = x_ref[...]
        y = x * lax.rsqrt(jnp.mean(x * x, axis=-1, keepdims=True) + EPS) * g_ref[...]
        h_scr[...] = (y * (1.0 + _mod(sc_ref)) + _mod(sh_ref)).astype(BF16)

    if blocks_per_seq is None:
        o_ref[...] = _dot(h_scr[...], w_ref[...])
        return

    n_conv = CONV_DIM // o_ref.shape[1]

    @pl.when(j >= n_conv)
    def _():
        o_ref[...] = _dot(h_scr[...], w_ref[...])

    for blk in range(n_conv):
        @pl.when(j == blk)
        def _(blk=blk):
            tm = o_ref.shape[0]
            cw = cw_ref[...]
            first = pl.program_id(0) % blocks_per_seq == 0
            prev = jnp.where(first, 0.0, carry_s[blk])

            def taps(xx):
                y = xx * cw[CONV_W - 1:CONV_W]
                for s in range(1, CONV_W):
                    y = y + pltpu.roll(xx, s, 0) * cw[CONV_W - 1 - s:CONV_W - s]
                return y

            n_piece = 4 if tm % (4 * CONV_TAIL) == 0 else 1
            for r in range(n_piece):
                rows = slice(r * tm // n_piece, (r + 1) * tm // n_piece)
                raw = _dot(h_scr[rows, :], w_ref[...])
                head = taps(jnp.concatenate([prev, raw[:CONV_TAIL]], axis=0))[CONV_TAIL:]
                y = jnp.concatenate([head, taps(raw)[CONV_TAIL:]], axis=0)
                prev = raw[raw.shape[0] - CONV_TAIL:]
                y = y * _sigmoid(y)
                if blk == 2:
                    o_ref[rows, :] = y
                else:
                    for h in range(o_ref.shape[1] // HEAD_DIM):
                        yh = _l2norm(y[:, h * HEAD_DIM:(h + 1) * HEAD_DIM])
                        o_ref[rows, h * HEAD_DIM:(h + 1) * HEAD_DIM] = yh * (HEAD_DIM ** -0.5) if blk == 0 else yh
            carry_s[blk] = prev
            tail_ref[0] = prev


def _inproj_call(x2d, g, mod, w_bf, tm, tn, conv_w=None, seq=None):
    m_rows = x2d.shape[0]
    n_cols = w_bf.shape[1]
    in_specs = [pl.BlockSpec((tm, D_MODEL), lambda i, j: (i, 0)),
                pl.BlockSpec((1, D_MODEL), lambda i, j: (0, 0)),
                mod.spec(0, 2), mod.spec(1, 2),
                pl.BlockSpec((D_MODEL, tn), lambda i, j: (0, j))]
    args = [x2d, g.reshape(1, D_MODEL), mod.arr, mod.arr, w_bf]
    out_shape = jax.ShapeDtypeStruct((m_rows, n_cols), F32)
    out_specs = pl.BlockSpec((tm, tn), lambda i, j: (i, j))
    scratch = [pltpu.VMEM((tm, D_MODEL), BF16)]
    bps = None
    if conv_w is not None:
        assert tn == KEY_DIM_A == VAL_DIM_A and OFF_CONV == 0, "q, k, v must each be one leading column block"
        bps = seq // tm
        n_conv = CONV_DIM // tn
        last = n_conv - 1
        in_specs.append(pl.BlockSpec((CONV_W, tn), lambda i, j: (0, jnp.minimum(j, last))))
        args.append(conv_w)
        out_shape = (out_shape, jax.ShapeDtypeStruct((m_rows // tm, CONV_TAIL, CONV_DIM), F32))
        out_specs = (out_specs, pl.BlockSpec((1, CONV_TAIL, tn), lambda i, j: (i, 0, jnp.minimum(j, last))))
        scratch.append(pltpu.VMEM((n_conv, CONV_TAIL, tn), F32))
    return pl.pallas_call(
        functools.partial(_inproj_kernel, bps),
        out_shape=out_shape,
        grid=(m_rows // tm, n_cols // tn),
        in_specs=in_specs,
        out_specs=out_specs,
        scratch_shapes=scratch,
        compiler_params=_cparams(("arbitrary", "arbitrary")),
        name="inproj",
    )(*args)


def _kvprep_kernel(kb_ref, vb_ref, ki_ref, lg_ref, lb_ref, k_ref, v_ref, kidx_ref, *bf_refs):
    kb = kb_ref[...]
    vb = vb_ref[...]
    for n in range(N_KV_B):
        k_ref[:, n, :] = kb[:, n * HEAD_DIM:(n + 1) * HEAD_DIM]
        v_ref[:, n, :] = vb[:, n * HEAD_DIM:(n + 1) * HEAD_DIM]
    x = ki_ref[...]
    mu = jnp.mean(x, axis=-1, keepdims=True)
    xc = x - mu
    var = jnp.mean(xc * xc, axis=-1, keepdims=True)
    ki = xc * lax.rsqrt(var + EPS) * lg_ref[...] + lb_ref[...]
    kidx_ref[...] = ki
    if bf_refs:
        kbf_ref, va_ref, kibf_ref = bf_refs
        kbf_ref[...] = kb.astype(BF16)
        kibf_ref[...] = ki.astype(BF16)
        hd = HEAD_DIM
        ones_col = jnp.where(lax.broadcasted_iota(I32, (vb.shape[0], hd), 1) == 0, 1.0, 0.0)
        va_ref[...] = jnp.concatenate([piece for n in range(N_KV_B) for piece in (vb[:, n * hd:(n + 1) * hd], ones_col)],
                                      axis=1).astype(BF16)


def _kvprep_call(proj, ln_g, ln_b, tm, with_bf16=False):
    m_rows = proj.shape[0]
    kvw = N_KV_B * HEAD_DIM
    row = lambda i: (i, 0)
    kv_shape = jax.ShapeDtypeStruct((m_rows, N_KV_B, HEAD_DIM), F32)
    kv_spec = pl.BlockSpec((tm, N_KV_B, HEAD_DIM), lambda i: (i, 0, 0))
    out_shape = [kv_shape, kv_shape, jax.ShapeDtypeStruct((m_rows, D_IDX), F32)]
    out_specs = [kv_spec, kv_spec, pl.BlockSpec((tm, D_IDX), row)]
    if with_bf16:
        out_shape += [jax.ShapeDtypeStruct((m_rows, kvw), BF16), jax.ShapeDtypeStruct((m_rows, 2 * kvw), BF16),
                      jax.ShapeDtypeStruct((m_rows, D_IDX), BF16)]
        out_specs += [pl.BlockSpec((tm, kvw), row), pl.BlockSpec((tm, 2 * kvw), row), pl.BlockSpec((tm, D_IDX), row)]
    return pl.pallas_call(
        _kvprep_kernel,
        out_shape=tuple(out_shape),
        grid=(m_rows // tm,),
        in_specs=[pl.BlockSpec((tm, kvw), lambda i: (i, OFF_KB // kvw)),
                  pl.BlockSpec((tm, kvw), lambda i: (i, OFF_VB // kvw)),
                  pl.BlockSpec((tm, D_IDX), lambda i: (i, OFF_KI // D_IDX)),
                  pl.BlockSpec((1, D_IDX), lambda i: (0, 0)),
                  pl.BlockSpec((1, D_IDX), lambda i: (0, 0))],
        out_specs=tuple(out_specs),
        compiler_params=_cparams(("arbitrary",)),
        name="kvprep",
    )(proj, proj, proj, ln_g.reshape(1, D_IDX), ln_b.reshape(1, D_IDX))


def _gdn_prompt_kernel(q_ref, k_ref, v_ref, z_ref, sm_ref, alog_ref, dtb_ref, ng_ref,
                       o_ref, s_out_ref,
                       gc_s, bt_s, l_s, a_s, x_s, u_s, w_s, mn_s, pq_s, sts_s):
    seq = q_ref.shape[1]
    hd = HEAD_DIM
    grp = GDN_GROUP
    chunk = GDN_CHUNK
    n_grp = seq // grp
    n_chunk = seq // chunk
    cpg = grp // chunk
    h = pl.program_id(1)

    row = lax.broadcasted_iota(I32, (seq, hd), 0)
    lane = lax.broadcasted_iota(I32, (seq, hd), 1)

    qn_s, kn_s, vn_s = q_ref.at[0], k_ref.at[0], v_ref.at[0]

    sm = sm_ref[0]
    g_all = -jnp.exp(alog_ref[...]) * _softplus(sm + dtb_ref[...])
    b_all = _sigmoid(sm)
    g_col = jnp.sum(jnp.where(lane == h + SM_A, g_all, 0.0), axis=1, keepdims=True)
    b_col = jnp.sum(jnp.where(lane == h + SM_B, b_all, 0.0), axis=1, keepdims=True)
    gc = jnp.broadcast_to(g_col, (seq, hd))
    pos = row & (chunk - 1)
    s = 1
    while s < chunk:
        gc = gc + jnp.where(pos >= s, pltpu.roll(gc, s, 0), 0.0)
        s *= 2
    gc_s[...] = gc
    bt_s[...] = jnp.broadcast_to(b_col, (seq, hd))

    ii = lax.broadcasted_iota(I32, (grp, grp), 0)
    jj = lax.broadcasted_iota(I32, (grp, grp), 1)
    xr = ii ^ jj
    same = (xr >> int(math.log2(chunk))) == 0
    causal = same & (ii >= jj)
    strict = same & (ii > jj)
    eye = (ii == jj).astype(F32)

    def build(gi, _):
        r0 = pl.multiple_of(gi * grp, grp)
        kk = kn_s[pl.ds(r0, grp), :]
        qq = qn_s[pl.ds(r0, grp), :]
        bt = bt_s[pl.ds(r0, grp), :]
        gcg = gc_s[pl.ds(r0, grp), :]
        kb = (kk * bt).astype(BF16)
        kkb = kk.astype(BF16)
        kkt = _dot_nt(kb, kkb)
        qkt = _dot_nt(qq.astype(BF16), kkb)
        colb = jnp.concatenate([gcg, gcg], axis=1)
        diff = colb - colb.T
        decay = jnp.where(causal, jnp.exp(jnp.where(causal, diff, 0.0)), 0.0)
        lmat = jnp.where(strict, kkt * decay, 0.0)
        l_s[gi] = lmat
        a_s[pl.ds(r0, grp), :] = jnp.where(causal, qkt * decay, 0.0)
        x_s[gi] = eye - jnp.where(xr == 1, lmat, 0.0)
        return 0

    lax.fori_loop(0, n_grp, build, 0, unroll=True)

    for lvl in range(1, int(math.log2(chunk))):
        def level(gi, _, lvl=lvl):
            xm = x_s[gi]
            cl = jnp.where((xr >> lvl) == 1, l_s[gi], 0.0).astype(BF16)
            xb = xm.astype(BF16)
            x_s[gi] = xm - _dot(_dot(xb, cl).astype(BF16), xb)
            return 0

        lax.fori_loop(0, n_grp, level, 0, unroll=True)

    def apply(gi, _):
        r0 = pl.multiple_of(gi * grp, grp)
        kk = kn_s[pl.ds(r0, grp), :]
        vv = vn_s[pl.ds(r0, grp), :]
        bt = bt_s[pl.ds(r0, grp), :]
        gcg = gc_s[pl.ds(r0, grp), :]
        rhs = jnp.concatenate([vv * bt, kk * bt * jnp.exp(gcg)], axis=1).astype(BF16)
        uw = _dot(x_s[gi].astype(BF16), rhs)
        u_s[pl.ds(r0, grp), :] = uw[:, :hd]
        w_s[pl.ds(r0, grp), :] = uw[:, hd:]
        for cc in range(cpg):
            lo = cc * chunk
            g_last = gcg[lo + chunk - 1:lo + chunk, :]
            k_dec = (kk[lo:lo + chunk] * jnp.exp(g_last - gcg[lo:lo + chunk])).astype(BF16)
            wu = jnp.concatenate([uw[lo:lo + chunk, hd:], uw[lo:lo + chunk, :hd]], axis=1).astype(BF16)
            mn_s[gi * cpg + cc] = _dot_tn(k_dec, wu)
        return 0

    lax.fori_loop(0, n_grp, apply, 0, unroll=True)

    assert n_chunk % 2 == 0
    n_pair = n_chunk // 2

    def decay(c):
        return jnp.exp(gc_s[pl.ds(c * chunk + chunk - 1, 1), :])

    def combine(p, _):
        mn0, mn1 = mn_s[2 * p], mn_s[2 * p + 1]
        a0, a1 = decay(2 * p), decay(2 * p + 1)
        cross = _dot(mn1[:, :hd].astype(BF16), mn0.astype(BF16))
        pq_s[p] = jnp.concatenate([a1 * mn0[:, :hd] + a0 * mn1[:, :hd] - cross[:, :hd],
                                   a1 * mn0[:, hd:] - cross[:, hd:] + mn1[:, hd:]], axis=1)
        return 0

    lax.fori_loop(0, n_pair, combine, 0, unroll=True)

    def scan(p, st):
        sts_s[2 * p] = st
        qr = pq_s[p]
        return st * (decay(2 * p) * decay(2 * p + 1)) - _dot(qr[:, :hd].astype(BF16), st.astype(BF16)) + qr[:, hd:]

    s_out_ref[0, 0] = lax.fori_loop(0, n_pair, scan, jnp.zeros((hd, hd), F32))

    def middle(p, _):
        st = sts_s[2 * p]
        mn0 = mn_s[2 * p]
        sts_s[2 * p + 1] = st * decay(2 * p) - _dot(mn0[:, :hd].astype(BF16), st.astype(BF16)) + mn0[:, hd:]
        return 0

    lax.fori_loop(0, n_pair, middle, 0, unroll=True)

    ng = ng_ref[...]

    def emit(gi, _):
        r0 = pl.multiple_of(gi * grp, grp)
        gcg = gc_s[pl.ds(r0, grp), :]
        qg = qn_s[pl.ds(r0, grp), :] * jnp.exp(gcg)
        wg = w_s[pl.ds(r0, grp), :]
        ws, qs = [], []
        for cc in range(cpg):
            lo = cc * chunk
            stb = sts_s[gi * cpg + cc].astype(BF16)
            both = _dot(jnp.concatenate([wg[lo:lo + chunk], qg[lo:lo + chunk]], axis=0).astype(BF16), stb)
            ws.append(both[:chunk])
            qs.append(both[chunk:])
        v_new = u_s[pl.ds(r0, grp), :] - jnp.concatenate(ws, axis=0)
        o = jnp.concatenate(qs, axis=0) + _dot(a_s[pl.ds(r0, grp), :].astype(BF16), v_new.astype(BF16))
        zc = z_ref[0, pl.ds(r0, grp), :]
        on = o * lax.rsqrt(jnp.mean(o * o, axis=-1, keepdims=True) + EPS) * ng
        o_ref[0, pl.ds(r0, grp), :] = (on * (zc * _sigmoid(zc))).astype(o_ref.dtype)
        return 0

    lax.fori_loop(0, n_grp, emit, 0, unroll=True)


def _lane_vec(v, off):
    return jnp.zeros((1, LANE), F32).at[0, off:off + v.shape[0]].set(v.astype(F32))


def _gdn_prompt_call(proj3, a_log, dt_bias, norm_g):
    bsz, seq, _ = proj3.shape
    hd = HEAD_DIM
    nh = N_HEADS_A
    n_grp = seq // GDN_GROUP
    col = lambda base: (lambda b, h: (b, 0, base // hd + h))
    vec = pl.BlockSpec((1, LANE), lambda b, h: (0, 0))
    return pl.pallas_call(
        _gdn_prompt_kernel,
        out_shape=(jax.ShapeDtypeStruct((bsz, seq, VAL_DIM_A), BF16),
                   jax.ShapeDtypeStruct((bsz, nh, hd, hd), F32)),
        grid=(bsz, nh),
        in_specs=[pl.BlockSpec((1, seq, hd), col(OFF_CONV)),
                  pl.BlockSpec((1, seq, hd), col(OFF_CONV + KEY_DIM_A)),
                  pl.BlockSpec((1, seq, hd), col(OFF_CONV + 2 * KEY_DIM_A)),
                  pl.BlockSpec((1, seq, hd), col(OFF_Z)),
                  pl.BlockSpec((1, seq, LANE), lambda b, h: (b, 0, OFF_SM // LANE)),
                  vec, vec, vec],
        out_specs=(pl.BlockSpec((1, seq, hd), lambda b, h: (b, 0, h)),
                   pl.BlockSpec((1, 1, hd, hd), lambda b, h: (b, h, 0, 0))),
        scratch_shapes=[pltpu.VMEM((seq, hd), F32), pltpu.VMEM((seq, hd), F32),
                        pltpu.VMEM((n_grp, GDN_GROUP, GDN_GROUP), F32),
                        pltpu.VMEM((seq, GDN_GROUP), F32),
                        pltpu.VMEM((n_grp, GDN_GROUP, GDN_GROUP), F32),
                        pltpu.VMEM((seq, hd), F32), pltpu.VMEM((seq, hd), F32),
                        pltpu.VMEM((seq // GDN_CHUNK, hd, 2 * hd), F32),
                        pltpu.VMEM((seq // GDN_CHUNK // 2, hd, 2 * hd), F32),
                        pltpu.VMEM((seq // GDN_CHUNK, hd, hd), F32)],
        compiler_params=_cparams(("arbitrary", "arbitrary")),
        name="gdn_prompt",
    )(proj3, proj3, proj3, proj3, proj3,
      _lane_vec(a_log, SM_A), _lane_vec(dt_bias, SM_A), norm_g.reshape(1, hd).astype(F32))


def _t5_bucket(dist):
    n = jnp.maximum(dist, 0)
    max_exact = N_BUCKETS // 2
    nf = jnp.maximum(n, 1).astype(F32)
    large = max_exact + (jnp.log(nf / max_exact) / math.log(MAX_DISTANCE / max_exact)
                         * (N_BUCKETS - max_exact)).astype(I32)
    large = jnp.minimum(large, N_BUCKETS - 1)
    return jnp.where(n < max_exact, n, large)


def _bias_from_bucket(bucket, rb_ref, head):
    out = jnp.zeros(bucket.shape, F32)
    for b in range(N_BUCKETS):
        out = jnp.where(bucket == b, rb_ref[b, head], out)
    return out


def _order_key(score):
    bits = pltpu.bitcast(score + 0.0, I32)
    return bits ^ ((bits >> 31) & 0x7FFFFFFF)


I16 = jnp.int16
HALF_BIAS = 1 << 15
HALF_ROWS = 16
KEY_POS_INF = 0x7F800000
KEY_NEG_INF = INT_MIN + 0x7FFFFF


def _kth_largest_key(count_ge, shape, n_sel):
    def body(i, tau):
        bit = 31 - i
        cand = jnp.where(i == 0, tau ^ INT_MIN, tau | (1 << bit))
        return jnp.where(count_ge(cand) >= n_sel, cand, tau)

    return lax.fori_loop(0, 32, body, jnp.full(shape, INT_MIN, I32))


def _attn_prompt_kernel(n_sel, qi_ref, sm_ref, qb_ref, ki_ref, k_ref, va_ref, rb_ref, o_ref,
                        qi_s, qb_s, wt_s, keys_s, hi_s, lo_s, tau_s, bias_s, m_s, acc_s):
    tq, tk = ATT_TQ, ATT_TK
    hd = HEAD_DIM
    bi = pl.program_id(0)
    qblk = pl.program_id(1)
    n_kb = (qblk * tq) // tk + 1
    krow = lax.broadcasted_iota(I32, (tk, tq), 0)
    qcol = lax.broadcasted_iota(I32, (tk, tq), 1)

    @pl.when((bi == 0) & (qblk == 0))
    def _():
        for off in range(bias_s.shape[1]):
            bucket = _t5_bucket(off * tk + krow - qcol)
            for h in range(N_HEADS_B):
                bias_s[h, off] = _bias_from_bucket(bucket, rb_ref, h)

    qi_s[...] = qi_ref[0].astype(BF16)
    qb_s[...] = qb_ref[0].astype(BF16)
    wt_s[...] = sm_ref[0].T * (N_IDX_HEADS ** -0.5)
    t_pos = qblk * tq + qcol

    def idx_body(j, _):
        k0 = pl.multiple_of(j * tk, tk)
        kt = ki_ref[0, pl.ds(k0, tk), :]
        acc = jnp.zeros((tk, tq), F32)
        for h in range(N_IDX_HEADS):
            s = _dot_nt(kt, qi_s[:, h * D_IDX:(h + 1) * D_IDX])
            acc = acc + jnp.maximum(s, 0.0) * wt_s[SM_W + h:SM_W + h + 1, :]
        acc = acc * (D_IDX ** -0.5)
        key = jnp.where(k0 + krow <= t_pos, _order_key(acc), INT_MIN)
        keys_s[j] = key
        hi_s[j] = (key >> 16).astype(I16)
        lo_s[j] = ((key & 0xFFFF) - HALF_BIAS).astype(I16)
        return 0

    lax.fori_loop(0, n_kb, idx_body, 0)

    def kth_largest_half(half_s, n_at_start):
        def count_ge(cand):
            def cb(j, c):
                hit = jnp.where(half_s[j] >= cand, jnp.int16(1), jnp.int16(0))
                for r in range(tk // HALF_ROWS):
                    c = c + hit[r * HALF_ROWS:(r + 1) * HALF_ROWS]
                return c

            c = lax.fori_loop(0, n_kb, cb, jnp.zeros((HALF_ROWS, tq), I16))
            return jnp.sum(c.astype(I32), axis=0, keepdims=True)

        def body(i, carry):
            t, n_t = carry
            cand = jnp.where(i == 0, t ^ (-HALF_BIAS), t | (1 << (15 - i)))
            n_cand = count_ge(cand.astype(I16))
            ok = n_cand >= n_sel
            return jnp.where(ok, cand, t), jnp.where(ok, n_cand, n_t)

        return lax.fori_loop(0, 16, body, (jnp.full((1, tq), -HALF_BIAS, I32), n_at_start))

    tau_hi, n_ge_hi = kth_largest_half(hi_s, jnp.zeros((1, tq), I32))
    tau_hi16 = tau_hi.astype(I16)

    def narrow(j, _):
        hi = hi_s[j]
        lo_s[j] = jnp.where(hi > tau_hi16, jnp.int16(HALF_BIAS - 1),
                            jnp.where(hi == tau_hi16, lo_s[j], jnp.int16(-HALF_BIAS)))
        return 0

    lax.fori_loop(0, n_kb, narrow, 0)
    tau_lo, n_ge_tau = kth_largest_half(lo_s, n_ge_hi)
    tau = (tau_hi << 16) | ((tau_lo + HALF_BIAS) & 0xFFFF)
    tau = jnp.maximum(tau, KEY_NEG_INF + 1)

    def count_where(pred):
        def cb(j, c):
            hit = jnp.where(pred(keys_s[j], j * tk + krow), 1, 0)
            return c + jnp.sum(hit.reshape(tk // 8, 8, tq), axis=0)

        c = lax.fori_loop(0, n_kb, cb, jnp.zeros((8, tq), I32))
        return jnp.sum(c, axis=0, keepdims=True)

    @pl.when(jnp.max(n_ge_tau) > n_sel)
    def _():
        quota = n_sel - count_where(lambda k, pos: k > tau)
        pos_bits = (keys_s.shape[0] * tk - 1).bit_length()

        def body(i, last):
            cand = last | (1 << (pos_bits - 1 - i))
            before = count_where(lambda k, pos: (k == tau) & (pos < cand))
            return jnp.where(before < quota, cand, last)

        last_kept = lax.fori_loop(0, pos_bits, body, jnp.zeros((1, tq), I32))

        def demote(j, _):
            k = keys_s[j]
            keys_s[j] = jnp.where((k == tau) & (j * tk + krow > last_kept), tau - 1, k)
            return 0

        lax.fori_loop(0, n_kb, demote, 0)

    scale = hd ** -0.5
    lane_reps = tk // LANE
    tau_rep = jnp.broadcast_to(tau, (LANE, tq)).T
    tau_s[...] = jnp.concatenate([tau_rep] * lane_reps, axis=1)

    def block_inputs(j):
        kq = keys_s[j].T
        sel = jnp.where(kq < KEY_POS_INF, kq, INT_MIN) >= tau_s[...]
        return k_ref[0, pl.ds(pl.multiple_of(j * tk, tk), tk), :], sel

    def masked_logits(j, h, kt, sel):
        n = h // GQA_GROUP
        boff = jnp.minimum(qblk - j, bias_s.shape[1] - 1)
        lg = _dot_nt(qb_s[:, h * hd:(h + 1) * hd], kt[:, n * hd:(n + 1) * hd]) * scale + bias_s[h, boff]
        return jnp.where(sel, lg, NEG_BIG)

    m_s[...] = jnp.full(m_s.shape, NEG_BIG, F32)

    def max_body(j, _):
        kt, sel = block_inputs(j)
        for h in range(N_HEADS_B):
            lg = masked_logits(j, h, kt, sel)
            part = lg[:, :LANE]
            for r in range(1, lane_reps):
                part = jnp.maximum(part, lg[:, r * LANE:(r + 1) * LANE])
            m_s[h] = jnp.maximum(m_s[h], part)
        return 0

    lax.fori_loop(0, n_kb, max_body, 0)
    for h in range(N_HEADS_B):
        m_s[h] = jnp.broadcast_to(jnp.max(m_s[h], axis=1, keepdims=True), (tq, LANE))
    acc_s[...] = jnp.zeros(acc_s.shape, F32)

    def sum_body(j, _):
        kt, sel = block_inputs(j)
        va = va_ref[0, pl.ds(pl.multiple_of(j * tk, tk), tk), :]
        for h in range(N_HEADS_B):
            n = h // GQA_GROUP
            mrow = jnp.concatenate([m_s[h]] * lane_reps, axis=1)
            p = jnp.exp(masked_logits(j, h, kt, sel) - mrow)
            acc_s[h] += _dot(p.astype(BF16), va[:, n * 2 * hd:(n + 1) * 2 * hd])
        return 0

    lax.fori_loop(0, n_kb, sum_body, 0)
    for h in range(N_HEADS_B):
        a = acc_s[h]
        o_ref[0, :, h * hd:(h + 1) * hd] = (a[:, :hd] / a[:, hd:hd + 1]).astype(o_ref.dtype)


def _attn_prompt_call(proj3, ki_bf, k_bf, va_bf, rel_bias):
    bsz, seq, _ = proj3.shape
    tq, tk = ATT_TQ, ATT_TK
    assert tq == tk, "the bias-tile offsets assume square blocks"
    n_sel = min(TOPK_MAX, seq // 4)
    qiw = N_IDX_HEADS * D_IDX
    qbw = N_HEADS_B * HEAD_DIM
    kvw = N_KV_B * HEAD_DIM
    n_off = -(-(MAX_DISTANCE + tk - 1) // tk) + 1
    return pl.pallas_call(
        functools.partial(_attn_prompt_kernel, n_sel),
        out_shape=jax.ShapeDtypeStruct((bsz, seq, qbw), BF16),
        grid=(bsz, seq // tq),
        in_specs=[pl.BlockSpec((1, tq, qiw), lambda b, i: (b, i, OFF_QI // qiw)),
                  pl.BlockSpec((1, tq, LANE), lambda b, i: (b, i, OFF_SM // LANE)),
                  pl.BlockSpec((1, tq, qbw), lambda b, i: (b, i, OFF_QB // qbw)),
                  pl.BlockSpec((1, seq, D_IDX), lambda b, i: (b, 0, 0)),
                  pl.BlockSpec((1, seq, kvw), lambda b, i: (b, 0, 0)),
                  pl.BlockSpec((1, seq, 2 * kvw), lambda b, i: (b, 0, 0)),
                  pl.BlockSpec(memory_space=pltpu.SMEM)],
        out_specs=pl.BlockSpec((1, tq, qbw), lambda b, i: (b, i, 0)),
        scratch_shapes=[pltpu.VMEM((tq, qiw), BF16), pltpu.VMEM((tq, qbw), BF16),
                        pltpu.VMEM((LANE, tq), F32),
                        pltpu.VMEM((seq // tk, tk, tq), I32),
                        pltpu.VMEM((seq // tk, tk, tq), I16), pltpu.VMEM((seq // tk, tk, tq), I16),
                        pltpu.VMEM((tq, tk), I32),
                        pltpu.VMEM((N_HEADS_B, n_off, tq, tk), F32),
                        pltpu.VMEM((N_HEADS_B, tq, LANE), F32),
                        pltpu.VMEM((N_HEADS_B, tq, 2 * HEAD_DIM), F32)],
        compiler_params=_cparams(("arbitrary", "arbitrary")),
        name="attn_prompt",
    )(proj3, proj3, proj3, ki_bf, k_bf, va_bf, rel_bias.astype(F32))


def _rms(x, g):
    return x * lax.rsqrt(jnp.mean(x * x, axis=-1, keepdims=True) + EPS) * g


def _outproj_kernel(oa_ref, ob_ref, x_ref, wa_ref, wb_ref, pg_ref, g1_ref, p2_ref, sh2_ref, sc2_ref, x1_ref, h2_ref):
    rows = x_ref.shape[0]
    per_row = _mod(g1_ref).shape[0] == rows
    n_piece = 2 if rows % 16 == 0 and rows >= 256 else 1
    for r in range(n_piece):
        sl = slice(r * rows // n_piece, (r + 1) * rows // n_piece)
        pick = (lambda ref: _mod(ref)[sl]) if per_row else _mod
        mix = _dot(oa_ref[sl, :], wa_ref[...]) + _dot(ob_ref[sl, :], wb_ref[...])
        x1 = x_ref[sl, :] + pick(g1_ref) * _rms(mix, pg_ref[...])
        x1_ref[sl, :] = x1
        h2_ref[sl, :] = (_rms(x1, p2_ref[...]) * (1.0 + pick(sc2_ref)) + pick(sh2_ref)).astype(BF16)


def _outproj_call(oa, ob, x2d, w_out_bf, post1_g, pre2_g, mod, tm):
    m_rows = x2d.shape[0]
    half = w_out_bf.shape[0] // 2
    row = lambda i: (i, 0)
    vec = pl.BlockSpec((1, D_MODEL), lambda i: (0, 0))
    return pl.pallas_call(
        _outproj_kernel,
        out_shape=(jax.ShapeDtypeStruct((m_rows, D_MODEL), F32), jax.ShapeDtypeStruct((m_rows, D_MODEL), BF16)),
        grid=(m_rows // tm,),
        in_specs=[pl.BlockSpec((tm, half), row), pl.BlockSpec((tm, half), row), pl.BlockSpec((tm, D_MODEL), row),
                  pl.BlockSpec((half, D_MODEL), lambda i: (0, 0)), pl.BlockSpec((half, D_MODEL), lambda i: (1, 0)),
                  vec, mod.spec(2, 1), vec, mod.spec(3, 1), mod.spec(4, 1)],
        out_specs=(pl.BlockSpec((tm, D_MODEL), row), pl.BlockSpec((tm, D_MODEL), row)),
        compiler_params=_cparams(("arbitrary",)),
        name="outproj",
    )(oa, ob, x2d, w_out_bf, w_out_bf, post1_g.reshape(1, D_MODEL), mod.arr, pre2_g.reshape(1, D_MODEL),
      mod.arr, mod.arr)


def _ffn_kernel(h_ref, x1_ref, w1_ref, w2_ref, pg_ref, g2_ref, o_ref, acc_s):
    kk = pl.program_id(1)

    @pl.when(kk == 0)
    def _():
        acc_s[...] = jnp.zeros_like(acc_s)

    a = jnp.maximum(_dot(h_ref[...], w1_ref[...]), 0.0)
    acc_s[...] += _dot((a * a).astype(BF16), w2_ref[...])

    @pl.when(kk == pl.num_programs(1) - 1)
    def _():
        o_ref[...] = x1_ref[...] + _mod(g2_ref) * _rms(acc_s[...], pg_ref[...])


def _ffn_call(h2, x1, w1_bf, w2_bf, post2_g, mod, tm, tf):
    m_rows = x1.shape[0]
    return pl.pallas_call(
        _ffn_kernel,
        out_shape=jax.ShapeDtypeStruct((m_rows, D_MODEL), F32),
        grid=(m_rows // tm, D_FF // tf),
        in_specs=[pl.BlockSpec((tm, D_MODEL), lambda i, k: (i, 0)),
                  pl.BlockSpec((tm, D_MODEL), lambda i, k: (i, 0)),
                  pl.BlockSpec((D_MODEL, tf), lambda i, k: (0, k)),
                  pl.BlockSpec((tf, D_MODEL), lambda i, k: (k, 0)),
                  pl.BlockSpec((1, D_MODEL), lambda i, k: (0, 0)),
                  mod.spec(5, 2)],
        out_specs=pl.BlockSpec((tm, D_MODEL), lambda i, k: (i, 0)),
        scratch_shapes=[pltpu.VMEM((tm, D_MODEL), F32)],
        compiler_params=_cparams(("arbitrary", "arbitrary")),
        name="ffn",
    )(h2, x1, w1_bf, w2_bf, post2_g.reshape(1, D_MODEL), mod.arr)


def _gdn_sample_kernel(cin_ref, z_ref, sm_ref, sconv_ref, ssm_ref, cw_ref, alog_ref, dtb_ref, ng_ref,
                       o_ref, ssm_out_ref, conv_out_ref):
    hd = HEAD_DIM
    nh = N_HEADS_A
    prev = sconv_ref[0]
    xin = cin_ref[0]
    cw = cw_ref[...]
    y = xin * cw[CONV_W - 1:CONV_W]
    for s in range(CONV_W - 1):
        y = y + prev[s:s + 1] * cw[s:s + 1]
    y = y * _sigmoid(y)
    conv_out_ref[0, 0:CONV_W - 2, :] = prev[1:CONV_W - 1]
    conv_out_ref[0, CONV_W - 2:CONV_W - 1, :] = xin

    def heads(base):
        return jnp.concatenate([y[:, base + h * hd:base + (h + 1) * hd] for h in range(nh)], axis=0)

    q8 = _l2norm(heads(0)) * (hd ** -0.5)
    k8 = _l2norm(heads(KEY_DIM_A))
    v8 = heads(2 * KEY_DIM_A)
    sm = sm_ref[0]
    g_all = -jnp.exp(alog_ref[...]) * _softplus(sm + dtb_ref[...])
    b_all = _sigmoid(sm)
    kt = jnp.concatenate([k8, jnp.zeros((LANE - nh, hd), F32)], axis=0).T
    k8b = k8.astype(BF16)
    q8b = q8.astype(BF16)
    z = z_ref[0]
    ng = ng_ref[...]
    for h in range(nh):
        s1 = ssm_ref[0, h] * jnp.exp(g_all[:, SM_A + h:SM_A + h + 1])
        kv = _dot(k8b, s1.astype(BF16))[h:h + 1]
        delta = (v8[h:h + 1] - kv) * b_all[:, SM_B + h:SM_B + h + 1]
        s2 = s1 + kt[:, h:h + 1] * delta
        ssm_out_ref[0, h] = s2
        o = _dot(q8b, s2.astype(BF16))[h:h + 1]
        zh = z[:, h * hd:(h + 1) * hd]
        on = o * lax.rsqrt(jnp.mean(o * o, axis=-1, keepdims=True) + EPS) * ng
        o_ref[0, :, h * hd:(h + 1) * hd] = (on * (zh * _sigmoid(zh))).astype(o_ref.dtype)


def _gdn_sample_call(proj_s3, state_conv, state_ssm, conv_w, a_log, dt_bias, norm_g):
    nb = proj_s3.shape[0]
    hd, nh = HEAD_DIM, N_HEADS_A
    vec = pl.BlockSpec((1, LANE), lambda b: (0, 0))
    return pl.pallas_call(
        _gdn_sample_kernel,
        out_shape=(jax.ShapeDtypeStruct((nb, 1, VAL_DIM_A), BF16),
                   jax.ShapeDtypeStruct((nb, nh, hd, hd), F32),
                   jax.ShapeDtypeStruct((nb, CONV_W - 1, CONV_DIM), F32)),
        grid=(nb,),
        in_specs=[pl.BlockSpec((1, 1, CONV_DIM), lambda b: (b, 0, OFF_CONV // CONV_DIM)),
                  pl.BlockSpec((1, 1, VAL_DIM_A), lambda b: (b, 0, OFF_Z // VAL_DIM_A)),
                  pl.BlockSpec((1, 1, LANE), lambda b: (b, 0, OFF_SM // LANE)),
                  pl.BlockSpec((1, CONV_W - 1, CONV_DIM), lambda b: (b, 0, 0)),
                  pl.BlockSpec((1, nh, hd, hd), lambda b: (b, 0, 0, 0)),
                  pl.BlockSpec((CONV_W, CONV_DIM), lambda b: (0, 0)),
                  vec, vec, vec],
        out_specs=(pl.BlockSpec((1, 1, VAL_DIM_A), lambda b: (b, 0, 0)),
                   pl.BlockSpec((1, nh, hd, hd), lambda b: (b, 0, 0, 0)),
                   pl.BlockSpec((1, CONV_W - 1, CONV_DIM), lambda b: (b, 0, 0))),
        compiler_params=_cparams(("arbitrary",)),
        name="gdn_sample",
    )(proj_s3, proj_s3, proj_s3, state_conv, state_ssm, conv_w,
      _lane_vec(a_log, SM_A), _lane_vec(dt_bias, SM_A), norm_g.reshape(1, hd).astype(F32))


IDX_PAGES_PER_MATMUL = 4


def _idx_sample_kernel(pt_sm, qi_ref, wi_ref, kidx_hbm, o_ref, buf, sem):
    b = pl.program_id(0)
    n_seq = pl.num_programs(0)
    n_pages = buf.shape[1]

    def page_copy(page, slot, p):
        return pltpu.make_async_copy(kidx_hbm.at[page], buf.at[slot, p], sem.at[slot])

    def gather(seq, slot):
        def body(p, _):
            page_copy(pt_sm[seq, p], slot, p).start()
            return 0

        lax.fori_loop(0, n_pages, body, 0, unroll=8)

    @pl.when(b == 0)
    def _():
        gather(0, 0)

    @pl.when(b + 1 < n_seq)
    def _():
        gather(b + 1, (b + 1) % 2)

    slot = b % 2

    def wait_body(p, _):
        page_copy(0, slot, p).wait()
        return 0

    lax.fori_loop(0, n_pages, wait_body, 0, unroll=8)

    qb = qi_ref[0].astype(BF16)
    w = wi_ref[0] * (N_IDX_HEADS ** -0.5)

    ppm = math.gcd(n_pages, IDX_PAGES_PER_MATMUL)

    def score(i, _):
        p0 = pl.multiple_of(i * ppm, ppm)
        keys = buf[slot, pl.ds(p0, ppm)].reshape(ppm * PAGE_SIZE, D_IDX).astype(BF16)
        s = _dot_nt(qb, keys)
        sc = jnp.sum(jnp.maximum(s, 0.0) * w, axis=0, keepdims=True) * (D_IDX ** -0.5)
        for k in range(ppm):
            o_ref[0, pl.ds(p0 + k, 1), :] = sc[:, k * PAGE_SIZE:(k + 1) * PAGE_SIZE]
        return 0

    lax.fori_loop(0, n_pages // ppm, score, 0, unroll=4)


def _idx_sample_call(page_table, qi3, wi3, cache_kidx):
    nb, n_pages = page_table.shape
    grid_spec = pltpu.PrefetchScalarGridSpec(
        num_scalar_prefetch=1,
        grid=(nb,),
        in_specs=[pl.BlockSpec((1, N_IDX_HEADS, D_IDX), lambda b, pt: (b, 0, 0)),
                  pl.BlockSpec((1, N_IDX_HEADS, 1), lambda b, pt: (b, 0, 0)),
                  pl.BlockSpec(memory_space=pl.ANY)],
        out_specs=pl.BlockSpec((1, n_pages, PAGE_SIZE), lambda b, pt: (b, 0, 0)),
        scratch_shapes=[pltpu.VMEM((2, n_pages, PAGE_SIZE, D_IDX), F32), pltpu.SemaphoreType.DMA((2,))],
    )
    return pl.pallas_call(
        _idx_sample_kernel,
        out_shape=jax.ShapeDtypeStruct((nb, n_pages, PAGE_SIZE), F32),
        grid_spec=grid_spec,
        compiler_params=_cparams(("arbitrary",)),
        name="idx_sample",
    )(page_table, qi3, wi3, cache_kidx)


IDX_FLAG = 1 << 20


def _topk_sample_kernel(n_sel, sc_ref, qi_ref, wi_ref, knew_ref, idx_ref, keys_s, keyw_s, rank_s, flag_s, cnt_s):
    nb, n_past = sc_ref.shape
    n_blk = n_past // LANE
    lane = lax.broadcasted_iota(I32, (nb, LANE), 1)

    kn = knew_ref[...].astype(BF16)
    q = qi_ref[...]
    w = wi_ref[...] * (N_IDX_HEADS ** -0.5)
    s_new = jnp.zeros((nb, 1), F32)
    for h in range(N_IDX_HEADS):
        qh = q[:, h * D_IDX:(h + 1) * D_IDX].astype(BF16).astype(F32)
        dot = jnp.sum(qh * kn.astype(F32), axis=1, keepdims=True)
        s_new = s_new + jnp.maximum(dot, 0.0) * w[:, h:h + 1]
    s_new = s_new * (D_IDX ** -0.5)
    key_new = _order_key(s_new)

    def to_keys(j, _):
        keys_s[j] = _order_key(sc_ref[:, pl.ds(pl.multiple_of(j * LANE, LANE), LANE)])
        return 0

    lax.fori_loop(0, n_blk, to_keys, 0, unroll=8)
    keyw_s[...] = _order_key(sc_ref[...])
    wide = min(n_past, 16 * LANE)

    def count(pred):
        acc = jnp.zeros((nb, LANE), I32)
        for c in range(n_past // wide):
            hit = jnp.where(pred(keyw_s[:, c * wide:(c + 1) * wide]), 1, 0)
            for k in range(wide // LANE):
                acc = acc + hit[:, k * LANE:(k + 1) * LANE]
        return jnp.sum(acc, axis=1, keepdims=True)

    def count_ge(cand):
        return count(lambda k: k >= cand) + jnp.where(key_new >= cand, 1, 0)

    tau = jnp.maximum(_kth_largest_key(count_ge, (nb, 1), n_sel), INT_MIN + 1)

    quota = n_sel - (count(lambda k: k > tau) + jnp.where(key_new > tau, 1, 0))

    ri = lax.broadcasted_iota(I32, (LANE, LANE), 0)
    ci = lax.broadcasted_iota(I32, (LANE, LANE), 1)
    upper = (ri <= ci).astype(BF16)

    def excl_prefix(flag):
        return (_dot(flag.astype(BF16), upper) - flag).astype(I32)

    def rank_body(j, carry):
        base, base_eq = carry
        kj = keys_s[j]
        eq = kj == tau
        eqf = jnp.where(eq, 1.0, 0.0)
        take = (kj > tau) | (eq & (base_eq + excl_prefix(eqf) < quota))
        takef = jnp.where(take, 1.0, 0.0)
        n_take = jnp.sum(takef, axis=1, keepdims=True)
        rows = pl.ds(pl.multiple_of(j * nb, nb), nb)
        rank_s[rows, :] = jnp.where(take, (base + excl_prefix(takef) + 1).astype(F32), 0.0)
        flag_s[rows, :] = jnp.where((kj < KEY_POS_INF) & (kj > KEY_NEG_INF), 0.0, 1.0)
        cnt_s[rows, :] = jnp.broadcast_to(n_take, (nb, LANE))
        return (base + n_take.astype(I32), base_eq + jnp.sum(eqf, axis=1, keepdims=True).astype(I32))

    zero = jnp.zeros((nb, 1), I32)
    n_before_new, eq_before_new = lax.fori_loop(0, n_blk, rank_body, (zero, zero), unroll=4)

    new_sel = (key_new > tau) | ((key_new == tau) & (eq_before_new < quota))
    new_fin = (key_new < KEY_POS_INF) & (key_new > KEY_NEG_INF)
    new_pos = jnp.where(new_fin, n_past, n_past | IDX_FLAG)
    bi = lax.broadcasted_iota(I32, (n_blk, n_blk), 0)
    bj = lax.broadcasted_iota(I32, (n_blk, n_blk), 1)
    before = (bj < bi).astype(BF16)
    blk_id = lax.broadcasted_iota(I32, (n_blk, LANE), 0).astype(F32)
    slot_lane = lax.broadcasted_iota(I32, (n_blk, n_sel), 1).astype(F32)
    slot_row = lax.broadcasted_iota(I32, (n_sel, LANE), 0)
    lane_sel = lax.broadcasted_iota(I32, (n_sel, LANE), 1)
    reps = n_sel // LANE

    def per_seq(b, _):
        rows = pl.ds(b, n_blk, stride=nb)
        cnt = cnt_s[rows, :]
        first = _dot(before, cnt.astype(BF16))
        first_w = jnp.concatenate([first] * reps, axis=1)
        cnt_w = jnp.concatenate([cnt] * reps, axis=1)
        onehot = jnp.where((slot_lane >= first_w) & (slot_lane < first_w + cnt_w), 1.0, 0.0).astype(BF16)
        picked = _dot_tn(onehot, jnp.concatenate([rank_s[rows, :], flag_s[rows, :], blk_id], axis=1).astype(BF16))
        hit = picked[:, :LANE] == (slot_row + 1).astype(F32)
        pos = (picked[:, 2 * LANE:] * LANE + picked[:, LANE:2 * LANE] * IDX_FLAG).astype(I32) + lane_sel
        col = jnp.sum(jnp.where(hit, pos, 0), axis=1, keepdims=True)
        rb = lax.broadcasted_iota(I32, (nb, 1), 0) == b
        nbn = jnp.sum(jnp.where(rb, n_before_new, 0), axis=0, keepdims=True)
        nsel = jnp.sum(jnp.where(rb & new_sel, 1, 0), axis=0, keepdims=True)
        npos = jnp.sum(jnp.where(rb, new_pos, 0), axis=0, keepdims=True)
        col = col + jnp.where((slot_row[:, 0:1] == nbn) & (nsel > 0), npos, 0)
        idx_ref[b] = col
        return 0

    lax.fori_loop(0, nb, per_seq, 0)


def _attn_sample_kernel(n_past, idx_sm, pt_sm, idxc_ref, q_ref, kn_ref, vn_ref, rb_ref, ck_hbm, cv_hbm, o_ref,
                        kbuf, vbuf, sem):
    b = pl.program_id(0)
    n_seq = pl.num_programs(0)
    n_sel = kbuf.shape[1] // N_KV_B
    hd = HEAD_DIM
    page_shift = int(math.log2(PAGE_SIZE))

    def row_copies(buf, r, page, slot):
        dst = pl.ds(N_KV_B * r, N_KV_B)
        return (pltpu.make_async_copy(ck_hbm.at[page, slot], kbuf.at[buf, dst, :], sem.at[0, buf]),
                pltpu.make_async_copy(cv_hbm.at[page, slot], vbuf.at[buf, dst, :], sem.at[1, buf]))

    def gather(seq, buf):
        def start(r, _):
            p = jnp.minimum(idx_sm[seq, r] & (IDX_FLAG - 1), n_past - 1)
            ck, cv = row_copies(buf, r, pt_sm[seq, p >> page_shift], p & (PAGE_SIZE - 1))
            ck.start()
            cv.start()
            return 0

        lax.fori_loop(0, n_sel, start, 0, unroll=8)

    @pl.when(b == 0)
    def _():
        gather(0, 0)

    @pl.when(b + 1 < n_seq)
    def _():
        gather(b + 1, (b + 1) % 2)

    cur = b % 2

    def wait(r, _):
        ck, cv = row_copies(cur, r, 0, 0)
        ck.wait()
        cv.wait()
        return 0

    lax.fori_loop(0, n_sel, wait, 0, unroll=8)

    idxc = idxc_ref[0]
    valid = idxc < IDX_FLAG
    idx = idxc & (IDX_FLAG - 1)
    is_new = idx >= n_past
    bucket = _t5_bucket(n_past - idx)
    bias = jnp.zeros((n_sel, LANE), F32)
    for bkt in range(N_BUCKETS):
        bias = jnp.where(bucket == bkt, rb_ref[bkt:bkt + 1, :], bias)
    q_pad = jnp.concatenate([q_ref[0], jnp.zeros((LANE - N_HEADS_B, hd), F32)], axis=0).astype(BF16)
    lane = lax.broadcasted_iota(I32, (n_sel, LANE), 1)
    lg = jnp.zeros((n_sel, LANE), F32)
    vals = []
    for n in range(N_KV_B):
        rows_n = pl.ds(n, n_sel, stride=N_KV_B)
        kn = jnp.where(is_new, kn_ref[0, :, n * hd:(n + 1) * hd], kbuf[cur, rows_n, :]).astype(BF16)
        vals.append(jnp.where(is_new, vn_ref[0, :, n * hd:(n + 1) * hd], vbuf[cur, rows_n, :]))
        lg = jnp.where(lane // GQA_GROUP == n, _dot_nt(kn, q_pad), lg)
    lg = jnp.where(valid, lg * (hd ** -0.5) + bias, NEG_BIG)
    m = jnp.max(lg, axis=0, keepdims=True)
    p = jnp.where(valid, jnp.exp(lg - m), 0.0)
    p = p / jnp.sum(p, axis=0, keepdims=True)
    for h in range(N_HEADS_B):
        o_h = jnp.sum(p[:, h:h + 1] * vals[h // GQA_GROUP], axis=0, keepdims=True)
        o_ref[0, :, h * hd:(h + 1) * hd] = o_h.astype(o_ref.dtype)


def _attn_sample_call(idx3, page_table, q3, k_new, v_new, rel_bias, cache_k, cache_v):
    nb, n_sel, _ = idx3.shape
    n_past = page_table.shape[1] * PAGE_SIZE
    kvw = N_KV_B * HEAD_DIM
    qbw = N_HEADS_B * HEAD_DIM
    grid_spec = pltpu.PrefetchScalarGridSpec(
        num_scalar_prefetch=2,
        grid=(nb,),
        in_specs=[pl.BlockSpec((1, n_sel, 1), lambda b, *_: (b, 0, 0)),
                  pl.BlockSpec((1, N_HEADS_B, HEAD_DIM), lambda b, *_: (b, 0, 0)),
                  pl.BlockSpec((1, 1, kvw), lambda b, *_: (b, 0, 0)),
                  pl.BlockSpec((1, 1, kvw), lambda b, *_: (b, 0, 0)),
                  pl.BlockSpec((N_BUCKETS, LANE), lambda b, *_: (0, 0)),
                  pl.BlockSpec(memory_space=pl.ANY),
                  pl.BlockSpec(memory_space=pl.ANY)],
        out_specs=pl.BlockSpec((1, 1, qbw), lambda b, *_: (b, 0, 0)),
        scratch_shapes=[pltpu.VMEM((2, n_sel * N_KV_B, HEAD_DIM), F32), pltpu.VMEM((2, n_sel * N_KV_B, HEAD_DIM), F32),
                        pltpu.SemaphoreType.DMA((2, 2))],
    )
    return pl.pallas_call(
        functools.partial(_attn_sample_kernel, n_past),
        out_shape=jax.ShapeDtypeStruct((nb, 1, qbw), BF16),
        grid_spec=grid_spec,
        compiler_params=_cparams(("arbitrary",)),
        name="attn_sample",
    )(idx3.reshape(nb, n_sel), page_table, idx3, q3, k_new.reshape(nb, 1, kvw), v_new.reshape(nb, 1, kvw),
      jnp.pad(rel_bias.astype(F32), ((0, 0), (0, LANE - N_HEADS_B))), cache_k, cache_v)


def _topk_sample_call(scores2, qi2, wi2, ki_new, n_sel):
    nb, n_past = scores2.shape
    n_blk = n_past // LANE
    return pl.pallas_call(
        functools.partial(_topk_sample_kernel, n_sel),
        out_shape=jax.ShapeDtypeStruct((nb, n_sel, 1), I32),
        scratch_shapes=[pltpu.VMEM((n_blk, nb, LANE), I32), pltpu.VMEM((nb, n_past), I32),
                        pltpu.VMEM((n_blk * nb, LANE), F32), pltpu.VMEM((n_blk * nb, LANE), F32),
                        pltpu.VMEM((n_blk * nb, LANE), F32)],
        compiler_params=pltpu.CompilerParams(vmem_limit_bytes=VMEM_LIMIT),
        name="topk_sample",
    )(scores2, qi2, wi2, ki_new)


WPREP_COLS = 256


def _wprep_plan():
    src = [0]
    for s in SPLIT_SIZES:
        src.append(src[-1] + s)
    conv, a_raw, b_raw, z, q_b, k_b, v_b, q_i, w_i, k_i = src[:10]
    starts = {OFF_CONV: conv, OFF_Z: z, OFF_QI: q_i, OFF_QB: q_b, OFF_KB: k_b, OFF_VB: v_b}
    sizes = {OFF_CONV: CONV_DIM, OFF_Z: VAL_DIM_A, OFF_QI: N_IDX_HEADS * D_IDX, OFF_QB: N_HEADS_B * HEAD_DIM,
             OFF_KB: N_KV_B * HEAD_DIM, OFF_VB: N_KV_B * HEAD_DIM}
    offs = []
    for off, start in starts.items():
        assert off % WPREP_COLS == 0 and sizes[off] % WPREP_COLS == 0
        offs += [(off + c, start + c) for c in range(0, sizes[off], WPREP_COLS)]
    offs = [s for _, s in sorted(offs)]
    assert len(offs) * WPREP_COLS == OFF_KI and OFF_KI + D_IDX == OFF_SM and OFF_SM + LANE == OFF_KI + WPREP_COLS
    assert a_raw + N_HEADS_A == b_raw and w_i + N_IDX_HEADS == k_i and k_i + D_IDX == src[10]
    tail_start = src[10] - WPREP_COLS
    offs += [tail_start, 0]
    return offs, a_raw, w_i - tail_start, k_i - tail_start


def _wprep_kernel(offs_ref, wt_ref, ab_ref, o_ref):
    del offs_ref
    i = pl.program_id(0)
    n_blk = pl.num_programs(0)
    _, _, wi_at, ki_at = _wprep_plan()

    @pl.when(i < n_blk - 2)
    def _():
        o_ref[...] = wt_ref[0].T.astype(o_ref.dtype)

    @pl.when(i == n_blk - 2)
    def _():
        t = wt_ref[0].T
        ab = jnp.concatenate([ab_ref[0], jnp.zeros((LANE - ab_ref.shape[1], ab_ref.shape[2]), F32)], axis=0).T
        small = jnp.concatenate([ab[:, :SM_W], t[:, wi_at:wi_at + N_IDX_HEADS],
                                 jnp.zeros((t.shape[0], LANE - SM_W - N_IDX_HEADS), F32)], axis=1)
        o_ref[...] = jnp.concatenate([t[:, ki_at:ki_at + D_IDX], small], axis=1).astype(o_ref.dtype)

    @pl.when(i == n_blk - 1)
    def _():
        o_ref[...] = jnp.zeros(o_ref.shape, o_ref.dtype)


def _prep_w_in(w_all, layer):
    offs, ab_at, _, _ = _wprep_plan()
    wt = jnp.swapaxes(w_all, 1, 2)
    d_in = wt.shape[2]
    n_ab = 2 * N_HEADS_A
    assert ab_at % n_ab == 0
    grid_spec = pltpu.PrefetchScalarGridSpec(
        num_scalar_prefetch=1,
        grid=(NP // WPREP_COLS,),
        in_specs=[pl.BlockSpec((pl.Element(1), pl.Element(WPREP_COLS), pl.Element(d_in)),
                               lambda i, offs_ref: (layer, pl.multiple_of(offs_ref[i], n_ab), 0)),
                  pl.BlockSpec((1, n_ab, d_in), lambda i, offs_ref: (layer, ab_at // n_ab, 0))],
        out_specs=pl.BlockSpec((d_in, WPREP_COLS), lambda i, offs_ref: (0, i)),
    )
    return pl.pallas_call(
        _wprep_kernel,
        out_shape=jax.ShapeDtypeStruct((d_in, NP), BF16),
        grid_spec=grid_spec,
        compiler_params=_cparams(("arbitrary",)),
        name="wprep",
    )(jnp.asarray(offs, I32), wt, wt)


def kernel(x_prompt, x_sample, c_prompt, c_sample, cache_k, cache_v, cache_kidx, state_ssm, state_conv, page_table,
           w_ada, b_ada, pre1_g, post1_g, pre2_g, post2_g, w_in, w_out, conv_w, a_log, dt_bias, gdn_norm_g,
           idx_knorm_g, idx_knorm_b, rel_bias, w_ff1, w_ff2):
    bsz, seq, _ = x_prompt.shape
    nb, dec_seq, _ = x_sample.shape
    assert dec_seq == 1, "the sample path handles one new token per sequence"
    depth = w_in.shape[0]
    n_past = page_table.shape[1] * PAGE_SIZE
    n_sel_s = min(TOPK_MAX, (n_past + dec_seq) // 4)
    kvw = N_KV_B * HEAD_DIM
    tm_in, tn_in = min(1024, seq), 1024
    tm_out = min(512, seq)
    tf = 1024

    hp = x_prompt.reshape(bsz * seq, D_MODEL)
    hs = x_sample.reshape(nb, D_MODEL)
    c_all = jnp.concatenate([c_sample, c_prompt], axis=0)
    new_p, new_s = [], []
    for l in range(depth):
        m = _adaln_call(c_all, w_ada[l], b_ada[l])
        w_in_bf = _prep_w_in(w_in, l)
        w_out_bf = w_out[l].astype(BF16)
        w1_bf = w_ff1[l].astype(BF16)
        w2_bf = w_ff2[l].astype(BF16)

        proj, conv_tail = _inproj_call(hp, pre1_g[l], _ModSpec(m, False, nb, seq, tm_in), w_in_bf, tm_in, tn_in,
                                       conv_w[l], seq)
        k_p, v_p, ki_p, k_bf, va_bf, ki_bf = _kvprep_call(proj, idx_knorm_g[l], idx_knorm_b[l], tm_in, True)
        proj3 = proj.reshape(bsz, seq, NP)
        o_a, ssm_p = _gdn_prompt_call(proj3, a_log[l], dt_bias[l], gdn_norm_g[l])
        o_b = _attn_prompt_call(proj3, ki_bf.reshape(bsz, seq, D_IDX), k_bf.reshape(bsz, seq, kvw),
                                va_bf.reshape(bsz, seq, 2 * kvw), rel_bias)
        mod_p = _ModSpec(m, False, nb, seq, tm_out)
        x1, h2 = _outproj_call(o_a.reshape(bsz * seq, VAL_DIM_A), o_b.reshape(bsz * seq, N_HEADS_B * HEAD_DIM), hp,
                               w_out_bf, post1_g[l], pre2_g[l], mod_p, tm_out)
        hp = _ffn_call(h2, x1, w1_bf, w2_bf, post2_g[l], mod_p, tm_out, tf)
        conv_p = conv_tail.reshape(bsz, seq // tm_in, CONV_TAIL, CONV_DIM)[:, -1, CONV_TAIL - (CONV_W - 1):, :]
        new_p.append((k_p.reshape(bsz, seq, N_KV_B, HEAD_DIM), v_p.reshape(bsz, seq, N_KV_B, HEAD_DIM),
                      ki_p.reshape(bsz, seq, D_IDX), ssm_p, conv_p))

        mod_s = _ModSpec(m, True, 0, 1, nb)
        proj_s = _inproj_call(hs, pre1_g[l], mod_s, w_in_bf, nb, tn_in)
        k_s, v_s, ki_s = _kvprep_call(proj_s, idx_knorm_g[l], idx_knorm_b[l], nb)
        o_a_s, ssm_s, conv_s = _gdn_sample_call(proj_s.reshape(nb, 1, NP), state_conv[l], state_ssm[l], conv_w[l],
                                                a_log[l], dt_bias[l], gdn_norm_g[l])
        qi2 = proj_s[:, OFF_QI:OFF_QI + N_IDX_HEADS * D_IDX]
        wi2 = proj_s[:, OFF_SM + SM_W:OFF_SM + SM_W + N_IDX_HEADS]
        scores = _idx_sample_call(page_table, qi2.reshape(nb, N_IDX_HEADS, D_IDX), wi2.reshape(nb, N_IDX_HEADS, 1),
                                  cache_kidx[l])
        idx3 = _topk_sample_call(scores.reshape(nb, n_past), qi2, wi2, ki_s, n_sel_s)
        q3 = proj_s[:, OFF_QB:OFF_QB + N_HEADS_B * HEAD_DIM].reshape(nb, N_HEADS_B, HEAD_DIM)
        o_b_s = _attn_sample_call(idx3, page_table, q3, k_s, v_s, rel_bias, cache_k[l], cache_v[l])
        x1s, h2s = _outproj_call(o_a_s.reshape(nb, VAL_DIM_A), o_b_s.reshape(nb, N_HEADS_B * HEAD_DIM), hs,
                                 w_out_bf, post1_g[l], pre2_g[l], mod_s, nb)
        hs = _ffn_call(h2s, x1s, w1_bf, w2_bf, post2_g[l], mod_s, nb, tf)
        new_s.append((k_s.reshape(nb, 1, N_KV_B, HEAD_DIM), v_s.reshape(nb, 1, N_KV_B, HEAD_DIM),
                      ki_s.reshape(nb, 1, D_IDX), ssm_s, conv_s))

    p_k, p_v, p_kidx, p_ssm, p_conv = [jnp.stack([st[i] for st in new_p]) for i in range(5)]
    s_k, s_v, s_kidx, s_ssm, s_conv = [jnp.stack([st[i] for st in new_s]) for i in range(5)]
    return (hp.reshape(bsz, seq, D_MODEL), hs.reshape(nb, 1, D_MODEL), p_k, p_v, p_kidx, p_ssm, p_conv,
            s_k, s_v, s_kidx, s_ssm, s_conv)
```

```python
import functools
import math

import jax
import jax.numpy as jnp
from jax import lax
from jax.experimental import pallas as pl
from jax.experimental.pallas import tpu as pltpu

F32 = jnp.float32
BF16 = jnp.bfloat16
I32 = jnp.int32

D_MODEL = 2048
PAGE_SIZE = 128
HEAD_DIM = 128
N_HEADS_A = 8
KEY_DIM_A = N_HEADS_A * HEAD_DIM
VAL_DIM_A = N_HEADS_A * HEAD_DIM
CONV_DIM = 2 * KEY_DIM_A + VAL_DIM_A
CONV_W = 4
N_HEADS_B = 8
N_KV_B = 2
GQA_GROUP = N_HEADS_B // N_KV_B
N_IDX_HEADS = 16
D_IDX = 128
TOPK_MAX = 256
N_BUCKETS = 32
MAX_DISTANCE = 128
D_FF = 4 * D_MODEL
EPS = 1e-6

SPLIT_SIZES = (CONV_DIM, N_HEADS_A, N_HEADS_A, VAL_DIM_A, N_HEADS_B * HEAD_DIM, N_KV_B * HEAD_DIM,
               N_KV_B * HEAD_DIM, N_IDX_HEADS * D_IDX, N_IDX_HEADS, D_IDX)

OFF_CONV = 0
OFF_Z = OFF_CONV + CONV_DIM
OFF_QI = OFF_Z + VAL_DIM_A
OFF_QB = OFF_QI + N_IDX_HEADS * D_IDX
OFF_KB = OFF_QB + N_HEADS_B * HEAD_DIM
OFF_VB = OFF_KB + N_KV_B * HEAD_DIM
OFF_KI = OFF_VB + N_KV_B * HEAD_DIM
OFF_SM = OFF_KI + D_IDX
NP = 8192
SM_A, SM_B, SM_W = 0, N_HEADS_A, 2 * N_HEADS_A

LANE = 128
VMEM_LIMIT = 56 * 1024 * 1024
INT_MIN = -(2 ** 31)
NEG_BIG = -1e30

GDN_CHUNK = 128
GDN_GROUP = 256
ATT_TQ = 256
ATT_TK = 256


def _cparams(sem):
    return pltpu.CompilerParams(dimension_semantics=sem, vmem_limit_bytes=VMEM_LIMIT)


def _sigmoid(x):
    return 1.0 / (1.0 + jnp.exp(-x))


def _softplus(x):
    return jnp.maximum(x, 0.0) + jnp.log(1.0 + jnp.exp(-jnp.abs(x)))


def _dot(a, b):
    return jnp.dot(a, b, preferred_element_type=F32)


def _dot_nt(a, b):
    return lax.dot_general(a, b, (((1,), (1,)), ((), ())), preferred_element_type=F32)


def _dot_tn(a, b):
    return lax.dot_general(a, b, (((0,), (0,)), ((), ())), preferred_element_type=F32)


def _mod(ref):
    v = ref[...]
    return v[0] if v.ndim == 3 else v


def _adaln_kernel(c_ref, w_ref, b_ref, o_ref):
    c = c_ref[...]
    s = (c * _sigmoid(c)).astype(BF16)
    o_ref[...] = _dot(s, w_ref[...].astype(BF16)) + b_ref[...]


def _adaln_call(c_all, w_ada, b_ada):
    n_rows = c_all.shape[0]
    n_out = w_ada.shape[1]
    tn = 1024
    return pl.pallas_call(
        _adaln_kernel,
        out_shape=jax.ShapeDtypeStruct((n_rows, n_out), F32),
        grid=(n_out // tn,),
        in_specs=[pl.BlockSpec((n_rows, D_MODEL), lambda j: (0, 0)),
                  pl.BlockSpec((D_MODEL, tn), lambda j: (0, j)),
                  pl.BlockSpec((1, tn), lambda j: (0, j))],
        out_specs=pl.BlockSpec((n_rows, tn), lambda j: (0, j)),
        compiler_params=_cparams(("arbitrary",)),
        name="adaln",
    )(c_all, w_ada, b_ada.reshape(1, n_out))


class _ModSpec:
    def __init__(self, m2, per_row, row_off, rows_per_batch, tm):
        self.per_row = per_row
        self.arr = m2 if per_row else m2.reshape(m2.shape[0], 1, m2.shape[1])
        self.row_off = row_off
        self.blocks_per_batch = None if per_row else rows_per_batch // tm
        self.tm = tm

    def spec(self, k, grid_rank):
        if self.per_row:
            if grid_rank == 1:
                return pl.BlockSpec((self.tm, D_MODEL), lambda i: (0, k))
            return pl.BlockSpec((self.tm, D_MODEL), lambda i, j: (0, k))
        bpb, off = self.blocks_per_batch, self.row_off
        if grid_rank == 1:
            return pl.BlockSpec((1, 1, D_MODEL), lambda i: (off + i // bpb, 0, k))
        return pl.BlockSpec((1, 1, D_MODEL), lambda i, j: (off + i // bpb, 0, k))


CONV_TAIL = 8


def _l2norm(x):
    return x * lax.rsqrt(jnp.sum(x * x, axis=-1, keepdims=True) + EPS)


def _inproj_kernel(blocks_per_seq, x_ref, g_ref, sh_ref, sc_ref, w_ref, *rest):
    j = pl.program_id(1)
    if blocks_per_seq is None:
        o_ref, h_scr = rest
    else:
        cw_ref, o_ref, tail_ref, h_scr, carry_s = rest

    @pl.when(j == 0)
    def _():
        x = x_ref[...]
        y = x * lax.rsqrt(jnp.mean(x * x, axis=-1, keepdims=True) + EPS) * g_ref[...]
        h_scr[...] = (y * (1.0 + _mod(sc_ref)) + _mod(sh_ref)).astype(BF16)

    if blocks_per_seq is None:
        o_ref[...] = _dot(h_scr[...], w_ref[...])
        return

    n_conv = CONV_DIM // o_ref.shape[1]

    @pl.when(j >= n_conv)
    def _():
        o_ref[...] = _dot(h_scr[...], w_ref[...])

    for blk in range(n_conv):
        @pl.when(j == blk)
        def _(blk=blk):
            raw = _dot(h_scr[...], w_ref[...])
            tm = raw.shape[0]
            cw = cw_ref[...]
            first = pl.program_id(0) % blocks_per_seq == 0
            prev = jnp.where(first, 0.0, carry_s[blk])
            carry_s[blk] = raw[tm - CONV_TAIL:]
            tail_ref[0] = raw[tm - CONV_TAIL:]

            def taps(xx):
                y = xx * cw[CONV_W - 1:CONV_W]
                for s in range(1, CONV_W):
                    y = y + pltpu.roll(xx, s, 0) * cw[CONV_W - 1 - s:CONV_W - s]
                return y

            head = taps(jnp.concatenate([prev, raw[:CONV_TAIL]], axis=0))[CONV_TAIL:]
            y = jnp.concatenate([head, taps(raw)[CONV_TAIL:]], axis=0)
            y = y * _sigmoid(y)
            if blk == 2:
                o_ref[...] = y
            else:
                for h in range(o_ref.shape[1] // HEAD_DIM):
                    yh = _l2norm(y[:, h * HEAD_DIM:(h + 1) * HEAD_DIM])
                    o_ref[:, h * HEAD_DIM:(h + 1) * HEAD_DIM] = yh * (HEAD_DIM ** -0.5) if blk == 0 else yh


def _inproj_call(x2d, g, mod, w_bf, tm, tn, conv_w=None, seq=None):
    m_rows = x2d.shape[0]
    n_cols = w_bf.shape[1]
    in_specs = [pl.BlockSpec((tm, D_MODEL), lambda i, j: (i, 0)),
                pl.BlockSpec((1, D_MODEL), lambda i, j: (0, 0)),
                mod.spec(0, 2), mod.spec(1, 2),
                pl.BlockSpec((D_MODEL, tn), lambda i, j: (0, j))]
    args = [x2d, g.reshape(1, D_MODEL), mod.arr, mod.arr, w_bf]
    out_shape = jax.ShapeDtypeStruct((m_rows, n_cols), F32)
    out_specs = pl.BlockSpec((tm, tn), lambda i, j: (i, j))
    scratch = [pltpu.VMEM((tm, D_MODEL), BF16)]
    bps = None
    if conv_w is not None:
        assert tn == KEY_DIM_A == VAL_DIM_A and OFF_CONV == 0, "q, k, v must each be one leading column block"
        bps = seq // tm
        n_conv = CONV_DIM // tn
        last = n_conv - 1
        in_specs.append(pl.BlockSpec((CONV_W, tn), lambda i, j: (0, jnp.minimum(j, last))))
        args.append(conv_w)
        out_shape = (out_shape, jax.ShapeDtypeStruct((m_rows // tm, CONV_TAIL, CONV_DIM), F32))
        out_specs = (out_specs, pl.BlockSpec((1, CONV_TAIL, tn), lambda i, j: (i, 0, jnp.minimum(j, last))))
        scratch.append(pltpu.VMEM((n_conv, CONV_TAIL, tn), F32))
    return pl.pallas_call(
        functools.partial(_inproj_kernel, bps),
        out_shape=out_shape,
        grid=(m_rows // tm, n_cols // tn),
        in_specs=in_specs,
        out_specs=out_specs,
        scratch_shapes=scratch,
        compiler_params=_cparams(("arbitrary", "arbitrary")),
        name="inproj",
    )(*args)


def _kvprep_kernel(kb_ref, vb_ref, ki_ref, lg_ref, lb_ref, k_ref, v_ref, kidx_ref, *bf_refs):
    kb = kb_ref[...]
    vb = vb_ref[...]
    for n in range(N_KV_B):
        k_ref[:, n, :] = kb[:, n * HEAD_DIM:(n + 1) * HEAD_DIM]
        v_ref[:, n, :] = vb[:, n * HEAD_DIM:(n + 1) * HEAD_DIM]
    x = ki_ref[...]
    mu = jnp.mean(x, axis=-1, keepdims=True)
    xc = x - mu
    var = jnp.mean(xc * xc, axis=-1, keepdims=True)
    ki = xc * lax.rsqrt(var + EPS) * lg_ref[...] + lb_ref[...]
    kidx_ref[...] = ki
    if bf_refs:
        kbf_ref, va_ref, kibf_ref = bf_refs
        kbf_ref[...] = kb.astype(BF16)
        kibf_ref[...] = ki.astype(BF16)
        hd = HEAD_DIM
        ones_col = jnp.where(lax.broadcasted_iota(I32, (vb.shape[0], hd), 1) == 0, 1.0, 0.0)
        va_ref[...] = jnp.concatenate([piece for n in range(N_KV_B) for piece in (vb[:, n * hd:(n + 1) * hd], ones_col)],
                                      axis=1).astype(BF16)


def _kvprep_call(proj, ln_g, ln_b, tm, with_bf16=False):
    m_rows = proj.shape[0]
    kvw = N_KV_B * HEAD_DIM
    row = lambda i: (i, 0)
    kv_shape = jax.ShapeDtypeStruct((m_rows, N_KV_B, HEAD_DIM), F32)
    kv_spec = pl.BlockSpec((tm, N_KV_B, HEAD_DIM), lambda i: (i, 0, 0))
    out_shape = [kv_shape, kv_shape, jax.ShapeDtypeStruct((m_rows, D_IDX), F32)]
    out_specs = [kv_spec, kv_spec, pl.BlockSpec((tm, D_IDX), row)]
    if with_bf16:
        out_shape += [jax.ShapeDtypeStruct((m_rows, kvw), BF16), jax.ShapeDtypeStruct((m_rows, 2 * kvw), BF16),
                      jax.ShapeDtypeStruct((m_rows, D_IDX), BF16)]
        out_specs += [pl.BlockSpec((tm, kvw), row), pl.BlockSpec((tm, 2 * kvw), row), pl.BlockSpec((tm, D_IDX), row)]
    return pl.pallas_call(
        _kvprep_kernel,
        out_shape=tuple(out_shape),
        grid=(m_rows // tm,),
        in_specs=[pl.BlockSpec((tm, kvw), lambda i: (i, OFF_KB // kvw)),
                  pl.BlockSpec((tm, kvw), lambda i: (i, OFF_VB // kvw)),
                  pl.BlockSpec((tm, D_IDX), lambda i: (i, OFF_KI // D_IDX)),
                  pl.BlockSpec((1, D_IDX), lambda i: (0, 0)),
                  pl.BlockSpec((1, D_IDX), lambda i: (0, 0))],
        out_specs=tuple(out_specs),
        compiler_params=_cparams(("arbitrary",)),
        name="kvprep",
    )(proj, proj, proj, ln_g.reshape(1, D_IDX), ln_b.reshape(1, D_IDX))


def _gdn_prompt_kernel(q_ref, k_ref, v_ref, z_ref, sm_ref, alog_ref, dtb_ref, ng_ref,
                       o_ref, s_out_ref,
                       gc_s, bt_s, l_s, a_s, x_s, u_s, w_s, mn_s, pq_s, sts_s):
    seq = q_ref.shape[1]
    hd = HEAD_DIM
    grp = GDN_GROUP
    chunk = GDN_CHUNK
    n_grp = seq // grp
    n_chunk = seq // chunk
    cpg = grp // chunk
    h = pl.program_id(1)

    row = lax.broadcasted_iota(I32, (seq, hd), 0)
    lane = lax.broadcasted_iota(I32, (seq, hd), 1)

    qn_s, kn_s, vn_s = q_ref.at[0], k_ref.at[0], v_ref.at[0]

    sm = sm_ref[0]
    g_all = -jnp.exp(alog_ref[...]) * _softplus(sm + dtb_ref[...])
    b_all = _sigmoid(sm)
    g_col = jnp.sum(jnp.where(lane == h + SM_A, g_all, 0.0), axis=1, keepdims=True)
    b_col = jnp.sum(jnp.where(lane == h + SM_B, b_all, 0.0), axis=1, keepdims=True)
    gc = jnp.broadcast_to(g_col, (seq, hd))
    pos = row & (chunk - 1)
    s = 1
    while s < chunk:
        gc = gc + jnp.where(pos >= s, pltpu.roll(gc, s, 0), 0.0)
        s *= 2
    gc_s[...] = gc
    bt_s[...] = jnp.broadcast_to(b_col, (seq, hd))

    ii = lax.broadcasted_iota(I32, (grp, grp), 0)
    jj = lax.broadcasted_iota(I32, (grp, grp), 1)
    xr = ii ^ jj
    same = (xr >> int(math.log2(chunk))) == 0
    causal = same & (ii >= jj)
    strict = same & (ii > jj)
    eye = (ii == jj).astype(F32)

    def build(gi, _):
        r0 = pl.multiple_of(gi * grp, grp)
        kk = kn_s[pl.ds(r0, grp), :]
        qq = qn_s[pl.ds(r0, grp), :]
        bt = bt_s[pl.ds(r0, grp), :]
        gcg = gc_s[pl.ds(r0, grp), :]
        kb = (kk * bt).astype(BF16)
        kkb = kk.astype(BF16)
        kkt = _dot_nt(kb, kkb)
        qkt = _dot_nt(qq.astype(BF16), kkb)
        colb = jnp.concatenate([gcg, gcg], axis=1)
        diff = colb - colb.T
        decay = jnp.where(causal, jnp.exp(jnp.where(causal, diff, 0.0)), 0.0)
        lmat = jnp.where(strict, kkt * decay, 0.0)
        l_s[gi] = lmat
        a_s[pl.ds(r0, grp), :] = jnp.where(causal, qkt * decay, 0.0)
        x_s[gi] = eye - jnp.where(xr == 1, lmat, 0.0)
        return 0

    lax.fori_loop(0, n_grp, build, 0, unroll=True)

    for lvl in range(1, int(math.log2(chunk))):
        def level(gi, _, lvl=lvl):
            xm = x_s[gi]
            cl = jnp.where((xr >> lvl) == 1, l_s[gi], 0.0).astype(BF16)
            xb = xm.astype(BF16)
            x_s[gi] = xm - _dot(_dot(xb, cl).astype(BF16), xb)
            return 0

        lax.fori_loop(0, n_grp, level, 0, unroll=True)

    def apply(gi, _):
        r0 = pl.multiple_of(gi * grp, grp)
        kk = kn_s[pl.ds(r0, grp), :]
        vv = vn_s[pl.ds(r0, grp), :]
        bt = bt_s[pl.ds(r0, grp), :]
        gcg = gc_s[pl.ds(r0, grp), :]
        rhs = jnp.concatenate([vv * bt, kk * bt * jnp.exp(gcg)], axis=1).astype(BF16)
        uw = _dot(x_s[gi].astype(BF16), rhs)
        u_s[pl.ds(r0, grp), :] = uw[:, :hd]
        w_s[pl.ds(r0, grp), :] = uw[:, hd:]
        for cc in range(cpg):
            lo = cc * chunk
            g_last = gcg[lo + chunk - 1:lo + chunk, :]
            k_dec = (kk[lo:lo + chunk] * jnp.exp(g_last - gcg[lo:lo + chunk])).astype(BF16)
            wu = jnp.concatenate([uw[lo:lo + chunk, hd:], uw[lo:lo + chunk, :hd]], axis=1).astype(BF16)
            mn_s[gi * cpg + cc] = _dot_tn(k_dec, wu)
        return 0

    lax.fori_loop(0, n_grp, apply, 0, unroll=True)

    assert n_chunk % 2 == 0
    n_pair = n_chunk // 2

    def decay(c):
        return jnp.exp(gc_s[pl.ds(c * chunk + chunk - 1, 1), :])

    def combine(p, _):
        mn0, mn1 = mn_s[2 * p], mn_s[2 * p + 1]
        a0, a1 = decay(2 * p), decay(2 * p + 1)
        cross = _dot(mn1[:, :hd].astype(BF16), mn0.astype(BF16))
        pq_s[p] = jnp.concatenate([a1 * mn0[:, :hd] + a0 * mn1[:, :hd] - cross[:, :hd],
                                   a1 * mn0[:, hd:] - cross[:, hd:] + mn1[:, hd:]], axis=1)
        return 0

    lax.fori_loop(0, n_pair, combine, 0, unroll=True)

    def scan(p, st):
        sts_s[2 * p] = st
        qr = pq_s[p]
        return st * (decay(2 * p) * decay(2 * p + 1)) - _dot(qr[:, :hd].astype(BF16), st.astype(BF16)) + qr[:, hd:]

    s_out_ref[0, 0] = lax.fori_loop(0, n_pair, scan, jnp.zeros((hd, hd), F32))

    def middle(p, _):
        st = sts_s[2 * p]
        mn0 = mn_s[2 * p]
        sts_s[2 * p + 1] = st * decay(2 * p) - _dot(mn0[:, :hd].astype(BF16), st.astype(BF16)) + mn0[:, hd:]
        return 0

    lax.fori_loop(0, n_pair, middle, 0, unroll=True)

    ng = ng_ref[...]

    def emit(gi, _):
        r0 = pl.multiple_of(gi * grp, grp)
        gcg = gc_s[pl.ds(r0, grp), :]
        qg = qn_s[pl.ds(r0, grp), :] * jnp.exp(gcg)
        wg = w_s[pl.ds(r0, grp), :]
        ws, qs = [], []
        for cc in range(cpg):
            lo = cc * chunk
            stb = sts_s[gi * cpg + cc].astype(BF16)
            both = _dot(jnp.concatenate([wg[lo:lo + chunk], qg[lo:lo + chunk]], axis=0).astype(BF16), stb)
            ws.append(both[:chunk])
            qs.append(both[chunk:])
        v_new = u_s[pl.ds(r0, grp), :] - jnp.concatenate(ws, axis=0)
        o = jnp.concatenate(qs, axis=0) + _dot(a_s[pl.ds(r0, grp), :].astype(BF16), v_new.astype(BF16))
        zc = z_ref[0, pl.ds(r0, grp), :]
        on = o * lax.rsqrt(jnp.mean(o * o, axis=-1, keepdims=True) + EPS) * ng
        o_ref[0, pl.ds(r0, grp), :] = (on * (zc * _sigmoid(zc))).astype(o_ref.dtype)
        return 0

    lax.fori_loop(0, n_grp, emit, 0, unroll=True)


def _lane_vec(v, off):
    return jnp.zeros((1, LANE), F32).at[0, off:off + v.shape[0]].set(v.astype(F32))


def _gdn_prompt_call(proj3, a_log, dt_bias, norm_g):
    bsz, seq, _ = proj3.shape
    hd = HEAD_DIM
    nh = N_HEADS_A
    n_grp = seq // GDN_GROUP
    col = lambda base: (lambda b, h: (b, 0, base // hd + h))
    vec = pl.BlockSpec((1, LANE), lambda b, h: (0, 0))
    return pl.pallas_call(
        _gdn_prompt_kernel,
        out_shape=(jax.ShapeDtypeStruct((bsz, seq, VAL_DIM_A), BF16),
                   jax.ShapeDtypeStruct((bsz, nh, hd, hd), F32)),
        grid=(bsz, nh),
        in_specs=[pl.BlockSpec((1, seq, hd), col(OFF_CONV)),
                  pl.BlockSpec((1, seq, hd), col(OFF_CONV + KEY_DIM_A)),
                  pl.BlockSpec((1, seq, hd), col(OFF_CONV + 2 * KEY_DIM_A)),
                  pl.BlockSpec((1, seq, hd), col(OFF_Z)),
                  pl.BlockSpec((1, seq, LANE), lambda b, h: (b, 0, OFF_SM // LANE)),
                  vec, vec, vec],
        out_specs=(pl.BlockSpec((1, seq, hd), lambda b, h: (b, 0, h)),
                   pl.BlockSpec((1, 1, hd, hd), lambda b, h: (b, h, 0, 0))),
        scratch_shapes=[pltpu.VMEM((seq, hd), F32), pltpu.VMEM((seq, hd), F32),
                        pltpu.VMEM((n_grp, GDN_GROUP, GDN_GROUP), F32),
                        pltpu.VMEM((seq, GDN_GROUP), F32),
                        pltpu.VMEM((n_grp, GDN_GROUP, GDN_GROUP), F32),
                        pltpu.VMEM((seq, hd), F32), pltpu.VMEM((seq, hd), F32),
                        pltpu.VMEM((seq // GDN_CHUNK, hd, 2 * hd), F32),
                        pltpu.VMEM((seq // GDN_CHUNK // 2, hd, 2 * hd), F32),
                        pltpu.VMEM((seq // GDN_CHUNK, hd, hd), F32)],
        compiler_params=_cparams(("arbitrary", "arbitrary")),
        name="gdn_prompt",
    )(proj3, proj3, proj3, proj3, proj3,
      _lane_vec(a_log, SM_A), _lane_vec(dt_bias, SM_A), norm_g.reshape(1, hd).astype(F32))


def _t5_bucket(dist):
    n = jnp.maximum(dist, 0)
    max_exact = N_BUCKETS // 2
    nf = jnp.maximum(n, 1).astype(F32)
    large = max_exact + (jnp.log(nf / max_exact) / math.log(MAX_DISTANCE / max_exact)
                         * (N_BUCKETS - max_exact)).astype(I32)
    large = jnp.minimum(large, N_BUCKETS - 1)
    return jnp.where(n < max_exact, n, large)


def _bias_from_bucket(bucket, rb_ref, head):
    out = jnp.zeros(bucket.shape, F32)
    for b in range(N_BUCKETS):
        out = jnp.where(bucket == b, rb_ref[b, head], out)
    return out


def _order_key(score):
    bits = pltpu.bitcast(score + 0.0, I32)
    return bits ^ ((bits >> 31) & 0x7FFFFFFF)


I16 = jnp.int16
HALF_BIAS = 1 << 15
HALF_ROWS = 16
KEY_POS_INF = 0x7F800000
KEY_NEG_INF = INT_MIN + 0x7FFFFF


def _kth_largest_key(count_ge, shape, n_sel):
    def body(i, tau):
        bit = 31 - i
        cand = jnp.where(i == 0, tau ^ INT_MIN, tau | (1 << bit))
        return jnp.where(count_ge(cand) >= n_sel, cand, tau)

    return lax.fori_loop(0, 32, body, jnp.full(shape, INT_MIN, I32))


def _attn_prompt_kernel(n_sel, qi_ref, sm_ref, qb_ref, ki_ref, k_ref, va_ref, rb_ref, o_ref,
                        qi_s, qb_s, wt_s, keys_s, hi_s, lo_s, tau_s, bias_s, m_s, acc_s):
    tq, tk = ATT_TQ, ATT_TK
    hd = HEAD_DIM
    bi = pl.program_id(0)
    qblk = pl.program_id(1)
    n_kb = (qblk * tq) // tk + 1
    krow = lax.broadcasted_iota(I32, (tk, tq), 0)
    qcol = lax.broadcasted_iota(I32, (tk, tq), 1)

    @pl.when((bi == 0) & (qblk == 0))
    def _():
        for off in range(bias_s.shape[1]):
            bucket = _t5_bucket(off * tk + krow - qcol)
            for h in range(N_HEADS_B):
                bias_s[h, off] = _bias_from_bucket(bucket, rb_ref, h)

    qi_s[...] = qi_ref[0].astype(BF16)
    qb_s[...] = qb_ref[0].astype(BF16)
    wt_s[...] = sm_ref[0].T * (N_IDX_HEADS ** -0.5)
    t_pos = qblk * tq + qcol

    def idx_body(j, _):
        k0 = pl.multiple_of(j * tk, tk)
        kt = ki_ref[0, pl.ds(k0, tk), :]
        acc = jnp.zeros((tk, tq), F32)
        for h in range(N_IDX_HEADS):
            s = _dot_nt(kt, qi_s[:, h * D_IDX:(h + 1) * D_IDX])
            acc = acc + jnp.maximum(s, 0.0) * wt_s[SM_W + h:SM_W + h + 1, :]
        acc = acc * (D_IDX ** -0.5)
        key = jnp.where(k0 + krow <= t_pos, _order_key(acc), INT_MIN)
        keys_s[j] = key
        hi_s[j] = (key >> 16).astype(I16)
        lo_s[j] = ((key & 0xFFFF) - HALF_BIAS).astype(I16)
        return 0

    lax.fori_loop(0, n_kb, idx_body, 0)

    def kth_largest_half(half_s, n_at_start):
        def count_ge(cand):
            def cb(j, c):
                hit = jnp.where(half_s[j] >= cand, jnp.int16(1), jnp.int16(0))
                for r in range(tk // HALF_ROWS):
                    c = c + hit[r * HALF_ROWS:(r + 1) * HALF_ROWS]
                return c

            c = lax.fori_loop(0, n_kb, cb, jnp.zeros((HALF_ROWS, tq), I16))
            return jnp.sum(c.astype(I32), axis=0, keepdims=True)

        def body(i, carry):
            t, n_t = carry
            cand = jnp.where(i == 0, t ^ (-HALF_BIAS), t | (1 << (15 - i)))
            n_cand = count_ge(cand.astype(I16))
            ok = n_cand >= n_sel
            return jnp.where(ok, cand, t), jnp.where(ok, n_cand, n_t)

        return lax.fori_loop(0, 16, body, (jnp.full((1, tq), -HALF_BIAS, I32), n_at_start))

    tau_hi, n_ge_hi = kth_largest_half(hi_s, jnp.zeros((1, tq), I32))
    tau_hi16 = tau_hi.astype(I16)

    def narrow(j, _):
        hi = hi_s[j]
        lo_s[j] = jnp.where(hi > tau_hi16, jnp.int16(HALF_BIAS - 1),
                            jnp.where(hi == tau_hi16, lo_s[j], jnp.int16(-HALF_BIAS)))
        return 0

    lax.fori_loop(0, n_kb, narrow, 0)
    tau_lo, n_ge_tau = kth_largest_half(lo_s, n_ge_hi)
    tau = (tau_hi << 16) | ((tau_lo + HALF_BIAS) & 0xFFFF)
    tau = jnp.maximum(tau, KEY_NEG_INF + 1)

    def count_where(pred):
        def cb(j, c):
            hit = jnp.where(pred(keys_s[j], j * tk + krow), 1, 0)
            return c + jnp.sum(hit.reshape(tk // 8, 8, tq), axis=0)

        c = lax.fori_loop(0, n_kb, cb, jnp.zeros((8, tq), I32))
        return jnp.sum(c, axis=0, keepdims=True)

    @pl.when(jnp.max(n_ge_tau) > n_sel)
    def _():
        quota = n_sel - count_where(lambda k, pos: k > tau)
        pos_bits = (keys_s.shape[0] * tk - 1).bit_length()

        def body(i, last):
            cand = last | (1 << (pos_bits - 1 - i))
            before = count_where(lambda k, pos: (k == tau) & (pos < cand))
            return jnp.where(before < quota, cand, last)

        last_kept = lax.fori_loop(0, pos_bits, body, jnp.zeros((1, tq), I32))

        def demote(j, _):
            k = keys_s[j]
            keys_s[j] = jnp.where((k == tau) & (j * tk + krow > last_kept), tau - 1, k)
            return 0

        lax.fori_loop(0, n_kb, demote, 0)

    scale = hd ** -0.5
    lane_reps = tk // LANE
    tau_rep = jnp.broadcast_to(tau, (LANE, tq)).T
    tau_s[...] = jnp.concatenate([tau_rep] * lane_reps, axis=1)

    def block_inputs(j):
        kq = keys_s[j].T
        sel = jnp.where(kq < KEY_POS_INF, kq, INT_MIN) >= tau_s[...]
        return k_ref[0, pl.ds(pl.multiple_of(j * tk, tk), tk), :], sel

    def masked_logits(j, h, kt, sel):
        n = h // GQA_GROUP
        boff = jnp.minimum(qblk - j, bias_s.shape[1] - 1)
        lg = _dot_nt(qb_s[:, h * hd:(h + 1) * hd], kt[:, n * hd:(n + 1) * hd]) * scale + bias_s[h, boff]
        return jnp.where(sel, lg, NEG_BIG)

    m_s[...] = jnp.full(m_s.shape, NEG_BIG, F32)

    def max_body(j, _):
        kt, sel = block_inputs(j)
        for h in range(N_HEADS_B):
            lg = masked_logits(j, h, kt, sel)
            part = lg[:, :LANE]
            for r in range(1, lane_reps):
                part = jnp.maximum(part, lg[:, r * LANE:(r + 1) * LANE])
            m_s[h] = jnp.maximum(m_s[h], part)
        return 0

    lax.fori_loop(0, n_kb, max_body, 0)
    for h in range(N_HEADS_B):
        m_s[h] = jnp.broadcast_to(jnp.max(m_s[h], axis=1, keepdims=True), (tq, LANE))
    acc_s[...] = jnp.zeros(acc_s.shape, F32)

    def sum_body(j, _):
        kt, sel = block_inputs(j)
        va = va_ref[0, pl.ds(pl.multiple_of(j * tk, tk), tk), :]
        for h in range(N_HEADS_B):
            n = h // GQA_GROUP
            mrow = jnp.concatenate([m_s[h]] * lane_reps, axis=1)
            p = jnp.exp(masked_logits(j, h, kt, sel) - mrow)
            acc_s[h] += _dot(p.astype(BF16), va[:, n * 2 * hd:(n + 1) * 2 * hd])
        return 0

    lax.fori_loop(0, n_kb, sum_body, 0)
    for h in range(N_HEADS_B):
        a = acc_s[h]
        o_ref[0, :, h * hd:(h + 1) * hd] = (a[:, :hd] / a[:, hd:hd + 1]).astype(o_ref.dtype)


def _attn_prompt_call(proj3, ki_bf, k_bf, va_bf, rel_bias):
    bsz, seq, _ = proj3.shape
    tq, tk = ATT_TQ, ATT_TK
    assert tq == tk, "the bias-tile offsets assume square blocks"
    n_sel = min(TOPK_MAX, seq // 4)
    qiw = N_IDX_HEADS * D_IDX
    qbw = N_HEADS_B * HEAD_DIM
    kvw = N_KV_B * HEAD_DIM
    n_off = -(-(MAX_DISTANCE + tk - 1) // tk) + 1
    return pl.pallas_call(
        functools.partial(_attn_prompt_kernel, n_sel),
        out_shape=jax.ShapeDtypeStruct((bsz, seq, qbw), BF16),
        grid=(bsz, seq // tq),
        in_specs=[pl.BlockSpec((1, tq, qiw), lambda b, i: (b, i, OFF_QI // qiw)),
                  pl.BlockSpec((1, tq, LANE), lambda b, i: (b, i, OFF_SM // LANE)),
                  pl.BlockSpec((1, tq, qbw), lambda b, i: (b, i, OFF_QB // qbw)),
                  pl.BlockSpec((1, seq, D_IDX), lambda b, i: (b, 0, 0)),
                  pl.BlockSpec((1, seq, kvw), lambda b, i: (b, 0, 0)),
                  pl.BlockSpec((1, seq, 2 * kvw), lambda b, i: (b, 0, 0)),
                  pl.BlockSpec(memory_space=pltpu.SMEM)],
        out_specs=pl.BlockSpec((1, tq, qbw), lambda b, i: (b, i, 0)),
        scratch_shapes=[pltpu.VMEM((tq, qiw), BF16), pltpu.VMEM((tq, qbw), BF16),
                        pltpu.VMEM((LANE, tq), F32),
                        pltpu.VMEM((seq // tk, tk, tq), I32),
                        pltpu.VMEM((seq // tk, tk, tq), I16), pltpu.VMEM((seq // tk, tk, tq), I16),
                        pltpu.VMEM((tq, tk), I32),
                        pltpu.VMEM((N_HEADS_B, n_off, tq, tk), F32),
                        pltpu.VMEM((N_HEADS_B, tq, LANE), F32),
                        pltpu.VMEM((N_HEADS_B, tq, 2 * HEAD_DIM), F32)],
        compiler_params=_cparams(("arbitrary", "arbitrary")),
        name="attn_prompt",
    )(proj3, proj3, proj3, ki_bf, k_bf, va_bf, rel_bias.astype(F32))


def _rms(x, g):
    return x * lax.rsqrt(jnp.mean(x * x, axis=-1, keepdims=True) + EPS) * g


def _outproj_kernel(oa_ref, ob_ref, x_ref, wa_ref, wb_ref, pg_ref, g1_ref, p2_ref, sh2_ref, sc2_ref, x1_ref, h2_ref):
    rows = x_ref.shape[0]
    per_row = _mod(g1_ref).shape[0] == rows
    n_piece = 2 if rows % 16 == 0 and rows >= 256 else 1
    for r in range(n_piece):
        sl = slice(r * rows // n_piece, (r + 1) * rows // n_piece)
        pick = (lambda ref: _mod(ref)[sl]) if per_row else _mod
        mix = _dot(oa_ref[sl, :], wa_ref[...]) + _dot(ob_ref[sl, :], wb_ref[...])
        x1 = x_ref[sl, :] + pick(g1_ref) * _rms(mix, pg_ref[...])
        x1_ref[sl, :] = x1
        h2_ref[sl, :] = (_rms(x1, p2_ref[...]) * (1.0 + pick(sc2_ref)) + pick(sh2_ref)).astype(BF16)


def _outproj_call(oa, ob, x2d, w_out_bf, post1_g, pre2_g, mod, tm):
    m_rows = x2d.shape[0]
    half = w_out_bf.shape[0] // 2
    row = lambda i: (i, 0)
    vec = pl.BlockSpec((1, D_MODEL), lambda i: (0, 0))
    return pl.pallas_call(
        _outproj_kernel,
        out_shape=(jax.ShapeDtypeStruct((m_rows, D_MODEL), F32), jax.ShapeDtypeStruct((m_rows, D_MODEL), BF16)),
        grid=(m_rows // tm,),
        in_specs=[pl.BlockSpec((tm, half), row), pl.BlockSpec((tm, half), row), pl.BlockSpec((tm, D_MODEL), row),
                  pl.BlockSpec((half, D_MODEL), lambda i: (0, 0)), pl.BlockSpec((half, D_MODEL), lambda i: (1, 0)),
                  vec, mod.spec(2, 1), vec, mod.spec(3, 1), mod.spec(4, 1)],
        out_specs=(pl.BlockSpec((tm, D_MODEL), row), pl.BlockSpec((tm, D_MODEL), row)),
        compiler_params=_cparams(("arbitrary",)),
        name="outproj",
    )(oa, ob, x2d, w_out_bf, w_out_bf, post1_g.reshape(1, D_MODEL), mod.arr, pre2_g.reshape(1, D_MODEL),
      mod.arr, mod.arr)


def _ffn_kernel(h_ref, x1_ref, w1_ref, w2_ref, pg_ref, g2_ref, o_ref, acc_s):
    kk = pl.program_id(1)

    @pl.when(kk == 0)
    def _():
        acc_s[...] = jnp.zeros_like(acc_s)

    a = jnp.maximum(_dot(h_ref[...], w1_ref[...]), 0.0)
    acc_s[...] += _dot((a * a).astype(BF16), w2_ref[...])

    @pl.when(kk == pl.num_programs(1) - 1)
    def _():
        o_ref[...] = x1_ref[...] + _mod(g2_ref) * _rms(acc_s[...], pg_ref[...])


def _ffn_call(h2, x1, w1_bf, w2_bf, post2_g, mod, tm, tf):
    m_rows = x1.shape[0]
    return pl.pallas_call(
        _ffn_kernel,
        out_shape=jax.ShapeDtypeStruct((m_rows, D_MODEL), F32),
        grid=(m_rows // tm, D_FF // tf),
        in_specs=[pl.BlockSpec((tm, D_MODEL), lambda i, k: (i, 0)),
                  pl.BlockSpec((tm, D_MODEL), lambda i, k: (i, 0)),
                  pl.BlockSpec((D_MODEL, tf), lambda i, k: (0, k)),
                  pl.BlockSpec((tf, D_MODEL), lambda i, k: (k, 0)),
                  pl.BlockSpec((1, D_MODEL), lambda i, k: (0, 0)),
                  mod.spec(5, 2)],
        out_specs=pl.BlockSpec((tm, D_MODEL), lambda i, k: (i, 0)),
        scratch_shapes=[pltpu.VMEM((tm, D_MODEL), F32)],
        compiler_params=_cparams(("arbitrary", "arbitrary")),
        name="ffn",
    )(h2, x1, w1_bf, w2_bf, post2_g.reshape(1, D_MODEL), mod.arr)


def _gdn_sample_kernel(cin_ref, z_ref, sm_ref, sconv_ref, ssm_ref, cw_ref, alog_ref, dtb_ref, ng_ref,
                       o_ref, ssm_out_ref, conv_out_ref):
    hd = HEAD_DIM
    nh = N_HEADS_A
    prev = sconv_ref[0]
    xin = cin_ref[0]
    cw = cw_ref[...]
    y = xin * cw[CONV_W - 1:CONV_W]
    for s in range(CONV_W - 1):
        y = y + prev[s:s + 1] * cw[s:s + 1]
    y = y * _sigmoid(y)
    conv_out_ref[0, 0:CONV_W - 2, :] = prev[1:CONV_W - 1]
    conv_out_ref[0, CONV_W - 2:CONV_W - 1, :] = xin

    def heads(base):
        return jnp.concatenate([y[:, base + h * hd:base + (h + 1) * hd] for h in range(nh)], axis=0)

    q8 = _l2norm(heads(0)) * (hd ** -0.5)
    k8 = _l2norm(heads(KEY_DIM_A))
    v8 = heads(2 * KEY_DIM_A)
    sm = sm_ref[0]
    g_all = -jnp.exp(alog_ref[...]) * _softplus(sm + dtb_ref[...])
    b_all = _sigmoid(sm)
    kt = jnp.concatenate([k8, jnp.zeros((LANE - nh, hd), F32)], axis=0).T
    k8b = k8.astype(BF16)
    q8b = q8.astype(BF16)
    z = z_ref[0]
    ng = ng_ref[...]
    for h in range(nh):
        s1 = ssm_ref[0, h] * jnp.exp(g_all[:, SM_A + h:SM_A + h + 1])
        kv = _dot(k8b, s1.astype(BF16))[h:h + 1]
        delta = (v8[h:h + 1] - kv) * b_all[:, SM_B + h:SM_B + h + 1]
        s2 = s1 + kt[:, h:h + 1] * delta
        ssm_out_ref[0, h] = s2
        o = _dot(q8b, s2.astype(BF16))[h:h + 1]
        zh = z[:, h * hd:(h + 1) * hd]
        on = o * lax.rsqrt(jnp.mean(o * o, axis=-1, keepdims=True) + EPS) * ng
        o_ref[0, :, h * hd:(h + 1) * hd] = (on * (zh * _sigmoid(zh))).astype(o_ref.dtype)


def _gdn_sample_call(proj_s3, state_conv, state_ssm, conv_w, a_log, dt_bias, norm_g):
    nb = proj_s3.shape[0]
    hd, nh = HEAD_DIM, N_HEADS_A
    vec = pl.BlockSpec((1, LANE), lambda b: (0, 0))
    return pl.pallas_call(
        _gdn_sample_kernel,
        out_shape=(jax.ShapeDtypeStruct((nb, 1, VAL_DIM_A), BF16),
                   jax.ShapeDtypeStruct((nb, nh, hd, hd), F32),
                   jax.ShapeDtypeStruct((nb, CONV_W - 1, CONV_DIM), F32)),
        grid=(nb,),
        in_specs=[pl.BlockSpec((1, 1, CONV_DIM), lambda b: (b, 0, OFF_CONV // CONV_DIM)),
                  pl.BlockSpec((1, 1, VAL_DIM_A), lambda b: (b, 0, OFF_Z // VAL_DIM_A)),
                  pl.BlockSpec((1, 1, LANE), lambda b: (b, 0, OFF_SM // LANE)),
                  pl.BlockSpec((1, CONV_W - 1, CONV_DIM), lambda b: (b, 0, 0)),
                  pl.BlockSpec((1, nh, hd, hd), lambda b: (b, 0, 0, 0)),
                  pl.BlockSpec((CONV_W, CONV_DIM), lambda b: (0, 0)),
                  vec, vec, vec],
        out_specs=(pl.BlockSpec((1, 1, VAL_DIM_A), lambda b: (b, 0, 0)),
                   pl.BlockSpec((1, nh, hd, hd), lambda b: (b, 0, 0, 0)),
                   pl.BlockSpec((1, CONV_W - 1, CONV_DIM), lambda b: (b, 0, 0))),
        compiler_params=_cparams(("arbitrary",)),
        name="gdn_sample",
    )(proj_s3, proj_s3, proj_s3, state_conv, state_ssm, conv_w,
      _lane_vec(a_log, SM_A), _lane_vec(dt_bias, SM_A), norm_g.reshape(1, hd).astype(F32))


IDX_PAGES_PER_MATMUL = 4


def _idx_sample_kernel(pt_sm, qi_ref, wi_ref, kidx_hbm, o_ref, buf, sem):
    b = pl.program_id(0)
    n_seq = pl.num_programs(0)
    n_pages = buf.shape[1]

    def page_copy(page, slot, p):
        return pltpu.make_async_copy(kidx_hbm.at[page], buf.at[slot, p], sem.at[slot])

    def gather(seq, slot):
        def body(p, _):
            page_copy(pt_sm[seq, p], slot, p).start()
            return 0

        lax.fori_loop(0, n_pages, body, 0, unroll=8)

    @pl.when(b == 0)
    def _():
        gather(0, 0)

    @pl.when(b + 1 < n_seq)
    def _():
        gather(b + 1, (b + 1) % 2)

    slot = b % 2

    def wait_body(p, _):
        page_copy(0, slot, p).wait()
        return 0

    lax.fori_loop(0, n_pages, wait_body, 0, unroll=8)

    qb = qi_ref[0].astype(BF16)
    w = wi_ref[0] * (N_IDX_HEADS ** -0.5)

    ppm = math.gcd(n_pages, IDX_PAGES_PER_MATMUL)

    def score(i, _):
        p0 = pl.multiple_of(i * ppm, ppm)
        keys = buf[slot, pl.ds(p0, ppm)].reshape(ppm * PAGE_SIZE, D_IDX).astype(BF16)
        s = _dot_nt(qb, keys)
        sc = jnp.sum(jnp.maximum(s, 0.0) * w, axis=0, keepdims=True) * (D_IDX ** -0.5)
        for k in range(ppm):
            o_ref[0, pl.ds(p0 + k, 1), :] = sc[:, k * PAGE_SIZE:(k + 1) * PAGE_SIZE]
        return 0

    lax.fori_loop(0, n_pages // ppm, score, 0, unroll=4)


def _idx_sample_call(page_table, qi3, wi3, cache_kidx):
    nb, n_pages = page_table.shape
    grid_spec = pltpu.PrefetchScalarGridSpec(
        num_scalar_prefetch=1,
        grid=(nb,),
        in_specs=[pl.BlockSpec((1, N_IDX_HEADS, D_IDX), lambda b, pt: (b, 0, 0)),
                  pl.BlockSpec((1, N_IDX_HEADS, 1), lambda b, pt: (b, 0, 0)),
                  pl.BlockSpec(memory_space=pl.ANY)],
        out_specs=pl.BlockSpec((1, n_pages, PAGE_SIZE), lambda b, pt: (b, 0, 0)),
        scratch_shapes=[pltpu.VMEM((2, n_pages, PAGE_SIZE, D_IDX), F32), pltpu.SemaphoreType.DMA((2,))],
    )
    return pl.pallas_call(
        _idx_sample_kernel,
        out_shape=jax.ShapeDtypeStruct((nb, n_pages, PAGE_SIZE), F32),
        grid_spec=grid_spec,
        compiler_params=_cparams(("arbitrary",)),
        name="idx_sample",
    )(page_table, qi3, wi3, cache_kidx)


IDX_FLAG = 1 << 20


def _topk_sample_kernel(n_sel, sc_ref, qi_ref, wi_ref, knew_ref, idx_ref, keys_s, keyw_s, rank_s, flag_s, cnt_s):
    nb, n_past = sc_ref.shape
    n_blk = n_past // LANE

    kn = knew_ref[...].astype(BF16)
    q = qi_ref[...]
    w = wi_ref[...] * (N_IDX_HEADS ** -0.5)
    s_new = jnp.zeros((nb, 1), F32)
    for h in range(N_IDX_HEADS):
        qh = q[:, h * D_IDX:(h + 1) * D_IDX].astype(BF16).astype(F32)
        dot = jnp.sum(qh * kn.astype(F32), axis=1, keepdims=True)
        s_new = s_new + jnp.maximum(dot, 0.0) * w[:, h:h + 1]
    s_new = s_new * (D_IDX ** -0.5)
    key_new = _order_key(s_new)

    def to_keys(j, _):
        keys_s[j] = _order_key(sc_ref[:, pl.ds(pl.multiple_of(j * LANE, LANE), LANE)])
        return 0

    lax.fori_loop(0, n_blk, to_keys, 0, unroll=8)
    keyw_s[...] = _order_key(sc_ref[...])
    wide = min(n_past, 16 * LANE)

    def count(pred):
        acc = jnp.zeros((nb, LANE), I32)
        for c in range(n_past // wide):
            hit = jnp.where(pred(keyw_s[:, c * wide:(c + 1) * wide]), 1, 0)
            for k in range(wide // LANE):
                acc = acc + hit[:, k * LANE:(k + 1) * LANE]
        return jnp.sum(acc, axis=1, keepdims=True)

    def count_ge(cand):
        return count(lambda k: k >= cand) + jnp.where(key_new >= cand, 1, 0)

    tau = jnp.maximum(_kth_largest_key(count_ge, (nb, 1), n_sel), INT_MIN + 1)

    quota = n_sel - (count(lambda k: k > tau) + jnp.where(key_new > tau, 1, 0))

    ri = lax.broadcasted_iota(I32, (LANE, LANE), 0)
    ci = lax.broadcasted_iota(I32, (LANE, LANE), 1)
    upper = (ri <= ci).astype(BF16)

    def excl_prefix(flag):
        return (_dot(flag.astype(BF16), upper) - flag).astype(I32)

    def rank_body(j, carry):
        base, base_eq = carry
        kj = keys_s[j]
        eq = kj == tau
        eqf = jnp.where(eq, 1.0, 0.0)
        take = (kj > tau) | (eq & (base_eq + excl_prefix(eqf) < quota))
        takef = jnp.where(take, 1.0, 0.0)
        n_take = jnp.sum(takef, axis=1, keepdims=True)
        rows = pl.ds(pl.multiple_of(j * nb, nb), nb)
        rank_s[rows, :] = jnp.where(take, (base + excl_prefix(takef) + 1).astype(F32), 0.0)
        flag_s[rows, :] = jnp.where((kj < KEY_POS_INF) & (kj > KEY_NEG_INF), 0.0, 1.0)
        cnt_s[rows, :] = jnp.broadcast_to(n_take, (nb, LANE))
        return (base + n_take.astype(I32), base_eq + jnp.sum(eqf, axis=1, keepdims=True).astype(I32))

    zero = jnp.zeros((nb, 1), I32)
    n_before_new, eq_before_new = lax.fori_loop(0, n_blk, rank_body, (zero, zero), unroll=4)

    new_sel = (key_new > tau) | ((key_new == tau) & (eq_before_new < quota))
    new_fin = (key_new < KEY_POS_INF) & (key_new > KEY_NEG_INF)
    new_pos = jnp.where(new_fin, n_past, n_past | IDX_FLAG)
    bi = lax.broadcasted_iota(I32, (n_blk, n_blk), 0)
    bj = lax.broadcasted_iota(I32, (n_blk, n_blk), 1)
    before = (bj < bi).astype(BF16)
    blk_id = lax.broadcasted_iota(I32, (n_blk, LANE), 0).astype(F32)
    slot_lane = lax.broadcasted_iota(I32, (n_blk, n_sel), 1).astype(F32)
    slot_row = lax.broadcasted_iota(I32, (n_sel, LANE), 0)
    lane_sel = lax.broadcasted_iota(I32, (n_sel, LANE), 1)
    reps = n_sel // LANE

    def per_seq(b, _):
        rows = pl.ds(b, n_blk, stride=nb)
        cnt = cnt_s[rows, :]
        first = _dot(before, cnt.astype(BF16))
        first_w = jnp.concatenate([first] * reps, axis=1)
        cnt_w = jnp.concatenate([cnt] * reps, axis=1)
        onehot = jnp.where((slot_lane >= first_w) & (slot_lane < first_w + cnt_w), 1.0, 0.0).astype(BF16)
        picked = _dot_tn(onehot, jnp.concatenate([rank_s[rows, :], flag_s[rows, :], blk_id], axis=1).astype(BF16))
        hit = picked[:, :LANE] == (slot_row + 1).astype(F32)
        pos = (picked[:, 2 * LANE:] * LANE + picked[:, LANE:2 * LANE] * IDX_FLAG).astype(I32) + lane_sel
        col = jnp.sum(jnp.where(hit, pos, 0), axis=1, keepdims=True)
        rb = lax.broadcasted_iota(I32, (nb, 1), 0) == b
        nbn = jnp.sum(jnp.where(rb, n_before_new, 0), axis=0, keepdims=True)
        nsel = jnp.sum(jnp.where(rb & new_sel, 1, 0), axis=0, keepdims=True)
        npos = jnp.sum(jnp.where(rb, new_pos, 0), axis=0, keepdims=True)
        col = col + jnp.where((slot_row[:, 0:1] == nbn) & (nsel > 0), npos, 0)
        idx_ref[b] = col
        return 0

    lax.fori_loop(0, nb, per_seq, 0)


def _attn_sample_kernel(n_past, idx_sm, pt_sm, idxc_ref, q_ref, kn_ref, vn_ref, rb_ref, ck_hbm, cv_hbm, o_ref,
                        kbuf, vbuf, sem):
    b = pl.program_id(0)
    n_seq = pl.num_programs(0)
    n_sel = kbuf.shape[1] // N_KV_B
    hd = HEAD_DIM
    page_shift = int(math.log2(PAGE_SIZE))

    def row_copies(buf, r, page, slot):
        dst = pl.ds(N_KV_B * r, N_KV_B)
        return (pltpu.make_async_copy(ck_hbm.at[page, slot], kbuf.at[buf, dst, :], sem.at[0, buf]),
                pltpu.make_async_copy(cv_hbm.at[page, slot], vbuf.at[buf, dst, :], sem.at[1, buf]))

    def gather(seq, buf):
        def start(r, _):
            p = jnp.minimum(idx_sm[seq, r] & (IDX_FLAG - 1), n_past - 1)
            ck, cv = row_copies(buf, r, pt_sm[seq, p >> page_shift], p & (PAGE_SIZE - 1))
            ck.start()
            cv.start()
            return 0

        lax.fori_loop(0, n_sel, start, 0, unroll=8)

    @pl.when(b == 0)
    def _():
        gather(0, 0)

    @pl.when(b + 1 < n_seq)
    def _():
        gather(b + 1, (b + 1) % 2)

    cur = b % 2

    def wait(r, _):
        ck, cv = row_copies(cur, r, 0, 0)
        ck.wait()
        cv.wait()
        return 0

    lax.fori_loop(0, n_sel, wait, 0, unroll=8)

    idxc = idxc_ref[0]
    valid = idxc < IDX_FLAG
    idx = idxc & (IDX_FLAG - 1)
    is_new = idx >= n_past
    bucket = _t5_bucket(n_past - idx)
    bias = jnp.zeros((n_sel, LANE), F32)
    for bkt in range(N_BUCKETS):
        bias = jnp.where(bucket == bkt, rb_ref[bkt:bkt + 1, :], bias)
    q_pad = jnp.concatenate([q_ref[0], jnp.zeros((LANE - N_HEADS_B, hd), F32)], axis=0).astype(BF16)
    lane = lax.broadcasted_iota(I32, (n_sel, LANE), 1)
    lg = jnp.zeros((n_sel, LANE), F32)
    vals = []
    for n in range(N_KV_B):
        rows_n = pl.ds(n, n_sel, stride=N_KV_B)
        kn = jnp.where(is_new, kn_ref[0, :, n * hd:(n + 1) * hd], kbuf[cur, rows_n, :]).astype(BF16)
        vals.append(jnp.where(is_new, vn_ref[0, :, n * hd:(n + 1) * hd], vbuf[cur, rows_n, :]))
        lg = jnp.where(lane // GQA_GROUP == n, _dot_nt(kn, q_pad), lg)
    lg = jnp.where(valid, lg * (hd ** -0.5) + bias, NEG_BIG)
    m = jnp.max(lg, axis=0, keepdims=True)
    p = jnp.where(valid, jnp.exp(lg - m), 0.0)
    p = p / jnp.sum(p, axis=0, keepdims=True)
    for h in range(N_HEADS_B):
        o_h = jnp.sum(p[:, h:h + 1] * vals[h // GQA_GROUP], axis=0, keepdims=True)
        o_ref[0, :, h * hd:(h + 1) * hd] = o_h.astype(o_ref.dtype)


def _attn_sample_call(idx3, page_table, q3, k_new, v_new, rel_bias, cache_k, cache_v):
    nb, n_sel, _ = idx3.shape
    n_past = page_table.shape[1] * PAGE_SIZE
    kvw = N_KV_B * HEAD_DIM
    qbw = N_HEADS_B * HEAD_DIM
    grid_spec = pltpu.PrefetchScalarGridSpec(
        num_scalar_prefetch=2,
        grid=(nb,),
        in_specs=[pl.BlockSpec((1, n_sel, 1), lambda b, *_: (b, 0, 0)),
                  pl.BlockSpec((1, N_HEADS_B, HEAD_DIM), lambda b, *_: (b, 0, 0)),
                  pl.BlockSpec((1, 1, kvw), lambda b, *_: (b, 0, 0)),
                  pl.BlockSpec((1, 1, kvw), lambda b, *_: (b, 0, 0)),
                  pl.BlockSpec((N_BUCKETS, LANE), lambda b, *_: (0, 0)),
                  pl.BlockSpec(memory_space=pl.ANY),
                  pl.BlockSpec(memory_space=pl.ANY)],
        out_specs=pl.BlockSpec((1, 1, qbw), lambda b, *_: (b, 0, 0)),
        scratch_shapes=[pltpu.VMEM((2, n_sel * N_KV_B, HEAD_DIM), F32), pltpu.VMEM((2, n_sel * N_KV_B, HEAD_DIM), F32),
                        pltpu.SemaphoreType.DMA((2, 2))],
    )
    return pl.pallas_call(
        functools.partial(_attn_sample_kernel, n_past),
        out_shape=jax.ShapeDtypeStruct((nb, 1, qbw), BF16),
        grid_spec=grid_spec,
        compiler_params=_cparams(("arbitrary",)),
        name="attn_sample",
    )(idx3.reshape(nb, n_sel), page_table, idx3, q3, k_new.reshape(nb, 1, kvw), v_new.reshape(nb, 1, kvw),
      jnp.pad(rel_bias.astype(F32), ((0, 0), (0, LANE - N_HEADS_B))), cache_k, cache_v)


def _topk_sample_call(scores2, qi2, wi2, ki_new, n_sel):
    nb, n_past = scores2.shape
    n_blk = n_past // LANE
    return pl.pallas_call(
        functools.partial(_topk_sample_kernel, n_sel),
        out_shape=jax.ShapeDtypeStruct((nb, n_sel, 1), I32),
        scratch_shapes=[pltpu.VMEM((n_blk, nb, LANE), I32), pltpu.VMEM((nb, n_past), I32),
                        pltpu.VMEM((n_blk * nb, LANE), F32), pltpu.VMEM((n_blk * nb, LANE), F32),
                        pltpu.VMEM((n_blk * nb, LANE), F32)],
        compiler_params=pltpu.CompilerParams(vmem_limit_bytes=VMEM_LIMIT),
        name="topk_sample",
    )(scores2, qi2, wi2, ki_new)


WPREP_COLS = 256


def _wprep_plan():
    src = [0]
    for s in SPLIT_SIZES:
        src.append(src[-1] + s)
    conv, a_raw, b_raw, z, q_b, k_b, v_b, q_i, w_i, k_i = src[:10]
    starts = {OFF_CONV: conv, OFF_Z: z, OFF_QI: q_i, OFF_QB: q_b, OFF_KB: k_b, OFF_VB: v_b}
    sizes = {OFF_CONV: CONV_DIM, OFF_Z: VAL_DIM_A, OFF_QI: N_IDX_HEADS * D_IDX, OFF_QB: N_HEADS_B * HEAD_DIM,
             OFF_KB: N_KV_B * HEAD_DIM, OFF_VB: N_KV_B * HEAD_DIM}
    offs = []
    for off, start in starts.items():
        assert off % WPREP_COLS == 0 and sizes[off] % WPREP_COLS == 0
        offs += [(off + c, start + c) for c in range(0, sizes[off], WPREP_COLS)]
    offs = [s for _, s in sorted(offs)]
    assert len(offs) * WPREP_COLS == OFF_KI and OFF_KI + D_IDX == OFF_SM and OFF_SM + LANE == OFF_KI + WPREP_COLS
    assert a_raw + N_HEADS_A == b_raw and w_i + N_IDX_HEADS == k_i and k_i + D_IDX == src[10]
    tail_start = src[10] - WPREP_COLS
    offs += [tail_start, 0]
    return offs, a_raw, w_i - tail_start, k_i - tail_start


def _wprep_kernel(offs_ref, wt_ref, ab_ref, o_ref):
    del offs_ref
    i = pl.program_id(0)
    n_blk = pl.num_programs(0)
    _, _, wi_at, ki_at = _wprep_plan()

    @pl.when(i < n_blk - 2)
    def _():
        o_ref[...] = wt_ref[0].T.astype(o_ref.dtype)

    @pl.when(i == n_blk - 2)
    def _():
        t = wt_ref[0].T
        ab = jnp.concatenate([ab_ref[0], jnp.zeros((LANE - ab_ref.shape[1], ab_ref.shape[2]), F32)], axis=0).T
        small = jnp.concatenate([ab[:, :SM_W], t[:, wi_at:wi_at + N_IDX_HEADS],
                                 jnp.zeros((t.shape[0], LANE - SM_W - N_IDX_HEADS), F32)], axis=1)
        o_ref[...] = jnp.concatenate([t[:, ki_at:ki_at + D_IDX], small], axis=1).astype(o_ref.dtype)

    @pl.when(i == n_blk - 1)
    def _():
        o_ref[...] = jnp.zeros(o_ref.shape, o_ref.dtype)


def _prep_w_in(w_all, layer):
    offs, ab_at, _, _ = _wprep_plan()
    wt = jnp.swapaxes(w_all, 1, 2)
    d_in = wt.shape[2]
    n_ab = 2 * N_HEADS_A
    assert ab_at % n_ab == 0
    grid_spec = pltpu.PrefetchScalarGridSpec(
        num_scalar_prefetch=1,
        grid=(NP // WPREP_COLS,),
        in_specs=[pl.BlockSpec((pl.Element(1), pl.Element(WPREP_COLS), pl.Element(d_in)),
                               lambda i, offs_ref: (layer, pl.multiple_of(offs_ref[i], n_ab), 0)),
                  pl.BlockSpec((1, n_ab, d_in), lambda i, offs_ref: (layer, ab_at // n_ab, 0))],
        out_specs=pl.BlockSpec((d_in, WPREP_COLS), lambda i, offs_ref: (0, i)),
    )
    return pl.pallas_call(
        _wprep_kernel,
        out_shape=jax.ShapeDtypeStruct((d_in, NP), BF16),
        grid_spec=grid_spec,
        compiler_params=_cparams(("arbitrary",)),
        name="wprep",
    )(jnp.asarray(offs, I32), wt, wt)


def kernel(x_prompt, x_sample, c_prompt, c_sample, cache_k, cache_v, cache_kidx, state_ssm, state_conv, page_table,
           w_ada, b_ada, pre1_g, post1_g, pre2_g, post2_g, w_in, w_out, conv_w, a_log, dt_bias, gdn_norm_g,
           idx_knorm_g, idx_knorm_b, rel_bias, w_ff1, w_ff2):
    bsz, seq, _ = x_prompt.shape
    nb, dec_seq, _ = x_sample.shape
    assert dec_seq == 1, "the sample path handles one new token per sequence"
    depth = w_in.shape[0]
    n_past = page_table.shape[1] * PAGE_SIZE
    n_sel_s = min(TOPK_MAX, (n_past + dec_seq) // 4)
    kvw = N_KV_B * HEAD_DIM
    tm_in, tn_in = min(1024, seq), 1024
    tm_out = min(512, seq)
    tf = 1024

    hp = x_prompt.reshape(bsz * seq, D_MODEL)
    hs = x_sample.reshape(nb, D_MODEL)
    c_all = jnp.concatenate([c_sample, c_prompt], axis=0)
    new_p, new_s = [], []
    for l in range(depth):
        m = _adaln_call(c_all, w_ada[l], b_ada[l])
        w_in_bf = _prep_w_in(w_in, l)
        w_out_bf = w_out[l].astype(BF16)
        w1_bf = w_ff1[l].astype(BF16)
        w2_bf = w_ff2[l].astype(BF16)

        proj, conv_tail = _inproj_call(hp, pre1_g[l], _ModSpec(m, False, nb, seq, tm_in), w_in_bf, tm_in, tn_in,
                                       conv_w[l], seq)
        k_p, v_p, ki_p, k_bf, va_bf, ki_bf = _kvprep_call(proj, idx_knorm_g[l], idx_knorm_b[l], tm_in, True)
        proj3 = proj.reshape(bsz, seq, NP)
        o_a, ssm_p = _gdn_prompt_call(proj3, a_log[l], dt_bias[l], gdn_norm_g[l])
        o_b = _attn_prompt_call(proj3, ki_bf.reshape(bsz, seq, D_IDX), k_bf.reshape(bsz, seq, kvw),
                                va_bf.reshape(bsz, seq, 2 * kvw), rel_bias)
        mod_p = _ModSpec(m, False, nb, seq, tm_out)
        x1, h2 = _outproj_call(o_a.reshape(bsz * seq, VAL_DIM_A), o_b.reshape(bsz * seq, N_HEADS_B * HEAD_DIM), hp,
                               w_out_bf, post1_g[l], pre2_g[l], mod_p, tm_out)
        hp = _ffn_call(h2, x1, w1_bf, w2_bf, post2_g[l], mod_p, tm_out, tf)
        conv_p = conv_tail.reshape(bsz, seq // tm_in, CONV_TAIL, CONV_DIM)[:, -1, CONV_TAIL - (CONV_W - 1):, :]
        new_p.append((k_p.reshape(bsz, seq, N_KV_B, HEAD_DIM), v_p.reshape(bsz, seq, N_KV_B, HEAD_DIM),
                      ki_p.reshape(bsz, seq, D_IDX), ssm_p, conv_p))

        mod_s = _ModSpec(m, True, 0, 1, nb)
        proj_s = _inproj_call(hs, pre1_g[l], mod_s, w_in_bf, nb, tn_in)
        k_s, v_s, ki_s = _kvprep_call(proj_s, idx_knorm_g[l], idx_knorm_b[l], nb)
        o_a_s, ssm_s, conv_s = _gdn_sample_call(proj_s.reshape(nb, 1, NP), state_conv[l], state_ssm[l], conv_w[l],
                                                a_log[l], dt_bias[l], gdn_norm_g[l])
        qi2 = proj_s[:, OFF_QI:OFF_QI + N_IDX_HEADS * D_IDX]
        wi2 = proj_s[:, OFF_SM + SM_W:OFF_SM + SM_W + N_IDX_HEADS]
        scores = _idx_sample_call(page_table, qi2.reshape(nb, N_IDX_HEADS, D_IDX), wi2.reshape(nb, N_IDX_HEADS, 1),
                                  cache_kidx[l])
        idx3 = _topk_sample_call(scores.reshape(nb, n_past), qi2, wi2, ki_s, n_sel_s)
        q3 = proj_s[:, OFF_QB:OFF_QB + N_HEADS_B * HEAD_DIM].reshape(nb, N_HEADS_B, HEAD_DIM)
        o_b_s = _attn_sample_call(idx3, page_table, q3, k_s, v_s, rel_bias, cache_k[l], cache_v[l])
        x1s, h2s = _outproj_call(o_a_s.reshape(nb, VAL_DIM_A), o_b_s.reshape(nb, N_HEADS_B * HEAD_DIM), hs,
                                 w_out_bf, post1_g[l], pre2_g[l], mod_s, nb)
        hs = _ffn_call(h2s, x1s, w1_bf, w2_bf, post2_g[l], mod_s, nb, tf)
        new_s.append((k_s.reshape(nb, 1, N_KV_B, HEAD_DIM), v_s.reshape(nb, 1, N_KV_B, HEAD_DIM),
                      ki_s.reshape(nb, 1, D_IDX), ssm_s, conv_s))

    p_k, p_v, p_kidx, p_ssm, p_conv = [jnp.stack([st[i] for st in new_p]) for i in range(5)]
    s_k, s_v, s_kidx, s_ssm, s_conv = [jnp.stack([st[i] for st in new_s]) for i in range(5)]
    return (hp.reshape(bsz, seq, D_MODEL), hs.reshape(nb, 1, D_MODEL), p_k, p_v, p_kidx, p_ssm, p_conv,
            s_k, s_v, s_kidx, s_ssm, s_conv)
```
